```python
import math
import jax, jax.numpy as jnp
from jax import lax
import numpy as np

D_MODEL = 1024
BATCH = 8
SEQ = 4096
DEPTH = 2

FOX_HEADS = 8
FOX_DIM = 64
FOX_W = FOX_HEADS * FOX_DIM
DSA_HEADS = 8
DSA_DIM = 64
DSA_W = DSA_HEADS * DSA_DIM
IDX_HEADS = 8
IDX_DIM = 64
TOPK_MAX = 256
D_FF = 2816
QB = 128
ALPHA = (2.0 * DEPTH) ** 0.25
BETA = (8.0 * DEPTH) ** -0.25
LN_EPS = 1e-5
NEG_INF = -1e30
IDX_SCALE = (IDX_HEADS ** -0.5) * (IDX_DIM ** -0.5)

IN_WIDTHS = (FOX_W, FOX_W, FOX_W, FOX_HEADS,
             DSA_W, DSA_W, DSA_W, IDX_HEADS * IDX_DIM, IDX_DIM, IDX_HEADS,
             D_MODEL, D_MODEL)
IN_SCALES = (1.0, 1.0, BETA, 1.0,
             1.0, 1.0, BETA, 1.0, 1.0, 1.0,
             1.0, 1.0)
N_IN = sum(IN_WIDTHS)

kernel_name = "fox_dsa_gated_hybrid_deepnorm"


def _layer_norm(x, g, b):
    xf = x.astype(jnp.float32)
    mu = jnp.mean(xf, axis=-1, keepdims=True)
    var = jnp.mean(jnp.square(xf - mu), axis=-1, keepdims=True)
    y = (xf - mu) * lax.rsqrt(var + LN_EPS)
    return (y * g.astype(jnp.float32) + b.astype(jnp.float32)).astype(x.dtype)


def _alibi_slopes(n_heads):
    h = jnp.arange(n_heads, dtype=jnp.float32)
    return jnp.exp2(-8.0 * (h + 1.0) / n_heads)


def _split_columns(proj):
    points = [int(p) for p in np.cumsum(IN_WIDTHS)[:-1]]
    return jnp.split(proj, points, axis=-1)


def _fox_attention(q, k, v, f_logit):
    B, L, H, dh = q.shape
    scale = dh ** -0.5
    c = jnp.cumsum(jax.nn.log_sigmoid(f_logit.astype(jnp.float32)), axis=1)
    c = jnp.transpose(c, (0, 2, 1))
    kpos = jnp.arange(L)

    def block(i):
        start = i * QB
        qb = lax.dynamic_slice_in_dim(q, start, QB, axis=1)
        cb = lax.dynamic_slice_in_dim(c, start, QB, axis=2)
        qpos = start + jnp.arange(QB)
        s = jnp.einsum('bqhd,bkhd->bhqk', qb, k).astype(jnp.float32) * scale
        s = s + (cb[..., None] - c[:, :, None, :])
        s = jnp.where(kpos[None, :] <= qpos[:, None], s, NEG_INF)
        p = jax.nn.softmax(s, axis=-1).astype(v.dtype)
        return jnp.einsum('bhqk,bkhd->bqhd', p, v)

    o = lax.map(block, jnp.arange(L // QB))
    return jnp.transpose(o, (1, 0, 2, 3, 4)).reshape(B, L, H * dh)


def _dsa_attention(q, k, v, iq, ik, iw):
    B, L, H, dh = q.shape
    scale = dh ** -0.5
    topk = min(TOPK_MAX, L // 4)
    slopes = _alibi_slopes(H)
    kpos = jnp.arange(L)
    gather = jax.vmap(lambda arr, idx: arr[idx])

    def block(i):
        start = i * QB
        qpos = start + jnp.arange(QB)
        iqb = lax.dynamic_slice_in_dim(iq, start, QB, axis=1)
        iwb = lax.dynamic_slice_in_dim(iw, start, QB, axis=1)
        rel = jax.nn.relu(jnp.einsum('bqhd,bkd->bqhk', iqb, ik).astype(jnp.float32))
        score = jnp.einsum('bqh,bqhk->bqk', iwb.astype(jnp.float32) * IDX_SCALE, rel)
        score = jnp.where(kpos[None, None, :] <= qpos[None, :, None], score, NEG_INF)
        _, idx = lax.top_k(score, topk)
        kg = gather(k, idx)
        vg = gather(v, idx)
        qb = lax.dynamic_slice_in_dim(q, start, QB, axis=1)
        dist = (qpos[None, :, None] - idx).astype(jnp.float32)
        s = jnp.einsum('bqhd,bqkhd->bhqk', qb, kg).astype(jnp.float32) * scale
        s = s - slopes[None, :, None, None] * dist[:, None]
        s = jnp.where((dist >= 0.0)[:, None], s, NEG_INF)
        p = jax.nn.softmax(s, axis=-1).astype(v.dtype)
        return jnp.einsum('bhqk,bqkhd->bqhd', p, vg)

    o = lax.map(block, jnp.arange(L // QB))
    return jnp.transpose(o, (1, 0, 2, 3, 4)).reshape(B, L, H * dh)


def setup_inputs(seed: int = 0) -> dict:
    key = jax.random.key(seed)
    keys = jax.random.split(key, 32)
    f32 = jnp.float32

    def nrm(k, shape, fan_in, scale=1.0):
        return jax.random.normal(k, shape, f32) * (scale * fan_in ** -0.5)

    x = jax.random.normal(keys[0], (BATCH, SEQ, D_MODEL), f32)
    in_keys = jax.random.split(keys[1], len(IN_WIDTHS))
    w_in = jnp.concatenate(
        [nrm(kk, (DEPTH, D_MODEL, w), D_MODEL, s)
         for kk, w, s in zip(in_keys, IN_WIDTHS, IN_SCALES)], axis=-1)
    b_forget = jax.random.uniform(keys[2], (DEPTH, FOX_HEADS), f32, 1.0, 6.0)
    w_branch_a = nrm(keys[3], (DEPTH, FOX_W, D_MODEL), FOX_W)
    w_branch_b = nrm(keys[4], (DEPTH, DSA_W, D_MODEL), DSA_W)
    w_out = nrm(keys[5], (DEPTH, D_MODEL, D_MODEL), D_MODEL, BETA)
    ln1_g = 1.0 + 0.02 * jax.random.normal(keys[6], (DEPTH, D_MODEL), f32)
    ln1_b = 0.02 * jax.random.normal(keys[7], (DEPTH, D_MODEL), f32)
    w_ffn_in = nrm(keys[8], (DEPTH, D_MODEL, 2 * D_FF), D_MODEL, BETA)
    w_ffn_out = nrm(keys[9], (DEPTH, D_FF, D_MODEL), D_FF, BETA)
    ln2_g = 1.0 + 0.02 * jax.random.normal(keys[10], (DEPTH, D_MODEL), f32)
    ln2_b = 0.02 * jax.random.normal(keys[11], (DEPTH, D_MODEL), f32)
    return {"x": x, "w_in": w_in, "b_forget": b_forget,
            "w_branch_a": w_branch_a, "w_branch_b": w_branch_b, "w_out": w_out,
            "ln1_g": ln1_g, "ln1_b": ln1_b,
            "w_ffn_in": w_ffn_in, "w_ffn_out": w_ffn_out,
            "ln2_g": ln2_g, "ln2_b": ln2_b}


def reference(x, w_in, b_forget, w_branch_a, w_branch_b, w_out,
              ln1_g, ln1_b, w_ffn_in, w_ffn_out, ln2_g, ln2_b):
    B, L, D = x.shape
    for l in range(DEPTH):
        proj = x @ w_in[l]
        (fq, fk, fv, f_logit, dq, dk, dv, iq, ik, iw, g_a, g_b) = _split_columns(proj)
        o_a = _fox_attention(fq.reshape(B, L, FOX_HEADS, FOX_DIM),
                             fk.reshape(B, L, FOX_HEADS, FOX_DIM),
                             fv.reshape(B, L, FOX_HEADS, FOX_DIM),
                             f_logit + b_forget[l])
        o_b = _dsa_attention(dq.reshape(B, L, DSA_HEADS, DSA_DIM),
                             dk.reshape(B, L, DSA_HEADS, DSA_DIM),
                             dv.reshape(B, L, DSA_HEADS, DSA_DIM),
                             iq.reshape(B, L, IDX_HEADS, IDX_DIM), ik, iw)
        merged = (jax.nn.sigmoid(g_a) * (o_a @ w_branch_a[l])
                  + jax.nn.sigmoid(g_b) * (o_b @ w_branch_b[l]))
        x = _layer_norm(ALPHA * x + merged @ w_out[l], ln1_g[l], ln1_b[l])
        gate, up = jnp.split(x @ w_ffn_in[l], 2, axis=-1)
        ffn = (jax.nn.silu(gate) * up) @ w_ffn_out[l]
        x = _layer_norm(ALPHA * x + ffn, ln2_g[l], ln2_b[l])
    return x
```

```python
import functools

import numpy as np
import jax
import jax.numpy as jnp
from jax import lax
from jax.experimental import pallas as pl
from jax.experimental.pallas import tpu as pltpu

F32 = jnp.float32
BF16 = jnp.bfloat16
I32 = jnp.int32

D_MODEL = 1024
HEADS = 8
HEAD_DIM = 64
PAIRS = HEADS // 2
D_FF = 2816
TOPK = 256
LANES = 128
NEG_INF = -1e30
LN_EPS = 1e-5
ATTN_SCALE = HEAD_DIM ** -0.5
IDX_SCALE = (HEADS ** -0.5) * (HEAD_DIM ** -0.5)

_OFF_FLOGIT = 1536
_OFF_DQ = 1544
_OFF_IK = 3592
_OFF_IW = 3656
_OFF_GA = 3664
_N_IN = 5712

_CB_FQ, _CB_FK, _CB_FV, _CB_DQ, _CB_DK, _CB_DV, _CB_GA, _CB_GB, _CB_IQ = 0, 4, 8, 12, 16, 20, 24, 32, 40
N_MAIN_BLOCKS = 44
N_MAIN = N_MAIN_BLOCKS * LANES
_SM_IK, _SM_IW, _SM_FL = 0, 64, 72

ROW_TILE = 512
TQ = 256
TK = 256
CUMSUM_BLOCK = 256

INT_MIN = -(2 ** 31)


def _monotone_key_of(value):
    b = int(np.float32(value).view(np.int32))
    return b ^ ((b >> 31) & 0x7FFFFFFF)


NEG_INF_KEY = _monotone_key_of(NEG_INF)
VMEM_LIMIT = 56 * 1024 * 1024


def _params(sem):
    return pltpu.CompilerParams(dimension_semantics=sem, vmem_limit_bytes=VMEM_LIMIT)


def _const_spec(shape):
    nd = len(shape)
    return pl.BlockSpec(shape, lambda *_: (0,) * nd, pipeline_mode=pl.Buffered(1))


def _proj_kernel(x_ref, wm_ref, ws_ref, om_ref, os_ref):
    xb = x_ref[0].astype(BF16)
    os_ref[0] = jnp.dot(xb, ws_ref[...], preferred_element_type=F32)
    group = 4
    for c in range(N_MAIN_BLOCKS // group):
        w = wm_ref[:, c * group * LANES:(c + 1) * group * LANES]
        r = jnp.dot(xb, w, preferred_element_type=F32).astype(BF16)
        for g in range(group):
            om_ref[0, c * group + g] = r[:, g * LANES:(g + 1) * LANES]


def _project(x, w_main, w_small):
    B, L, D = x.shape
    tm = ROW_TILE
    return pl.pallas_call(
        _proj_kernel,
        grid=(B, L // tm),
        in_specs=[pl.BlockSpec((1, tm, D), lambda b, i: (b, i, 0)),
                  _const_spec((D, N_MAIN)),
                  _const_spec((D, LANES))],
        out_specs=[pl.BlockSpec((1, N_MAIN_BLOCKS, tm, LANES), lambda b, i: (b, 0, i, 0)),
                   pl.BlockSpec((1, tm, LANES), lambda b, i: (b, i, 0))],
        out_shape=[jax.ShapeDtypeStruct((B, N_MAIN_BLOCKS, L, LANES), BF16),
                   jax.ShapeDtypeStruct((B, L, LANES), F32)],
        compiler_params=_params(("parallel", "parallel")),
    )(x, w_main, w_small)


def _split3(v):
    hi = v.astype(BF16).astype(F32)
    r = v - hi
    mid = r.astype(BF16).astype(F32)
    lo = (r - mid).astype(BF16).astype(F32)
    return hi, mid, lo


def _cumsum_rows(x):
    n = CUMSUM_BLOCK
    L = x.shape[0]
    r = lax.broadcasted_iota(I32, (n, n), 0)
    c = lax.broadcasted_iota(I32, (n, n), 1)
    tri = jnp.where(c <= r, 1.0, 0.0).astype(BF16)
    carry = jnp.zeros((1, x.shape[1]), F32)
    outs = []
    for blk in range(L // n):
        hi, mid, lo = _split3(x[blk * n:(blk + 1) * n])
        y = (jnp.dot(tri, hi.astype(BF16), preferred_element_type=F32)
             + jnp.dot(tri, mid.astype(BF16), preferred_element_type=F32)
             + jnp.dot(tri, lo.astype(BF16), preferred_element_type=F32)) + carry
        carry = y[n - 1:n, :]
        outs.append(y)
    return jnp.concatenate(outs, axis=0)


def _lane_terms(lane, base, terms):
    out = jnp.zeros(lane.shape, F32)
    for k, t in enumerate(terms):
        out = jnp.where(lane == base + k, t, out)
    return out


def _store_transposed_chunks(ref, lead, xT):
    L = xT.shape[1]
    for j in range(L // TK):
        ref[lead + (j,)] = xT[:, j * TK:(j + 1) * TK]


def _fox_prep_kernel(q_ref, k_ref, v_ref, s_ref, bias_ref, qT_ref, ka_ref, vT_ref):
    p = pl.program_id(1)
    L = q_ref.shape[2]
    z = s_ref[0] + bias_ref[...]
    c = _cumsum_rows(jax.nn.log_sigmoid(z))
    lane = lax.broadcasted_iota(I32, (L, LANES), 1)
    q = q_ref[0, 0].astype(F32) * ATTN_SCALE
    k = k_ref[0, 0].astype(F32)
    for e in range(2):
        c_h = jnp.sum(jnp.where(lane == _SM_FL + 2 * p + e, c, 0.0), axis=1, keepdims=True)
        hi, mid, lo = _split3(c_h)
        data = (lane >= e * HEAD_DIM) & (lane < (e + 1) * HEAD_DIM)
        base = (1 - e) * HEAD_DIM
        q_aug = jnp.where(data, q, _lane_terms(lane, base, [hi, mid, lo, 1.0, 1.0, 1.0]))
        k_aug = jnp.where(data, k, _lane_terms(lane, base, [1.0, 1.0, 1.0, -hi, -mid, -lo]))
        qT_ref[0, e] = q_aug.T.astype(BF16)
        ka_ref[0, e] = k_aug.astype(BF16)
    _store_transposed_chunks(vT_ref, (0, 0), v_ref[0, 0].astype(F32).T.astype(BF16))


def _fox_prep(p4, small, bias_row):
    B, _, L, _ = p4.shape
    nk = L // TK

    def pair_spec(cb):
        return pl.BlockSpec((1, 1, L, LANES), lambda b, p: (b, cb + p, 0, 0))

    return pl.pallas_call(
        _fox_prep_kernel,
        grid=(B, PAIRS),
        in_specs=[pair_spec(_CB_FQ), pair_spec(_CB_FK), pair_spec(_CB_FV),
                  pl.BlockSpec((1, L, LANES), lambda b, p: (b, 0, 0)),
                  pl.BlockSpec((1, LANES), lambda b, p: (0, 0))],
        out_specs=[pl.BlockSpec((1, 2, LANES, L), lambda b, p: (b, p, 0, 0)),
                   pl.BlockSpec((1, 2, L, LANES), lambda b, p: (b, p, 0, 0)),
                   pl.BlockSpec((1, 1, nk, LANES, TK), lambda b, p: (b, p, 0, 0, 0))],
        out_shape=[jax.ShapeDtypeStruct((B, HEADS, LANES, L), BF16),
                   jax.ShapeDtypeStruct((B, HEADS, L, LANES), BF16),
                   jax.ShapeDtypeStruct((B, PAIRS, nk, LANES, TK), BF16)],
        compiler_params=_params(("parallel", "arbitrary")),
    )(p4, p4, p4, small, bias_row)


def _dsa_prep_kernel(q_ref, k_ref, v_ref, iq_ref, s_ref, qT_ref, ka_ref, vT_ref, iqT_ref, iwT_ref):
    p = pl.program_id(1)
    L = q_ref.shape[2]
    lane = lax.broadcasted_iota(I32, (L, LANES), 1)
    pos = lax.broadcasted_iota(I32, (L, 1), 0)
    pos_hi = lax.shift_right_logical(pos, 6).astype(F32)
    pos_lo = (pos & 63).astype(F32)
    q = q_ref[0, 0].astype(F32) * ATTN_SCALE
    k = k_ref[0, 0].astype(F32)
    for e in range(2):
        h = jnp.full((L, 1), 2 * p + e, I32)
        slope = lax.shift_left(jnp.ones((L, 1), I32), 7 - h).astype(F32) * (1.0 / 256.0)
        data = (lane >= e * HEAD_DIM) & (lane < (e + 1) * HEAD_DIM)
        base = (1 - e) * HEAD_DIM
        q_aug = jnp.where(data, q, _lane_terms(
            lane, base, [-(slope * 64.0) * pos_hi, -slope * pos_lo, slope * 64.0, slope]))
        k_aug = jnp.where(data, k, _lane_terms(lane, base, [1.0, 1.0, pos_hi, pos_lo]))
        qT_ref[0, e] = q_aug.T.astype(BF16)
        ka_ref[0, e] = k_aug.astype(BF16)
    _store_transposed_chunks(vT_ref, (0, 0), v_ref[0, 0].astype(F32).T.astype(BF16))
    iqT_ref[0, 0] = iq_ref[0, 0].astype(F32).T.astype(BF16)
    sT = s_ref[0].T
    iwT_ref[0] = sT[_SM_IW:_SM_IW + HEADS, :] * IDX_SCALE


def _dsa_prep(p4, small):
    B, _, L, _ = p4.shape
    nk = L // TK

    def pair_spec(cb):
        return pl.BlockSpec((1, 1, L, LANES), lambda b, p: (b, cb + p, 0, 0))

    return pl.pallas_call(
        _dsa_prep_kernel,
        grid=(B, PAIRS),
        in_specs=[pair_spec(_CB_DQ), pair_spec(_CB_DK), pair_spec(_CB_DV), pair_spec(_CB_IQ),
                  pl.BlockSpec((1, L, LANES), lambda b, p: (b, 0, 0))],
        out_specs=[pl.BlockSpec((1, 2, LANES, L), lambda b, p: (b, p, 0, 0)),
                   pl.BlockSpec((1, 2, L, LANES), lambda b, p: (b, p, 0, 0)),
                   pl.BlockSpec((1, 1, nk, LANES, TK), lambda b, p: (b, p, 0, 0, 0)),
                   pl.BlockSpec((1, 1, LANES, L), lambda b, p: (b, p, 0, 0)),
                   pl.BlockSpec((1, HEADS, L), lambda b, p: (b, 0, 0))],
        out_shape=[jax.ShapeDtypeStruct((B, HEADS, LANES, L), BF16),
                   jax.ShapeDtypeStruct((B, HEADS, L, LANES), BF16),
                   jax.ShapeDtypeStruct((B, PAIRS, nk, LANES, TK), BF16),
                   jax.ShapeDtypeStruct((B, PAIRS, LANES, L), BF16),
                   jax.ShapeDtypeStruct((B, HEADS, L), F32)],
        compiler_params=_params(("parallel", "arbitrary")),
    )(p4, p4, p4, p4, small)


def _softmax_step(e, sT, vT, m_sc, l_sc, acc_sc):
    m_old = m_sc[e]
    m_new = jnp.maximum(m_old, jnp.max(sT, axis=0, keepdims=True))
    alpha = jnp.exp(m_old - m_new)
    pT = jnp.exp(sT - m_new)
    l_sc[e] = alpha * l_sc[e] + jnp.sum(pT, axis=0, keepdims=True)
    acc_sc[e] = alpha * acc_sc[e] + jnp.dot(vT, pT.astype(BF16), preferred_element_type=F32)
    m_sc[e] = m_new


def _init_softmax(m_sc, l_sc, acc_sc):
    m_sc[...] = jnp.full(m_sc.shape, NEG_INF, F32)
    l_sc[...] = jnp.zeros(l_sc.shape, F32)
    acc_sc[...] = jnp.zeros(acc_sc.shape, F32)


def _finish_softmax(o_ref, l_sc, acc_sc):
    top = acc_sc[0][:HEAD_DIM] / l_sc[0]
    bot = acc_sc[1][HEAD_DIM:] / l_sc[1]
    o_ref[0] = jnp.concatenate([top, bot], axis=0).T.astype(o_ref.dtype)


def _causal(shape):
    return lax.broadcasted_iota(I32, shape, 0) <= lax.broadcasted_iota(I32, shape, 1)


def _fox_kernel(qT_ref, k_ref, vT_ref, o_ref, m_sc, l_sc, acc_sc):
    i = pl.program_id(2)
    _init_softmax(m_sc, l_sc, acc_sc)

    def chunk(j, masked):
        ks = pl.multiple_of(j * TK, TK)
        vT = vT_ref[0, 0, j]
        for e in range(2):
            sT = jnp.dot(k_ref[0, e, pl.ds(ks, TK), :], qT_ref[0, e], preferred_element_type=F32)
            if masked:
                sT = jnp.where(_causal(sT.shape), sT, NEG_INF)
            _softmax_step(e, sT, vT, m_sc, l_sc, acc_sc)

    def body(j, carry):
        chunk(j, False)
        return carry

    lax.fori_loop(0, i, body, 0)
    chunk(i, True)
    _finish_softmax(o_ref, l_sc, acc_sc)


def _fox_attention(qT, ka, vT):
    B, _, _, L = qT.shape
    nk = L // TK
    return pl.pallas_call(
        _fox_kernel,
        grid=(B, PAIRS, L // TQ),
        in_specs=[pl.BlockSpec((1, 2, LANES, TQ), lambda b, p, i: (b, p, 0, i)),
                  pl.BlockSpec((1, 2, L, LANES), lambda b, p, i: (b, p, 0, 0)),
                  pl.BlockSpec((1, 1, nk, LANES, TK), lambda b, p, i: (b, p, 0, 0, 0))],
        out_specs=pl.BlockSpec((1, TQ, LANES), lambda b, p, i: (b, i, p)),
        out_shape=jax.ShapeDtypeStruct((B, L, PAIRS * LANES), BF16),
        scratch_shapes=[pltpu.VMEM((2, 1, TQ), F32), pltpu.VMEM((2, 1, TQ), F32),
                        pltpu.VMEM((2, LANES, TQ), F32)],
        compiler_params=_params(("parallel", "parallel", "arbitrary")),
    )(qT, ka, vT)


def _count_rows(mask):
    ones = jnp.where(mask, 1, 0).astype(I32)
    return jnp.sum(ones.reshape(TK // 8, 8, TQ), axis=0)


def _dsa_kernel(qT_ref, k_ref, vT_ref, iqT_ref, iwT_ref, s_ref, o_ref,
                keys_sc, thr_sc, m_sc, l_sc, acc_sc):
    i = pl.program_id(1)
    p = pl.program_id(2)
    n_chunks = i + 1

    @pl.when(p == 0)
    def _select():
        wts = iwT_ref[0]

        def score_chunk(j, masked):
            ks = pl.multiple_of(j * TK, TK)
            ik = s_ref[0, pl.ds(ks, TK), :][:, _SM_IK:_SM_IK + HEAD_DIM].astype(BF16)
            acc = jnp.zeros((TK, TQ), F32)
            for h in range(HEADS):
                iq = iqT_ref[0, h // 2, (h % 2) * HEAD_DIM:(h % 2 + 1) * HEAD_DIM, :]
                rel = jnp.maximum(jnp.dot(ik, iq, preferred_element_type=F32), 0.0)
                acc = acc + wts[h:h + 1, :] * rel
            if masked:
                acc = jnp.where(_causal(acc.shape), acc, NEG_INF)
            acc = jnp.where(acc == 0.0, 0.0, acc)
            bits = pltpu.bitcast(acc, I32)
            keys_sc[pl.ds(ks, TK), :] = bits ^ (lax.shift_right_arithmetic(bits, 31) & 0x7FFFFFFF)

        def score_body(j, carry):
            score_chunk(j, False)
            return carry

        lax.fori_loop(0, i, score_body, 0)
        score_chunk(i, True)

        def count(pred):
            def body(j, cnt):
                ks = pl.multiple_of(j * TK, TK)
                return cnt + _count_rows(pred(keys_sc[pl.ds(ks, TK), :]))
            cnt8 = lax.fori_loop(0, n_chunks, body, jnp.zeros((8, TQ), I32))
            return jnp.sum(cnt8, axis=0, keepdims=True)

        zero = jnp.zeros((1, TQ), I32)
        t0 = jnp.where(count(lambda kk: kk >= zero) >= TOPK, 0, INT_MIN).astype(I32)

        def bit_body(bi, t):
            cand = t | jnp.left_shift(jnp.ones((1, TQ), I32), 30 - bi)
            return jnp.where(count(lambda kk: kk >= cand) >= TOPK, cand, t)

        thr = lax.fori_loop(0, 31, bit_body, t0)
        thr_sc[...] = thr

        n_ge = count(lambda kk: kk >= thr)
        excess = jnp.where((n_ge > TOPK) & (thr > NEG_INF_KEY), 1, 0)

        @pl.when(jnp.max(excess) > 0)
        def _break_ties():
            need = TOPK - count(lambda kk: kk > thr)
            r = lax.broadcasted_iota(I32, (TK, TK), 0)
            c = lax.broadcasted_iota(I32, (TK, TK), 1)
            strict_lower = jnp.where(c < r, 1.0, 0.0).astype(BF16)

            def body(j, seen):
                ks = pl.multiple_of(j * TK, TK)
                kk = keys_sc[pl.ds(ks, TK), :]
                tie = kk == thr
                tie_f = jnp.where(tie, 1.0, 0.0)
                before = jnp.dot(strict_lower, tie_f.astype(BF16), preferred_element_type=F32) + seen
                drop = tie & (before >= need.astype(F32))
                keys_sc[pl.ds(ks, TK), :] = jnp.where(drop, INT_MIN, kk)
                return seen + jnp.sum(tie_f, axis=0, keepdims=True)

            lax.fori_loop(0, n_chunks, body, jnp.zeros((1, TQ), F32))

    _init_softmax(m_sc, l_sc, acc_sc)
    thr = thr_sc[...]

    def chunk(j, masked):
        ks = pl.multiple_of(j * TK, TK)
        sel = keys_sc[pl.ds(ks, TK), :] >= thr
        bias = jnp.where(sel, 0.0, NEG_INF)
        if masked:
            bias = jnp.where(_causal(bias.shape), bias, NEG_INF)
        vT = vT_ref[0, p, j]
        for e in range(2):
            sT = jnp.dot(k_ref[0, 2 * p + e, pl.ds(ks, TK), :], qT_ref[0, e],
                         preferred_element_type=F32) + bias
            _softmax_step(e, sT, vT, m_sc, l_sc, acc_sc)

    def body(j, carry):
        chunk(j, False)
        return carry

    lax.fori_loop(0, i, body, 0)
    chunk(i, True)
    _finish_softmax(o_ref, l_sc, acc_sc)


def _dsa_attention(qT, ka, vT, iqT, iwT, small):
    B, _, _, L = qT.shape
    nk = L // TK
    return pl.pallas_call(
        _dsa_kernel,
        grid=(B, L // TQ, PAIRS),
        in_specs=[pl.BlockSpec((1, 2, LANES, TQ), lambda b, i, p: (b, p, 0, i)),
                  pl.BlockSpec((1, HEADS, L, LANES), lambda b, i, p: (b, 0, 0, 0)),
                  pl.BlockSpec((1, PAIRS, nk, LANES, TK), lambda b, i, p: (b, 0, 0, 0, 0)),
                  pl.BlockSpec((1, PAIRS, LANES, TQ), lambda b, i, p: (b, 0, 0, i)),
                  pl.BlockSpec((1, HEADS, TQ), lambda b, i, p: (b, 0, i)),
                  pl.BlockSpec((1, L, LANES), lambda b, i, p: (b, 0, 0))],
        out_specs=pl.BlockSpec((1, TQ, LANES), lambda b, i, p: (b, i, p)),
        out_shape=jax.ShapeDtypeStruct((B, L, PAIRS * LANES), BF16),
        scratch_shapes=[pltpu.VMEM((L, TQ), I32), pltpu.VMEM((1, TQ), I32),
                        pltpu.VMEM((2, 1, TQ), F32), pltpu.VMEM((2, 1, TQ), F32),
                        pltpu.VMEM((2, LANES, TQ), F32)],
        compiler_params=_params(("parallel", "arbitrary", "arbitrary")),
    )(qT, ka, vT, iqT, iwT, small)


def _layer_norm(z, g, b):
    mu = jnp.mean(z, axis=-1, keepdims=True)
    zc = z - mu
    var = jnp.mean(zc * zc, axis=-1, keepdims=True)
    return zc * lax.rsqrt(var + LN_EPS) * g + b


def _merge_kernel(x_ref, oa_ref, ob_ref, ga_ref, gb_ref, wa_ref, wb_ref, wo_ref, g_ref, b_ref,
                  out_ref, *, alpha):
    a = jnp.dot(oa_ref[0], wa_ref[...], preferred_element_type=F32)
    b = jnp.dot(ob_ref[0], wb_ref[...], preferred_element_type=F32)
    n_gate_blocks = D_MODEL // LANES
    gate_a = jnp.concatenate([ga_ref[0, c] for c in range(n_gate_blocks)], axis=1).astype(F32)
    gate_b = jnp.concatenate([gb_ref[0, c] for c in range(n_gate_blocks)], axis=1).astype(F32)
    merged = jax.nn.sigmoid(gate_a) * a + jax.nn.sigmoid(gate_b) * b
    y = jnp.dot(merged.astype(BF16), wo_ref[...], preferred_element_type=F32)
    out_ref[0] = _layer_norm(alpha * x_ref[0] + y, g_ref[...], b_ref[...])


def _merge(x, oa, ob, p4, wa, wb, wo, g, b, alpha):
    B, L, D = x.shape
    tm = ROW_TILE
    nb = D // LANES
    row = lambda b_, i: (b_, i, 0)
    return pl.pallas_call(
        functools.partial(_merge_kernel, alpha=alpha),
        grid=(B, L // tm),
        in_specs=[pl.BlockSpec((1, tm, D), row),
                  pl.BlockSpec((1, tm, PAIRS * LANES), row),
                  pl.BlockSpec((1, tm, PAIRS * LANES), row),
                  pl.BlockSpec((1, nb, tm, LANES), lambda b_, i: (b_, _CB_GA // nb, i, 0)),
                  pl.BlockSpec((1, nb, tm, LANES), lambda b_, i: (b_, _CB_GB // nb, i, 0)),
                  _const_spec(wa.shape), _const_spec(wb.shape), _const_spec(wo.shape),
                  _const_spec(g.shape), _const_spec(b.shape)],
        out_specs=pl.BlockSpec((1, tm, D), row),
        out_shape=jax.ShapeDtypeStruct((B, L, D), F32),
        compiler_params=_params(("parallel", "parallel")),
    )(x, oa, ob, p4, p4, wa, wb, wo, g, b)


FF_CHUNK = 256


def _ffn_kernel(x_ref, wi_ref, wo_ref, g_ref, b_ref, out_ref, h_sc, *, alpha):
    x = x_ref[0]
    xb = x.astype(BF16)
    for c in range(D_FF // FF_CHUNK):
        lo = c * FF_CHUNK
        gate = jnp.dot(xb, wi_ref[:, lo:lo + FF_CHUNK], preferred_element_type=F32)
        up = jnp.dot(xb, wi_ref[:, D_FF + lo:D_FF + lo + FF_CHUNK], preferred_element_type=F32)
        h_sc[:, lo:lo + FF_CHUNK] = (jax.nn.silu(gate) * up).astype(BF16)
    y = jnp.dot(h_sc[...], wo_ref[...], preferred_element_type=F32)
    out_ref[0] = _layer_norm(alpha * x + y, g_ref[...], b_ref[...])


def _ffn(x, wi, wo, g, b, alpha):
    B, L, D = x.shape
    tm = ROW_TILE
    row = lambda b_, i: (b_, i, 0)
    return pl.pallas_call(
        functools.partial(_ffn_kernel, alpha=alpha),
        grid=(B, L // tm),
        in_specs=[pl.BlockSpec((1, tm, D), row),
                  _const_spec(wi.shape), _const_spec(wo.shape),
                  _const_spec(g.shape), _const_spec(b.shape)],
        out_specs=pl.BlockSpec((1, tm, D), row),
        out_shape=jax.ShapeDtypeStruct((B, L, D), F32),
        scratch_shapes=[pltpu.VMEM((tm, D_FF), BF16)],
        compiler_params=_params(("parallel", "parallel")),
    )(x, wi, wo, g, b)


def _split_w_in(w):
    fox = w[:, 0:_OFF_FLOGIT]
    dsa = w[:, _OFF_DQ:_OFF_DQ + 3 * 512]
    iq = w[:, _OFF_DQ + 3 * 512:_OFF_IK]
    gates = w[:, _OFF_GA:_N_IN]
    w_main = jnp.concatenate([fox, dsa, gates, iq], axis=1).astype(BF16)
    pad = jnp.zeros((w.shape[0], LANES - HEAD_DIM - 2 * HEADS), w.dtype)
    w_small = jnp.concatenate([w[:, _OFF_IK:_OFF_IW], w[:, _OFF_IW:_OFF_GA],
                               w[:, _OFF_FLOGIT:_OFF_DQ], pad], axis=1).astype(BF16)
    return w_main, w_small


def kernel(x, w_in, b_forget, w_branch_a, w_branch_b, w_out, ln1_g, ln1_b, w_ffn_in, w_ffn_out, ln2_g, ln2_b):
    depth = w_in.shape[0]
    alpha = (2.0 * depth) ** 0.25
    for l in range(depth):
        w_main, w_small = _split_w_in(w_in[l])
        bias_row = jnp.zeros((1, LANES), F32).at[0, _SM_FL:_SM_FL + HEADS].set(b_forget[l])
        p4, small = _project(x, w_main, w_small)
        fqT, fka, fvT = _fox_prep(p4, small, bias_row)
        dqT, dka, dvT, iqT, iwT = _dsa_prep(p4, small)
        o_a = _fox_attention(fqT, fka, fvT)
        o_b = _dsa_attention(dqT, dka, dvT, iqT, iwT, small)
        x = _merge(x, o_a, o_b, p4,
                   w_branch_a[l].astype(BF16), w_branch_b[l].astype(BF16), w_out[l].astype(BF16),
                   ln1_g[l][None, :], ln1_b[l][None, :], alpha)
        x = _ffn(x, w_ffn_in[l].astype(BF16), w_ffn_out[l].astype(BF16),
                 ln2_g[l][None, :], ln2_b[l][None, :], alpha)
    return x
```

```python
import functools

import numpy as np
import jax
import jax.numpy as jnp
from jax import lax
from jax.experimental import pallas as pl
from jax.experimental.pallas import tpu as pltpu

F32 = jnp.float32
BF16 = jnp.bfloat16
I32 = jnp.int32

D_MODEL = 1024
HEADS = 8
HEAD_DIM = 64
PAIRS = HEADS // 2
D_FF = 2816
TOPK = 256
LANES = 128
NEG_INF = -1e30
LN_EPS = 1e-5
ATTN_SCALE = HEAD_DIM ** -0.5
IDX_SCALE = (HEADS ** -0.5) * (HEAD_DIM ** -0.5)

_OFF_FLOGIT = 1536
_OFF_DQ = 1544
_OFF_IK = 3592
_OFF_IW = 3656
_OFF_GA = 3664
_N_IN = 5712

_CB_FQ, _CB_FK, _CB_FV, _CB_DQ, _CB_DK, _CB_DV, _CB_GA, _CB_GB, _CB_IQ = 0, 4, 8, 12, 16, 20, 24, 32, 40
N_MAIN_BLOCKS = 44
N_MAIN = N_MAIN_BLOCKS * LANES
_SM_IK, _SM_IW, _SM_FL = 0, 64, 72

ROW_TILE = 512
TQ = 512
TK = 512
CUMSUM_BLOCK = 256

INT_MIN = -(2 ** 31)


def _monotone_key_of(value):
    b = int(np.float32(value).view(np.int32))
    return b ^ ((b >> 31) & 0x7FFFFFFF)


NEG_INF_KEY = _monotone_key_of(NEG_INF)
VMEM_LIMIT = 56 * 1024 * 1024


def _params(sem):
    return pltpu.CompilerParams(dimension_semantics=sem, vmem_limit_bytes=VMEM_LIMIT)


def _const_spec(shape):
    nd = len(shape)
    return pl.BlockSpec(shape, lambda *_: (0,) * nd, pipeline_mode=pl.Buffered(1))


def _proj_kernel(x_ref, wm_ref, ws_ref, om_ref, os_ref):
    xb = x_ref[0].astype(BF16)
    os_ref[0] = jnp.dot(xb, ws_ref[...], preferred_element_type=F32)
    group = 4
    for c in range(N_MAIN_BLOCKS // group):
        w = wm_ref[:, c * group * LANES:(c + 1) * group * LANES]
        r = jnp.dot(xb, w, preferred_element_type=F32).astype(BF16)
        for g in range(group):
            om_ref[0, c * group + g] = r[:, g * LANES:(g + 1) * LANES]


def _project(x, w_main, w_small):
    B, L, D = x.shape
    tm = ROW_TILE
    return pl.pallas_call(
        _proj_kernel,
        grid=(B, L // tm),
        in_specs=[pl.BlockSpec((1, tm, D), lambda b, i: (b, i, 0)),
                  _const_spec((D, N_MAIN)),
                  _const_spec((D, LANES))],
        out_specs=[pl.BlockSpec((1, N_MAIN_BLOCKS, tm, LANES), lambda b, i: (b, 0, i, 0)),
                   pl.BlockSpec((1, tm, LANES), lambda b, i: (b, i, 0))],
        out_shape=[jax.ShapeDtypeStruct((B, N_MAIN_BLOCKS, L, LANES), BF16),
                   jax.ShapeDtypeStruct((B, L, LANES), F32)],
        compiler_params=_params(("parallel", "parallel")),
    )(x, w_main, w_small)


def _split3(v):
    hi = v.astype(BF16).astype(F32)
    r = v - hi
    mid = r.astype(BF16).astype(F32)
    lo = (r - mid).astype(BF16).astype(F32)
    return hi, mid, lo


def _cumsum_rows(x):
    n = CUMSUM_BLOCK
    L = x.shape[0]
    r = lax.broadcasted_iota(I32, (n, n), 0)
    c = lax.broadcasted_iota(I32, (n, n), 1)
    tri = jnp.where(c <= r, 1.0, 0.0).astype(BF16)
    carry = jnp.zeros((1, x.shape[1]), F32)
    outs = []
    for blk in range(L // n):
        hi, mid, lo = _split3(x[blk * n:(blk + 1) * n])
        y = (jnp.dot(tri, hi.astype(BF16), preferred_element_type=F32)
             + jnp.dot(tri, mid.astype(BF16), preferred_element_type=F32)
             + jnp.dot(tri, lo.astype(BF16), preferred_element_type=F32)) + carry
        carry = y[n - 1:n, :]
        outs.append(y)
    return jnp.concatenate(outs, axis=0)


def _lane_terms(lane, base, terms):
    out = jnp.zeros(lane.shape, F32)
    for k, t in enumerate(terms):
        out = jnp.where(lane == base + k, t, out)
    return out


def _store_transposed_chunks(ref, lead, xT):
    L = xT.shape[1]
    for j in range(L // TK):
        ref[lead + (j,)] = xT[:, j * TK:(j + 1) * TK]


def _fox_prep_kernel(q_ref, k_ref, v_ref, s_ref, bias_ref, qT_ref, ka_ref, vT_ref):
    p = pl.program_id(1)
    L = q_ref.shape[2]
    z = s_ref[0] + bias_ref[...]
    c = _cumsum_rows(jax.nn.log_sigmoid(z))
    lane = lax.broadcasted_iota(I32, (L, LANES), 1)
    q = q_ref[0, 0].astype(F32) * ATTN_SCALE
    k = k_ref[0, 0].astype(F32)
    for e in range(2):
        c_h = jnp.sum(jnp.where(lane == _SM_FL + 2 * p + e, c, 0.0), axis=1, keepdims=True)
        hi, mid, lo = _split3(c_h)
        data = (lane >= e * HEAD_DIM) & (lane < (e + 1) * HEAD_DIM)
        base = (1 - e) * HEAD_DIM
        q_aug = jnp.where(data, q, _lane_terms(lane, base, [hi, mid, lo, 1.0, 1.0, 1.0]))
        k_aug = jnp.where(data, k, _lane_terms(lane, base, [1.0, 1.0, 1.0, -hi, -mid, -lo]))
        qT_ref[0, e] = q_aug.T.astype(BF16)
        ka_ref[0, e] = k_aug.astype(BF16)
    _store_transposed_chunks(vT_ref, (0, 0), v_ref[0, 0].astype(F32).T.astype(BF16))


def _fox_prep(p4, small, bias_row):
    B, _, L, _ = p4.shape
    nk = L // TK

    def pair_spec(cb):
        return pl.BlockSpec((1, 1, L, LANES), lambda b, p: (b, cb + p, 0, 0))

    return pl.pallas_call(
        _fox_prep_kernel,
        grid=(B, PAIRS),
        in_specs=[pair_spec(_CB_FQ), pair_spec(_CB_FK), pair_spec(_CB_FV),
                  pl.BlockSpec((1, L, LANES), lambda b, p: (b, 0, 0)),
                  pl.BlockSpec((1, LANES), lambda b, p: (0, 0))],
        out_specs=[pl.BlockSpec((1, 2, LANES, L), lambda b, p: (b, p, 0, 0)),
                   pl.BlockSpec((1, 2, L, LANES), lambda b, p: (b, p, 0, 0)),
                   pl.BlockSpec((1, 1, nk, LANES, TK), lambda b, p: (b, p, 0, 0, 0))],
        out_shape=[jax.ShapeDtypeStruct((B, HEADS, LANES, L), BF16),
                   jax.ShapeDtypeStruct((B, HEADS, L, LANES), BF16),
                   jax.ShapeDtypeStruct((B, PAIRS, nk, LANES, TK), BF16)],
        compiler_params=_params(("parallel", "arbitrary")),
    )(p4, p4, p4, small, bias_row)


def _dsa_prep_kernel(q_ref, k_ref, v_ref, iq_ref, s_ref, qT_ref, ka_ref, vT_ref, iqT_ref, iwT_ref):
    p = pl.program_id(1)
    L = q_ref.shape[2]
    lane = lax.broadcasted_iota(I32, (L, LANES), 1)
    pos = lax.broadcasted_iota(I32, (L, 1), 0)
    pos_hi = lax.shift_right_logical(pos, 6).astype(F32)
    pos_lo = (pos & 63).astype(F32)
    q = q_ref[0, 0].astype(F32) * ATTN_SCALE
    k = k_ref[0, 0].astype(F32)
    for e in range(2):
        h = jnp.full((L, 1), 2 * p + e, I32)
        slope = lax.shift_left(jnp.ones((L, 1), I32), 7 - h).astype(F32) * (1.0 / 256.0)
        data = (lane >= e * HEAD_DIM) & (lane < (e + 1) * HEAD_DIM)
        base = (1 - e) * HEAD_DIM
        q_aug = jnp.where(data, q, _lane_terms(
            lane, base, [-(slope * 64.0) * pos_hi, -slope * pos_lo, slope * 64.0, slope]))
        k_aug = jnp.where(data, k, _lane_terms(lane, base, [1.0, 1.0, pos_hi, pos_lo]))
        qT_ref[0, e] = q_aug.T.astype(BF16)
        ka_ref[0, e] = k_aug.astype(BF16)
    _store_transposed_chunks(vT_ref, (0, 0), v_ref[0, 0].astype(F32).T.astype(BF16))
    iqT_ref[0, 0] = iq_ref[0, 0].astype(F32).T.astype(BF16)
    sT = s_ref[0].T
    iwT_ref[0] = sT[_SM_IW:_SM_IW + HEADS, :] * IDX_SCALE


def _dsa_prep(p4, small):
    B, _, L, _ = p4.shape
    nk = L // TK

    def pair_spec(cb):
        return pl.BlockSpec((1, 1, L, LANES), lambda b, p: (b, cb + p, 0, 0))

    return pl.pallas_call(
        _dsa_prep_kernel,
        grid=(B, PAIRS),
        in_specs=[pair_spec(_CB_DQ), pair_spec(_CB_DK), pair_spec(_CB_DV), pair_spec(_CB_IQ),
                  pl.BlockSpec((1, L, LANES), lambda b, p: (b, 0, 0))],
        out_specs=[pl.BlockSpec((1, 2, LANES, L), lambda b, p: (b, p, 0, 0)),
                   pl.BlockSpec((1, 2, L, LANES), lambda b, p: (b, p, 0, 0)),
                   pl.BlockSpec((1, 1, nk, LANES, TK), lambda b, p: (b, p, 0, 0, 0)),
                   pl.BlockSpec((1, 1, LANES, L), lambda b, p: (b, p, 0, 0)),
                   pl.BlockSpec((1, HEADS, L), lambda b, p: (b, 0, 0))],
        out_shape=[jax.ShapeDtypeStruct((B, HEADS, LANES, L), BF16),
                   jax.ShapeDtypeStruct((B, HEADS, L, LANES), BF16),
                   jax.ShapeDtypeStruct((B, PAIRS, nk, LANES, TK), BF16),
                   jax.ShapeDtypeStruct((B, PAIRS, LANES, L), BF16),
                   jax.ShapeDtypeStruct((B, HEADS, L), F32)],
        compiler_params=_params(("parallel", "arbitrary")),
    )(p4, p4, p4, p4, small)


def _fold8(x, op):
    return op(x.reshape(x.shape[0] // 8, 8, x.shape[1]), axis=0)


def _score_sweep(j, bias, k_ref, qT_ref, s_sc, mx):
    ks = pl.multiple_of(j * TK, TK)
    out = []
    for e in range(2):
        s = jnp.dot(k_ref[0, e, pl.ds(ks, TK), :], qT_ref[0, e], preferred_element_type=F32)
        if bias is not None:
            s = s + bias
        s_sc[e, pl.ds(ks, TK), :] = s
        out.append(jnp.maximum(mx[e], _fold8(s, jnp.max)))
    return tuple(out)


def _value_sweep(n_chunks, mx, vT_ref, vT_lead, s_sc, acc_sc, o_ref):
    m = [jnp.max(mx[e], axis=0, keepdims=True) for e in range(2)]
    acc_sc[...] = jnp.zeros(acc_sc.shape, F32)

    def body(j, l8):
        ks = pl.multiple_of(j * TK, TK)
        vT = vT_ref[vT_lead + (j,)]
        out = []
        for e in range(2):
            p = jnp.exp(s_sc[e, pl.ds(ks, TK), :] - m[e])
            acc_sc[e] += jnp.dot(vT[e * HEAD_DIM:(e + 1) * HEAD_DIM, :], p.astype(BF16),
                                 preferred_element_type=F32)
            out.append(l8[e] + _fold8(p, jnp.sum))
        return tuple(out)

    zeros8 = jnp.zeros((8, TQ), F32)
    l8 = lax.fori_loop(0, n_chunks, body, (zeros8, zeros8))
    rows = [acc_sc[e] / jnp.sum(l8[e], axis=0, keepdims=True) for e in range(2)]
    o_ref[0] = jnp.concatenate(rows, axis=0).T.astype(o_ref.dtype)


def _causal(shape):
    return lax.broadcasted_iota(I32, shape, 0) <= lax.broadcasted_iota(I32, shape, 1)


def _max_init():
    init = jnp.full((8, TQ), NEG_INF, F32)
    return (init, init)


def _fox_kernel(qT_ref, k_ref, vT_ref, o_ref, s_sc, acc_sc):
    i = pl.program_id(2)
    mx = lax.fori_loop(0, i, lambda j, mx: _score_sweep(j, None, k_ref, qT_ref, s_sc, mx), _max_init())
    causal_bias = jnp.where(_causal((TK, TQ)), 0.0, NEG_INF)
    mx = _score_sweep(i, causal_bias, k_ref, qT_ref, s_sc, mx)
    _value_sweep(i + 1, mx, vT_ref, (0, 0), s_sc, acc_sc, o_ref)


def _fox_attention(qT, ka, vT):
    B, _, _, L = qT.shape
    nk = L // TK
    return pl.pallas_call(
        _fox_kernel,
        grid=(B, PAIRS, L // TQ),
        in_specs=[pl.BlockSpec((1, 2, LANES, TQ), lambda b, p, i: (b, p, 0, i)),
                  pl.BlockSpec((1, 2, L, LANES), lambda b, p, i: (b, p, 0, 0)),
                  pl.BlockSpec((1, 1, nk, LANES, TK), lambda b, p, i: (b, p, 0, 0, 0))],
        out_specs=pl.BlockSpec((1, TQ, LANES), lambda b, p, i: (b, i, p)),
        out_shape=jax.ShapeDtypeStruct((B, L, PAIRS * LANES), BF16),
        scratch_shapes=[pltpu.VMEM((2, L, TQ), F32), pltpu.VMEM((2, HEAD_DIM, TQ), F32)],
        compiler_params=_params(("parallel", "parallel", "arbitrary")),
    )(qT, ka, vT)


def _count_rows(mask):
    return _fold8(jnp.where(mask, 1, 0).astype(I32), jnp.sum)


def _dsa_kernel(qT_ref, k_ref, vT_ref, iqT_ref, iwT_ref, s_ref, o_ref, keys_sc, s_sc, acc_sc):
    i = pl.program_id(1)
    p = pl.program_id(2)
    n_chunks = i + 1

    @pl.when(p == 0)
    def _select():
        wts = iwT_ref[0]

        def score_chunk(j, masked):
            ks = pl.multiple_of(j * TK, TK)
            ik = s_ref[0, pl.ds(ks, TK), :][:, _SM_IK:_SM_IK + HEAD_DIM].astype(BF16)
            acc = jnp.zeros((TK, TQ), F32)
            for h in range(HEADS):
                iq = iqT_ref[0, h // 2, (h % 2) * HEAD_DIM:(h % 2 + 1) * HEAD_DIM, :]
                rel = jnp.maximum(jnp.dot(ik, iq, preferred_element_type=F32), 0.0)
                acc = acc + wts[h:h + 1, :] * rel
            if masked:
                acc = jnp.where(_causal(acc.shape), acc, NEG_INF)
            acc = jnp.where(acc == 0.0, 0.0, acc)
            bits = pltpu.bitcast(acc, I32)
            keys_sc[pl.ds(ks, TK), :] = bits ^ (lax.shift_right_arithmetic(bits, 31) & 0x7FFFFFFF)

        def score_body(j, carry):
            score_chunk(j, False)
            return carry

        lax.fori_loop(0, i, score_body, 0)
        score_chunk(i, True)

        def count(pred):
            def body(j, cnt):
                ks = pl.multiple_of(j * TK, TK)
                return cnt + _count_rows(pred(keys_sc[pl.ds(ks, TK), :]))
            cnt8 = lax.fori_loop(0, n_chunks, body, jnp.zeros((8, TQ), I32))
            return jnp.sum(cnt8, axis=0, keepdims=True)

        zero = jnp.zeros((1, TQ), I32)
        t0 = jnp.where(count(lambda kk: kk >= zero) >= TOPK, 0, INT_MIN).astype(I32)

        def bit_body(bi, t):
            cand = t | jnp.left_shift(jnp.ones((1, TQ), I32), 30 - bi)
            return jnp.where(count(lambda kk: kk >= cand) >= TOPK, cand, t)

        thr = lax.fori_loop(0, 31, bit_body, t0)

        n_ge = count(lambda kk: kk >= thr)
        excess = jnp.where((n_ge > TOPK) & (thr > NEG_INF_KEY), 1, 0)

        @pl.when(jnp.max(excess) > 0)
        def _break_ties():
            need = TOPK - count(lambda kk: kk > thr)
            r = lax.broadcasted_iota(I32, (TK, TK), 0)
            c = lax.broadcasted_iota(I32, (TK, TK), 1)
            strict_lower = jnp.where(c < r, 1.0, 0.0).astype(BF16)

            def body(j, seen):
                ks = pl.multiple_of(j * TK, TK)
                kk = keys_sc[pl.ds(ks, TK), :]
                tie = kk == thr
                tie_f = jnp.where(tie, 1.0, 0.0)
                before = jnp.dot(strict_lower, tie_f.astype(BF16), preferred_element_type=F32) + seen
                drop = tie & (before >= need.astype(F32))
                keys_sc[pl.ds(ks, TK), :] = jnp.where(drop, INT_MIN, kk)
                return seen + jnp.sum(tie_f, axis=0, keepdims=True)

            lax.fori_loop(0, n_chunks, body, jnp.zeros((1, TQ), F32))

        def bias_chunk(j, masked):
            ks = pl.multiple_of(j * TK, TK)
            bias = jnp.where(keys_sc[pl.ds(ks, TK), :] >= thr, 0.0, NEG_INF)
            if masked:
                bias = jnp.where(_causal(bias.shape), bias, NEG_INF)
            keys_sc[pl.ds(ks, TK), :] = pltpu.bitcast(bias, I32)

        def bias_body(j, carry):
            bias_chunk(j, False)
            return carry

        lax.fori_loop(0, i, bias_body, 0)
        bias_chunk(i, True)

    def sweep(j, mx):
        ks = pl.multiple_of(j * TK, TK)
        bias = pltpu.bitcast(keys_sc[pl.ds(ks, TK), :], F32)
        return _score_sweep(j, bias, k_ref, qT_ref, s_sc, mx)

    mx = lax.fori_loop(0, n_chunks, sweep, _max_init())
    _value_sweep(n_chunks, mx, vT_ref, (0, 0), s_sc, acc_sc, o_ref)


def _dsa_attention(qT, ka, vT, iqT, iwT, small):
    B, _, _, L = qT.shape
    nk = L // TK
    return pl.pallas_call(
        _dsa_kernel,
        grid=(B, L // TQ, PAIRS),
        in_specs=[pl.BlockSpec((1, 2, LANES, TQ), lambda b, i, p: (b, p, 0, i)),
                  pl.BlockSpec((1, 2, L, LANES), lambda b, i, p: (b, p, 0, 0)),
                  pl.BlockSpec((1, 1, nk, LANES, TK), lambda b, i, p: (b, p, 0, 0, 0)),
                  pl.BlockSpec((1, PAIRS, LANES, TQ), lambda b, i, p: (b, 0, 0, i)),
                  pl.BlockSpec((1, HEADS, TQ), lambda b, i, p: (b, 0, i)),
                  pl.BlockSpec((1, L, LANES), lambda b, i, p: (b, 0, 0))],
        out_specs=pl.BlockSpec((1, TQ, LANES), lambda b, i, p: (b, i, p)),
        out_shape=jax.ShapeDtypeStruct((B, L, PAIRS * LANES), BF16),
        scratch_shapes=[pltpu.VMEM((L, TQ), I32), pltpu.VMEM((2, L, TQ), F32),
                        pltpu.VMEM((2, HEAD_DIM, TQ), F32)],
        compiler_params=_params(("parallel", "arbitrary", "arbitrary")),
    )(qT, ka, vT, iqT, iwT, small)


def _layer_norm(z, g, b):
    mu = jnp.mean(z, axis=-1, keepdims=True)
    zc = z - mu
    var = jnp.mean(zc * zc, axis=-1, keepdims=True)
    return zc * lax.rsqrt(var + LN_EPS) * g + b


def _merge_kernel(x_ref, oa_ref, ob_ref, ga_ref, gb_ref, wa_ref, wb_ref, wo_ref, g_ref, b_ref,
                  out_ref, *, alpha):
    a = jnp.dot(oa_ref[0], wa_ref[...], preferred_element_type=F32)
    b = jnp.dot(ob_ref[0], wb_ref[...], preferred_element_type=F32)
    n_gate_blocks = D_MODEL // LANES
    gate_a = jnp.concatenate([ga_ref[0, c] for c in range(n_gate_blocks)], axis=1).astype(F32)
    gate_b = jnp.concatenate([gb_ref[0, c] for c in range(n_gate_blocks)], axis=1).astype(F32)
    merged = jax.nn.sigmoid(gate_a) * a + jax.nn.sigmoid(gate_b) * b
    y = jnp.dot(merged.astype(BF16), wo_ref[...], preferred_element_type=F32)
    out_ref[0] = _layer_norm(alpha * x_ref[0] + y, g_ref[...], b_ref[...])


def _merge(x, oa, ob, p4, wa, wb, wo, g, b, alpha):
    B, L, D = x.shape
    tm = ROW_TILE
    nb = D // LANES
    row = lambda b_, i: (b_, i, 0)
    return pl.pallas_call(
        functools.partial(_merge_kernel, alpha=alpha),
        grid=(B, L // tm),
        in_specs=[pl.BlockSpec((1, tm, D), row),
                  pl.BlockSpec((1, tm, PAIRS * LANES), row),
                  pl.BlockSpec((1, tm, PAIRS * LANES), row),
                  pl.BlockSpec((1, nb, tm, LANES), lambda b_, i: (b_, _CB_GA // nb, i, 0)),
                  pl.BlockSpec((1, nb, tm, LANES), lambda b_, i: (b_, _CB_GB // nb, i, 0)),
                  _const_spec(wa.shape), _const_spec(wb.shape), _const_spec(wo.shape),
                  _const_spec(g.shape), _const_spec(b.shape)],
        out_specs=pl.BlockSpec((1, tm, D), row),
        out_shape=jax.ShapeDtypeStruct((B, L, D), F32),
        compiler_params=_params(("parallel", "parallel")),
    )(x, oa, ob, p4, p4, wa, wb, wo, g, b)


FF_CHUNK = 256


def _ffn_kernel(x_ref, wi_ref, wo_ref, g_ref, b_ref, out_ref, h_sc, *, alpha):
    x = x_ref[0]
    xb = x.astype(BF16)
    for c in range(D_FF // FF_CHUNK):
        lo = c * FF_CHUNK
        gate = jnp.dot(xb, wi_ref[:, lo:lo + FF_CHUNK], preferred_element_type=F32)
        up = jnp.dot(xb, wi_ref[:, D_FF + lo:D_FF + lo + FF_CHUNK], preferred_element_type=F32)
        h_sc[:, lo:lo + FF_CHUNK] = (jax.nn.silu(gate) * up).astype(BF16)
    y = jnp.dot(h_sc[...], wo_ref[...], preferred_element_type=F32)
    out_ref[0] = _layer_norm(alpha * x + y, g_ref[...], b_ref[...])


def _ffn(x, wi, wo, g, b, alpha):
    B, L, D = x.shape
    tm = ROW_TILE
    row = lambda b_, i: (b_, i, 0)
    return pl.pallas_call(
        functools.partial(_ffn_kernel, alpha=alpha),
        grid=(B, L // tm),
        in_specs=[pl.BlockSpec((1, tm, D), row),
                  _const_spec(wi.shape), _const_spec(wo.shape),
                  _const_spec(g.shape), _const_spec(b.shape)],
        out_specs=pl.BlockSpec((1, tm, D), row),
        out_shape=jax.ShapeDtypeStruct((B, L, D), F32),
        scratch_shapes=[pltpu.VMEM((tm, D_FF), BF16)],
        compiler_params=_params(("parallel", "parallel")),
    )(x, wi, wo, g, b)


def _split_w_in(w):
    fox = w[:, 0:_OFF_FLOGIT]
    dsa = w[:, _OFF_DQ:_OFF_DQ + 3 * 512]
    iq = w[:, _OFF_DQ + 3 * 512:_OFF_IK]
    gates = w[:, _OFF_GA:_N_IN]
    w_main = jnp.concatenate([fox, dsa, gates, iq], axis=1).astype(BF16)
    pad = jnp.zeros((w.shape[0], LANES - HEAD_DIM - 2 * HEADS), w.dtype)
    w_small = jnp.concatenate([w[:, _OFF_IK:_OFF_IW], w[:, _OFF_IW:_OFF_GA],
                               w[:, _OFF_FLOGIT:_OFF_DQ], pad], axis=1).astype(BF16)
    return w_main, w_small


def kernel(x, w_in, b_forget, w_branch_a, w_branch_b, w_out, ln1_g, ln1_b, w_ffn_in, w_ffn_out, ln2_g, ln2_b):
    depth = w_in.shape[0]
    alpha = (2.0 * depth) ** 0.25
    for l in range(depth):
        w_main, w_small = _split_w_in(w_in[l])
        bias_row = jnp.zeros((1, LANES), F32).at[0, _SM_FL:_SM_FL + HEADS].set(b_forget[l])
        p4, small = _project(x, w_main, w_small)
        fqT, fka, fvT = _fox_prep(p4, small, bias_row)
        dqT, dka, dvT, iqT, iwT = _dsa_prep(p4, small)
        o_a = _fox_attention(fqT, fka, fvT)
        o_b = _dsa_attention(dqT, dka, dvT, iqT, iwT, small)
        x = _merge(x, o_a, o_b, p4,
                   w_branch_a[l].astype(BF16), w_branch_b[l].astype(BF16), w_out[l].astype(BF16),
                   ln1_g[l][None, :], ln1_b[l][None, :], alpha)
        x = _ffn(x, w_ffn_in[l].astype(BF16), w_ffn_out[l].astype(BF16),
                 ln2_g[l][None, :], ln2_b[l][None, :], alpha)
    return x
```

```python
import functools

import numpy as np
import jax
import jax.numpy as jnp
from jax import lax
from jax.experimental import pallas as pl
from jax.experimental.pallas import tpu as pltpu

F32 = jnp.float32
BF16 = jnp.bfloat16
I32 = jnp.int32

D_MODEL = 1024
HEADS = 8
HEAD_DIM = 64
PAIRS = HEADS // 2
D_FF = 2816
TOPK = 256
LANES = 128
NEG_INF = -1e30
LN_EPS = 1e-5
ATTN_SCALE = HEAD_DIM ** -0.5
IDX_SCALE = (HEADS ** -0.5) * (HEAD_DIM ** -0.5)

_OFF_FLOGIT = 1536
_OFF_DQ = 1544
_OFF_IK = 3592
_OFF_IW = 3656
_OFF_GA = 3664
_N_IN = 5712

_CB_FQ, _CB_FK, _CB_FV, _CB_DQ, _CB_DK, _CB_DV, _CB_GA, _CB_GB, _CB_IQ = 0, 4, 8, 12, 16, 20, 24, 32, 40
N_MAIN_BLOCKS = 44
N_MAIN = N_MAIN_BLOCKS * LANES
_SM_IK, _SM_IW, _SM_FL = 0, 64, 72

ROW_TILE = 512
TQ = 512
TK = 512
CUMSUM_BLOCK = 256
V_ROWS = HEAD_DIM + 16
LOG2E = 1.4426950408889634

INT_MIN = -(2 ** 31)


def _monotone_key_of(value):
    b = int(np.float32(value).view(np.int32))
    return b ^ ((b >> 31) & 0x7FFFFFFF)


NEG_INF_KEY = _monotone_key_of(NEG_INF)
VMEM_LIMIT = 56 * 1024 * 1024


def _params(sem):
    return pltpu.CompilerParams(dimension_semantics=sem, vmem_limit_bytes=VMEM_LIMIT)


def _const_spec(shape):
    nd = len(shape)
    return pl.BlockSpec(shape, lambda *_: (0,) * nd, pipeline_mode=pl.Buffered(1))


def _proj_kernel(x_ref, wm_ref, ws_ref, om_ref, os_ref):
    xb = x_ref[0].astype(BF16)
    os_ref[0] = jnp.dot(xb, ws_ref[...], preferred_element_type=F32)
    group = 4
    for c in range(N_MAIN_BLOCKS // group):
        w = wm_ref[:, c * group * LANES:(c + 1) * group * LANES]
        r = jnp.dot(xb, w, preferred_element_type=F32).astype(BF16)
        for g in range(group):
            om_ref[0, c * group + g] = r[:, g * LANES:(g + 1) * LANES]


def _project(x, w_main, w_small):
    B, L, D = x.shape
    tm = ROW_TILE
    return pl.pallas_call(
        _proj_kernel,
        grid=(B, L // tm),
        in_specs=[pl.BlockSpec((1, tm, D), lambda b, i: (b, i, 0)),
                  _const_spec((D, N_MAIN)),
                  _const_spec((D, LANES))],
        out_specs=[pl.BlockSpec((1, N_MAIN_BLOCKS, tm, LANES), lambda b, i: (b, 0, i, 0)),
                   pl.BlockSpec((1, tm, LANES), lambda b, i: (b, i, 0))],
        out_shape=[jax.ShapeDtypeStruct((B, N_MAIN_BLOCKS, L, LANES), BF16),
                   jax.ShapeDtypeStruct((B, L, LANES), F32)],
        compiler_params=_params(("parallel", "parallel")),
    )(x, w_main, w_small)


def _split3(v):
    hi = v.astype(BF16).astype(F32)
    r = v - hi
    mid = r.astype(BF16).astype(F32)
    lo = (r - mid).astype(BF16).astype(F32)
    return hi, mid, lo


def _cumsum_rows(x):
    n = CUMSUM_BLOCK
    L = x.shape[0]
    r = lax.broadcasted_iota(I32, (n, n), 0)
    c = lax.broadcasted_iota(I32, (n, n), 1)
    tri = jnp.where(c <= r, 1.0, 0.0).astype(BF16)
    carry = jnp.zeros((1, x.shape[1]), F32)
    outs = []
    for blk in range(L // n):
        hi, mid, lo = _split3(x[blk * n:(blk + 1) * n])
        y = (jnp.dot(tri, hi.astype(BF16), preferred_element_type=F32)
             + jnp.dot(tri, mid.astype(BF16), preferred_element_type=F32)
             + jnp.dot(tri, lo.astype(BF16), preferred_element_type=F32)) + carry
        carry = y[n - 1:n, :]
        outs.append(y)
    return jnp.concatenate(outs, axis=0)


def _lane_terms(lane, base, terms):
    out = jnp.zeros(lane.shape, F32)
    for k, t in enumerate(terms):
        out = jnp.where(lane == base + k, t, out)
    return out


def _store_value_chunks(ref, v_pair):
    L = v_pair.shape[0]
    vT = v_pair.astype(F32).T.astype(BF16)
    ones = jnp.ones((V_ROWS - HEAD_DIM, L), BF16)
    for e in range(2):
        rows = jnp.concatenate([vT[e * HEAD_DIM:(e + 1) * HEAD_DIM], ones], axis=0)
        for j in range(L // TK):
            ref[0, e, j] = rows[:, j * TK:(j + 1) * TK]


def _fox_prep_kernel(q_ref, k_ref, v_ref, s_ref, bias_ref, qT_ref, ka_ref, vT_ref):
    p = pl.program_id(1)
    L = q_ref.shape[2]
    z = s_ref[0] + bias_ref[...]
    c = _cumsum_rows(jax.nn.log_sigmoid(z))
    lane = lax.broadcasted_iota(I32, (L, LANES), 1)
    q = q_ref[0, 0].astype(F32) * ATTN_SCALE
    k = k_ref[0, 0].astype(F32)
    for e in range(2):
        c_h = jnp.sum(jnp.where(lane == _SM_FL + 2 * p + e, c, 0.0), axis=1, keepdims=True)
        hi, mid, lo = _split3(c_h)
        data = (lane >= e * HEAD_DIM) & (lane < (e + 1) * HEAD_DIM)
        base = (1 - e) * HEAD_DIM
        q_aug = jnp.where(data, q, _lane_terms(lane, base, [hi, mid, lo, 1.0, 1.0, 1.0]))
        k_aug = jnp.where(data, k, _lane_terms(lane, base, [1.0, 1.0, 1.0, -hi, -mid, -lo]))
        qT_ref[0, e] = q_aug.T.astype(BF16)
        ka_ref[0, e] = k_aug.astype(BF16)
    _store_value_chunks(vT_ref, v_ref[0, 0])


def _fox_prep(p4, small, bias_row):
    B, _, L, _ = p4.shape
    nk = L // TK

    def pair_spec(cb):
        return pl.BlockSpec((1, 1, L, LANES), lambda b, p: (b, cb + p, 0, 0))

    return pl.pallas_call(
        _fox_prep_kernel,
        grid=(B, PAIRS),
        in_specs=[pair_spec(_CB_FQ), pair_spec(_CB_FK), pair_spec(_CB_FV),
                  pl.BlockSpec((1, L, LANES), lambda b, p: (b, 0, 0)),
                  pl.BlockSpec((1, LANES), lambda b, p: (0, 0))],
        out_specs=[pl.BlockSpec((1, 2, LANES, L), lambda b, p: (b, p, 0, 0)),
                   pl.BlockSpec((1, 2, L, LANES), lambda b, p: (b, p, 0, 0)),
                   pl.BlockSpec((1, 2, nk, V_ROWS, TK), lambda b, p: (b, p, 0, 0, 0))],
        out_shape=[jax.ShapeDtypeStruct((B, HEADS, LANES, L), BF16),
                   jax.ShapeDtypeStruct((B, HEADS, L, LANES), BF16),
                   jax.ShapeDtypeStruct((B, HEADS, nk, V_ROWS, TK), BF16)],
        compiler_params=_params(("parallel", "arbitrary")),
    )(p4, p4, p4, small, bias_row)


def _dsa_prep_kernel(q_ref, k_ref, v_ref, iq_ref, s_ref, qT_ref, ka_ref, vT_ref, iqT_ref, iwT_ref):
    p = pl.program_id(1)
    L = q_ref.shape[2]
    lane = lax.broadcasted_iota(I32, (L, LANES), 1)
    pos = lax.broadcasted_iota(I32, (L, 1), 0)
    pos_hi = lax.shift_right_logical(pos, 6).astype(F32)
    pos_lo = (pos & 63).astype(F32)
    q = q_ref[0, 0].astype(F32) * ATTN_SCALE
    k = k_ref[0, 0].astype(F32)
    for e in range(2):
        h = jnp.full((L, 1), 2 * p + e, I32)
        slope = lax.shift_left(jnp.ones((L, 1), I32), 7 - h).astype(F32) * (1.0 / 256.0)
        data = (lane >= e * HEAD_DIM) & (lane < (e + 1) * HEAD_DIM)
        base = (1 - e) * HEAD_DIM
        q_aug = jnp.where(data, q, _lane_terms(
            lane, base, [-(slope * 64.0) * pos_hi, -slope * pos_lo, slope * 64.0, slope]))
        k_aug = jnp.where(data, k, _lane_terms(lane, base, [1.0, 1.0, pos_hi, pos_lo]))
        qT_ref[0, e] = q_aug.T.astype(BF16)
        ka_ref[0, e] = k_aug.astype(BF16)
    _store_value_chunks(vT_ref, v_ref[0, 0])
    iqT_ref[0, 0] = iq_ref[0, 0].astype(F32).T.astype(BF16)
    sT = s_ref[0].T
    iwT_ref[0] = sT[_SM_IW:_SM_IW + HEADS, :] * IDX_SCALE


def _dsa_prep(p4, small):
    B, _, L, _ = p4.shape
    nk = L // TK

    def pair_spec(cb):
        return pl.BlockSpec((1, 1, L, LANES), lambda b, p: (b, cb + p, 0, 0))

    return pl.pallas_call(
        _dsa_prep_kernel,
        grid=(B, PAIRS),
        in_specs=[pair_spec(_CB_DQ), pair_spec(_CB_DK), pair_spec(_CB_DV), pair_spec(_CB_IQ),
                  pl.BlockSpec((1, L, LANES), lambda b, p: (b, 0, 0))],
        out_specs=[pl.BlockSpec((1, 2, LANES, L), lambda b, p: (b, p, 0, 0)),
                   pl.BlockSpec((1, 2, L, LANES), lambda b, p: (b, p, 0, 0)),
                   pl.BlockSpec((1, 2, nk, V_ROWS, TK), lambda b, p: (b, p, 0, 0, 0)),
                   pl.BlockSpec((1, 1, LANES, L), lambda b, p: (b, p, 0, 0)),
                   pl.BlockSpec((1, HEADS, L), lambda b, p: (b, 0, 0))],
        out_shape=[jax.ShapeDtypeStruct((B, HEADS, LANES, L), BF16),
                   jax.ShapeDtypeStruct((B, HEADS, L, LANES), BF16),
                   jax.ShapeDtypeStruct((B, HEADS, nk, V_ROWS, TK), BF16),
                   jax.ShapeDtypeStruct((B, PAIRS, LANES, L), BF16),
                   jax.ShapeDtypeStruct((B, HEADS, L), F32)],
        compiler_params=_params(("parallel", "arbitrary")),
    )(p4, p4, p4, p4, small)


def _fold8(x, op):
    return op(x.reshape(x.shape[0] // 8, 8, x.shape[1]), axis=0)


def _for_chunk_spans(n_chunks, fn, init):
    n_pairs = lax.shift_right_logical(n_chunks, 1)
    carry = lax.fori_loop(0, n_pairs, lambda jj, c: fn(2 * jj, 2, c), init)
    return lax.fori_loop(0, n_chunks & 1, lambda _, c: fn(n_chunks - 1, 1, c), carry)


def _score_span(j0, n, bias_fn, k_ref, qT_ref, s_sc, mx):
    rows = pl.ds(pl.multiple_of(j0 * TK, TK), n * TK)
    bias = None if bias_fn is None else bias_fn(rows)
    out = []
    for e in range(2):
        s = jnp.dot(k_ref[0, e, rows, :], qT_ref[0, e], preferred_element_type=F32) * LOG2E
        if bias is not None:
            s = s + bias
        s_sc[e, rows, :] = s
        out.append(jnp.maximum(mx[e], _fold8(s, jnp.max)))
    return tuple(out)


def _value_sweep(n_chunks, mx, vT_ref, s_sc, acc_sc, o_ref):
    m = [jnp.max(mx[e], axis=0, keepdims=True) for e in range(2)]
    acc_sc[...] = jnp.zeros(acc_sc.shape, F32)

    def span(j0, n, carry):
        row0 = pl.multiple_of(j0 * TK, TK)
        for e in range(2):
            for sub in range(n):
                rows = pl.ds(pl.multiple_of(row0 + sub * TK, TK), TK)
                p = jnp.exp2(s_sc[e, rows, :] - m[e])
                acc_sc[e] += jnp.dot(vT_ref[0, e, j0 + sub], p.astype(BF16),
                                     preferred_element_type=F32)
        return carry

    _for_chunk_spans(n_chunks, span, 0)
    rows = [acc_sc[e][:HEAD_DIM] / acc_sc[e][HEAD_DIM:HEAD_DIM + 1] for e in range(2)]
    o_ref[0] = jnp.concatenate(rows, axis=0).T.astype(o_ref.dtype)


def _causal(shape):
    return lax.broadcasted_iota(I32, shape, 0) <= lax.broadcasted_iota(I32, shape, 1)


def _max_init():
    init = jnp.full((8, TQ), NEG_INF, F32)
    return (init, init)


def _fox_kernel(qT_ref, k_ref, vT_ref, o_ref, s_sc, acc_sc):
    i = pl.program_id(2)
    mx = _for_chunk_spans(i, lambda j0, n, mx: _score_span(j0, n, None, k_ref, qT_ref, s_sc, mx), _max_init())
    causal_bias = jnp.where(_causal((TK, TQ)), 0.0, NEG_INF)
    mx = _score_span(i, 1, lambda rows: causal_bias, k_ref, qT_ref, s_sc, mx)
    _value_sweep(i + 1, mx, vT_ref, s_sc, acc_sc, o_ref)


def _fox_attention(qT, ka, vT):
    B, _, _, L = qT.shape
    nk = L // TK
    return pl.pallas_call(
        _fox_kernel,
        grid=(B, PAIRS, L // TQ),
        in_specs=[pl.BlockSpec((1, 2, LANES, TQ), lambda b, p, i: (b, p, 0, i)),
                  pl.BlockSpec((1, 2, L, LANES), lambda b, p, i: (b, p, 0, 0)),
                  pl.BlockSpec((1, 2, nk, V_ROWS, TK), lambda b, p, i: (b, p, 0, 0, 0))],
        out_specs=pl.BlockSpec((1, TQ, LANES), lambda b, p, i: (b, i, p)),
        out_shape=jax.ShapeDtypeStruct((B, L, PAIRS * LANES), BF16),
        scratch_shapes=[pltpu.VMEM((2, L, TQ), F32), pltpu.VMEM((2, V_ROWS, TQ), F32)],
        compiler_params=_params(("parallel", "parallel", "arbitrary")),
    )(qT, ka, vT)


def _count_rows(mask):
    return _fold8(jnp.where(mask, 1, 0).astype(I32), jnp.sum)


def _dsa_kernel(qT_ref, k_ref, vT_ref, iqT_ref, iwT_ref, s_ref, o_ref, keys_sc, s_sc, acc_sc):
    i = pl.program_id(1)
    p = pl.program_id(2)
    n_chunks = i + 1

    @pl.when(p == 0)
    def _select():
        wts = iwT_ref[0]

        def score_chunk(j, masked):
            ks = pl.multiple_of(j * TK, TK)
            ik = s_ref[0, pl.ds(ks, TK), :][:, _SM_IK:_SM_IK + HEAD_DIM].astype(BF16)
            acc = jnp.zeros((TK, TQ), F32)
            for h in range(HEADS):
                iq = iqT_ref[0, h // 2, (h % 2) * HEAD_DIM:(h % 2 + 1) * HEAD_DIM, :]
                rel = jnp.maximum(jnp.dot(ik, iq, preferred_element_type=F32), 0.0)
                acc = acc + wts[h:h + 1, :] * rel
            if masked:
                acc = jnp.where(_causal(acc.shape), acc, NEG_INF)
            acc = jnp.where(acc == 0.0, 0.0, acc)
            bits = pltpu.bitcast(acc, I32)
            keys_sc[pl.ds(ks, TK), :] = bits ^ (lax.shift_right_arithmetic(bits, 31) & 0x7FFFFFFF)

        def score_body(j, carry):
            score_chunk(j, False)
            return carry

        lax.fori_loop(0, i, score_body, 0)
        score_chunk(i, True)

        def count(pred):
            def body(j, cnt):
                ks = pl.multiple_of(j * TK, TK)
                return cnt + _count_rows(pred(keys_sc[pl.ds(ks, TK), :]))
            cnt8 = lax.fori_loop(0, n_chunks, body, jnp.zeros((8, TQ), I32))
            return jnp.sum(cnt8, axis=0, keepdims=True)

        zero = jnp.zeros((1, TQ), I32)
        t0 = jnp.where(count(lambda kk: kk >= zero) >= TOPK, 0, INT_MIN).astype(I32)

        def bit_body(bi, t):
            cand = t | jnp.left_shift(jnp.ones((1, TQ), I32), 30 - bi)
            return jnp.where(count(lambda kk: kk >= cand) >= TOPK, cand, t)

        thr = lax.fori_loop(0, 31, bit_body, t0)

        n_ge = count(lambda kk: kk >= thr)
        excess = jnp.where((n_ge > TOPK) & (thr > NEG_INF_KEY), 1, 0)

        @pl.when(jnp.max(excess) > 0)
        def _break_ties():
            need = TOPK - count(lambda kk: kk > thr)
            r = lax.broadcasted_iota(I32, (TK, TK), 0)
            c = lax.broadcasted_iota(I32, (TK, TK), 1)
            strict_lower = jnp.where(c < r, 1.0, 0.0).astype(BF16)

            def body(j, seen):
                ks = pl.multiple_of(j * TK, TK)
                kk = keys_sc[pl.ds(ks, TK), :]
                tie = kk == thr
                tie_f = jnp.where(tie, 1.0, 0.0)
                before = jnp.dot(strict_lower, tie_f.astype(BF16), preferred_element_type=F32) + seen
                drop = tie & (before >= need.astype(F32))
                keys_sc[pl.ds(ks, TK), :] = jnp.where(drop, INT_MIN, kk)
                return seen + jnp.sum(tie_f, axis=0, keepdims=True)

            lax.fori_loop(0, n_chunks, body, jnp.zeros((1, TQ), F32))

        def bias_chunk(j, masked):
            ks = pl.multiple_of(j * TK, TK)
            bias = jnp.where(keys_sc[pl.ds(ks, TK), :] >= thr, 0.0, NEG_INF)
            if masked:
                bias = jnp.where(_causal(bias.shape), bias, NEG_INF)
            keys_sc[pl.ds(ks, TK), :] = pltpu.bitcast(bias, I32)

        def bias_body(j, carry):
            bias_chunk(j, False)
            return carry

        lax.fori_loop(0, i, bias_body, 0)
        bias_chunk(i, True)

    def stored_bias(rows):
        return pltpu.bitcast(keys_sc[rows, :], F32)

    mx = _for_chunk_spans(
        n_chunks, lambda j0, n, mx: _score_span(j0, n, stored_bias, k_ref, qT_ref, s_sc, mx), _max_init())
    _value_sweep(n_chunks, mx, vT_ref, s_sc, acc_sc, o_ref)


def _dsa_attention(qT, ka, vT, iqT, iwT, small):
    B, _, _, L = qT.shape
    nk = L // TK
    return pl.pallas_call(
        _dsa_kernel,
        grid=(B, L // TQ, PAIRS),
        in_specs=[pl.BlockSpec((1, 2, LANES, TQ), lambda b, i, p: (b, p, 0, i)),
                  pl.BlockSpec((1, 2, L, LANES), lambda b, i, p: (b, p, 0, 0)),
                  pl.BlockSpec((1, 2, nk, V_ROWS, TK), lambda b, i, p: (b, p, 0, 0, 0)),
                  pl.BlockSpec((1, PAIRS, LANES, TQ), lambda b, i, p: (b, 0, 0, i)),
                  pl.BlockSpec((1, HEADS, TQ), lambda b, i, p: (b, 0, i)),
                  pl.BlockSpec((1, L, LANES), lambda b, i, p: (b, 0, 0))],
        out_specs=pl.BlockSpec((1, TQ, LANES), lambda b, i, p: (b, i, p)),
        out_shape=jax.ShapeDtypeStruct((B, L, PAIRS * LANES), BF16),
        scratch_shapes=[pltpu.VMEM((L, TQ), I32), pltpu.VMEM((2, L, TQ), F32),
                        pltpu.VMEM((2, V_ROWS, TQ), F32)],
        compiler_params=_params(("parallel", "arbitrary", "arbitrary")),
    )(qT, ka, vT, iqT, iwT, small)


def _layer_norm(z, g, b):
    mu = jnp.mean(z, axis=-1, keepdims=True)
    zc = z - mu
    var = jnp.mean(zc * zc, axis=-1, keepdims=True)
    return zc * lax.rsqrt(var + LN_EPS) * g + b


def _merge_kernel(x_ref, oa_ref, ob_ref, ga_ref, gb_ref, wa_ref, wb_ref, wo_ref, g_ref, b_ref,
                  out_ref, *, alpha):
    a = jnp.dot(oa_ref[0], wa_ref[...], preferred_element_type=F32)
    b = jnp.dot(ob_ref[0], wb_ref[...], preferred_element_type=F32)
    n_gate_blocks = D_MODEL // LANES
    gate_a = jnp.concatenate([ga_ref[0, c] for c in range(n_gate_blocks)], axis=1).astype(F32)
    gate_b = jnp.concatenate([gb_ref[0, c] for c in range(n_gate_blocks)], axis=1).astype(F32)
    merged = jax.nn.sigmoid(gate_a) * a + jax.nn.sigmoid(gate_b) * b
    y = jnp.dot(merged.astype(BF16), wo_ref[...], preferred_element_type=F32)
    out_ref[0] = _layer_norm(alpha * x_ref[0] + y, g_ref[...], b_ref[...])


def _merge(x, oa, ob, p4, wa, wb, wo, g, b, alpha):
    B, L, D = x.shape
    tm = ROW_TILE
    nb = D // LANES
    row = lambda b_, i: (b_, i, 0)
    return pl.pallas_call(
        functools.partial(_merge_kernel, alpha=alpha),
        grid=(B, L // tm),
        in_specs=[pl.BlockSpec((1, tm, D), row),
                  pl.BlockSpec((1, tm, PAIRS * LANES), row),
                  pl.BlockSpec((1, tm, PAIRS * LANES), row),
                  pl.BlockSpec((1, nb, tm, LANES), lambda b_, i: (b_, _CB_GA // nb, i, 0)),
                  pl.BlockSpec((1, nb, tm, LANES), lambda b_, i: (b_, _CB_GB // nb, i, 0)),
                  _const_spec(wa.shape), _const_spec(wb.shape), _const_spec(wo.shape),
                  _const_spec(g.shape), _const_spec(b.shape)],
        out_specs=pl.BlockSpec((1, tm, D), row),
        out_shape=jax.ShapeDtypeStruct((B, L, D), F32),
        compiler_params=_params(("parallel", "parallel")),
    )(x, oa, ob, p4, p4, wa, wb, wo, g, b)


FF_CHUNK = 256


def _ffn_kernel(x_ref, wi_ref, wo_ref, g_ref, b_ref, out_ref, h_sc, *, alpha):
    x = x_ref[0]
    xb = x.astype(BF16)
    for c in range(D_FF // FF_CHUNK):
        lo = c * FF_CHUNK
        gate = jnp.dot(xb, wi_ref[:, lo:lo + FF_CHUNK], preferred_element_type=F32)
        up = jnp.dot(xb, wi_ref[:, D_FF + lo:D_FF + lo + FF_CHUNK], preferred_element_type=F32)
        h_sc[:, lo:lo + FF_CHUNK] = (jax.nn.silu(gate) * up).astype(BF16)
    y = jnp.dot(h_sc[...], wo_ref[...], preferred_element_type=F32)
    out_ref[0] = _layer_norm(alpha * x + y, g_ref[...], b_ref[...])


def _ffn(x, wi, wo, g, b, alpha):
    B, L, D = x.shape
    tm = ROW_TILE
    row = lambda b_, i: (b_, i, 0)
    return pl.pallas_call(
        functools.partial(_ffn_kernel, alpha=alpha),
        grid=(B, L // tm),
        in_specs=[pl.BlockSpec((1, tm, D), row),
                  _const_spec(wi.shape), _const_spec(wo.shape),
                  _const_spec(g.shape), _const_spec(b.shape)],
        out_specs=pl.BlockSpec((1, tm, D), row),
        out_shape=jax.ShapeDtypeStruct((B, L, D), F32),
        scratch_shapes=[pltpu.VMEM((tm, D_FF), BF16)],
        compiler_params=_params(("parallel", "parallel")),
    )(x, wi, wo, g, b)


def _split_w_in(w):
    fox = w[:, 0:_OFF_FLOGIT]
    dsa = w[:, _OFF_DQ:_OFF_DQ + 3 * 512]
    iq = w[:, _OFF_DQ + 3 * 512:_OFF_IK]
    gates = w[:, _OFF_GA:_N_IN]
    w_main = jnp.concatenate([fox, dsa, gates, iq], axis=1).astype(BF16)
    pad = jnp.zeros((w.shape[0], LANES - HEAD_DIM - 2 * HEADS), w.dtype)
    w_small = jnp.concatenate([w[:, _OFF_IK:_OFF_IW], w[:, _OFF_IW:_OFF_GA],
                               w[:, _OFF_FLOGIT:_OFF_DQ], pad], axis=1).astype(BF16)
    return w_main, w_small


def kernel(x, w_in, b_forget, w_branch_a, w_branch_b, w_out, ln1_g, ln1_b, w_ffn_in, w_ffn_out, ln2_g, ln2_b):
    depth = w_in.shape[0]
    alpha = (2.0 * depth) ** 0.25
    for l in range(depth):
        w_main, w_small = _split_w_in(w_in[l])
        bias_row = jnp.zeros((1, LANES), F32).at[0, _SM_FL:_SM_FL + HEADS].set(b_forget[l])
        p4, small = _project(x, w_main, w_small)
        fqT, fka, fvT = _fox_prep(p4, small, bias_row)
        dqT, dka, dvT, iqT, iwT = _dsa_prep(p4, small)
        o_a = _fox_attention(fqT, fka, fvT)
        o_b = _dsa_attention(dqT, dka, dvT, iqT, iwT, small)
        x = _merge(x, o_a, o_b, p4,
                   w_branch_a[l].astype(BF16), w_branch_b[l].astype(BF16), w_out[l].astype(BF16),
                   ln1_g[l][None, :], ln1_b[l][None, :], alpha)
        x = _ffn(x, w_ffn_in[l].astype(BF16), w_ffn_out[l].astype(BF16),
                 ln2_g[l][None, :], ln2_b[l][None, :], alpha)
    return x
```

```python
import functools

import numpy as np
import jax
import jax.numpy as jnp
from jax import lax
from jax.experimental import pallas as pl
from jax.experimental.pallas import tpu as pltpu

F32 = jnp.float32
BF16 = jnp.bfloat16
I32 = jnp.int32
I16 = jnp.int16

D_MODEL = 1024
HEADS = 8
HEAD_DIM = 64
PAIRS = HEADS // 2
D_FF = 2816
TOPK = 256
LANES = 128
NEG_INF = -1e30
LN_EPS = 1e-5
ATTN_SCALE = HEAD_DIM ** -0.5
IDX_SCALE = (HEADS ** -0.5) * (HEAD_DIM ** -0.5)

_OFF_FLOGIT = 1536
_OFF_DQ = 1544
_OFF_IK = 3592
_OFF_IW = 3656
_OFF_GA = 3664
_N_IN = 5712

_CB_FQ, _CB_FK, _CB_FV, _CB_DQ, _CB_DK, _CB_DV, _CB_GA, _CB_GB, _CB_IQ = 0, 4, 8, 12, 16, 20, 24, 32, 40
N_MAIN_BLOCKS = 44
N_MAIN = N_MAIN_BLOCKS * LANES
_SM_IK, _SM_IW, _SM_FL = 0, 64, 72

ROW_TILE = 512
TQ = 512
TK = 512
CUMSUM_BLOCK = 256
V_ROWS = HEAD_DIM + 16
LOG2E = 1.4426950408889634

INT_MIN = -(2 ** 31)


def _monotone_key_of(value):
    b = int(np.float32(value).view(np.int32))
    return b ^ ((b >> 31) & 0x7FFFFFFF)


NEG_INF_KEY = _monotone_key_of(NEG_INF)
VMEM_LIMIT = 56 * 1024 * 1024


def _params(sem):
    return pltpu.CompilerParams(dimension_semantics=sem, vmem_limit_bytes=VMEM_LIMIT)


def _const_spec(shape):
    nd = len(shape)
    return pl.BlockSpec(shape, lambda *_: (0,) * nd, pipeline_mode=pl.Buffered(1))


def _proj_kernel(x_ref, wm_ref, ws_ref, om_ref, os_ref):
    xb = x_ref[0].astype(BF16)
    os_ref[0] = jnp.dot(xb, ws_ref[...], preferred_element_type=F32)
    group = 4
    for c in range(N_MAIN_BLOCKS // group):
        w = wm_ref[:, c * group * LANES:(c + 1) * group * LANES]
        r = jnp.dot(xb, w, preferred_element_type=F32).astype(BF16)
        for g in range(group):
            om_ref[0, c * group + g] = r[:, g * LANES:(g + 1) * LANES]


def _project(x, w_main, w_small):
    B, L, D = x.shape
    tm = ROW_TILE
    return pl.pallas_call(
        _proj_kernel,
        grid=(B, L // tm),
        in_specs=[pl.BlockSpec((1, tm, D), lambda b, i: (b, i, 0)),
                  _const_spec((D, N_MAIN)),
                  _const_spec((D, LANES))],
        out_specs=[pl.BlockSpec((1, N_MAIN_BLOCKS, tm, LANES), lambda b, i: (b, 0, i, 0)),
                   pl.BlockSpec((1, tm, LANES), lambda b, i: (b, i, 0))],
        out_shape=[jax.ShapeDtypeStruct((B, N_MAIN_BLOCKS, L, LANES), BF16),
                   jax.ShapeDtypeStruct((B, L, LANES), F32)],
        compiler_params=_params(("parallel", "parallel")),
    )(x, w_main, w_small)


def _split3(v):
    hi = v.astype(BF16).astype(F32)
    r = v - hi
    mid = r.astype(BF16).astype(F32)
    lo = (r - mid).astype(BF16).astype(F32)
    return hi, mid, lo


def _cumsum_rows(x):
    n = CUMSUM_BLOCK
    L = x.shape[0]
    r = lax.broadcasted_iota(I32, (n, n), 0)
    c = lax.broadcasted_iota(I32, (n, n), 1)
    tri = jnp.where(c <= r, 1.0, 0.0).astype(BF16)
    carry = jnp.zeros((1, x.shape[1]), F32)
    outs = []
    for blk in range(L // n):
        hi, mid, lo = _split3(x[blk * n:(blk + 1) * n])
        y = (jnp.dot(tri, hi.astype(BF16), preferred_element_type=F32)
             + jnp.dot(tri, mid.astype(BF16), preferred_element_type=F32)
             + jnp.dot(tri, lo.astype(BF16), preferred_element_type=F32)) + carry
        carry = y[n - 1:n, :]
        outs.append(y)
    return jnp.concatenate(outs, axis=0)


def _lane_terms(lane, base, terms):
    out = jnp.zeros(lane.shape, F32)
    for k, t in enumerate(terms):
        out = jnp.where(lane == base + k, t, out)
    return out


def _store_value_chunks(ref, v_pair):
    L = v_pair.shape[0]
    vT = v_pair.astype(F32).T.astype(BF16)
    ones = jnp.ones((V_ROWS - HEAD_DIM, L), BF16)
    for e in range(2):
        rows = jnp.concatenate([vT[e * HEAD_DIM:(e + 1) * HEAD_DIM], ones], axis=0)
        for j in range(L // TK):
            ref[0, e, j] = rows[:, j * TK:(j + 1) * TK]


def _fox_prep_kernel(q_ref, k_ref, v_ref, s_ref, bias_ref, qT_ref, ka_ref, vT_ref):
    p = pl.program_id(1)
    L = q_ref.shape[2]
    z = s_ref[0] + bias_ref[...]
    c = _cumsum_rows(jax.nn.log_sigmoid(z))
    lane = lax.broadcasted_iota(I32, (L, LANES), 1)
    q = q_ref[0, 0].astype(F32) * ATTN_SCALE
    k = k_ref[0, 0].astype(F32)
    for e in range(2):
        c_h = jnp.sum(jnp.where(lane == _SM_FL + 2 * p + e, c, 0.0), axis=1, keepdims=True)
        hi, mid, lo = _split3(c_h)
        data = (lane >= e * HEAD_DIM) & (lane < (e + 1) * HEAD_DIM)
        base = (1 - e) * HEAD_DIM
        q_aug = jnp.where(data, q, _lane_terms(lane, base, [hi, mid, lo, 1.0, 1.0, 1.0]))
        k_aug = jnp.where(data, k, _lane_terms(lane, base, [1.0, 1.0, 1.0, -hi, -mid, -lo]))
        qT_ref[0, e] = q_aug.T.astype(BF16)
        ka_ref[0, e] = k_aug.astype(BF16)
    _store_value_chunks(vT_ref, v_ref[0, 0])


def _fox_prep(p4, small, bias_row):
    B, _, L, _ = p4.shape
    nk = L // TK

    def pair_spec(cb):
        return pl.BlockSpec((1, 1, L, LANES), lambda b, p: (b, cb + p, 0, 0))

    return pl.pallas_call(
        _fox_prep_kernel,
        grid=(B, PAIRS),
        in_specs=[pair_spec(_CB_FQ), pair_spec(_CB_FK), pair_spec(_CB_FV),
                  pl.BlockSpec((1, L, LANES), lambda b, p: (b, 0, 0)),
                  pl.BlockSpec((1, LANES), lambda b, p: (0, 0))],
        out_specs=[pl.BlockSpec((1, 2, LANES, L), lambda b, p: (b, p, 0, 0)),
                   pl.BlockSpec((1, 2, L, LANES), lambda b, p: (b, p, 0, 0)),
                   pl.BlockSpec((1, 2, nk, V_ROWS, TK), lambda b, p: (b, p, 0, 0, 0))],
        out_shape=[jax.ShapeDtypeStruct((B, HEADS, LANES, L), BF16),
                   jax.ShapeDtypeStruct((B, HEADS, L, LANES), BF16),
                   jax.ShapeDtypeStruct((B, HEADS, nk, V_ROWS, TK), BF16)],
        compiler_params=_params(("parallel", "arbitrary")),
    )(p4, p4, p4, small, bias_row)


def _dsa_prep_kernel(q_ref, k_ref, v_ref, iq_ref, s_ref, qT_ref, ka_ref, vT_ref, iqT_ref, iwT_ref):
    p = pl.program_id(1)
    L = q_ref.shape[2]
    lane = lax.broadcasted_iota(I32, (L, LANES), 1)
    pos = lax.broadcasted_iota(I32, (L, 1), 0)
    pos_hi = lax.shift_right_logical(pos, 6).astype(F32)
    pos_lo = (pos & 63).astype(F32)
    q = q_ref[0, 0].astype(F32) * ATTN_SCALE
    k = k_ref[0, 0].astype(F32)
    for e in range(2):
        h = jnp.full((L, 1), 2 * p + e, I32)
        slope = lax.shift_left(jnp.ones((L, 1), I32), 7 - h).astype(F32) * (1.0 / 256.0)
        data = (lane >= e * HEAD_DIM) & (lane < (e + 1) * HEAD_DIM)
        base = (1 - e) * HEAD_DIM
        q_aug = jnp.where(data, q, _lane_terms(
            lane, base, [-(slope * 64.0) * pos_hi, -slope * pos_lo, slope * 64.0, slope]))
        k_aug = jnp.where(data, k, _lane_terms(lane, base, [1.0, 1.0, pos_hi, pos_lo]))
        qT_ref[0, e] = q_aug.T.astype(BF16)
        ka_ref[0, e] = k_aug.astype(BF16)
    _store_value_chunks(vT_ref, v_ref[0, 0])
    iqT_ref[0, 0] = iq_ref[0, 0].astype(F32).T.astype(BF16)
    sT = s_ref[0].T
    iwT_ref[0] = sT[_SM_IW:_SM_IW + HEADS, :] * IDX_SCALE


def _dsa_prep(p4, small):
    B, _, L, _ = p4.shape
    nk = L // TK

    def pair_spec(cb):
        return pl.BlockSpec((1, 1, L, LANES), lambda b, p: (b, cb + p, 0, 0))

    return pl.pallas_call(
        _dsa_prep_kernel,
        grid=(B, PAIRS),
        in_specs=[pair_spec(_CB_DQ), pair_spec(_CB_DK), pair_spec(_CB_DV), pair_spec(_CB_IQ),
                  pl.BlockSpec((1, L, LANES), lambda b, p: (b, 0, 0))],
        out_specs=[pl.BlockSpec((1, 2, LANES, L), lambda b, p: (b, p, 0, 0)),
                   pl.BlockSpec((1, 2, L, LANES), lambda b, p: (b, p, 0, 0)),
                   pl.BlockSpec((1, 2, nk, V_ROWS, TK), lambda b, p: (b, p, 0, 0, 0)),
                   pl.BlockSpec((1, 1, LANES, L), lambda b, p: (b, p, 0, 0)),
                   pl.BlockSpec((1, HEADS, L), lambda b, p: (b, 0, 0))],
        out_shape=[jax.ShapeDtypeStruct((B, HEADS, LANES, L), BF16),
                   jax.ShapeDtypeStruct((B, HEADS, L, LANES), BF16),
                   jax.ShapeDtypeStruct((B, HEADS, nk, V_ROWS, TK), BF16),
                   jax.ShapeDtypeStruct((B, PAIRS, LANES, L), BF16),
                   jax.ShapeDtypeStruct((B, HEADS, L), F32)],
        compiler_params=_params(("parallel", "arbitrary")),
    )(p4, p4, p4, p4, small)


def _fold8(x, op):
    return op(x.reshape(x.shape[0] // 8, 8, x.shape[1]), axis=0)


def _for_chunk_spans(n_chunks, fn, init):
    n_pairs = lax.shift_right_logical(n_chunks, 1)
    carry = lax.fori_loop(0, n_pairs, lambda jj, c: fn(2 * jj, 2, c), init)
    return lax.fori_loop(0, n_chunks & 1, lambda _, c: fn(n_chunks - 1, 1, c), carry)


def _score_span(j0, n, bias_fn, k_ref, qT_ref, s_sc, mx):
    rows = pl.ds(pl.multiple_of(j0 * TK, TK), n * TK)
    bias = None if bias_fn is None else bias_fn(rows)
    out = []
    for e in range(2):
        s = jnp.dot(k_ref[0, e, rows, :], qT_ref[0, e], preferred_element_type=F32) * LOG2E
        if bias is not None:
            s = s + bias
        s_sc[e, rows, :] = s
        out.append(jnp.maximum(mx[e], _fold8(s, jnp.max)))
    return tuple(out)


def _value_sweep(n_chunks, mx, vT_ref, s_sc, acc_sc, o_ref):
    m = [jnp.max(mx[e], axis=0, keepdims=True) for e in range(2)]
    acc_sc[...] = jnp.zeros(acc_sc.shape, F32)

    def span(j0, n, carry):
        row0 = pl.multiple_of(j0 * TK, TK)
        for e in range(2):
            for sub in range(n):
                rows = pl.ds(pl.multiple_of(row0 + sub * TK, TK), TK)
                p = jnp.exp2(s_sc[e, rows, :] - m[e])
                acc_sc[e] += jnp.dot(vT_ref[0, e, j0 + sub], p.astype(BF16),
                                     preferred_element_type=F32)
        return carry

    _for_chunk_spans(n_chunks, span, 0)
    rows = [acc_sc[e][:HEAD_DIM] / acc_sc[e][HEAD_DIM:HEAD_DIM + 1] for e in range(2)]
    o_ref[0] = jnp.concatenate(rows, axis=0).T.astype(o_ref.dtype)


def _causal(shape):
    return lax.broadcasted_iota(I32, shape, 0) <= lax.broadcasted_iota(I32, shape, 1)


def _max_init():
    init = jnp.full((8, TQ), NEG_INF, F32)
    return (init, init)


def _fox_kernel(qT_ref, k_ref, vT_ref, o_ref, s_sc, acc_sc):
    i = pl.program_id(2)
    mx = _for_chunk_spans(i, lambda j0, n, mx: _score_span(j0, n, None, k_ref, qT_ref, s_sc, mx), _max_init())
    causal_bias = jnp.where(_causal((TK, TQ)), 0.0, NEG_INF)
    mx = _score_span(i, 1, lambda rows: causal_bias, k_ref, qT_ref, s_sc, mx)
    _value_sweep(i + 1, mx, vT_ref, s_sc, acc_sc, o_ref)


def _fox_attention(qT, ka, vT):
    B, _, _, L = qT.shape
    nk = L // TK
    return pl.pallas_call(
        _fox_kernel,
        grid=(B, PAIRS, L // TQ),
        in_specs=[pl.BlockSpec((1, 2, LANES, TQ), lambda b, p, i: (b, p, 0, i)),
                  pl.BlockSpec((1, 2, L, LANES), lambda b, p, i: (b, p, 0, 0)),
                  pl.BlockSpec((1, 2, nk, V_ROWS, TK), lambda b, p, i: (b, p, 0, 0, 0))],
        out_specs=pl.BlockSpec((1, TQ, LANES), lambda b, p, i: (b, i, p)),
        out_shape=jax.ShapeDtypeStruct((B, L, PAIRS * LANES), BF16),
        scratch_shapes=[pltpu.VMEM((2, L, TQ), F32), pltpu.VMEM((2, V_ROWS, TQ), F32)],
        compiler_params=_params(("parallel", "parallel", "arbitrary")),
    )(qT, ka, vT)


def _count_rows(mask):
    return _fold8(jnp.where(mask, 1, 0).astype(I32), jnp.sum)


def _dsa_kernel(qT_ref, k_ref, vT_ref, iqT_ref, iwT_ref, s_ref, o_ref, keys_sc, hi_sc, lo_sc, s_sc, acc_sc):
    i = pl.program_id(1)
    p = pl.program_id(2)
    n_chunks = i + 1

    @pl.when(p == 0)
    def _select():
        wts = iwT_ref[0]

        def score_chunk(j, masked):
            ks = pl.multiple_of(j * TK, TK)
            ik = s_ref[0, pl.ds(ks, TK), :][:, _SM_IK:_SM_IK + HEAD_DIM].astype(BF16)
            acc = jnp.zeros((TK, TQ), F32)
            for h in range(HEADS):
                iq = iqT_ref[0, h // 2, (h % 2) * HEAD_DIM:(h % 2 + 1) * HEAD_DIM, :]
                rel = jnp.maximum(jnp.dot(ik, iq, preferred_element_type=F32), 0.0)
                acc = acc + wts[h:h + 1, :] * rel
            if masked:
                acc = jnp.where(_causal(acc.shape), acc, NEG_INF)
            acc = jnp.where(acc == 0.0, 0.0, acc)
            bits = pltpu.bitcast(acc, I32)
            key = bits ^ (lax.shift_right_arithmetic(bits, 31) & 0x7FFFFFFF)
            keys_sc[pl.ds(ks, TK), :] = key
            hi_sc[pl.ds(ks, TK), :] = lax.shift_right_arithmetic(key, 16).astype(I16)
            lo_sc[pl.ds(ks, TK), :] = ((key & 0xFFFF) - 2 ** 15).astype(I16)

        def score_body(j, carry):
            score_chunk(j, False)
            return carry

        lax.fori_loop(0, i, score_body, 0)
        score_chunk(i, True)

        def count(pred):
            def body(j, cnt):
                ks = pl.multiple_of(j * TK, TK)
                return cnt + _count_rows(pred(keys_sc[pl.ds(ks, TK), :]))
            cnt8 = lax.fori_loop(0, n_chunks, body, jnp.zeros((8, TQ), I32))
            return jnp.sum(cnt8, axis=0, keepdims=True)

        def count16(ref, pred):
            one, nil = jnp.ones((), I16), jnp.zeros((), I16)

            def body(j, cnt):
                ks = pl.multiple_of(j * TK, TK)
                hit = jnp.where(pred(ref[pl.ds(ks, TK), :]), one, nil)
                for r in range(TK // 16):
                    cnt = cnt + hit[r * 16:(r + 1) * 16, :]
                return cnt
            cnt16 = lax.fori_loop(0, n_chunks, body, jnp.zeros((16, TQ), I16))
            return jnp.sum(cnt16.astype(I32), axis=0, keepdims=True)

        def select16(ref, k):
            zero = jnp.zeros((1, TQ), I16)
            t0 = jnp.where(count16(ref, lambda v: v >= zero) >= k, 0, -(2 ** 15)).astype(I32)

            def bit_body(bi, t):
                cand = t | jnp.left_shift(jnp.ones((1, TQ), I32), 14 - bi)
                cand16 = cand.astype(I16)
                return jnp.where(count16(ref, lambda v: v >= cand16) >= k, cand, t)

            return lax.fori_loop(0, 15, bit_body, t0)

        t_hi = select16(hi_sc, TOPK)
        t_hi16 = t_hi.astype(I16)
        n_above = count16(hi_sc, lambda v: v > t_hi16)
        k_lo = TOPK - n_above

        def restrict_body(j, carry):
            ks = pl.multiple_of(j * TK, TK)
            in_bucket = hi_sc[pl.ds(ks, TK), :] == t_hi16
            lo_sc[pl.ds(ks, TK), :] = jnp.where(in_bucket, lo_sc[pl.ds(ks, TK), :], jnp.full((), -(2 ** 15), I16))
            return carry

        lax.fori_loop(0, n_chunks, restrict_body, 0)
        t_lo = select16(lo_sc, k_lo)
        thr = jnp.left_shift(t_hi, 16) | (t_lo + 2 ** 15)

        t_lo16 = t_lo.astype(I16)
        n_ge = n_above + count16(lo_sc, lambda v: v >= t_lo16)
        excess = jnp.where((n_ge > TOPK) & (thr > NEG_INF_KEY), 1, 0)

        @pl.when(jnp.max(excess) > 0)
        def _break_ties():
            need = TOPK - count(lambda kk: kk > thr)
            r = lax.broadcasted_iota(I32, (TK, TK), 0)
            c = lax.broadcasted_iota(I32, (TK, TK), 1)
            strict_lower = jnp.where(c < r, 1.0, 0.0).astype(BF16)

            def body(j, seen):
                ks = pl.multiple_of(j * TK, TK)
                kk = keys_sc[pl.ds(ks, TK), :]
                tie = kk == thr
                tie_f = jnp.where(tie, 1.0, 0.0)
                before = jnp.dot(strict_lower, tie_f.astype(BF16), preferred_element_type=F32) + seen
                drop = tie & (before >= need.astype(F32))
                keys_sc[pl.ds(ks, TK), :] = jnp.where(drop, INT_MIN, kk)
                return seen + jnp.sum(tie_f, axis=0, keepdims=True)

            lax.fori_loop(0, n_chunks, body, jnp.zeros((1, TQ), F32))

        def bias_chunk(j, masked):
            ks = pl.multiple_of(j * TK, TK)
            bias = jnp.where(keys_sc[pl.ds(ks, TK), :] >= thr, 0.0, NEG_INF)
            if masked:
                bias = jnp.where(_causal(bias.shape), bias, NEG_INF)
            keys_sc[pl.ds(ks, TK), :] = pltpu.bitcast(bias, I32)

        def bias_body(j, carry):
            bias_chunk(j, False)
            return carry

        lax.fori_loop(0, i, bias_body, 0)
        bias_chunk(i, True)

    def stored_bias(rows):
        return pltpu.bitcast(keys_sc[rows, :], F32)

    mx = _for_chunk_spans(
        n_chunks, lambda j0, n, mx: _score_span(j0, n, stored_bias, k_ref, qT_ref, s_sc, mx), _max_init())
    _value_sweep(n_chunks, mx, vT_ref, s_sc, acc_sc, o_ref)


def _dsa_attention(qT, ka, vT, iqT, iwT, small):
    B, _, _, L = qT.shape
    nk = L // TK
    return pl.pallas_call(
        _dsa_kernel,
        grid=(B, L // TQ, PAIRS),
        in_specs=[pl.BlockSpec((1, 2, LANES, TQ), lambda b, i, p: (b, p, 0, i)),
                  pl.BlockSpec((1, 2, L, LANES), lambda b, i, p: (b, p, 0, 0)),
                  pl.BlockSpec((1, 2, nk, V_ROWS, TK), lambda b, i, p: (b, p, 0, 0, 0)),
                  pl.BlockSpec((1, PAIRS, LANES, TQ), lambda b, i, p: (b, 0, 0, i)),
                  pl.BlockSpec((1, HEADS, TQ), lambda b, i, p: (b, 0, i)),
                  pl.BlockSpec((1, L, LANES), lambda b, i, p: (b, 0, 0))],
        out_specs=pl.BlockSpec((1, TQ, LANES), lambda b, i, p: (b, i, p)),
        out_shape=jax.ShapeDtypeStruct((B, L, PAIRS * LANES), BF16),
        scratch_shapes=[pltpu.VMEM((L, TQ), I32), pltpu.VMEM((L, TQ), I16), pltpu.VMEM((L, TQ), I16),
                        pltpu.VMEM((2, L, TQ), F32),
                        pltpu.VMEM((2, V_ROWS, TQ), F32)],
        compiler_params=_params(("parallel", "arbitrary", "arbitrary")),
    )(qT, ka, vT, iqT, iwT, small)


def _layer_norm(z, g, b):
    mu = jnp.mean(z, axis=-1, keepdims=True)
    zc = z - mu
    var = jnp.mean(zc * zc, axis=-1, keepdims=True)
    return zc * lax.rsqrt(var + LN_EPS) * g + b


def _merge_kernel(x_ref, oa_ref, ob_ref, ga_ref, gb_ref, wa_ref, wb_ref, wo_ref, g_ref, b_ref,
                  out_ref, *, alpha):
    a = jnp.dot(oa_ref[0], wa_ref[...], preferred_element_type=F32)
    b = jnp.dot(ob_ref[0], wb_ref[...], preferred_element_type=F32)
    n_gate_blocks = D_MODEL // LANES
    gate_a = jnp.concatenate([ga_ref[0, c] for c in range(n_gate_blocks)], axis=1).astype(F32)
    gate_b = jnp.concatenate([gb_ref[0, c] for c in range(n_gate_blocks)], axis=1).astype(F32)
    merged = jax.nn.sigmoid(gate_a) * a + jax.nn.sigmoid(gate_b) * b
    y = jnp.dot(merged.astype(BF16), wo_ref[...], preferred_element_type=F32)
    out_ref[0] = _layer_norm(alpha * x_ref[0] + y, g_ref[...], b_ref[...])


def _merge(x, oa, ob, p4, wa, wb, wo, g, b, alpha):
    B, L, D = x.shape
    tm = ROW_TILE
    nb = D // LANES
    row = lambda b_, i: (b_, i, 0)
    return pl.pallas_call(
        functools.partial(_merge_kernel, alpha=alpha),
        grid=(B, L // tm),
        in_specs=[pl.BlockSpec((1, tm, D), row),
                  pl.BlockSpec((1, tm, PAIRS * LANES), row),
                  pl.BlockSpec((1, tm, PAIRS * LANES), row),
                  pl.BlockSpec((1, nb, tm, LANES), lambda b_, i: (b_, _CB_GA // nb, i, 0)),
                  pl.BlockSpec((1, nb, tm, LANES), lambda b_, i: (b_, _CB_GB // nb, i, 0)),
                  _const_spec(wa.shape), _const_spec(wb.shape), _const_spec(wo.shape),
                  _const_spec(g.shape), _const_spec(b.shape)],
        out_specs=pl.BlockSpec((1, tm, D), row),
        out_shape=jax.ShapeDtypeStruct((B, L, D), F32),
        compiler_params=_params(("parallel", "parallel")),
    )(x, oa, ob, p4, p4, wa, wb, wo, g, b)


FF_CHUNK = 256


def _ffn_kernel(x_ref, wi_ref, wo_ref, g_ref, b_ref, out_ref, h_sc, *, alpha):
    x = x_ref[0]
    xb = x.astype(BF16)
    for c in range(D_FF // FF_CHUNK):
        lo = c * FF_CHUNK
        gate = jnp.dot(xb, wi_ref[:, lo:lo + FF_CHUNK], preferred_element_type=F32)
        up = jnp.dot(xb, wi_ref[:, D_FF + lo:D_FF + lo + FF_CHUNK], preferred_element_type=F32)
        h_sc[:, lo:lo + FF_CHUNK] = (jax.nn.silu(gate) * up).astype(BF16)
    y = jnp.dot(h_sc[...], wo_ref[...], preferred_element_type=F32)
    out_ref[0] = _layer_norm(alpha * x + y, g_ref[...], b_ref[...])


def _ffn(x, wi, wo, g, b, alpha):
    B, L, D = x.shape
    tm = ROW_TILE
    row = lambda b_, i: (b_, i, 0)
    return pl.pallas_call(
        functools.partial(_ffn_kernel, alpha=alpha),
        grid=(B, L // tm),
        in_specs=[pl.BlockSpec((1, tm, D), row),
                  _const_spec(wi.shape), _const_spec(wo.shape),
                  _const_spec(g.shape), _const_spec(b.shape)],
        out_specs=pl.BlockSpec((1, tm, D), row),
        out_shape=jax.ShapeDtypeStruct((B, L, D), F32),
        scratch_shapes=[pltpu.VMEM((tm, D_FF), BF16)],
        compiler_params=_params(("parallel", "parallel")),
    )(x, wi, wo, g, b)


def _split_w_in(w):
    fox = w[:, 0:_OFF_FLOGIT]
    dsa = w[:, _OFF_DQ:_OFF_DQ + 3 * 512]
    iq = w[:, _OFF_DQ + 3 * 512:_OFF_IK]
    gates = w[:, _OFF_GA:_N_IN]
    w_main = jnp.concatenate([fox, dsa, gates, iq], axis=1).astype(BF16)
    pad = jnp.zeros((w.shape[0], LANES - HEAD_DIM - 2 * HEADS), w.dtype)
    w_small = jnp.concatenate([w[:, _OFF_IK:_OFF_IW], w[:, _OFF_IW:_OFF_GA],
                               w[:, _OFF_FLOGIT:_OFF_DQ], pad], axis=1).astype(BF16)
    return w_main, w_small


def kernel(x, w_in, b_forget, w_branch_a, w_branch_b, w_out, ln1_g, ln1_b, w_ffn_in, w_ffn_out, ln2_g, ln2_b):
    depth = w_in.shape[0]
    alpha = (2.0 * depth) ** 0.25
    for l in range(depth):
        w_main, w_small = _split_w_in(w_in[l])
        bias_row = jnp.zeros((1, LANES), F32).at[0, _SM_FL:_SM_FL + HEADS].set(b_forget[l])
        p4, small = _project(x, w_main, w_small)
        fqT, fka, fvT = _fox_prep(p4, small, bias_row)
        dqT, dka, dvT, iqT, iwT = _dsa_prep(p4, small)
        o_a = _fox_attention(fqT, fka, fvT)
        o_b = _dsa_attention(dqT, dka, dvT, iqT, iwT, small)
        x = _merge(x, o_a, o_b, p4,
                   w_branch_a[l].astype(BF16), w_branch_b[l].astype(BF16), w_out[l].astype(BF16),
                   ln1_g[l][None, :], ln1_b[l][None, :], alpha)
        x = _ffn(x, w_ffn_in[l].astype(BF16), w_ffn_out[l].astype(BF16),
                 ln2_g[l][None, :], ln2_b[l][None, :], alpha)
    return x
```

```python
import functools

import numpy as np
import jax
import jax.numpy as jnp
from jax import lax
from jax.experimental import pallas as pl
from jax.experimental.pallas import tpu as pltpu

F32 = jnp.float32
BF16 = jnp.bfloat16
I32 = jnp.int32
I16 = jnp.int16

D_MODEL = 1024
HEADS = 8
HEAD_DIM = 64
PAIRS = HEADS // 2
D_FF = 2816
TOPK = 256
LANES = 128
NEG_INF = -1e30
LN_EPS = 1e-5
ATTN_SCALE = HEAD_DIM ** -0.5
IDX_SCALE = (HEADS ** -0.5) * (HEAD_DIM ** -0.5)

_OFF_FLOGIT = 1536
_OFF_DQ = 1544
_OFF_IK = 3592
_OFF_IW = 3656
_OFF_GA = 3664
_N_IN = 5712

_CB_FQ, _CB_FK, _CB_FV, _CB_DQ, _CB_DK, _CB_DV, _CB_GA, _CB_GB, _CB_IQ = 0, 4, 8, 12, 16, 20, 24, 32, 40
N_MAIN_BLOCKS = 44
N_MAIN = N_MAIN_BLOCKS * LANES
_SM_IK, _SM_IW, _SM_FL = 0, 64, 72

ROW_TILE = 512
TQ = 512
TK = 512
CUMSUM_BLOCK = 256
V_ROWS = HEAD_DIM + 16
LOG2E = 1.4426950408889634

INT_MIN = -(2 ** 31)


def _monotone_key_of(value):
    b = int(np.float32(value).view(np.int32))
    return b ^ ((b >> 31) & 0x7FFFFFFF)


NEG_INF_KEY = _monotone_key_of(NEG_INF)
VMEM_LIMIT = 56 * 1024 * 1024


def _params(sem):
    return pltpu.CompilerParams(dimension_semantics=sem, vmem_limit_bytes=VMEM_LIMIT)


def _const_spec(shape):
    nd = len(shape)
    return pl.BlockSpec(shape, lambda *_: (0,) * nd, pipeline_mode=pl.Buffered(1))


def _proj_kernel(x_ref, wm_ref, ws_ref, om_ref, os_ref):
    xb = x_ref[0].astype(BF16)
    os_ref[0] = jnp.dot(xb, ws_ref[...], preferred_element_type=F32)
    group = 4
    for c in range(N_MAIN_BLOCKS // group):
        w = wm_ref[:, c * group * LANES:(c + 1) * group * LANES]
        r = jnp.dot(xb, w, preferred_element_type=F32).astype(BF16)
        for g in range(group):
            om_ref[0, c * group + g] = r[:, g * LANES:(g + 1) * LANES]


def _project(x, w_main, w_small):
    B, L, D = x.shape
    tm = ROW_TILE
    return pl.pallas_call(
        _proj_kernel,
        grid=(B, L // tm),
        in_specs=[pl.BlockSpec((1, tm, D), lambda b, i: (b, i, 0)),
                  _const_spec((D, N_MAIN)),
                  _const_spec((D, LANES))],
        out_specs=[pl.BlockSpec((1, N_MAIN_BLOCKS, tm, LANES), lambda b, i: (b, 0, i, 0)),
                   pl.BlockSpec((1, tm, LANES), lambda b, i: (b, i, 0))],
        out_shape=[jax.ShapeDtypeStruct((B, N_MAIN_BLOCKS, L, LANES), BF16),
                   jax.ShapeDtypeStruct((B, L, LANES), F32)],
        compiler_params=_params(("parallel", "parallel")),
    )(x, w_main, w_small)


def _split3(v):
    hi = v.astype(BF16).astype(F32)
    r = v - hi
    mid = r.astype(BF16).astype(F32)
    lo = (r - mid).astype(BF16).astype(F32)
    return hi, mid, lo


def _cumsum_rows(x):
    n = CUMSUM_BLOCK
    L = x.shape[0]
    r = lax.broadcasted_iota(I32, (n, n), 0)
    c = lax.broadcasted_iota(I32, (n, n), 1)
    tri = jnp.where(c <= r, 1.0, 0.0).astype(BF16)
    carry = jnp.zeros((1, x.shape[1]), F32)
    outs = []
    for blk in range(L // n):
        hi, mid, lo = _split3(x[blk * n:(blk + 1) * n])
        y = (jnp.dot(tri, hi.astype(BF16), preferred_element_type=F32)
             + jnp.dot(tri, mid.astype(BF16), preferred_element_type=F32)
             + jnp.dot(tri, lo.astype(BF16), preferred_element_type=F32)) + carry
        carry = y[n - 1:n, :]
        outs.append(y)
    return jnp.concatenate(outs, axis=0)


def _lane_terms(lane, base, terms):
    out = jnp.zeros(lane.shape, F32)
    for k, t in enumerate(terms):
        out = jnp.where(lane == base + k, t, out)
    return out


def _store_value_chunks(ref, v_pair):
    L = v_pair.shape[0]
    vT = v_pair.astype(F32).T.astype(BF16)
    ones = jnp.ones((V_ROWS - HEAD_DIM, L), BF16)
    for e in range(2):
        rows = jnp.concatenate([vT[e * HEAD_DIM:(e + 1) * HEAD_DIM], ones], axis=0)
        for j in range(L // TK):
            ref[0, e, j] = rows[:, j * TK:(j + 1) * TK]


def _fox_prep_kernel(q_ref, k_ref, v_ref, s_ref, bias_ref, qT_ref, ka_ref, vT_ref):
    p = pl.program_id(1)
    L = q_ref.shape[2]
    z = s_ref[0] + bias_ref[...]
    c = _cumsum_rows(jax.nn.log_sigmoid(z))
    lane = lax.broadcasted_iota(I32, (L, LANES), 1)
    q = q_ref[0, 0].astype(F32) * ATTN_SCALE
    k = k_ref[0, 0].astype(F32)
    for e in range(2):
        c_h = jnp.sum(jnp.where(lane == _SM_FL + 2 * p + e, c, 0.0), axis=1, keepdims=True)
        hi, mid, lo = _split3(c_h)
        data = (lane >= e * HEAD_DIM) & (lane < (e + 1) * HEAD_DIM)
        base = (1 - e) * HEAD_DIM
        q_aug = jnp.where(data, q, _lane_terms(lane, base, [hi, mid, lo, 1.0, 1.0, 1.0]))
        k_aug = jnp.where(data, k, _lane_terms(lane, base, [1.0, 1.0, 1.0, -hi, -mid, -lo]))
        qT_ref[0, e] = q_aug.T.astype(BF16)
        ka_ref[0, e] = k_aug.astype(BF16)
    _store_value_chunks(vT_ref, v_ref[0, 0])


def _fox_prep(p4, small, bias_row):
    B, _, L, _ = p4.shape
    nk = L // TK

    def pair_spec(cb):
        return pl.BlockSpec((1, 1, L, LANES), lambda b, p: (b, cb + p, 0, 0))

    return pl.pallas_call(
        _fox_prep_kernel,
        grid=(B, PAIRS),
        in_specs=[pair_spec(_CB_FQ), pair_spec(_CB_FK), pair_spec(_CB_FV),
                  pl.BlockSpec((1, L, LANES), lambda b, p: (b, 0, 0)),
                  pl.BlockSpec((1, LANES), lambda b, p: (0, 0))],
        out_specs=[pl.BlockSpec((1, 2, LANES, L), lambda b, p: (b, p, 0, 0)),
                   pl.BlockSpec((1, 2, L, LANES), lambda b, p: (b, p, 0, 0)),
                   pl.BlockSpec((1, 2, nk, V_ROWS, TK), lambda b, p: (b, p, 0, 0, 0))],
        out_shape=[jax.ShapeDtypeStruct((B, HEADS, LANES, L), BF16),
                   jax.ShapeDtypeStruct((B, HEADS, L, LANES), BF16),
                   jax.ShapeDtypeStruct((B, HEADS, nk, V_ROWS, TK), BF16)],
        compiler_params=_params(("parallel", "arbitrary")),
    )(p4, p4, p4, small, bias_row)


def _dsa_prep_kernel(q_ref, k_ref, v_ref, iq_ref, s_ref, qT_ref, ka_ref, vT_ref, iqT_ref, iwT_ref):
    p = pl.program_id(1)
    L = q_ref.shape[2]
    lane = lax.broadcasted_iota(I32, (L, LANES), 1)
    pos = lax.broadcasted_iota(I32, (L, 1), 0)
    pos_hi = lax.shift_right_logical(pos, 6).astype(F32)
    pos_lo = (pos & 63).astype(F32)
    q = q_ref[0, 0].astype(F32) * ATTN_SCALE
    k = k_ref[0, 0].astype(F32)
    for e in range(2):
        h = jnp.full((L, 1), 2 * p + e, I32)
        slope = lax.shift_left(jnp.ones((L, 1), I32), 7 - h).astype(F32) * (1.0 / 256.0)
        data = (lane >= e * HEAD_DIM) & (lane < (e + 1) * HEAD_DIM)
        base = (1 - e) * HEAD_DIM
        q_aug = jnp.where(data, q, _lane_terms(
            lane, base, [-(slope * 64.0) * pos_hi, -slope * pos_lo, slope * 64.0, slope]))
        k_aug = jnp.where(data, k, _lane_terms(lane, base, [1.0, 1.0, pos_hi, pos_lo]))
        qT_ref[0, e] = q_aug.T.astype(BF16)
        ka_ref[0, e] = k_aug.astype(BF16)
    _store_value_chunks(vT_ref, v_ref[0, 0])
    iqT_ref[0, 0] = iq_ref[0, 0].astype(F32).T.astype(BF16)
    sT = s_ref[0].T
    iwT_ref[0] = sT[_SM_IW:_SM_IW + HEADS, :] * IDX_SCALE


def _dsa_prep(p4, small):
    B, _, L, _ = p4.shape
    nk = L // TK

    def pair_spec(cb):
        return pl.BlockSpec((1, 1, L, LANES), lambda b, p: (b, cb + p, 0, 0))

    return pl.pallas_call(
        _dsa_prep_kernel,
        grid=(B, PAIRS),
        in_specs=[pair_spec(_CB_DQ), pair_spec(_CB_DK), pair_spec(_CB_DV), pair_spec(_CB_IQ),
                  pl.BlockSpec((1, L, LANES), lambda b, p: (b, 0, 0))],
        out_specs=[pl.BlockSpec((1, 2, LANES, L), lambda b, p: (b, p, 0, 0)),
                   pl.BlockSpec((1, 2, L, LANES), lambda b, p: (b, p, 0, 0)),
                   pl.BlockSpec((1, 2, nk, V_ROWS, TK), lambda b, p: (b, p, 0, 0, 0)),
                   pl.BlockSpec((1, 1, LANES, L), lambda b, p: (b, p, 0, 0)),
                   pl.BlockSpec((1, HEADS, L), lambda b, p: (b, 0, 0))],
        out_shape=[jax.ShapeDtypeStruct((B, HEADS, LANES, L), BF16),
                   jax.ShapeDtypeStruct((B, HEADS, L, LANES), BF16),
                   jax.ShapeDtypeStruct((B, HEADS, nk, V_ROWS, TK), BF16),
                   jax.ShapeDtypeStruct((B, PAIRS, LANES, L), BF16),
                   jax.ShapeDtypeStruct((B, HEADS, L), F32)],
        compiler_params=_params(("parallel", "arbitrary")),
    )(p4, p4, p4, p4, small)


def _fold8(x, op):
    return op(x.reshape(x.shape[0] // 8, 8, x.shape[1]), axis=0)


def _for_chunk_spans(n_chunks, fn, init):
    n_pairs = lax.shift_right_logical(n_chunks, 1)
    carry = lax.fori_loop(0, n_pairs, lambda jj, c: fn(2 * jj, 2, c), init)
    return lax.fori_loop(0, n_chunks & 1, lambda _, c: fn(n_chunks - 1, 1, c), carry)


def _causal(shape):
    return lax.broadcasted_iota(I32, shape, 0) <= lax.broadcasted_iota(I32, shape, 1)


def _attend_heads(n_full, bias_fn, qT_ref, k_ref, vT_ref, s_a, s_b, acc_sc, oT_sc, o_ref):
    def phase(score_h, s_dst, value_h, s_src, mx_prev):
        if value_h is not None:
            m = jnp.max(mx_prev, axis=0, keepdims=True)
            acc_sc[...] = jnp.zeros(acc_sc.shape, F32)

        def span(j0, n, mx, diag):
            row0 = pl.multiple_of(j0 * TK, TK)
            if score_h is not None:
                rows = pl.ds(row0, n * TK)
                s = jnp.dot(k_ref[0, score_h, rows, :], qT_ref[0, score_h], preferred_element_type=F32) * LOG2E
                bias = bias_fn(rows, diag)
                if bias is not None:
                    s = s + bias
                s_dst[rows, :] = s
                mx = jnp.maximum(mx, _fold8(s, jnp.max))
            if value_h is not None:
                for sub in range(n):
                    rows = pl.ds(pl.multiple_of(row0 + sub * TK, TK), TK)
                    p = jnp.exp2(s_src[rows, :] - m)
                    acc_sc[...] += jnp.dot(vT_ref[0, value_h, j0 + sub], p.astype(BF16),
                                           preferred_element_type=F32)
            return mx

        mx = _for_chunk_spans(n_full, lambda j0, n, mx: span(j0, n, mx, False), jnp.full((8, TQ), NEG_INF, F32))
        mx = span(n_full, 1, mx, True)
        if value_h is not None:
            oT_sc[value_h] = acc_sc[:HEAD_DIM] / acc_sc[HEAD_DIM:HEAD_DIM + 1]
        return mx

    mx_a = phase(0, s_a, None, None, None)

    def two_heads(q, mx_a):
        mx_b = phase(2 * q + 1, s_b, 2 * q, s_a, mx_a)
        return phase(2 * q + 2, s_a, 2 * q + 1, s_b, mx_b)

    mx_a = lax.fori_loop(0, PAIRS - 1, two_heads, mx_a)
    mx_b = phase(HEADS - 1, s_b, HEADS - 2, s_a, mx_a)
    phase(None, None, HEADS - 1, s_b, mx_b)
    for p in range(PAIRS):
        pair = jnp.concatenate([oT_sc[2 * p], oT_sc[2 * p + 1]], axis=0)
        o_ref[0, :, p * LANES:(p + 1) * LANES] = pair.T.astype(o_ref.dtype)


def _head_specs(L, index):
    nk = L // TK
    return [pl.BlockSpec((1, HEADS, L, LANES), lambda *g: (index(*g), 0, 0, 0), pipeline_mode=pl.Buffered(1)),
            pl.BlockSpec((1, HEADS, nk, V_ROWS, TK), lambda *g: (index(*g), 0, 0, 0, 0), pipeline_mode=pl.Buffered(1))]


def _head_scratch(L):
    return [pltpu.VMEM((L, TQ), F32), pltpu.VMEM((L, TQ), F32),
            pltpu.VMEM((V_ROWS, TQ), F32), pltpu.VMEM((HEADS, HEAD_DIM, TQ), F32)]


def _fox_kernel(qT_ref, k_ref, vT_ref, o_ref, s_a, s_b, acc_sc, oT_sc):
    i = pl.program_id(1)

    def bias_fn(rows, diag):
        return jnp.where(_causal((TK, TQ)), 0.0, NEG_INF) if diag else None

    _attend_heads(i, bias_fn, qT_ref, k_ref, vT_ref, s_a, s_b, acc_sc, oT_sc, o_ref)


def _fox_attention(qT, ka, vT):
    B, _, _, L = qT.shape
    return pl.pallas_call(
        _fox_kernel,
        grid=(B, L // TQ),
        in_specs=[pl.BlockSpec((1, HEADS, LANES, TQ), lambda b, i: (b, 0, 0, i))] + _head_specs(L, lambda b, i: b),
        out_specs=pl.BlockSpec((1, TQ, PAIRS * LANES), lambda b, i: (b, i, 0)),
        out_shape=jax.ShapeDtypeStruct((B, L, PAIRS * LANES), BF16),
        scratch_shapes=_head_scratch(L),
        compiler_params=_params(("parallel", "arbitrary")),
    )(qT, ka, vT)


def _count_rows(mask):
    return _fold8(jnp.where(mask, 1, 0).astype(I32), jnp.sum)


def _dsa_kernel(qT_ref, k_ref, vT_ref, iqT_ref, iwT_ref, s_ref, o_ref,
                keys_sc, hi_sc, lo_sc, s_a, s_b, acc_sc, oT_sc):
    i = pl.program_id(1)
    n_chunks = i + 1

    def _select():
        wts = iwT_ref[0]

        def score_chunk(j, masked):
            ks = pl.multiple_of(j * TK, TK)
            ik = s_ref[0, pl.ds(ks, TK), :][:, _SM_IK:_SM_IK + HEAD_DIM].astype(BF16)
            acc = jnp.zeros((TK, TQ), F32)
            for h in range(HEADS):
                iq = iqT_ref[0, h // 2, (h % 2) * HEAD_DIM:(h % 2 + 1) * HEAD_DIM, :]
                rel = jnp.maximum(jnp.dot(ik, iq, preferred_element_type=F32), 0.0)
                acc = acc + wts[h:h + 1, :] * rel
            if masked:
                acc = jnp.where(_causal(acc.shape), acc, NEG_INF)
            acc = jnp.where(acc == 0.0, 0.0, acc)
            bits = pltpu.bitcast(acc, I32)
            key = bits ^ (lax.shift_right_arithmetic(bits, 31) & 0x7FFFFFFF)
            keys_sc[pl.ds(ks, TK), :] = key
            hi_sc[pl.ds(ks, TK), :] = lax.shift_right_arithmetic(key, 16).astype(I16)
            lo_sc[pl.ds(ks, TK), :] = ((key & 0xFFFF) - 2 ** 15).astype(I16)

        def score_body(j, carry):
            score_chunk(j, False)
            return carry

        lax.fori_loop(0, i, score_body, 0)
        score_chunk(i, True)

        def count(pred):
            def body(j, cnt):
                ks = pl.multiple_of(j * TK, TK)
                return cnt + _count_rows(pred(keys_sc[pl.ds(ks, TK), :]))
            cnt8 = lax.fori_loop(0, n_chunks, body, jnp.zeros((8, TQ), I32))
            return jnp.sum(cnt8, axis=0, keepdims=True)

        def count16(ref, pred):
            one, nil = jnp.ones((), I16), jnp.zeros((), I16)

            def body(j, cnt):
                ks = pl.multiple_of(j * TK, TK)
                hit = jnp.where(pred(ref[pl.ds(ks, TK), :]), one, nil)
                for r in range(TK // 16):
                    cnt = cnt + hit[r * 16:(r + 1) * 16, :]
                return cnt
            cnt16 = lax.fori_loop(0, n_chunks, body, jnp.zeros((16, TQ), I16))
            return jnp.sum(cnt16.astype(I32), axis=0, keepdims=True)

        def select16(ref, k):
            zero = jnp.zeros((1, TQ), I16)
            t0 = jnp.where(count16(ref, lambda v: v >= zero) >= k, 0, -(2 ** 15)).astype(I32)

            def bit_body(bi, t):
                cand = t | jnp.left_shift(jnp.ones((1, TQ), I32), 14 - bi)
                cand16 = cand.astype(I16)
                return jnp.where(count16(ref, lambda v: v >= cand16) >= k, cand, t)

            return lax.fori_loop(0, 15, bit_body, t0)

        t_hi = select16(hi_sc, TOPK)
        t_hi16 = t_hi.astype(I16)
        n_above = count16(hi_sc, lambda v: v > t_hi16)
        k_lo = TOPK - n_above

        def restrict_body(j, carry):
            ks = pl.multiple_of(j * TK, TK)
            in_bucket = hi_sc[pl.ds(ks, TK), :] == t_hi16
            lo_sc[pl.ds(ks, TK), :] = jnp.where(in_bucket, lo_sc[pl.ds(ks, TK), :], jnp.full((), -(2 ** 15), I16))
            return carry

        lax.fori_loop(0, n_chunks, restrict_body, 0)
        t_lo = select16(lo_sc, k_lo)
        thr = jnp.left_shift(t_hi, 16) | (t_lo + 2 ** 15)

        t_lo16 = t_lo.astype(I16)
        n_ge = n_above + count16(lo_sc, lambda v: v >= t_lo16)
        excess = jnp.where((n_ge > TOPK) & (thr > NEG_INF_KEY), 1, 0)

        @pl.when(jnp.max(excess) > 0)
        def _break_ties():
            need = TOPK - count(lambda kk: kk > thr)
            r = lax.broadcasted_iota(I32, (TK, TK), 0)
            c = lax.broadcasted_iota(I32, (TK, TK), 1)
            strict_lower = jnp.where(c < r, 1.0, 0.0).astype(BF16)

            def body(j, seen):
                ks = pl.multiple_of(j * TK, TK)
                kk = keys_sc[pl.ds(ks, TK), :]
                tie = kk == thr
                tie_f = jnp.where(tie, 1.0, 0.0)
                before = jnp.dot(strict_lower, tie_f.astype(BF16), preferred_element_type=F32) + seen
                drop = tie & (before >= need.astype(F32))
                keys_sc[pl.ds(ks, TK), :] = jnp.where(drop, INT_MIN, kk)
                return seen + jnp.sum(tie_f, axis=0, keepdims=True)

            lax.fori_loop(0, n_chunks, body, jnp.zeros((1, TQ), F32))

        def bias_chunk(j, masked):
            ks = pl.multiple_of(j * TK, TK)
            bias = jnp.where(keys_sc[pl.ds(ks, TK), :] >= thr, 0.0, NEG_INF)
            if masked:
                bias = jnp.where(_causal(bias.shape), bias, NEG_INF)
            keys_sc[pl.ds(ks, TK), :] = pltpu.bitcast(bias, I32)

        def bias_body(j, carry):
            bias_chunk(j, False)
            return carry

        lax.fori_loop(0, i, bias_body, 0)
        bias_chunk(i, True)

    _select()
    _attend_heads(i, lambda rows, diag: pltpu.bitcast(keys_sc[rows, :], F32),
                  qT_ref, k_ref, vT_ref, s_a, s_b, acc_sc, oT_sc, o_ref)


def _dsa_attention(qT, ka, vT, iqT, iwT, small):
    B, _, _, L = qT.shape
    return pl.pallas_call(
        _dsa_kernel,
        grid=(B, L // TQ),
        in_specs=[pl.BlockSpec((1, HEADS, LANES, TQ), lambda b, i: (b, 0, 0, i))] + _head_specs(L, lambda b, i: b)
        + [pl.BlockSpec((1, PAIRS, LANES, TQ), lambda b, i: (b, 0, 0, i)),
           pl.BlockSpec((1, HEADS, TQ), lambda b, i: (b, 0, i)),
           pl.BlockSpec((1, L, LANES), lambda b, i: (b, 0, 0), pipeline_mode=pl.Buffered(1))],
        out_specs=pl.BlockSpec((1, TQ, PAIRS * LANES), lambda b, i: (b, i, 0)),
        out_shape=jax.ShapeDtypeStruct((B, L, PAIRS * LANES), BF16),
        scratch_shapes=[pltpu.VMEM((L, TQ), I32), pltpu.VMEM((L, TQ), I16), pltpu.VMEM((L, TQ), I16)]
        + _head_scratch(L),
        compiler_params=_params(("parallel", "arbitrary")),
    )(qT, ka, vT, iqT, iwT, small)


def _layer_norm(z, g, b):
    mu = jnp.mean(z, axis=-1, keepdims=True)
    zc = z - mu
    var = jnp.mean(zc * zc, axis=-1, keepdims=True)
    return zc * lax.rsqrt(var + LN_EPS) * g + b


def _merge_kernel(x_ref, oa_ref, ob_ref, ga_ref, gb_ref, wa_ref, wb_ref, wo_ref, g_ref, b_ref,
                  out_ref, *, alpha):
    a = jnp.dot(oa_ref[0], wa_ref[...], preferred_element_type=F32)
    b = jnp.dot(ob_ref[0], wb_ref[...], preferred_element_type=F32)
    n_gate_blocks = D_MODEL // LANES
    gate_a = jnp.concatenate([ga_ref[0, c] for c in range(n_gate_blocks)], axis=1).astype(F32)
    gate_b = jnp.concatenate([gb_ref[0, c] for c in range(n_gate_blocks)], axis=1).astype(F32)
    merged = jax.nn.sigmoid(gate_a) * a + jax.nn.sigmoid(gate_b) * b
    y = jnp.dot(merged.astype(BF16), wo_ref[...], preferred_element_type=F32)
    out_ref[0] = _layer_norm(alpha * x_ref[0] + y, g_ref[...], b_ref[...])


def _merge(x, oa, ob, p4, wa, wb, wo, g, b, alpha):
    B, L, D = x.shape
    tm = ROW_TILE
    nb = D // LANES
    row = lambda b_, i: (b_, i, 0)
    return pl.pallas_call(
        functools.partial(_merge_kernel, alpha=alpha),
        grid=(B, L // tm),
        in_specs=[pl.BlockSpec((1, tm, D), row),
                  pl.BlockSpec((1, tm, PAIRS * LANES), row),
                  pl.BlockSpec((1, tm, PAIRS * LANES), row),
                  pl.BlockSpec((1, nb, tm, LANES), lambda b_, i: (b_, _CB_GA // nb, i, 0)),
                  pl.BlockSpec((1, nb, tm, LANES), lambda b_, i: (b_, _CB_GB // nb, i, 0)),
                  _const_spec(wa.shape), _const_spec(wb.shape), _const_spec(wo.shape),
                  _const_spec(g.shape), _const_spec(b.shape)],
        out_specs=pl.BlockSpec((1, tm, D), row),
        out_shape=jax.ShapeDtypeStruct((B, L, D), F32),
        compiler_params=_params(("parallel", "parallel")),
    )(x, oa, ob, p4, p4, wa, wb, wo, g, b)


FF_CHUNK = 256


def _ffn_kernel(x_ref, wi_ref, wo_ref, g_ref, b_ref, out_ref, h_sc, *, alpha):
    x = x_ref[0]
    xb = x.astype(BF16)
    for c in range(D_FF // FF_CHUNK):
        lo = c * FF_CHUNK
        gate = jnp.dot(xb, wi_ref[:, lo:lo + FF_CHUNK], preferred_element_type=F32)
        up = jnp.dot(xb, wi_ref[:, D_FF + lo:D_FF + lo + FF_CHUNK], preferred_element_type=F32)
        h_sc[:, lo:lo + FF_CHUNK] = (jax.nn.silu(gate) * up).astype(BF16)
    y = jnp.dot(h_sc[...], wo_ref[...], preferred_element_type=F32)
    out_ref[0] = _layer_norm(alpha * x + y, g_ref[...], b_ref[...])


def _ffn(x, wi, wo, g, b, alpha):
    B, L, D = x.shape
    tm = ROW_TILE
    row = lambda b_, i: (b_, i, 0)
    return pl.pallas_call(
        functools.partial(_ffn_kernel, alpha=alpha),
        grid=(B, L // tm),
        in_specs=[pl.BlockSpec((1, tm, D), row),
                  _const_spec(wi.shape), _const_spec(wo.shape),
                  _const_spec(g.shape), _const_spec(b.shape)],
        out_specs=pl.BlockSpec((1, tm, D), row),
        out_shape=jax.ShapeDtypeStruct((B, L, D), F32),
        scratch_shapes=[pltpu.VMEM((tm, D_FF), BF16)],
        compiler_params=_params(("parallel", "parallel")),
    )(x, wi, wo, g, b)


def _split_w_in(w):
    fox = w[:, 0:_OFF_FLOGIT]
    dsa = w[:, _OFF_DQ:_OFF_DQ + 3 * 512]
    iq = w[:, _OFF_DQ + 3 * 512:_OFF_IK]
    gates = w[:, _OFF_GA:_N_IN]
    w_main = jnp.concatenate([fox, dsa, gates, iq], axis=1).astype(BF16)
    pad = jnp.zeros((w.shape[0], LANES - HEAD_DIM - 2 * HEADS), w.dtype)
    w_small = jnp.concatenate([w[:, _OFF_IK:_OFF_IW], w[:, _OFF_IW:_OFF_GA],
                               w[:, _OFF_FLOGIT:_OFF_DQ], pad], axis=1).astype(BF16)
    return w_main, w_small


def kernel(x, w_in, b_forget, w_branch_a, w_branch_b, w_out, ln1_g, ln1_b, w_ffn_in, w_ffn_out, ln2_g, ln2_b):
    depth = w_in.shape[0]
    alpha = (2.0 * depth) ** 0.25
    for l in range(depth):
        w_main, w_small = _split_w_in(w_in[l])
        bias_row = jnp.zeros((1, LANES), F32).at[0, _SM_FL:_SM_FL + HEADS].set(b_forget[l])
        p4, small = _project(x, w_main, w_small)
        fqT, fka, fvT = _fox_prep(p4, small, bias_row)
        dqT, dka, dvT, iqT, iwT = _dsa_prep(p4, small)
        o_a = _fox_attention(fqT, fka, fvT)
        o_b = _dsa_attention(dqT, dka, dvT, iqT, iwT, small)
        x = _merge(x, o_a, o_b, p4,
                   w_branch_a[l].astype(BF16), w_branch_b[l].astype(BF16), w_out[l].astype(BF16),
                   ln1_g[l][None, :], ln1_b[l][None, :], alpha)
        x = _ffn(x, w_ffn_in[l].astype(BF16), w_ffn_out[l].astype(BF16),
                 ln2_g[l][None, :], ln2_b[l][None, :], alpha)
    return x
```

```python
import functools

import numpy as np
import jax
import jax.numpy as jnp
from jax import lax
from jax.experimental import pallas as pl
from jax.experimental.pallas import tpu as pltpu

F32 = jnp.float32
BF16 = jnp.bfloat16
I32 = jnp.int32
I16 = jnp.int16

D_MODEL = 1024
HEADS = 8
HEAD_DIM = 64
PAIRS = HEADS // 2
D_FF = 2816
TOPK = 256
LANES = 128
NEG_INF = -1e30
LN_EPS = 1e-5
ATTN_SCALE = HEAD_DIM ** -0.5
IDX_SCALE = (HEADS ** -0.5) * (HEAD_DIM ** -0.5)

_OFF_FLOGIT = 1536
_OFF_DQ = 1544
_OFF_IK = 3592
_OFF_IW = 3656
_OFF_GA = 3664
_N_IN = 5712

_CB_FQ, _CB_FK, _CB_FV, _CB_DQ, _CB_DK, _CB_DV, _CB_GA, _CB_GB, _CB_IQ = 0, 4, 8, 12, 16, 20, 24, 32, 40
N_MAIN_BLOCKS = 44
N_MAIN = N_MAIN_BLOCKS * LANES
_SM_IK, _SM_IW, _SM_FL = 0, 64, 72

ROW_TILE = 512
TQ = 512
TK = 512
CUMSUM_BLOCK = 256
V_ROWS = HEAD_DIM + 16
LOG2E = 1.4426950408889634

INT_MIN = -(2 ** 31)


def _monotone_key_of(value):
    b = int(np.float32(value).view(np.int32))
    return b ^ ((b >> 31) & 0x7FFFFFFF)


NEG_INF_KEY = _monotone_key_of(NEG_INF)
VMEM_LIMIT = 56 * 1024 * 1024


def _params(sem):
    return pltpu.CompilerParams(dimension_semantics=sem, vmem_limit_bytes=VMEM_LIMIT)


def _const_spec(shape):
    nd = len(shape)
    return pl.BlockSpec(shape, lambda *_: (0,) * nd, pipeline_mode=pl.Buffered(1))


def _proj_kernel(x_ref, wm_ref, ws_ref, om_ref, os_ref):
    xb = x_ref[0].astype(BF16)
    os_ref[0] = jnp.dot(xb, ws_ref[...], preferred_element_type=F32)
    group = 4
    for c in range(N_MAIN_BLOCKS // group):
        w = wm_ref[:, c * group * LANES:(c + 1) * group * LANES]
        r = jnp.dot(xb, w, preferred_element_type=F32).astype(BF16)
        for g in range(group):
            om_ref[0, c * group + g] = r[:, g * LANES:(g + 1) * LANES]


def _project(x, w_main, w_small):
    B, L, D = x.shape
    tm = ROW_TILE
    return pl.pallas_call(
        _proj_kernel,
        grid=(B, L // tm),
        in_specs=[pl.BlockSpec((1, tm, D), lambda b, i: (b, i, 0)),
                  _const_spec((D, N_MAIN)),
                  _const_spec((D, LANES))],
        out_specs=[pl.BlockSpec((1, N_MAIN_BLOCKS, tm, LANES), lambda b, i: (b, 0, i, 0)),
                   pl.BlockSpec((1, tm, LANES), lambda b, i: (b, i, 0))],
        out_shape=[jax.ShapeDtypeStruct((B, N_MAIN_BLOCKS, L, LANES), BF16),
                   jax.ShapeDtypeStruct((B, L, LANES), F32)],
        compiler_params=_params(("parallel", "parallel")),
    )(x, w_main, w_small)


def _split3(v):
    hi = v.astype(BF16).astype(F32)
    r = v - hi
    mid = r.astype(BF16).astype(F32)
    lo = (r - mid).astype(BF16).astype(F32)
    return hi, mid, lo


def _cumsum_rows(x):
    n = CUMSUM_BLOCK
    L = x.shape[0]
    r = lax.broadcasted_iota(I32, (n, n), 0)
    c = lax.broadcasted_iota(I32, (n, n), 1)
    tri = jnp.where(c <= r, 1.0, 0.0).astype(BF16)
    carry = jnp.zeros((1, x.shape[1]), F32)
    outs = []
    w = x.shape[1]
    for blk in range(L // n):
        parts = jnp.concatenate([p.astype(BF16) for p in _split3(x[blk * n:(blk + 1) * n])], axis=1)
        y3 = jnp.dot(tri, parts, preferred_element_type=F32)
        y = (y3[:, :w] + y3[:, w:2 * w] + y3[:, 2 * w:]) + carry
        carry = y[n - 1:n, :]
        outs.append(y)
    return jnp.concatenate(outs, axis=0)


def _placement(entries, n_cols):
    r = lax.broadcasted_iota(I32, (LANES, n_cols), 0)
    c = lax.broadcasted_iota(I32, (LANES, n_cols), 1)
    out = jnp.zeros((LANES, n_cols), F32)
    for src, dst, coef in entries:
        out = jnp.where((r == src) & (c == dst), coef, out)
    return out.astype(BF16)


def _lane_row(lo, hi, value=1.0):
    lane = lax.broadcasted_iota(I32, (1, LANES), 1)
    return jnp.where((lane >= lo) & (lane < hi), value, 0.0)


def _store_value_chunks(ref, v_pair):
    L = v_pair.shape[0]
    vT = v_pair.astype(F32).T.astype(BF16)
    ones = jnp.ones((V_ROWS - HEAD_DIM, L), BF16)
    for e in range(2):
        rows = jnp.concatenate([vT[e * HEAD_DIM:(e + 1) * HEAD_DIM], ones], axis=0)
        for j in range(L // TK):
            ref[0, e, j] = rows[:, j * TK:(j + 1) * TK]


def _fox_prep_kernel(q_ref, k_ref, v_ref, s_ref, bias_ref, qT_ref, ka_ref, vT_ref, parts_sc):
    p = pl.program_id(1)
    L = q_ref.shape[2]

    @pl.when(p == 0)
    def _cumulative_log_forget():
        z = s_ref[0] + bias_ref[...]
        c = _cumsum_rows(jax.nn.log_sigmoid(z))
        hi, mid, lo = _split3(c)
        lane = lax.broadcasted_iota(I32, (L, LANES), 1)
        packed = jnp.where(lane < _SM_FL + HEADS, hi,
                           jnp.where(lane < _SM_FL + 2 * HEADS, pltpu.roll(mid, HEADS, 1), pltpu.roll(lo, 2 * HEADS, 1)))
        parts_sc[...] = packed.astype(BF16)

    q_entries, k_entries = [], []
    for e in range(2):
        base = (1 - e) * HEAD_DIM
        for part in range(3):
            src = _SM_FL + part * HEADS + 2 * p + e
            q_entries.append((src, e * LANES + base + part, 1.0))
            k_entries.append((src, (2 + e) * LANES + base + 3 + part, -1.0))
    aug = jnp.dot(parts_sc[...], _placement(q_entries + k_entries, 4 * LANES), preferred_element_type=F32)
    q = q_ref[0, 0].astype(F32) * ATTN_SCALE
    k = k_ref[0, 0].astype(F32)
    for e in range(2):
        data = _lane_row(e * HEAD_DIM, (e + 1) * HEAD_DIM)
        base = (1 - e) * HEAD_DIM
        q_aug = q * data + aug[:, e * LANES:(e + 1) * LANES] + _lane_row(base + 3, base + 6)
        k_aug = k * data + aug[:, (2 + e) * LANES:(3 + e) * LANES] + _lane_row(base, base + 3)
        qT_ref[0, e] = q_aug.T.astype(BF16)
        ka_ref[0, e] = k_aug.astype(BF16)
    _store_value_chunks(vT_ref, v_ref[0, 0])


def _fox_prep(p4, small, bias_row):
    B, _, L, _ = p4.shape
    nk = L // TK

    def pair_spec(cb):
        return pl.BlockSpec((1, 1, L, LANES), lambda b, p: (b, cb + p, 0, 0))

    return pl.pallas_call(
        _fox_prep_kernel,
        grid=(B, PAIRS),
        in_specs=[pair_spec(_CB_FQ), pair_spec(_CB_FK), pair_spec(_CB_FV),
                  pl.BlockSpec((1, L, LANES), lambda b, p: (b, 0, 0)),
                  pl.BlockSpec((1, LANES), lambda b, p: (0, 0))],
        out_specs=[pl.BlockSpec((1, 2, LANES, L), lambda b, p: (b, p, 0, 0)),
                   pl.BlockSpec((1, 2, L, LANES), lambda b, p: (b, p, 0, 0)),
                   pl.BlockSpec((1, 2, nk, V_ROWS, TK), lambda b, p: (b, p, 0, 0, 0))],
        out_shape=[jax.ShapeDtypeStruct((B, HEADS, LANES, L), BF16),
                   jax.ShapeDtypeStruct((B, HEADS, L, LANES), BF16),
                   jax.ShapeDtypeStruct((B, HEADS, nk, V_ROWS, TK), BF16)],
        scratch_shapes=[pltpu.VMEM((L, LANES), BF16)],
        compiler_params=_params(("parallel", "arbitrary")),
    )(p4, p4, p4, small, bias_row)


def _dsa_prep_kernel(q_ref, k_ref, v_ref, iq_ref, s_ref, qT_ref, ka_ref, vT_ref, iqT_ref, iwT_ref):
    p = pl.program_id(1)
    L = q_ref.shape[2]
    lane = lax.broadcasted_iota(I32, (L, LANES), 1)
    pos = lax.broadcasted_iota(I32, (L, LANES), 0)
    pos_terms = jnp.where(lane == 0, lax.shift_right_logical(pos, 6),
                          jnp.where(lane == 1, pos & 63, jnp.where(lane == 2, 1, 0))).astype(F32).astype(BF16)
    entries = []
    for e in range(2):
        h = jnp.full((1, 4 * LANES), 2 * p + e, I32)
        slope = lax.shift_left(jnp.ones((1, 4 * LANES), I32), 7 - h).astype(F32) * (1.0 / 256.0)
        qc = e * LANES + (1 - e) * HEAD_DIM
        kc = (2 + e) * LANES + (1 - e) * HEAD_DIM
        entries += [(0, qc, -64.0 * slope), (1, qc + 1, -slope), (2, qc + 2, 64.0 * slope), (2, qc + 3, slope),
                    (2, kc, 1.0), (2, kc + 1, 1.0), (0, kc + 2, 1.0), (1, kc + 3, 1.0)]
    aug = jnp.dot(pos_terms, _placement(entries, 4 * LANES), preferred_element_type=F32)
    q = q_ref[0, 0].astype(F32) * ATTN_SCALE
    k = k_ref[0, 0].astype(F32)
    for e in range(2):
        data = _lane_row(e * HEAD_DIM, (e + 1) * HEAD_DIM)
        q_aug = q * data + aug[:, e * LANES:(e + 1) * LANES]
        k_aug = k * data + aug[:, (2 + e) * LANES:(3 + e) * LANES]
        qT_ref[0, e] = q_aug.T.astype(BF16)
        ka_ref[0, e] = k_aug.astype(BF16)
    _store_value_chunks(vT_ref, v_ref[0, 0])
    iqT_ref[0, 0] = iq_ref[0, 0].astype(F32).T.astype(BF16)
    sT = s_ref[0].T
    iwT_ref[0] = sT[_SM_IW:_SM_IW + HEADS, :] * IDX_SCALE


def _dsa_prep(p4, small):
    B, _, L, _ = p4.shape
    nk = L // TK

    def pair_spec(cb):
        return pl.BlockSpec((1, 1, L, LANES), lambda b, p: (b, cb + p, 0, 0))

    return pl.pallas_call(
        _dsa_prep_kernel,
        grid=(B, PAIRS),
        in_specs=[pair_spec(_CB_DQ), pair_spec(_CB_DK), pair_spec(_CB_DV), pair_spec(_CB_IQ),
                  pl.BlockSpec((1, L, LANES), lambda b, p: (b, 0, 0))],
        out_specs=[pl.BlockSpec((1, 2, LANES, L), lambda b, p: (b, p, 0, 0)),
                   pl.BlockSpec((1, 2, L, LANES), lambda b, p: (b, p, 0, 0)),
                   pl.BlockSpec((1, 2, nk, V_ROWS, TK), lambda b, p: (b, p, 0, 0, 0)),
                   pl.BlockSpec((1, 1, LANES, L), lambda b, p: (b, p, 0, 0)),
                   pl.BlockSpec((1, HEADS, L), lambda b, p: (b, 0, 0))],
        out_shape=[jax.ShapeDtypeStruct((B, HEADS, LANES, L), BF16),
                   jax.ShapeDtypeStruct((B, HEADS, L, LANES), BF16),
                   jax.ShapeDtypeStruct((B, HEADS, nk, V_ROWS, TK), BF16),
                   jax.ShapeDtypeStruct((B, PAIRS, LANES, L), BF16),
                   jax.ShapeDtypeStruct((B, HEADS, L), F32)],
        compiler_params=_params(("parallel", "arbitrary")),
    )(p4, p4, p4, p4, small)


def _fold8(x, op):
    return op(x.reshape(x.shape[0] // 8, 8, x.shape[1]), axis=0)


def _for_chunk_spans(n_chunks, fn, init):
    n_pairs = lax.shift_right_logical(n_chunks, 1)
    carry = lax.fori_loop(0, n_pairs, lambda jj, c: fn(2 * jj, 2, c), init)
    return lax.fori_loop(0, n_chunks & 1, lambda _, c: fn(n_chunks - 1, 1, c), carry)


def _causal(shape):
    return lax.broadcasted_iota(I32, shape, 0) <= lax.broadcasted_iota(I32, shape, 1)


def _attend_heads(n_full, bias_fn, qT_ref, k_ref, vT_ref, s_a, s_b, acc_sc, oT_sc, o_ref):
    def phase(score_h, s_dst, value_h, s_src, mx_prev):
        if value_h is not None:
            m = jnp.max(mx_prev, axis=0, keepdims=True)
            acc_sc[...] = jnp.zeros(acc_sc.shape, F32)

        def span(j0, n, mx, diag):
            row0 = pl.multiple_of(j0 * TK, TK)
            if score_h is not None:
                rows = pl.ds(row0, n * TK)
                s = jnp.dot(k_ref[0, score_h, rows, :], qT_ref[0, score_h], preferred_element_type=F32) * LOG2E
                bias = bias_fn(rows, diag)
                if bias is not None:
                    s = s + bias
                s_dst[rows, :] = s
                mx = jnp.maximum(mx, _fold8(s, jnp.max))
            if value_h is not None:
                for sub in range(n):
                    rows = pl.ds(pl.multiple_of(row0 + sub * TK, TK), TK)
                    p = jnp.exp2(s_src[rows, :] - m)
                    acc_sc[...] += jnp.dot(vT_ref[0, value_h, j0 + sub], p.astype(BF16),
                                           preferred_element_type=F32)
            return mx

        mx = _for_chunk_spans(n_full, lambda j0, n, mx: span(j0, n, mx, False), jnp.full((8, TQ), NEG_INF, F32))
        mx = span(n_full, 1, mx, True)
        if value_h is not None:
            oT_sc[value_h] = acc_sc[:HEAD_DIM] / acc_sc[HEAD_DIM:HEAD_DIM + 1]
        return mx

    mx_a = phase(0, s_a, None, None, None)

    def two_heads(q, mx_a):
        mx_b = phase(2 * q + 1, s_b, 2 * q, s_a, mx_a)
        return phase(2 * q + 2, s_a, 2 * q + 1, s_b, mx_b)

    mx_a = lax.fori_loop(0, PAIRS - 1, two_heads, mx_a)
    mx_b = phase(HEADS - 1, s_b, HEADS - 2, s_a, mx_a)
    phase(None, None, HEADS - 1, s_b, mx_b)
    for p in range(PAIRS):
        pair = jnp.concatenate([oT_sc[2 * p], oT_sc[2 * p + 1]], axis=0)
        o_ref[0, :, p * LANES:(p + 1) * LANES] = pair.T.astype(o_ref.dtype)


def _head_specs(L, index):
    nk = L // TK
    return [pl.BlockSpec((1, HEADS, L, LANES), lambda *g: (index(*g), 0, 0, 0), pipeline_mode=pl.Buffered(1)),
            pl.BlockSpec((1, HEADS, nk, V_ROWS, TK), lambda *g: (index(*g), 0, 0, 0, 0), pipeline_mode=pl.Buffered(1))]


def _head_scratch(L):
    return [pltpu.VMEM((L, TQ), F32), pltpu.VMEM((L, TQ), F32),
            pltpu.VMEM((V_ROWS, TQ), F32), pltpu.VMEM((HEADS, HEAD_DIM, TQ), F32)]


def _fox_kernel(qT_ref, k_ref, vT_ref, o_ref, s_a, s_b, acc_sc, oT_sc):
    i = pl.program_id(1)

    def bias_fn(rows, diag):
        return jnp.where(_causal((TK, TQ)), 0.0, NEG_INF) if diag else None

    _attend_heads(i, bias_fn, qT_ref, k_ref, vT_ref, s_a, s_b, acc_sc, oT_sc, o_ref)


def _fox_attention(qT, ka, vT):
    B, _, _, L = qT.shape
    return pl.pallas_call(
        _fox_kernel,
        grid=(B, L // TQ),
        in_specs=[pl.BlockSpec((1, HEADS, LANES, TQ), lambda b, i: (b, 0, 0, i))] + _head_specs(L, lambda b, i: b),
        out_specs=pl.BlockSpec((1, TQ, PAIRS * LANES), lambda b, i: (b, i, 0)),
        out_shape=jax.ShapeDtypeStruct((B, L, PAIRS * LANES), BF16),
        scratch_shapes=_head_scratch(L),
        compiler_params=_params(("parallel", "arbitrary")),
    )(qT, ka, vT)


def _count_rows(mask):
    return _fold8(jnp.where(mask, 1, 0).astype(I32), jnp.sum)


def _dsa_kernel(qT_ref, k_ref, vT_ref, iqT_ref, iwT_ref, s_ref, o_ref,
                keys_sc, hi_sc, lo_sc, s_a, s_b, acc_sc, oT_sc):
    i = pl.program_id(1)
    n_chunks = i + 1

    def _select():
        wts = iwT_ref[0]

        def score_chunk(j, masked):
            ks = pl.multiple_of(j * TK, TK)
            ik = s_ref[0, pl.ds(ks, TK), :][:, _SM_IK:_SM_IK + HEAD_DIM].astype(BF16)
            acc = jnp.zeros((TK, TQ), F32)
            for h in range(HEADS):
                iq = iqT_ref[0, h // 2, (h % 2) * HEAD_DIM:(h % 2 + 1) * HEAD_DIM, :]
                rel = jnp.maximum(jnp.dot(ik, iq, preferred_element_type=F32), 0.0)
                acc = acc + wts[h:h + 1, :] * rel
            if masked:
                acc = jnp.where(_causal(acc.shape), acc, NEG_INF)
            acc = jnp.where(acc == 0.0, 0.0, acc)
            bits = pltpu.bitcast(acc, I32)
            key = bits ^ (lax.shift_right_arithmetic(bits, 31) & 0x7FFFFFFF)
            keys_sc[pl.ds(ks, TK), :] = key
            hi_sc[pl.ds(ks, TK), :] = lax.shift_right_arithmetic(key, 16).astype(I16)
            lo_sc[pl.ds(ks, TK), :] = ((key & 0xFFFF) - 2 ** 15).astype(I16)

        def score_body(j, carry):
            score_chunk(j, False)
            return carry

        lax.fori_loop(0, i, score_body, 0)
        score_chunk(i, True)

        def count(pred):
            def body(j, cnt):
                ks = pl.multiple_of(j * TK, TK)
                return cnt + _count_rows(pred(keys_sc[pl.ds(ks, TK), :]))
            cnt8 = lax.fori_loop(0, n_chunks, body, jnp.zeros((8, TQ), I32))
            return jnp.sum(cnt8, axis=0, keepdims=True)

        def count16(ref, pred):
            one, nil = jnp.ones((), I16), jnp.zeros((), I16)

            def body(j, cnt):
                ks = pl.multiple_of(j * TK, TK)
                hit = jnp.where(pred(ref[pl.ds(ks, TK), :]), one, nil)
                for r in range(TK // 16):
                    cnt = cnt + hit[r * 16:(r + 1) * 16, :]
                return cnt
            cnt16 = lax.fori_loop(0, n_chunks, body, jnp.zeros((16, TQ), I16))
            return jnp.sum(cnt16.astype(I32), axis=0, keepdims=True)

        def select16(ref, k):
            zero = jnp.zeros((1, TQ), I16)
            t0 = jnp.where(count16(ref, lambda v: v >= zero) >= k, 0, -(2 ** 15)).astype(I32)

            def bit_body(bi, t):
                cand = t | jnp.left_shift(jnp.ones((1, TQ), I32), 14 - bi)
                cand16 = cand.astype(I16)
                return jnp.where(count16(ref, lambda v: v >= cand16) >= k, cand, t)

            return lax.fori_loop(0, 15, bit_body, t0)

        t_hi = select16(hi_sc, TOPK)
        t_hi16 = t_hi.astype(I16)
        n_above = count16(hi_sc, lambda v: v > t_hi16)
        k_lo = TOPK - n_above

        def restrict_body(j, carry):
            ks = pl.multiple_of(j * TK, TK)
            in_bucket = hi_sc[pl.ds(ks, TK), :] == t_hi16
            lo_sc[pl.ds(ks, TK), :] = jnp.where(in_bucket, lo_sc[pl.ds(ks, TK), :], jnp.full((), -(2 ** 15), I16))
            return carry

        lax.fori_loop(0, n_chunks, restrict_body, 0)
        t_lo = select16(lo_sc, k_lo)
        thr = jnp.left_shift(t_hi, 16) | (t_lo + 2 ** 15)

        t_lo16 = t_lo.astype(I16)
        n_ge = n_above + count16(lo_sc, lambda v: v >= t_lo16)
        excess = jnp.where((n_ge > TOPK) & (thr > NEG_INF_KEY), 1, 0)

        @pl.when(jnp.max(excess) > 0)
        def _break_ties():
            need = TOPK - count(lambda kk: kk > thr)
            r = lax.broadcasted_iota(I32, (TK, TK), 0)
            c = lax.broadcasted_iota(I32, (TK, TK), 1)
            strict_lower = jnp.where(c < r, 1.0, 0.0).astype(BF16)

            def body(j, seen):
                ks = pl.multiple_of(j * TK, TK)
                kk = keys_sc[pl.ds(ks, TK), :]
                tie = kk == thr
                tie_f = jnp.where(tie, 1.0, 0.0)
                before = jnp.dot(strict_lower, tie_f.astype(BF16), preferred_element_type=F32) + seen
                drop = tie & (before >= need.astype(F32))
                keys_sc[pl.ds(ks, TK), :] = jnp.where(drop, INT_MIN, kk)
                return seen + jnp.sum(tie_f, axis=0, keepdims=True)

            lax.fori_loop(0, n_chunks, body, jnp.zeros((1, TQ), F32))

        def bias_chunk(j, masked):
            ks = pl.multiple_of(j * TK, TK)
            bias = jnp.where(keys_sc[pl.ds(ks, TK), :] >= thr, 0.0, NEG_INF)
            if masked:
                bias = jnp.where(_causal(bias.shape), bias, NEG_INF)
            keys_sc[pl.ds(ks, TK), :] = pltpu.bitcast(bias, I32)

        def bias_body(j, carry):
            bias_chunk(j, False)
            return carry

        lax.fori_loop(0, i, bias_body, 0)
        bias_chunk(i, True)

    _select()
    _attend_heads(i, lambda rows, diag: pltpu.bitcast(keys_sc[rows, :], F32),
                  qT_ref, k_ref, vT_ref, s_a, s_b, acc_sc, oT_sc, o_ref)


def _dsa_attention(qT, ka, vT, iqT, iwT, small):
    B, _, _, L = qT.shape
    return pl.pallas_call(
        _dsa_kernel,
        grid=(B, L // TQ),
        in_specs=[pl.BlockSpec((1, HEADS, LANES, TQ), lambda b, i: (b, 0, 0, i))] + _head_specs(L, lambda b, i: b)
        + [pl.BlockSpec((1, PAIRS, LANES, TQ), lambda b, i: (b, 0, 0, i)),
           pl.BlockSpec((1, HEADS, TQ), lambda b, i: (b, 0, i)),
           pl.BlockSpec((1, L, LANES), lambda b, i: (b, 0, 0), pipeline_mode=pl.Buffered(1))],
        out_specs=pl.BlockSpec((1, TQ, PAIRS * LANES), lambda b, i: (b, i, 0)),
        out_shape=jax.ShapeDtypeStruct((B, L, PAIRS * LANES), BF16),
        scratch_shapes=[pltpu.VMEM((L, TQ), I32), pltpu.VMEM((L, TQ), I16), pltpu.VMEM((L, TQ), I16)]
        + _head_scratch(L),
        compiler_params=_params(("parallel", "arbitrary")),
    )(qT, ka, vT, iqT, iwT, small)


def _layer_norm(z, g, b):
    mu = jnp.mean(z, axis=-1, keepdims=True)
    zc = z - mu
    var = jnp.mean(zc * zc, axis=-1, keepdims=True)
    return zc * lax.rsqrt(var + LN_EPS) * g + b


def _merge_kernel(x_ref, oa_ref, ob_ref, ga_ref, gb_ref, wa_ref, wb_ref, wo_ref, g_ref, b_ref,
                  out_ref, *, alpha):
    a = jnp.dot(oa_ref[0], wa_ref[...], preferred_element_type=F32)
    b = jnp.dot(ob_ref[0], wb_ref[...], preferred_element_type=F32)
    n_gate_blocks = D_MODEL // LANES
    gate_a = jnp.concatenate([ga_ref[0, c] for c in range(n_gate_blocks)], axis=1).astype(F32)
    gate_b = jnp.concatenate([gb_ref[0, c] for c in range(n_gate_blocks)], axis=1).astype(F32)
    merged = jax.nn.sigmoid(gate_a) * a + jax.nn.sigmoid(gate_b) * b
    y = jnp.dot(merged.astype(BF16), wo_ref[...], preferred_element_type=F32)
    out_ref[0] = _layer_norm(alpha * x_ref[0] + y, g_ref[...], b_ref[...])


def _merge(x, oa, ob, p4, wa, wb, wo, g, b, alpha):
    B, L, D = x.shape
    tm = ROW_TILE
    nb = D // LANES
    row = lambda b_, i: (b_, i, 0)
    return pl.pallas_call(
        functools.partial(_merge_kernel, alpha=alpha),
        grid=(B, L // tm),
        in_specs=[pl.BlockSpec((1, tm, D), row),
                  pl.BlockSpec((1, tm, PAIRS * LANES), row),
                  pl.BlockSpec((1, tm, PAIRS * LANES), row),
                  pl.BlockSpec((1, nb, tm, LANES), lambda b_, i: (b_, _CB_GA // nb, i, 0)),
                  pl.BlockSpec((1, nb, tm, LANES), lambda b_, i: (b_, _CB_GB // nb, i, 0)),
                  _const_spec(wa.shape), _const_spec(wb.shape), _const_spec(wo.shape),
                  _const_spec(g.shape), _const_spec(b.shape)],
        out_specs=pl.BlockSpec((1, tm, D), row),
        out_shape=jax.ShapeDtypeStruct((B, L, D), F32),
        compiler_params=_params(("parallel", "parallel")),
    )(x, oa, ob, p4, p4, wa, wb, wo, g, b)


FF_CHUNK = 256


def _ffn_kernel(x_ref, wi_ref, wo_ref, g_ref, b_ref, out_ref, h_sc, *, alpha):
    x = x_ref[0]
    xb = x.astype(BF16)
    for c in range(D_FF // FF_CHUNK):
        lo = c * FF_CHUNK
        gate = jnp.dot(xb, wi_ref[:, lo:lo + FF_CHUNK], preferred_element_type=F32)
        up = jnp.dot(xb, wi_ref[:, D_FF + lo:D_FF + lo + FF_CHUNK], preferred_element_type=F32)
        h_sc[:, lo:lo + FF_CHUNK] = (jax.nn.silu(gate) * up).astype(BF16)
    y = jnp.dot(h_sc[...], wo_ref[...], preferred_element_type=F32)
    out_ref[0] = _layer_norm(alpha * x + y, g_ref[...], b_ref[...])


def _ffn(x, wi, wo, g, b, alpha):
    B, L, D = x.shape
    tm = ROW_TILE
    row = lambda b_, i: (b_, i, 0)
    return pl.pallas_call(
        functools.partial(_ffn_kernel, alpha=alpha),
        grid=(B, L // tm),
        in_specs=[pl.BlockSpec((1, tm, D), row),
                  _const_spec(wi.shape), _const_spec(wo.shape),
                  _const_spec(g.shape), _const_spec(b.shape)],
        out_specs=pl.BlockSpec((1, tm, D), row),
        out_shape=jax.ShapeDtypeStruct((B, L, D), F32),
        scratch_shapes=[pltpu.VMEM((tm, D_FF), BF16)],
        compiler_params=_params(("parallel", "parallel")),
    )(x, wi, wo, g, b)


def _split_w_in(w):
    fox = w[:, 0:_OFF_FLOGIT]
    dsa = w[:, _OFF_DQ:_OFF_DQ + 3 * 512]
    iq = w[:, _OFF_DQ + 3 * 512:_OFF_IK]
    gates = w[:, _OFF_GA:_N_IN]
    w_main = jnp.concatenate([fox, dsa, gates, iq], axis=1).astype(BF16)
    pad = jnp.zeros((w.shape[0], LANES - HEAD_DIM - 2 * HEADS), w.dtype)
    w_small = jnp.concatenate([w[:, _OFF_IK:_OFF_IW], w[:, _OFF_IW:_OFF_GA],
                               w[:, _OFF_FLOGIT:_OFF_DQ], pad], axis=1).astype(BF16)
    return w_main, w_small


def kernel(x, w_in, b_forget, w_branch_a, w_branch_b, w_out, ln1_g, ln1_b, w_ffn_in, w_ffn_out, ln2_g, ln2_b):
    depth = w_in.shape[0]
    alpha = (2.0 * depth) ** 0.25
    for l in range(depth):
        w_main, w_small = _split_w_in(w_in[l])
        bias_row = jnp.zeros((1, LANES), F32).at[0, _SM_FL:_SM_FL + HEADS].set(b_forget[l])
        p4, small = _project(x, w_main, w_small)
        fqT, fka, fvT = _fox_prep(p4, small, bias_row)
        dqT, dka, dvT, iqT, iwT = _dsa_prep(p4, small)
        o_a = _fox_attention(fqT, fka, fvT)
        o_b = _dsa_attention(dqT, dka, dvT, iqT, iwT, small)
        x = _merge(x, o_a, o_b, p4,
                   w_branch_a[l].astype(BF16), w_branch_b[l].astype(BF16), w_out[l].astype(BF16),
                   ln1_g[l][None, :], ln1_b[l][None, :], alpha)
        x = _ffn(x, w_ffn_in[l].astype(BF16), w_ffn_out[l].astype(BF16),
                 ln2_g[l][None, :], ln2_b[l][None, :], alpha)
    return x
```

```python
import functools

import numpy as np
import jax
import jax.numpy as jnp
from jax import lax
from jax.experimental import pallas as pl
from jax.experimental.pallas import tpu as pltpu

F32 = jnp.float32
BF16 = jnp.bfloat16
I32 = jnp.int32
I16 = jnp.int16

D_MODEL = 1024
HEADS = 8
HEAD_DIM = 64
PAIRS = HEADS // 2
D_FF = 2816
TOPK = 256
LANES = 128
NEG_INF = -1e30
LN_EPS = 1e-5
ATTN_SCALE = HEAD_DIM ** -0.5
IDX_SCALE = (HEADS ** -0.5) * (HEAD_DIM ** -0.5)

_OFF_FLOGIT = 1536
_OFF_DQ = 1544
_OFF_IK = 3592
_OFF_IW = 3656
_OFF_GA = 3664
_N_IN = 5712

_CB_FQ, _CB_FK, _CB_FV, _CB_DQ, _CB_DK, _CB_DV, _CB_GA, _CB_GB, _CB_IQ = 0, 4, 8, 12, 16, 20, 24, 32, 40
N_MAIN_BLOCKS = 44
N_MAIN = N_MAIN_BLOCKS * LANES
_SM_IK, _SM_IW, _SM_FL = 0, 64, 72

ROW_TILE = 512
TQ = 512
TK = 512
CUMSUM_BLOCK = 256
V_ROWS = HEAD_DIM + 16
LOG2E = 1.4426950408889634

INT_MIN = -(2 ** 31)


def _monotone_key_of(value):
    b = int(np.float32(value).view(np.int32))
    return b ^ ((b >> 31) & 0x7FFFFFFF)


NEG_INF_KEY = _monotone_key_of(NEG_INF)
VMEM_LIMIT = 60 * 1024 * 1024


def _params(sem):
    return pltpu.CompilerParams(dimension_semantics=sem, vmem_limit_bytes=VMEM_LIMIT)


def _const_spec(shape):
    nd = len(shape)
    return pl.BlockSpec(shape, lambda *_: (0,) * nd, pipeline_mode=pl.Buffered(1))


def _proj_kernel(x_ref, wm_ref, ws_ref, om_ref, os_ref):
    xb = x_ref[0].astype(BF16)
    os_ref[0] = jnp.dot(xb, ws_ref[...], preferred_element_type=F32)
    group = 4
    for c in range(N_MAIN_BLOCKS // group):
        w = wm_ref[:, c * group * LANES:(c + 1) * group * LANES]
        r = jnp.dot(xb, w, preferred_element_type=F32).astype(BF16)
        for g in range(group):
            om_ref[0, c * group + g] = r[:, g * LANES:(g + 1) * LANES]


def _project(x, w_main, w_small):
    B, L, D = x.shape
    tm = ROW_TILE
    return pl.pallas_call(
        _proj_kernel,
        grid=(B, L // tm),
        in_specs=[pl.BlockSpec((1, tm, D), lambda b, i: (b, i, 0)),
                  _const_spec((D, N_MAIN)),
                  _const_spec((D, LANES))],
        out_specs=[pl.BlockSpec((1, N_MAIN_BLOCKS, tm, LANES), lambda b, i: (b, 0, i, 0)),
                   pl.BlockSpec((1, tm, LANES), lambda b, i: (b, i, 0))],
        out_shape=[jax.ShapeDtypeStruct((B, N_MAIN_BLOCKS, L, LANES), BF16),
                   jax.ShapeDtypeStruct((B, L, LANES), F32)],
        compiler_params=_params(("parallel", "parallel")),
    )(x, w_main, w_small)


def _split3(v):
    hi = v.astype(BF16).astype(F32)
    r = v - hi
    mid = r.astype(BF16).astype(F32)
    lo = (r - mid).astype(BF16).astype(F32)
    return hi, mid, lo


def _cumsum_rows(x):
    n = CUMSUM_BLOCK
    L = x.shape[0]
    r = lax.broadcasted_iota(I32, (n, n), 0)
    c = lax.broadcasted_iota(I32, (n, n), 1)
    tri = jnp.where(c <= r, 1.0, 0.0).astype(BF16)
    carry = jnp.zeros((1, x.shape[1]), F32)
    outs = []
    w = x.shape[1]
    for blk in range(L // n):
        parts = jnp.concatenate([p.astype(BF16) for p in _split3(x[blk * n:(blk + 1) * n])], axis=1)
        y3 = jnp.dot(tri, parts, preferred_element_type=F32)
        y = (y3[:, :w] + y3[:, w:2 * w] + y3[:, 2 * w:]) + carry
        carry = y[n - 1:n, :]
        outs.append(y)
    return jnp.concatenate(outs, axis=0)


def _placement(entries, n_cols):
    r = lax.broadcasted_iota(I32, (LANES, n_cols), 0)
    c = lax.broadcasted_iota(I32, (LANES, n_cols), 1)
    out = jnp.zeros((LANES, n_cols), F32)
    for src, dst, coef in entries:
        out = jnp.where((r == src) & (c == dst), coef, out)
    return out.astype(BF16)


def _lane_row(lo, hi, value=1.0):
    lane = lax.broadcasted_iota(I32, (1, LANES), 1)
    return jnp.where((lane >= lo) & (lane < hi), value, 0.0)


def _store_value_chunks(ref, v_pair):
    L = v_pair.shape[0]
    vT = v_pair.astype(F32).T.astype(BF16)
    ones = jnp.ones((V_ROWS - HEAD_DIM, L), BF16)
    for e in range(2):
        rows = jnp.concatenate([vT[e * HEAD_DIM:(e + 1) * HEAD_DIM], ones], axis=0)
        for j in range(L // TK):
            ref[0, e, j] = rows[:, j * TK:(j + 1) * TK]


def _fox_prep_kernel(q_ref, k_ref, v_ref, s_ref, bias_ref, qT_ref, ka_ref, vT_ref, parts_sc):
    p = pl.program_id(1)
    L = q_ref.shape[2]

    @pl.when(p == 0)
    def _cumulative_log_forget():
        z = s_ref[0] + bias_ref[...]
        c = _cumsum_rows(jax.nn.log_sigmoid(z))
        hi, mid, lo = _split3(c)
        lane = lax.broadcasted_iota(I32, (L, LANES), 1)
        packed = jnp.where(lane < _SM_FL + HEADS, hi,
                           jnp.where(lane < _SM_FL + 2 * HEADS, pltpu.roll(mid, HEADS, 1), pltpu.roll(lo, 2 * HEADS, 1)))
        parts_sc[...] = packed.astype(BF16)

    q_entries, k_entries = [], []
    for e in range(2):
        base = (1 - e) * HEAD_DIM
        for part in range(3):
            src = _SM_FL + part * HEADS + 2 * p + e
            q_entries.append((src, e * LANES + base + part, 1.0))
            k_entries.append((src, (2 + e) * LANES + base + 3 + part, -1.0))
    aug = jnp.dot(parts_sc[...], _placement(q_entries + k_entries, 4 * LANES), preferred_element_type=F32)
    q = q_ref[0, 0].astype(F32) * ATTN_SCALE
    k = k_ref[0, 0].astype(F32)
    for e in range(2):
        data = _lane_row(e * HEAD_DIM, (e + 1) * HEAD_DIM)
        base = (1 - e) * HEAD_DIM
        q_aug = q * data + aug[:, e * LANES:(e + 1) * LANES] + _lane_row(base + 3, base + 6)
        k_aug = k * data + aug[:, (2 + e) * LANES:(3 + e) * LANES] + _lane_row(base, base + 3)
        qT_ref[0, e] = q_aug.T.astype(BF16)
        ka_ref[0, e] = k_aug.astype(BF16)
    _store_value_chunks(vT_ref, v_ref[0, 0])


def _fox_prep(p4, small, bias_row):
    B, _, L, _ = p4.shape
    nk = L // TK

    def pair_spec(cb):
        return pl.BlockSpec((1, 1, L, LANES), lambda b, p: (b, cb + p, 0, 0))

    return pl.pallas_call(
        _fox_prep_kernel,
        grid=(B, PAIRS),
        in_specs=[pair_spec(_CB_FQ), pair_spec(_CB_FK), pair_spec(_CB_FV),
                  pl.BlockSpec((1, L, LANES), lambda b, p: (b, 0, 0)),
                  pl.BlockSpec((1, LANES), lambda b, p: (0, 0))],
        out_specs=[pl.BlockSpec((1, 2, LANES, L), lambda b, p: (b, p, 0, 0)),
                   pl.BlockSpec((1, 2, L, LANES), lambda b, p: (b, p, 0, 0)),
                   pl.BlockSpec((1, 2, nk, V_ROWS, TK), lambda b, p: (b, p, 0, 0, 0))],
        out_shape=[jax.ShapeDtypeStruct((B, HEADS, LANES, L), BF16),
                   jax.ShapeDtypeStruct((B, HEADS, L, LANES), BF16),
                   jax.ShapeDtypeStruct((B, HEADS, nk, V_ROWS, TK), BF16)],
        scratch_shapes=[pltpu.VMEM((L, LANES), BF16)],
        compiler_params=_params(("parallel", "arbitrary")),
    )(p4, p4, p4, small, bias_row)


def _dsa_prep_kernel(q_ref, k_ref, v_ref, iq_ref, s_ref, qT_ref, ka_ref, vT_ref, iqT_ref, iwT_ref):
    p = pl.program_id(1)
    L = q_ref.shape[2]
    lane = lax.broadcasted_iota(I32, (L, LANES), 1)
    pos = lax.broadcasted_iota(I32, (L, LANES), 0)
    pos_terms = jnp.where(lane == 0, lax.shift_right_logical(pos, 6),
                          jnp.where(lane == 1, pos & 63, jnp.where(lane == 2, 1, 0))).astype(F32).astype(BF16)
    entries = []
    for e in range(2):
        h = jnp.full((1, 4 * LANES), 2 * p + e, I32)
        slope = lax.shift_left(jnp.ones((1, 4 * LANES), I32), 7 - h).astype(F32) * (1.0 / 256.0)
        qc = e * LANES + (1 - e) * HEAD_DIM
        kc = (2 + e) * LANES + (1 - e) * HEAD_DIM
        entries += [(0, qc, -64.0 * slope), (1, qc + 1, -slope), (2, qc + 2, 64.0 * slope), (2, qc + 3, slope),
                    (2, kc, 1.0), (2, kc + 1, 1.0), (0, kc + 2, 1.0), (1, kc + 3, 1.0)]
    aug = jnp.dot(pos_terms, _placement(entries, 4 * LANES), preferred_element_type=F32)
    q = q_ref[0, 0].astype(F32) * ATTN_SCALE
    k = k_ref[0, 0].astype(F32)
    for e in range(2):
        data = _lane_row(e * HEAD_DIM, (e + 1) * HEAD_DIM)
        q_aug = q * data + aug[:, e * LANES:(e + 1) * LANES]
        k_aug = k * data + aug[:, (2 + e) * LANES:(3 + e) * LANES]
        qT_ref[0, e] = q_aug.T.astype(BF16)
        ka_ref[0, e] = k_aug.astype(BF16)
    _store_value_chunks(vT_ref, v_ref[0, 0])
    iqT_ref[0, 0] = iq_ref[0, 0].astype(F32).T.astype(BF16)
    sT = s_ref[0].T
    iwT_ref[0] = sT[_SM_IW:_SM_IW + HEADS, :] * IDX_SCALE


def _dsa_prep(p4, small):
    B, _, L, _ = p4.shape
    nk = L // TK

    def pair_spec(cb):
        return pl.BlockSpec((1, 1, L, LANES), lambda b, p: (b, cb + p, 0, 0))

    return pl.pallas_call(
        _dsa_prep_kernel,
        grid=(B, PAIRS),
        in_specs=[pair_spec(_CB_DQ), pair_spec(_CB_DK), pair_spec(_CB_DV), pair_spec(_CB_IQ),
                  pl.BlockSpec((1, L, LANES), lambda b, p: (b, 0, 0))],
        out_specs=[pl.BlockSpec((1, 2, LANES, L), lambda b, p: (b, p, 0, 0)),
                   pl.BlockSpec((1, 2, L, LANES), lambda b, p: (b, p, 0, 0)),
                   pl.BlockSpec((1, 2, nk, V_ROWS, TK), lambda b, p: (b, p, 0, 0, 0)),
                   pl.BlockSpec((1, 1, LANES, L), lambda b, p: (b, p, 0, 0)),
                   pl.BlockSpec((1, HEADS, L), lambda b, p: (b, 0, 0))],
        out_shape=[jax.ShapeDtypeStruct((B, HEADS, LANES, L), BF16),
                   jax.ShapeDtypeStruct((B, HEADS, L, LANES), BF16),
                   jax.ShapeDtypeStruct((B, HEADS, nk, V_ROWS, TK), BF16),
                   jax.ShapeDtypeStruct((B, PAIRS, LANES, L), BF16),
                   jax.ShapeDtypeStruct((B, HEADS, L), F32)],
        compiler_params=_params(("parallel", "arbitrary")),
    )(p4, p4, p4, p4, small)


def _fold8(x, op):
    return op(x.reshape(x.shape[0] // 8, 8, x.shape[1]), axis=0)


def _for_chunk_spans(n_chunks, fn, init):
    n_pairs = lax.shift_right_logical(n_chunks, 1)
    carry = lax.fori_loop(0, n_pairs, lambda jj, c: fn(2 * jj, 2, c), init)
    return lax.fori_loop(0, n_chunks & 1, lambda _, c: fn(n_chunks - 1, 1, c), carry)


def _causal(shape):
    return lax.broadcasted_iota(I32, shape, 0) <= lax.broadcasted_iota(I32, shape, 1)


def _attend_heads(n_full, bias_fn, qT_ref, k_ref, vT_ref, s_a, s_b, acc_sc, oT_sc, o_ref):
    def phase(score_h, s_dst, value_h, s_src, mx_prev):
        if value_h is not None:
            m = jnp.max(mx_prev, axis=0, keepdims=True)
            acc_sc[...] = jnp.zeros(acc_sc.shape, F32)

        def span(j0, n, mx, diag):
            row0 = pl.multiple_of(j0 * TK, TK)
            if score_h is not None:
                rows = pl.ds(row0, n * TK)
                s = jnp.dot(k_ref[0, score_h, rows, :], qT_ref[0, score_h], preferred_element_type=F32) * LOG2E
                bias = bias_fn(rows, diag)
                if bias is not None:
                    s = s + bias
                s_dst[rows, :] = s
                mx = jnp.maximum(mx, _fold8(s, jnp.max))
            if value_h is not None:
                for sub in range(n):
                    rows = pl.ds(pl.multiple_of(row0 + sub * TK, TK), TK)
                    p = jnp.exp2(s_src[rows, :] - m)
                    acc_sc[...] += jnp.dot(vT_ref[0, value_h, j0 + sub], p.astype(BF16),
                                           preferred_element_type=F32)
            return mx

        mx = _for_chunk_spans(n_full, lambda j0, n, mx: span(j0, n, mx, False), jnp.full((8, TQ), NEG_INF, F32))
        mx = span(n_full, 1, mx, True)
        if value_h is not None:
            oT_sc[value_h] = acc_sc[:HEAD_DIM] / acc_sc[HEAD_DIM:HEAD_DIM + 1]
        return mx

    mx_a = phase(0, s_a, None, None, None)

    def two_heads(q, mx_a):
        mx_b = phase(2 * q + 1, s_b, 2 * q, s_a, mx_a)
        return phase(2 * q + 2, s_a, 2 * q + 1, s_b, mx_b)

    mx_a = lax.fori_loop(0, PAIRS - 1, two_heads, mx_a)
    mx_b = phase(HEADS - 1, s_b, HEADS - 2, s_a, mx_a)
    phase(None, None, HEADS - 1, s_b, mx_b)
    for p in range(PAIRS):
        pair = jnp.concatenate([oT_sc[2 * p], oT_sc[2 * p + 1]], axis=0)
        o_ref[0, :, p * LANES:(p + 1) * LANES] = pair.T.astype(o_ref.dtype)


L_SUM_MAX = 1e37


def _attend_heads_single_sweep(n_full, bias_fn, ref_score_fn, qT_ref, k_ref, vT_ref, g_sc, acc8_sc, o_ref):
    for h in range(HEADS):
        g_sc[h] = ref_score_fn(h)
    acc8_sc[...] = jnp.zeros(acc8_sc.shape, F32)

    def chunk(j, diag):
        rows = pl.ds(pl.multiple_of(j * TK, TK), TK)
        for h in range(HEADS):
            s = jnp.dot(k_ref[0, h, rows, :], qT_ref[0, h], preferred_element_type=F32) * LOG2E
            bias = bias_fn(rows, diag)
            if bias is not None:
                s = s + bias
            p = jnp.exp2(s - g_sc[h]).astype(BF16)
            acc8_sc[h] += jnp.dot(vT_ref[0, h, j], p, preferred_element_type=F32)

    def body(j, carry):
        chunk(j, False)
        return carry

    lax.fori_loop(0, n_full, body, 0)
    chunk(n_full, True)
    bad = jnp.zeros((1, TQ), I32)
    for p in range(PAIRS):
        halves = []
        for e in range(2):
            acc = acc8_sc[2 * p + e]
            l = acc[HEAD_DIM:HEAD_DIM + 1]
            bad = jnp.maximum(bad, jnp.where((l > 0.0) & (l < L_SUM_MAX), 0, 1))
            halves.append(acc[:HEAD_DIM] / l)
        o_ref[0, :, p * LANES:(p + 1) * LANES] = jnp.concatenate(halves, axis=0).T.astype(o_ref.dtype)
    return jnp.max(bad) > 0


def _head_specs(L, index):
    nk = L // TK
    return [pl.BlockSpec((1, HEADS, L, LANES), lambda *g: (index(*g), 0, 0, 0), pipeline_mode=pl.Buffered(1)),
            pl.BlockSpec((1, HEADS, nk, V_ROWS, TK), lambda *g: (index(*g), 0, 0, 0, 0), pipeline_mode=pl.Buffered(1))]


def _head_scratch(L):
    return [pltpu.VMEM((HEADS, 1, TQ), F32), pltpu.VMEM((HEADS, V_ROWS, TQ), F32),
            pltpu.VMEM((L, TQ), F32), pltpu.VMEM((L, TQ), F32),
            pltpu.VMEM((V_ROWS, TQ), F32), pltpu.VMEM((HEADS, HEAD_DIM, TQ), F32)]


def _fox_kernel(qT_ref, k_ref, vT_ref, o_ref, g_sc, acc8_sc, s_a, s_b, acc_sc, oT_sc):
    i = pl.program_id(1)

    def bias_fn(rows, diag):
        return jnp.where(_causal((TK, TQ)), 0.0, NEG_INF) if diag else None

    def diagonal_score(h):
        k_own = k_ref[0, h, pl.ds(pl.multiple_of(i * TK, TK), TK), :].astype(F32)
        return jnp.sum(k_own.T * qT_ref[0, h].astype(F32), axis=0, keepdims=True) * LOG2E

    unsafe = _attend_heads_single_sweep(i, bias_fn, diagonal_score, qT_ref, k_ref, vT_ref, g_sc, acc8_sc, o_ref)

    @pl.when(unsafe)
    def _exact():
        _attend_heads(i, bias_fn, qT_ref, k_ref, vT_ref, s_a, s_b, acc_sc, oT_sc, o_ref)


def _fox_attention(qT, ka, vT):
    B, _, _, L = qT.shape
    return pl.pallas_call(
        _fox_kernel,
        grid=(B, L // TQ),
        in_specs=[pl.BlockSpec((1, HEADS, LANES, TQ), lambda b, i: (b, 0, 0, i))] + _head_specs(L, lambda b, i: b),
        out_specs=pl.BlockSpec((1, TQ, PAIRS * LANES), lambda b, i: (b, i, 0)),
        out_shape=jax.ShapeDtypeStruct((B, L, PAIRS * LANES), BF16),
        scratch_shapes=_head_scratch(L),
        compiler_params=_params(("parallel", "arbitrary")),
    )(qT, ka, vT)


def _count_rows(mask):
    return _fold8(jnp.where(mask, 1, 0).astype(I32), jnp.sum)


def _dsa_kernel(qT_ref, k_ref, vT_ref, iqT_ref, iwT_ref, s_ref, o_ref,
                keys_sc, hi_sc, lo_sc, g_sc, acc8_sc, s_a, s_b, acc_sc, oT_sc):
    i = pl.program_id(1)
    n_chunks = i + 1

    def _select():
        wts = iwT_ref[0]

        def score_chunk(j, masked):
            ks = pl.multiple_of(j * TK, TK)
            ik = s_ref[0, pl.ds(ks, TK), :][:, _SM_IK:_SM_IK + HEAD_DIM].astype(BF16)
            acc = jnp.zeros((TK, TQ), F32)
            for h in range(HEADS):
                iq = iqT_ref[0, h // 2, (h % 2) * HEAD_DIM:(h % 2 + 1) * HEAD_DIM, :]
                rel = jnp.maximum(jnp.dot(ik, iq, preferred_element_type=F32), 0.0)
                acc = acc + wts[h:h + 1, :] * rel
            if masked:
                acc = jnp.where(_causal(acc.shape), acc, NEG_INF)
            acc = jnp.where(acc == 0.0, 0.0, acc)
            bits = pltpu.bitcast(acc, I32)
            key = bits ^ (lax.shift_right_arithmetic(bits, 31) & 0x7FFFFFFF)
            keys_sc[pl.ds(ks, TK), :] = key
            hi_sc[pl.ds(ks, TK), :] = lax.shift_right_arithmetic(key, 16).astype(I16)
            lo_sc[pl.ds(ks, TK), :] = ((key & 0xFFFF) - 2 ** 15).astype(I16)

        def score_body(j, carry):
            score_chunk(j, False)
            return carry

        lax.fori_loop(0, i, score_body, 0)
        score_chunk(i, True)

        def count(pred):
            def body(j, cnt):
                ks = pl.multiple_of(j * TK, TK)
                return cnt + _count_rows(pred(keys_sc[pl.ds(ks, TK), :]))
            cnt8 = lax.fori_loop(0, n_chunks, body, jnp.zeros((8, TQ), I32))
            return jnp.sum(cnt8, axis=0, keepdims=True)

        def count16(ref, pred):
            one, nil = jnp.ones((), I16), jnp.zeros((), I16)

            def body(j, cnt):
                ks = pl.multiple_of(j * TK, TK)
                hit = jnp.where(pred(ref[pl.ds(ks, TK), :]), one, nil)
                for r in range(TK // 16):
                    cnt = cnt + hit[r * 16:(r + 1) * 16, :]
                return cnt
            cnt16 = lax.fori_loop(0, n_chunks, body, jnp.zeros((16, TQ), I16))
            return jnp.sum(cnt16.astype(I32), axis=0, keepdims=True)

        def select16(ref, k):
            zero = jnp.zeros((1, TQ), I16)
            t0 = jnp.where(count16(ref, lambda v: v >= zero) >= k, 0, -(2 ** 15)).astype(I32)

            def bit_body(bi, t):
                cand = t | jnp.left_shift(jnp.ones((1, TQ), I32), 14 - bi)
                cand16 = cand.astype(I16)
                return jnp.where(count16(ref, lambda v: v >= cand16) >= k, cand, t)

            return lax.fori_loop(0, 15, bit_body, t0)

        t_hi = select16(hi_sc, TOPK)
        t_hi16 = t_hi.astype(I16)
        n_above = count16(hi_sc, lambda v: v > t_hi16)
        k_lo = TOPK - n_above

        def restrict_body(j, carry):
            ks = pl.multiple_of(j * TK, TK)
            in_bucket = hi_sc[pl.ds(ks, TK), :] == t_hi16
            lo_sc[pl.ds(ks, TK), :] = jnp.where(in_bucket, lo_sc[pl.ds(ks, TK), :], jnp.full((), -(2 ** 15), I16))
            return carry

        lax.fori_loop(0, n_chunks, restrict_body, 0)
        t_lo = select16(lo_sc, k_lo)
        thr = jnp.left_shift(t_hi, 16) | (t_lo + 2 ** 15)

        t_lo16 = t_lo.astype(I16)
        n_ge = n_above + count16(lo_sc, lambda v: v >= t_lo16)
        excess = jnp.where((n_ge > TOPK) & (thr > NEG_INF_KEY), 1, 0)

        @pl.when(jnp.max(excess) > 0)
        def _break_ties():
            need = TOPK - count(lambda kk: kk > thr)
            r = lax.broadcasted_iota(I32, (TK, TK), 0)
            c = lax.broadcasted_iota(I32, (TK, TK), 1)
            strict_lower = jnp.where(c < r, 1.0, 0.0).astype(BF16)

            def body(j, seen):
                ks = pl.multiple_of(j * TK, TK)
                kk = keys_sc[pl.ds(ks, TK), :]
                tie = kk == thr
                tie_f = jnp.where(tie, 1.0, 0.0)
                before = jnp.dot(strict_lower, tie_f.astype(BF16), preferred_element_type=F32) + seen
                drop = tie & (before >= need.astype(F32))
                keys_sc[pl.ds(ks, TK), :] = jnp.where(drop, INT_MIN, kk)
                return seen + jnp.sum(tie_f, axis=0, keepdims=True)

            lax.fori_loop(0, n_chunks, body, jnp.zeros((1, TQ), F32))

        def bias_chunk(j, masked, near8):
            ks = pl.multiple_of(j * TK, TK)
            bias = jnp.where(keys_sc[pl.ds(ks, TK), :] >= thr, 0.0, NEG_INF)
            if masked:
                bias = jnp.where(_causal(bias.shape), bias, NEG_INF)
            keys_sc[pl.ds(ks, TK), :] = pltpu.bitcast(bias, I32)
            position = lax.broadcasted_iota(I32, (TK, TQ), 0) + ks
            return jnp.maximum(near8, _fold8(jnp.where(bias == 0.0, position, -1), jnp.max))

        near8 = lax.fori_loop(0, i, lambda j, n8: bias_chunk(j, False, n8), jnp.full((8, TQ), -1, I32))
        return jnp.max(bias_chunk(i, True, near8), axis=0, keepdims=True)

    nearest = _select()

    def stored_bias(rows, diag):
        return pltpu.bitcast(keys_sc[rows, :], F32)

    def nearest_key_bias(h):
        query = i * TQ + lax.broadcasted_iota(I32, (1, TQ), 1)
        return (query - nearest).astype(F32) * (-(2.0 ** -(h + 1)) * LOG2E)

    unsafe = _attend_heads_single_sweep(i, stored_bias, nearest_key_bias, qT_ref, k_ref, vT_ref, g_sc, acc8_sc, o_ref)

    @pl.when(unsafe)
    def _exact():
        _attend_heads(i, stored_bias, qT_ref, k_ref, vT_ref, s_a, s_b, acc_sc, oT_sc, o_ref)


def _dsa_attention(qT, ka, vT, iqT, iwT, small):
    B, _, _, L = qT.shape
    return pl.pallas_call(
        _dsa_kernel,
        grid=(B, L // TQ),
        in_specs=[pl.BlockSpec((1, HEADS, LANES, TQ), lambda b, i: (b, 0, 0, i))] + _head_specs(L, lambda b, i: b)
        + [pl.BlockSpec((1, PAIRS, LANES, TQ), lambda b, i: (b, 0, 0, i)),
           pl.BlockSpec((1, HEADS, TQ), lambda b, i: (b, 0, i)),
           pl.BlockSpec((1, L, LANES), lambda b, i: (b, 0, 0), pipeline_mode=pl.Buffered(1))],
        out_specs=pl.BlockSpec((1, TQ, PAIRS * LANES), lambda b, i: (b, i, 0)),
        out_shape=jax.ShapeDtypeStruct((B, L, PAIRS * LANES), BF16),
        scratch_shapes=[pltpu.VMEM((L, TQ), I32), pltpu.VMEM((L, TQ), I16), pltpu.VMEM((L, TQ), I16)]
        + _head_scratch(L),
        compiler_params=_params(("parallel", "arbitrary")),
    )(qT, ka, vT, iqT, iwT, small)


def _layer_norm(z, g, b):
    mu = jnp.mean(z, axis=-1, keepdims=True)
    zc = z - mu
    var = jnp.mean(zc * zc, axis=-1, keepdims=True)
    return zc * lax.rsqrt(var + LN_EPS) * g + b


def _merge_kernel(x_ref, oa_ref, ob_ref, ga_ref, gb_ref, wa_ref, wb_ref, wo_ref, g_ref, b_ref,
                  out_ref, *, alpha):
    a = jnp.dot(oa_ref[0], wa_ref[...], preferred_element_type=F32)
    b = jnp.dot(ob_ref[0], wb_ref[...], preferred_element_type=F32)
    n_gate_blocks = D_MODEL // LANES
    gate_a = jnp.concatenate([ga_ref[0, c] for c in range(n_gate_blocks)], axis=1).astype(F32)
    gate_b = jnp.concatenate([gb_ref[0, c] for c in range(n_gate_blocks)], axis=1).astype(F32)
    merged = jax.nn.sigmoid(gate_a) * a + jax.nn.sigmoid(gate_b) * b
    y = jnp.dot(merged.astype(BF16), wo_ref[...], preferred_element_type=F32)
    out_ref[0] = _layer_norm(alpha * x_ref[0] + y, g_ref[...], b_ref[...])


def _merge(x, oa, ob, p4, wa, wb, wo, g, b, alpha):
    B, L, D = x.shape
    tm = ROW_TILE
    nb = D // LANES
    row = lambda b_, i: (b_, i, 0)
    return pl.pallas_call(
        functools.partial(_merge_kernel, alpha=alpha),
        grid=(B, L // tm),
        in_specs=[pl.BlockSpec((1, tm, D), row),
                  pl.BlockSpec((1, tm, PAIRS * LANES), row),
                  pl.BlockSpec((1, tm, PAIRS * LANES), row),
                  pl.BlockSpec((1, nb, tm, LANES), lambda b_, i: (b_, _CB_GA // nb, i, 0)),
                  pl.BlockSpec((1, nb, tm, LANES), lambda b_, i: (b_, _CB_GB // nb, i, 0)),
                  _const_spec(wa.shape), _const_spec(wb.shape), _const_spec(wo.shape),
                  _const_spec(g.shape), _const_spec(b.shape)],
        out_specs=pl.BlockSpec((1, tm, D), row),
        out_shape=jax.ShapeDtypeStruct((B, L, D), F32),
        compiler_params=_params(("parallel", "parallel")),
    )(x, oa, ob, p4, p4, wa, wb, wo, g, b)


FF_CHUNK = 256


def _ffn_kernel(x_ref, wi_ref, wo_ref, g_ref, b_ref, out_ref, h_sc, *, alpha):
    x = x_ref[0]
    xb = x.astype(BF16)
    for c in range(D_FF // FF_CHUNK):
        lo = c * FF_CHUNK
        gate = jnp.dot(xb, wi_ref[:, lo:lo + FF_CHUNK], preferred_element_type=F32)
        up = jnp.dot(xb, wi_ref[:, D_FF + lo:D_FF + lo + FF_CHUNK], preferred_element_type=F32)
        h_sc[:, lo:lo + FF_CHUNK] = (jax.nn.silu(gate) * up).astype(BF16)
    y = jnp.dot(h_sc[...], wo_ref[...], preferred_element_type=F32)
    out_ref[0] = _layer_norm(alpha * x + y, g_ref[...], b_ref[...])


def _ffn(x, wi, wo, g, b, alpha):
    B, L, D = x.shape
    tm = ROW_TILE
    row = lambda b_, i: (b_, i, 0)
    return pl.pallas_call(
        functools.partial(_ffn_kernel, alpha=alpha),
        grid=(B, L // tm),
        in_specs=[pl.BlockSpec((1, tm, D), row),
                  _const_spec(wi.shape), _const_spec(wo.shape),
                  _const_spec(g.shape), _const_spec(b.shape)],
        out_specs=pl.BlockSpec((1, tm, D), row),
        out_shape=jax.ShapeDtypeStruct((B, L, D), F32),
        scratch_shapes=[pltpu.VMEM((tm, D_FF), BF16)],
        compiler_params=_params(("parallel", "parallel")),
    )(x, wi, wo, g, b)


def _split_w_in(w):
    fox = w[:, 0:_OFF_FLOGIT]
    dsa = w[:, _OFF_DQ:_OFF_DQ + 3 * 512]
    iq = w[:, _OFF_DQ + 3 * 512:_OFF_IK]
    gates = w[:, _OFF_GA:_N_IN]
    w_main = jnp.concatenate([fox, dsa, gates, iq], axis=1).astype(BF16)
    pad = jnp.zeros((w.shape[0], LANES - HEAD_DIM - 2 * HEADS), w.dtype)
    w_small = jnp.concatenate([w[:, _OFF_IK:_OFF_IW], w[:, _OFF_IW:_OFF_GA],
                               w[:, _OFF_FLOGIT:_OFF_DQ], pad], axis=1).astype(BF16)
    return w_main, w_small


def kernel(x, w_in, b_forget, w_branch_a, w_branch_b, w_out, ln1_g, ln1_b, w_ffn_in, w_ffn_out, ln2_g, ln2_b):
    depth = w_in.shape[0]
    alpha = (2.0 * depth) ** 0.25
    for l in range(depth):
        w_main, w_small = _split_w_in(w_in[l])
        bias_row = jnp.zeros((1, LANES), F32).at[0, _SM_FL:_SM_FL + HEADS].set(b_forget[l])
        p4, small = _project(x, w_main, w_small)
        fqT, fka, fvT = _fox_prep(p4, small, bias_row)
        dqT, dka, dvT, iqT, iwT = _dsa_prep(p4, small)
        o_a = _fox_attention(fqT, fka, fvT)
        o_b = _dsa_attention(dqT, dka, dvT, iqT, iwT, small)
        x = _merge(x, o_a, o_b, p4,
                   w_branch_a[l].astype(BF16), w_branch_b[l].astype(BF16), w_out[l].astype(BF16),
                   ln1_g[l][None, :], ln1_b[l][None, :], alpha)
        x = _ffn(x, w_ffn_in[l].astype(BF16), w_ffn_out[l].astype(BF16),
                 ln2_g[l][None, :], ln2_b[l][None, :], alpha)
    return x
```

```python
import functools

import numpy as np
import jax
import jax.numpy as jnp
from jax import lax
from jax.experimental import pallas as pl
from jax.experimental.pallas import tpu as pltpu

F32 = jnp.float32
BF16 = jnp.bfloat16
I32 = jnp.int32
I16 = jnp.int16

D_MODEL = 1024
HEADS = 8
HEAD_DIM = 64
PAIRS = HEADS // 2
D_FF = 2816
TOPK = 256
LANES = 128
NEG_INF = -1e30
LN_EPS = 1e-5
ATTN_SCALE = HEAD_DIM ** -0.5
IDX_SCALE = (HEADS ** -0.5) * (HEAD_DIM ** -0.5)

_OFF_FLOGIT = 1536
_OFF_DQ = 1544
_OFF_IK = 3592
_OFF_IW = 3656
_OFF_GA = 3664
_N_IN = 5712

_CB_FQ, _CB_FK, _CB_FV, _CB_DQ, _CB_DK, _CB_DV, _CB_GA, _CB_GB, _CB_IQ = 0, 4, 8, 12, 16, 20, 24, 32, 40
N_MAIN_BLOCKS = 44
N_MAIN = N_MAIN_BLOCKS * LANES
_SM_IK, _SM_IW, _SM_FL = 0, 64, 72

ROW_TILE = 512
TQ = 512
TK = 512
CUMSUM_BLOCK = 256
V_ROWS = HEAD_DIM + 16
LOG2E = 1.4426950408889634

INT_MIN = -(2 ** 31)


def _monotone_key_of(value):
    b = int(np.float32(value).view(np.int32))
    return b ^ ((b >> 31) & 0x7FFFFFFF)


NEG_INF_KEY = _monotone_key_of(NEG_INF)
VMEM_LIMIT = 60 * 1024 * 1024


def _params(sem):
    return pltpu.CompilerParams(dimension_semantics=sem, vmem_limit_bytes=VMEM_LIMIT)


def _const_spec(shape):
    nd = len(shape)
    return pl.BlockSpec(shape, lambda *_: (0,) * nd, pipeline_mode=pl.Buffered(1))


def _proj_kernel(x_ref, wm_ref, ws_ref, om_ref, os_ref):
    xb = x_ref[0].astype(BF16)
    os_ref[0] = jnp.dot(xb, ws_ref[...], preferred_element_type=F32)
    group = 4
    for c in range(N_MAIN_BLOCKS // group):
        w = wm_ref[:, c * group * LANES:(c + 1) * group * LANES]
        r = jnp.dot(xb, w, preferred_element_type=F32).astype(BF16)
        for g in range(group):
            om_ref[0, c * group + g] = r[:, g * LANES:(g + 1) * LANES]


def _project(x, w_main, w_small):
    B, L, D = x.shape
    tm = ROW_TILE
    return pl.pallas_call(
        _proj_kernel,
        grid=(B, L // tm),
        in_specs=[pl.BlockSpec((1, tm, D), lambda b, i: (b, i, 0)),
                  _const_spec((D, N_MAIN)),
                  _const_spec((D, LANES))],
        out_specs=[pl.BlockSpec((1, N_MAIN_BLOCKS, tm, LANES), lambda b, i: (b, 0, i, 0)),
                   pl.BlockSpec((1, tm, LANES), lambda b, i: (b, i, 0))],
        out_shape=[jax.ShapeDtypeStruct((B, N_MAIN_BLOCKS, L, LANES), BF16),
                   jax.ShapeDtypeStruct((B, L, LANES), F32)],
        compiler_params=_params(("parallel", "parallel")),
    )(x, w_main, w_small)


def _split3(v):
    hi = v.astype(BF16).astype(F32)
    r = v - hi
    mid = r.astype(BF16).astype(F32)
    lo = (r - mid).astype(BF16).astype(F32)
    return hi, mid, lo


def _cumsum_rows(x):
    n = CUMSUM_BLOCK
    L = x.shape[0]
    r = lax.broadcasted_iota(I32, (n, n), 0)
    c = lax.broadcasted_iota(I32, (n, n), 1)
    tri = jnp.where(c <= r, 1.0, 0.0).astype(BF16)
    carry = jnp.zeros((1, x.shape[1]), F32)
    outs = []
    w = x.shape[1]
    for blk in range(L // n):
        parts = jnp.concatenate([p.astype(BF16) for p in _split3(x[blk * n:(blk + 1) * n])], axis=1)
        y3 = jnp.dot(tri, parts, preferred_element_type=F32)
        y = (y3[:, :w] + y3[:, w:2 * w] + y3[:, 2 * w:]) + carry
        carry = y[n - 1:n, :]
        outs.append(y)
    return jnp.concatenate(outs, axis=0)


def _placement(entries, n_cols):
    r = lax.broadcasted_iota(I32, (LANES, n_cols), 0)
    c = lax.broadcasted_iota(I32, (LANES, n_cols), 1)
    out = jnp.zeros((LANES, n_cols), F32)
    for src, dst, coef in entries:
        out = jnp.where((r == src) & (c == dst), coef, out)
    return out.astype(BF16)


def _lane_row(lo, hi, value=1.0):
    lane = lax.broadcasted_iota(I32, (1, LANES), 1)
    return jnp.where((lane >= lo) & (lane < hi), value, 0.0)


def _store_value_chunks(ref, v_pair):
    L = v_pair.shape[0]
    vT = v_pair.astype(F32).T.astype(BF16)
    ones = jnp.ones((V_ROWS - HEAD_DIM, L), BF16)
    for e in range(2):
        rows = jnp.concatenate([vT[e * HEAD_DIM:(e + 1) * HEAD_DIM], ones], axis=0)
        for j in range(L // TK):
            ref[0, e, j] = rows[:, j * TK:(j + 1) * TK]


def _fox_prep_kernel(q_ref, k_ref, v_ref, s_ref, bias_ref, qT_ref, ka_ref, vT_ref, parts_sc):
    p = pl.program_id(1)
    L = q_ref.shape[2]

    @pl.when(p == 0)
    def _cumulative_log_forget():
        z = s_ref[0] + bias_ref[...]
        c = _cumsum_rows(jax.nn.log_sigmoid(z))
        hi, mid, lo = _split3(c)
        lane = lax.broadcasted_iota(I32, (L, LANES), 1)
        packed = jnp.where(lane < _SM_FL + HEADS, hi,
                           jnp.where(lane < _SM_FL + 2 * HEADS, pltpu.roll(mid, HEADS, 1), pltpu.roll(lo, 2 * HEADS, 1)))
        parts_sc[...] = packed.astype(BF16)

    q_entries, k_entries = [], []
    for e in range(2):
        base = (1 - e) * HEAD_DIM
        for part in range(3):
            src = _SM_FL + part * HEADS + 2 * p + e
            q_entries.append((src, e * LANES + base + part, 1.0))
            k_entries.append((src, (2 + e) * LANES + base + 3 + part, -1.0))
    aug = jnp.dot(parts_sc[...], _placement(q_entries + k_entries, 4 * LANES), preferred_element_type=F32)
    q = q_ref[0, 0].astype(F32) * ATTN_SCALE
    k = k_ref[0, 0].astype(F32)
    for e in range(2):
        data = _lane_row(e * HEAD_DIM, (e + 1) * HEAD_DIM)
        base = (1 - e) * HEAD_DIM
        q_aug = q * data + aug[:, e * LANES:(e + 1) * LANES] + _lane_row(base + 3, base + 6)
        k_aug = k * data + aug[:, (2 + e) * LANES:(3 + e) * LANES] + _lane_row(base, base + 3)
        qT_ref[0, e] = q_aug.T.astype(BF16)
        ka_ref[0, e] = k_aug.astype(BF16)
    _store_value_chunks(vT_ref, v_ref[0, 0])


def _fox_prep(p4, small, bias_row):
    B, _, L, _ = p4.shape
    nk = L // TK

    def pair_spec(cb):
        return pl.BlockSpec((1, 1, L, LANES), lambda b, p: (b, cb + p, 0, 0))

    return pl.pallas_call(
        _fox_prep_kernel,
        grid=(B, PAIRS),
        in_specs=[pair_spec(_CB_FQ), pair_spec(_CB_FK), pair_spec(_CB_FV),
                  pl.BlockSpec((1, L, LANES), lambda b, p: (b, 0, 0)),
                  pl.BlockSpec((1, LANES), lambda b, p: (0, 0))],
        out_specs=[pl.BlockSpec((1, 2, LANES, L), lambda b, p: (b, p, 0, 0)),
                   pl.BlockSpec((1, 2, L, LANES), lambda b, p: (b, p, 0, 0)),
                   pl.BlockSpec((1, 2, nk, V_ROWS, TK), lambda b, p: (b, p, 0, 0, 0))],
        out_shape=[jax.ShapeDtypeStruct((B, HEADS, LANES, L), BF16),
                   jax.ShapeDtypeStruct((B, HEADS, L, LANES), BF16),
                   jax.ShapeDtypeStruct((B, HEADS, nk, V_ROWS, TK), BF16)],
        scratch_shapes=[pltpu.VMEM((L, LANES), BF16)],
        compiler_params=_params(("parallel", "arbitrary")),
    )(p4, p4, p4, small, bias_row)


def _dsa_prep_kernel(q_ref, k_ref, v_ref, iq_ref, s_ref, qT_ref, ka_ref, vT_ref, iqT_ref, iwT_ref):
    p = pl.program_id(1)
    L = q_ref.shape[2]
    lane = lax.broadcasted_iota(I32, (L, LANES), 1)
    pos = lax.broadcasted_iota(I32, (L, LANES), 0)
    pos_terms = jnp.where(lane == 0, lax.shift_right_logical(pos, 6),
                          jnp.where(lane == 1, pos & 63, jnp.where(lane == 2, 1, 0))).astype(F32).astype(BF16)
    entries = []
    for e in range(2):
        h = jnp.full((1, 4 * LANES), 2 * p + e, I32)
        slope = lax.shift_left(jnp.ones((1, 4 * LANES), I32), 7 - h).astype(F32) * (1.0 / 256.0)
        qc = e * LANES + (1 - e) * HEAD_DIM
        kc = (2 + e) * LANES + (1 - e) * HEAD_DIM
        entries += [(0, qc, -64.0 * slope), (1, qc + 1, -slope), (2, qc + 2, 64.0 * slope), (2, qc + 3, slope),
                    (2, kc, 1.0), (2, kc + 1, 1.0), (0, kc + 2, 1.0), (1, kc + 3, 1.0)]
    aug = jnp.dot(pos_terms, _placement(entries, 4 * LANES), preferred_element_type=F32)
    q = q_ref[0, 0].astype(F32) * ATTN_SCALE
    k = k_ref[0, 0].astype(F32)
    for e in range(2):
        data = _lane_row(e * HEAD_DIM, (e + 1) * HEAD_DIM)
        q_aug = q * data + aug[:, e * LANES:(e + 1) * LANES]
        k_aug = k * data + aug[:, (2 + e) * LANES:(3 + e) * LANES]
        qT_ref[0, e] = q_aug.T.astype(BF16)
        ka_ref[0, e] = k_aug.astype(BF16)
    _store_value_chunks(vT_ref, v_ref[0, 0])
    iqT_ref[0, 0] = iq_ref[0, 0].astype(F32).T.astype(BF16)
    sT = s_ref[0].T
    iwT_ref[0] = sT[_SM_IW:_SM_IW + HEADS, :] * IDX_SCALE


def _dsa_prep(p4, small):
    B, _, L, _ = p4.shape
    nk = L // TK

    def pair_spec(cb):
        return pl.BlockSpec((1, 1, L, LANES), lambda b, p: (b, cb + p, 0, 0))

    return pl.pallas_call(
        _dsa_prep_kernel,
        grid=(B, PAIRS),
        in_specs=[pair_spec(_CB_DQ), pair_spec(_CB_DK), pair_spec(_CB_DV), pair_spec(_CB_IQ),
                  pl.BlockSpec((1, L, LANES), lambda b, p: (b, 0, 0))],
        out_specs=[pl.BlockSpec((1, 2, LANES, L), lambda b, p: (b, p, 0, 0)),
                   pl.BlockSpec((1, 2, L, LANES), lambda b, p: (b, p, 0, 0)),
                   pl.BlockSpec((1, 2, nk, V_ROWS, TK), lambda b, p: (b, p, 0, 0, 0)),
                   pl.BlockSpec((1, 1, LANES, L), lambda b, p: (b, p, 0, 0)),
                   pl.BlockSpec((1, HEADS, L), lambda b, p: (b, 0, 0))],
        out_shape=[jax.ShapeDtypeStruct((B, HEADS, LANES, L), BF16),
                   jax.ShapeDtypeStruct((B, HEADS, L, LANES), BF16),
                   jax.ShapeDtypeStruct((B, HEADS, nk, V_ROWS, TK), BF16),
                   jax.ShapeDtypeStruct((B, PAIRS, LANES, L), BF16),
                   jax.ShapeDtypeStruct((B, HEADS, L), F32)],
        compiler_params=_params(("parallel", "arbitrary")),
    )(p4, p4, p4, p4, small)


def _fold8(x, op):
    return op(x.reshape(x.shape[0] // 8, 8, x.shape[1]), axis=0)


def _for_chunk_spans(n_chunks, fn, init):
    n_pairs = lax.shift_right_logical(n_chunks, 1)
    carry = lax.fori_loop(0, n_pairs, lambda jj, c: fn(2 * jj, 2, c), init)
    return lax.fori_loop(0, n_chunks & 1, lambda _, c: fn(n_chunks - 1, 1, c), carry)


def _causal(shape):
    return lax.broadcasted_iota(I32, shape, 0) <= lax.broadcasted_iota(I32, shape, 1)


def _attend_heads(n_full, bias_fn, qT_ref, k_ref, vT_ref, s_a, s_b, acc_sc, oT_sc, o_ref):
    def phase(score_h, s_dst, value_h, s_src, mx_prev):
        if value_h is not None:
            m = jnp.max(mx_prev, axis=0, keepdims=True)
            acc_sc[...] = jnp.zeros(acc_sc.shape, F32)

        def span(j0, n, mx, diag):
            row0 = pl.multiple_of(j0 * TK, TK)
            if score_h is not None:
                rows = pl.ds(row0, n * TK)
                s = jnp.dot(k_ref[0, score_h, rows, :], qT_ref[0, score_h], preferred_element_type=F32) * LOG2E
                bias = bias_fn(rows, diag)
                if bias is not None:
                    s = s + bias
                s_dst[rows, :] = s
                mx = jnp.maximum(mx, _fold8(s, jnp.max))
            if value_h is not None:
                for sub in range(n):
                    rows = pl.ds(pl.multiple_of(row0 + sub * TK, TK), TK)
                    p = jnp.exp2(s_src[rows, :] - m)
                    acc_sc[...] += jnp.dot(vT_ref[0, value_h, j0 + sub], p.astype(BF16),
                                           preferred_element_type=F32)
            return mx

        mx = _for_chunk_spans(n_full, lambda j0, n, mx: span(j0, n, mx, False), jnp.full((8, TQ), NEG_INF, F32))
        mx = span(n_full, 1, mx, True)
        if value_h is not None:
            oT_sc[value_h] = acc_sc[:HEAD_DIM] / acc_sc[HEAD_DIM:HEAD_DIM + 1]
        return mx

    mx_a = phase(0, s_a, None, None, None)

    def two_heads(q, mx_a):
        mx_b = phase(2 * q + 1, s_b, 2 * q, s_a, mx_a)
        return phase(2 * q + 2, s_a, 2 * q + 1, s_b, mx_b)

    mx_a = lax.fori_loop(0, PAIRS - 1, two_heads, mx_a)
    mx_b = phase(HEADS - 1, s_b, HEADS - 2, s_a, mx_a)
    phase(None, None, HEADS - 1, s_b, mx_b)
    for p in range(PAIRS):
        pair = jnp.concatenate([oT_sc[2 * p], oT_sc[2 * p + 1]], axis=0)
        o_ref[0, :, p * LANES:(p + 1) * LANES] = pair.T.astype(o_ref.dtype)


L_SUM_MAX = 1e37


def _attend_heads_single_sweep(n_full, bias_fn, ref_score_fn, qT_ref, k_ref, vT_ref, g_sc, acc8_sc, o_ref):
    for h in range(HEADS):
        g_sc[h] = ref_score_fn(h)
    acc8_sc[...] = jnp.zeros(acc8_sc.shape, F32)

    def chunk(j, diag):
        rows = pl.ds(pl.multiple_of(j * TK, TK), TK)

        def raw_scores(h):
            return jnp.dot(k_ref[0, h, rows, :], qT_ref[0, h], preferred_element_type=F32)

        nxt = raw_scores(0)
        for h in range(HEADS):
            s = nxt * LOG2E
            if h + 1 < HEADS:
                nxt = raw_scores(h + 1)
            bias = bias_fn(rows, diag)
            if bias is not None:
                s = s + bias
            p = jnp.exp2(s - g_sc[h]).astype(BF16)
            acc8_sc[h] += jnp.dot(vT_ref[0, h, j], p, preferred_element_type=F32)

    def body(j, carry):
        chunk(j, False)
        return carry

    lax.fori_loop(0, n_full, body, 0)
    chunk(n_full, True)
    bad = jnp.zeros((1, TQ), I32)
    for p in range(PAIRS):
        halves = []
        for e in range(2):
            acc = acc8_sc[2 * p + e]
            l = acc[HEAD_DIM:HEAD_DIM + 1]
            bad = jnp.maximum(bad, jnp.where((l > 0.0) & (l < L_SUM_MAX), 0, 1))
            halves.append(acc[:HEAD_DIM] / l)
        o_ref[0, :, p * LANES:(p + 1) * LANES] = jnp.concatenate(halves, axis=0).T.astype(o_ref.dtype)
    return jnp.max(bad) > 0


def _head_specs(L, index):
    nk = L // TK
    return [pl.BlockSpec((1, HEADS, L, LANES), lambda *g: (index(*g), 0, 0, 0), pipeline_mode=pl.Buffered(1)),
            pl.BlockSpec((1, HEADS, nk, V_ROWS, TK), lambda *g: (index(*g), 0, 0, 0, 0), pipeline_mode=pl.Buffered(1))]


def _head_scratch(L):
    return [pltpu.VMEM((HEADS, 1, TQ), F32), pltpu.VMEM((HEADS, V_ROWS, TQ), F32),
            pltpu.VMEM((L, TQ), F32), pltpu.VMEM((L, TQ), F32),
            pltpu.VMEM((V_ROWS, TQ), F32), pltpu.VMEM((HEADS, HEAD_DIM, TQ), F32)]


def _fox_kernel(qT_ref, k_ref, vT_ref, o_ref, g_sc, acc8_sc, s_a, s_b, acc_sc, oT_sc):
    i = pl.program_id(1)

    def bias_fn(rows, diag):
        return jnp.where(_causal((TK, TQ)), 0.0, NEG_INF) if diag else None

    def diagonal_score(h):
        k_own = k_ref[0, h, pl.ds(pl.multiple_of(i * TK, TK), TK), :].astype(F32)
        return jnp.sum(k_own.T * qT_ref[0, h].astype(F32), axis=0, keepdims=True) * LOG2E

    unsafe = _attend_heads_single_sweep(i, bias_fn, diagonal_score, qT_ref, k_ref, vT_ref, g_sc, acc8_sc, o_ref)

    @pl.when(unsafe)
    def _exact():
        _attend_heads(i, bias_fn, qT_ref, k_ref, vT_ref, s_a, s_b, acc_sc, oT_sc, o_ref)


def _fox_attention(qT, ka, vT):
    B, _, _, L = qT.shape
    return pl.pallas_call(
        _fox_kernel,
        grid=(B, L // TQ),
        in_specs=[pl.BlockSpec((1, HEADS, LANES, TQ), lambda b, i: (b, 0, 0, i))] + _head_specs(L, lambda b, i: b),
        out_specs=pl.BlockSpec((1, TQ, PAIRS * LANES), lambda b, i: (b, i, 0)),
        out_shape=jax.ShapeDtypeStruct((B, L, PAIRS * LANES), BF16),
        scratch_shapes=_head_scratch(L),
        compiler_params=_params(("parallel", "arbitrary")),
    )(qT, ka, vT)


def _count_rows(mask):
    return _fold8(jnp.where(mask, 1, 0).astype(I32), jnp.sum)


def _dsa_kernel(qT_ref, k_ref, vT_ref, iqT_ref, iwT_ref, s_ref, o_ref,
                keys_sc, hi_sc, lo_sc, g_sc, acc8_sc, s_a, s_b, acc_sc, oT_sc):
    i = pl.program_id(1)
    n_chunks = i + 1

    def _select():
        wts = iwT_ref[0]

        def score_chunk(j, masked):
            ks = pl.multiple_of(j * TK, TK)
            ik = s_ref[0, pl.ds(ks, TK), :][:, _SM_IK:_SM_IK + HEAD_DIM].astype(BF16)
            acc = jnp.zeros((TK, TQ), F32)
            for h in range(HEADS):
                iq = iqT_ref[0, h // 2, (h % 2) * HEAD_DIM:(h % 2 + 1) * HEAD_DIM, :]
                rel = jnp.maximum(jnp.dot(ik, iq, preferred_element_type=F32), 0.0)
                acc = acc + wts[h:h + 1, :] * rel
            if masked:
                acc = jnp.where(_causal(acc.shape), acc, NEG_INF)
            acc = jnp.where(acc == 0.0, 0.0, acc)
            bits = pltpu.bitcast(acc, I32)
            key = bits ^ (lax.shift_right_arithmetic(bits, 31) & 0x7FFFFFFF)
            keys_sc[pl.ds(ks, TK), :] = key
            hi_sc[pl.ds(ks, TK), :] = lax.shift_right_arithmetic(key, 16).astype(I16)
            lo_sc[pl.ds(ks, TK), :] = ((key & 0xFFFF) - 2 ** 15).astype(I16)

        def score_body(j, carry):
            score_chunk(j, False)
            return carry

        lax.fori_loop(0, i, score_body, 0)
        score_chunk(i, True)

        def count(pred):
            def body(j, cnt):
                ks = pl.multiple_of(j * TK, TK)
                return cnt + _count_rows(pred(keys_sc[pl.ds(ks, TK), :]))
            cnt8 = lax.fori_loop(0, n_chunks, body, jnp.zeros((8, TQ), I32))
            return jnp.sum(cnt8, axis=0, keepdims=True)

        def count16(ref, pred):
            one, nil = jnp.ones((), I16), jnp.zeros((), I16)

            def body(j, cnt):
                ks = pl.multiple_of(j * TK, TK)
                hit = jnp.where(pred(ref[pl.ds(ks, TK), :]), one, nil)
                for r in range(TK // 16):
                    cnt = cnt + hit[r * 16:(r + 1) * 16, :]
                return cnt
            cnt16 = lax.fori_loop(0, n_chunks, body, jnp.zeros((16, TQ), I16))
            return jnp.sum(cnt16.astype(I32), axis=0, keepdims=True)

        def select16(ref, k):
            zero = jnp.zeros((1, TQ), I16)
            t0 = jnp.where(count16(ref, lambda v: v >= zero) >= k, 0, -(2 ** 15)).astype(I32)

            def bit_body(bi, t):
                cand = t | jnp.left_shift(jnp.ones((1, TQ), I32), 14 - bi)
                cand16 = cand.astype(I16)
                return jnp.where(count16(ref, lambda v: v >= cand16) >= k, cand, t)

            return lax.fori_loop(0, 15, bit_body, t0)

        t_hi = select16(hi_sc, TOPK)
        t_hi16 = t_hi.astype(I16)
        n_above = count16(hi_sc, lambda v: v > t_hi16)
        k_lo = TOPK - n_above

        def restrict_body(j, carry):
            ks = pl.multiple_of(j * TK, TK)
            in_bucket = hi_sc[pl.ds(ks, TK), :] == t_hi16
            lo_sc[pl.ds(ks, TK), :] = jnp.where(in_bucket, lo_sc[pl.ds(ks, TK), :], jnp.full((), -(2 ** 15), I16))
            return carry

        lax.fori_loop(0, n_chunks, restrict_body, 0)
        t_lo = select16(lo_sc, k_lo)
        thr = jnp.left_shift(t_hi, 16) | (t_lo + 2 ** 15)

        t_lo16 = t_lo.astype(I16)
        n_ge = n_above + count16(lo_sc, lambda v: v >= t_lo16)
        excess = jnp.where((n_ge > TOPK) & (thr > NEG_INF_KEY), 1, 0)

        @pl.when(jnp.max(excess) > 0)
        def _break_ties():
            need = TOPK - count(lambda kk: kk > thr)
            r = lax.broadcasted_iota(I32, (TK, TK), 0)
            c = lax.broadcasted_iota(I32, (TK, TK), 1)
            strict_lower = jnp.where(c < r, 1.0, 0.0).astype(BF16)

            def body(j, seen):
                ks = pl.multiple_of(j * TK, TK)
                kk = keys_sc[pl.ds(ks, TK), :]
                tie = kk == thr
                tie_f = jnp.where(tie, 1.0, 0.0)
                before = jnp.dot(strict_lower, tie_f.astype(BF16), preferred_element_type=F32) + seen
                drop = tie & (before >= need.astype(F32))
                keys_sc[pl.ds(ks, TK), :] = jnp.where(drop, INT_MIN, kk)
                return seen + jnp.sum(tie_f, axis=0, keepdims=True)

            lax.fori_loop(0, n_chunks, body, jnp.zeros((1, TQ), F32))

        def bias_chunk(j, masked, near8):
            ks = pl.multiple_of(j * TK, TK)
            bias = jnp.where(keys_sc[pl.ds(ks, TK), :] >= thr, 0.0, NEG_INF)
            if masked:
                bias = jnp.where(_causal(bias.shape), bias, NEG_INF)
            keys_sc[pl.ds(ks, TK), :] = pltpu.bitcast(bias, I32)
            position = lax.broadcasted_iota(I32, (TK, TQ), 0) + ks
            return jnp.maximum(near8, _fold8(jnp.where(bias == 0.0, position, -1), jnp.max))

        near8 = lax.fori_loop(0, i, lambda j, n8: bias_chunk(j, False, n8), jnp.full((8, TQ), -1, I32))
        return jnp.max(bias_chunk(i, True, near8), axis=0, keepdims=True)

    nearest = _select()

    def stored_bias(rows, diag):
        return pltpu.bitcast(keys_sc[rows, :], F32)

    def nearest_key_bias(h):
        query = i * TQ + lax.broadcasted_iota(I32, (1, TQ), 1)
        return (query - nearest).astype(F32) * (-(2.0 ** -(h + 1)) * LOG2E)

    unsafe = _attend_heads_single_sweep(i, stored_bias, nearest_key_bias, qT_ref, k_ref, vT_ref, g_sc, acc8_sc, o_ref)

    @pl.when(unsafe)
    def _exact():
        _attend_heads(i, stored_bias, qT_ref, k_ref, vT_ref, s_a, s_b, acc_sc, oT_sc, o_ref)


def _dsa_attention(qT, ka, vT, iqT, iwT, small):
    B, _, _, L = qT.shape
    return pl.pallas_call(
        _dsa_kernel,
        grid=(B, L // TQ),
        in_specs=[pl.BlockSpec((1, HEADS, LANES, TQ), lambda b, i: (b, 0, 0, i))] + _head_specs(L, lambda b, i: b)
        + [pl.BlockSpec((1, PAIRS, LANES, TQ), lambda b, i: (b, 0, 0, i)),
           pl.BlockSpec((1, HEADS, TQ), lambda b, i: (b, 0, i)),
           pl.BlockSpec((1, L, LANES), lambda b, i: (b, 0, 0), pipeline_mode=pl.Buffered(1))],
        out_specs=pl.BlockSpec((1, TQ, PAIRS * LANES), lambda b, i: (b, i, 0)),
        out_shape=jax.ShapeDtypeStruct((B, L, PAIRS * LANES), BF16),
        scratch_shapes=[pltpu.VMEM((L, TQ), I32), pltpu.VMEM((L, TQ), I16), pltpu.VMEM((L, TQ), I16)]
        + _head_scratch(L),
        compiler_params=_params(("parallel", "arbitrary")),
    )(qT, ka, vT, iqT, iwT, small)


def _layer_norm(z, g, b):
    mu = jnp.mean(z, axis=-1, keepdims=True)
    zc = z - mu
    var = jnp.mean(zc * zc, axis=-1, keepdims=True)
    return zc * lax.rsqrt(var + LN_EPS) * g + b


def _merge_kernel(x_ref, oa_ref, ob_ref, ga_ref, gb_ref, wa_ref, wb_ref, wo_ref, g_ref, b_ref,
                  out_ref, *, alpha):
    a = jnp.dot(oa_ref[0], wa_ref[...], preferred_element_type=F32)
    b = jnp.dot(ob_ref[0], wb_ref[...], preferred_element_type=F32)
    n_gate_blocks = D_MODEL // LANES
    gate_a = jnp.concatenate([ga_ref[0, c] for c in range(n_gate_blocks)], axis=1).astype(F32)
    gate_b = jnp.concatenate([gb_ref[0, c] for c in range(n_gate_blocks)], axis=1).astype(F32)
    merged = jax.nn.sigmoid(gate_a) * a + jax.nn.sigmoid(gate_b) * b
    y = jnp.dot(merged.astype(BF16), wo_ref[...], preferred_element_type=F32)
    out_ref[0] = _layer_norm(alpha * x_ref[0] + y, g_ref[...], b_ref[...])


def _merge(x, oa, ob, p4, wa, wb, wo, g, b, alpha):
    B, L, D = x.shape
    tm = ROW_TILE
    nb = D // LANES
    row = lambda b_, i: (b_, i, 0)
    return pl.pallas_call(
        functools.partial(_merge_kernel, alpha=alpha),
        grid=(B, L // tm),
        in_specs=[pl.BlockSpec((1, tm, D), row),
                  pl.BlockSpec((1, tm, PAIRS * LANES), row),
                  pl.BlockSpec((1, tm, PAIRS * LANES), row),
                  pl.BlockSpec((1, nb, tm, LANES), lambda b_, i: (b_, _CB_GA // nb, i, 0)),
                  pl.BlockSpec((1, nb, tm, LANES), lambda b_, i: (b_, _CB_GB // nb, i, 0)),
                  _const_spec(wa.shape), _const_spec(wb.shape), _const_spec(wo.shape),
                  _const_spec(g.shape), _const_spec(b.shape)],
        out_specs=pl.BlockSpec((1, tm, D), row),
        out_shape=jax.ShapeDtypeStruct((B, L, D), F32),
        compiler_params=_params(("parallel", "parallel")),
    )(x, oa, ob, p4, p4, wa, wb, wo, g, b)


FF_CHUNK = 256


def _ffn_kernel(x_ref, wi_ref, wo_ref, g_ref, b_ref, out_ref, h_sc, *, alpha):
    x = x_ref[0]
    xb = x.astype(BF16)
    for c in range(D_FF // FF_CHUNK):
        lo = c * FF_CHUNK
        gate = jnp.dot(xb, wi_ref[:, lo:lo + FF_CHUNK], preferred_element_type=F32)
        up = jnp.dot(xb, wi_ref[:, D_FF + lo:D_FF + lo + FF_CHUNK], preferred_element_type=F32)
        h_sc[:, lo:lo + FF_CHUNK] = (jax.nn.silu(gate) * up).astype(BF16)
    y = jnp.dot(h_sc[...], wo_ref[...], preferred_element_type=F32)
    out_ref[0] = _layer_norm(alpha * x + y, g_ref[...], b_ref[...])


def _ffn(x, wi, wo, g, b, alpha):
    B, L, D = x.shape
    tm = ROW_TILE
    row = lambda b_, i: (b_, i, 0)
    return pl.pallas_call(
        functools.partial(_ffn_kernel, alpha=alpha),
        grid=(B, L // tm),
        in_specs=[pl.BlockSpec((1, tm, D), row),
                  _const_spec(wi.shape), _const_spec(wo.shape),
                  _const_spec(g.shape), _const_spec(b.shape)],
        out_specs=pl.BlockSpec((1, tm, D), row),
        out_shape=jax.ShapeDtypeStruct((B, L, D), F32),
        scratch_shapes=[pltpu.VMEM((tm, D_FF), BF16)],
        compiler_params=_params(("parallel", "parallel")),
    )(x, wi, wo, g, b)


def _split_w_in(w):
    fox = w[:, 0:_OFF_FLOGIT]
    dsa = w[:, _OFF_DQ:_OFF_DQ + 3 * 512]
    iq = w[:, _OFF_DQ + 3 * 512:_OFF_IK]
    gates = w[:, _OFF_GA:_N_IN]
    w_main = jnp.concatenate([fox, dsa, gates, iq], axis=1).astype(BF16)
    pad = jnp.zeros((w.shape[0], LANES - HEAD_DIM - 2 * HEADS), w.dtype)
    w_small = jnp.concatenate([w[:, _OFF_IK:_OFF_IW], w[:, _OFF_IW:_OFF_GA],
                               w[:, _OFF_FLOGIT:_OFF_DQ], pad], axis=1).astype(BF16)
    return w_main, w_small


def kernel(x, w_in, b_forget, w_branch_a, w_branch_b, w_out, ln1_g, ln1_b, w_ffn_in, w_ffn_out, ln2_g, ln2_b):
    depth = w_in.shape[0]
    alpha = (2.0 * depth) ** 0.25
    for l in range(depth):
        w_main, w_small = _split_w_in(w_in[l])
        bias_row = jnp.zeros((1, LANES), F32).at[0, _SM_FL:_SM_FL + HEADS].set(b_forget[l])
        p4, small = _project(x, w_main, w_small)
        fqT, fka, fvT = _fox_prep(p4, small, bias_row)
        dqT, dka, dvT, iqT, iwT = _dsa_prep(p4, small)
        o_a = _fox_attention(fqT, fka, fvT)
        o_b = _dsa_attention(dqT, dka, dvT, iqT, iwT, small)
        x = _merge(x, o_a, o_b, p4,
                   w_branch_a[l].astype(BF16), w_branch_b[l].astype(BF16), w_out[l].astype(BF16),
                   ln1_g[l][None, :], ln1_b[l][None, :], alpha)
        x = _ffn(x, w_ffn_in[l].astype(BF16), w_ffn_out[l].astype(BF16),
                 ln2_g[l][None, :], ln2_b[l][None, :], alpha)
    return x
```

```python
import functools

import numpy as np
import jax
import jax.numpy as jnp
from jax import lax
from jax.experimental import pallas as pl
from jax.experimental.pallas import tpu as pltpu

F32 = jnp.float32
BF16 = jnp.bfloat16
I32 = jnp.int32
I16 = jnp.int16

D_MODEL = 1024
HEADS = 8
HEAD_DIM = 64
PAIRS = HEADS // 2
D_FF = 2816
TOPK = 256
LANES = 128
NEG_INF = -1e30
LN_EPS = 1e-5
ATTN_SCALE = HEAD_DIM ** -0.5
IDX_SCALE = (HEADS ** -0.5) * (HEAD_DIM ** -0.5)

_OFF_FLOGIT = 1536
_OFF_DQ = 1544
_OFF_IK = 3592
_OFF_IW = 3656
_OFF_GA = 3664
_N_IN = 5712

_CB_FQ, _CB_FK, _CB_FV, _CB_DQ, _CB_DK, _CB_DV, _CB_GA, _CB_GB, _CB_IQ = 0, 4, 8, 12, 16, 20, 24, 32, 40
N_MAIN_BLOCKS = 44
N_MAIN = N_MAIN_BLOCKS * LANES
_SM_IK, _SM_IW, _SM_FL = 0, 64, 72

ROW_TILE = 512
TQ = 512
TK = 512
CUMSUM_BLOCK = 256
V_ROWS = HEAD_DIM + 16
LOG2E = 1.4426950408889634

INT_MIN = -(2 ** 31)


def _monotone_key_of(value):
    b = int(np.float32(value).view(np.int32))
    return b ^ ((b >> 31) & 0x7FFFFFFF)


NEG_INF_KEY = _monotone_key_of(NEG_INF)
VMEM_LIMIT = 60 * 1024 * 1024


def _params(sem):
    return pltpu.CompilerParams(dimension_semantics=sem, vmem_limit_bytes=VMEM_LIMIT)


def _const_spec(shape):
    nd = len(shape)
    return pl.BlockSpec(shape, lambda *_: (0,) * nd, pipeline_mode=pl.Buffered(1))


def _proj_kernel(x_ref, wm_ref, ws_ref, om_ref, os_ref):
    xb = x_ref[0].astype(BF16)
    os_ref[0] = jnp.dot(xb, ws_ref[...], preferred_element_type=F32)
    group = 4
    for c in range(N_MAIN_BLOCKS // group):
        w = wm_ref[:, c * group * LANES:(c + 1) * group * LANES]
        r = jnp.dot(xb, w, preferred_element_type=F32).astype(BF16)
        for g in range(group):
            om_ref[0, c * group + g] = r[:, g * LANES:(g + 1) * LANES]


def _project(x, w_main, w_small):
    B, L, D = x.shape
    tm = ROW_TILE
    return pl.pallas_call(
        _proj_kernel,
        grid=(B, L // tm),
        in_specs=[pl.BlockSpec((1, tm, D), lambda b, i: (b, i, 0)),
                  _const_spec((D, N_MAIN)),
                  _const_spec((D, LANES))],
        out_specs=[pl.BlockSpec((1, N_MAIN_BLOCKS, tm, LANES), lambda b, i: (b, 0, i, 0)),
                   pl.BlockSpec((1, tm, LANES), lambda b, i: (b, i, 0))],
        out_shape=[jax.ShapeDtypeStruct((B, N_MAIN_BLOCKS, L, LANES), BF16),
                   jax.ShapeDtypeStruct((B, L, LANES), F32)],
        compiler_params=_params(("parallel", "parallel")),
    )(x, w_main, w_small)


def _split3(v):
    hi = v.astype(BF16).astype(F32)
    r = v - hi
    mid = r.astype(BF16).astype(F32)
    lo = (r - mid).astype(BF16).astype(F32)
    return hi, mid, lo


def _cumsum_rows(x):
    n = CUMSUM_BLOCK
    L = x.shape[0]
    r = lax.broadcasted_iota(I32, (n, n), 0)
    c = lax.broadcasted_iota(I32, (n, n), 1)
    tri = jnp.where(c <= r, 1.0, 0.0).astype(BF16)
    carry = jnp.zeros((1, x.shape[1]), F32)
    outs = []
    w = x.shape[1]
    for blk in range(L // n):
        parts = jnp.concatenate([p.astype(BF16) for p in _split3(x[blk * n:(blk + 1) * n])], axis=1)
        y3 = jnp.dot(tri, parts, preferred_element_type=F32)
        y = (y3[:, :w] + y3[:, w:2 * w] + y3[:, 2 * w:]) + carry
        carry = y[n - 1:n, :]
        outs.append(y)
    return jnp.concatenate(outs, axis=0)


def _placement(entries, n_cols):
    r = lax.broadcasted_iota(I32, (LANES, n_cols), 0)
    c = lax.broadcasted_iota(I32, (LANES, n_cols), 1)
    out = jnp.zeros((LANES, n_cols), F32)
    for src, dst, coef in entries:
        out = jnp.where((r == src) & (c == dst), coef, out)
    return out.astype(BF16)


def _lane_row(lo, hi, value=1.0):
    lane = lax.broadcasted_iota(I32, (1, LANES), 1)
    return jnp.where((lane >= lo) & (lane < hi), value, 0.0)


def _store_value_chunks(ref, v_pair):
    L = v_pair.shape[0]
    vT = v_pair.astype(F32).T.astype(BF16)
    ones = jnp.ones((V_ROWS - HEAD_DIM, L), BF16)
    for e in range(2):
        rows = jnp.concatenate([vT[e * HEAD_DIM:(e + 1) * HEAD_DIM], ones], axis=0)
        for j in range(L // TK):
            ref[0, e, j] = rows[:, j * TK:(j + 1) * TK]


def _fox_prep_kernel(q_ref, k_ref, v_ref, s_ref, bias_ref, qT_ref, ka_ref, vT_ref, own_ref, parts_sc):
    p = pl.program_id(1)
    L = q_ref.shape[2]

    @pl.when(p == 0)
    def _cumulative_log_forget():
        z = s_ref[0] + bias_ref[...]
        c = _cumsum_rows(jax.nn.log_sigmoid(z))
        hi, mid, lo = _split3(c)
        lane = lax.broadcasted_iota(I32, (L, LANES), 1)
        packed = jnp.where(lane < _SM_FL + HEADS, hi,
                           jnp.where(lane < _SM_FL + 2 * HEADS, pltpu.roll(mid, HEADS, 1), pltpu.roll(lo, 2 * HEADS, 1)))
        parts_sc[...] = packed.astype(BF16)

    q_entries, k_entries = [], []
    for e in range(2):
        base = (1 - e) * HEAD_DIM
        for part in range(3):
            src = _SM_FL + part * HEADS + 2 * p + e
            q_entries.append((src, e * LANES + base + part, 1.0))
            k_entries.append((src, (2 + e) * LANES + base + 3 + part, -1.0))
    aug = jnp.dot(parts_sc[...], _placement(q_entries + k_entries, 4 * LANES), preferred_element_type=F32)
    q = q_ref[0, 0].astype(F32) * ATTN_SCALE
    k = k_ref[0, 0].astype(F32)
    for e in range(2):
        data = _lane_row(e * HEAD_DIM, (e + 1) * HEAD_DIM)
        base = (1 - e) * HEAD_DIM
        q_aug = q * data + aug[:, e * LANES:(e + 1) * LANES] + _lane_row(base + 3, base + 6)
        k_aug = k * data + aug[:, (2 + e) * LANES:(3 + e) * LANES] + _lane_row(base, base + 3)
        qT_ref[0, e] = q_aug.T.astype(BF16)
        ka_ref[0, e] = k_aug.astype(BF16)
        own_ref[0, e] = jnp.sum((q_aug * k_aug).T, axis=0, keepdims=True) * LOG2E
    _store_value_chunks(vT_ref, v_ref[0, 0])


def _fox_prep(p4, small, bias_row):
    B, _, L, _ = p4.shape
    nk = L // TK

    def pair_spec(cb):
        return pl.BlockSpec((1, 1, L, LANES), lambda b, p: (b, cb + p, 0, 0))

    return pl.pallas_call(
        _fox_prep_kernel,
        grid=(B, PAIRS),
        in_specs=[pair_spec(_CB_FQ), pair_spec(_CB_FK), pair_spec(_CB_FV),
                  pl.BlockSpec((1, L, LANES), lambda b, p: (b, 0, 0)),
                  pl.BlockSpec((1, LANES), lambda b, p: (0, 0))],
        out_specs=[pl.BlockSpec((1, 2, LANES, L), lambda b, p: (b, p, 0, 0)),
                   pl.BlockSpec((1, 2, L, LANES), lambda b, p: (b, p, 0, 0)),
                   pl.BlockSpec((1, 2, nk, V_ROWS, TK), lambda b, p: (b, p, 0, 0, 0)),
                   pl.BlockSpec((1, 2, 1, L), lambda b, p: (b, p, 0, 0))],
        out_shape=[jax.ShapeDtypeStruct((B, HEADS, LANES, L), BF16),
                   jax.ShapeDtypeStruct((B, HEADS, L, LANES), BF16),
                   jax.ShapeDtypeStruct((B, HEADS, nk, V_ROWS, TK), BF16),
                   jax.ShapeDtypeStruct((B, HEADS, 1, L), F32)],
        scratch_shapes=[pltpu.VMEM((L, LANES), BF16)],
        compiler_params=_params(("parallel", "arbitrary")),
    )(p4, p4, p4, small, bias_row)


def _dsa_prep_kernel(q_ref, k_ref, v_ref, iq_ref, s_ref, qT_ref, ka_ref, vT_ref, iqT_ref, iwT_ref):
    p = pl.program_id(1)
    L = q_ref.shape[2]
    lane = lax.broadcasted_iota(I32, (L, LANES), 1)
    pos = lax.broadcasted_iota(I32, (L, LANES), 0)
    pos_terms = jnp.where(lane == 0, lax.shift_right_logical(pos, 6),
                          jnp.where(lane == 1, pos & 63, jnp.where(lane == 2, 1, 0))).astype(F32).astype(BF16)
    entries = []
    for e in range(2):
        h = jnp.full((1, 4 * LANES), 2 * p + e, I32)
        slope = lax.shift_left(jnp.ones((1, 4 * LANES), I32), 7 - h).astype(F32) * (1.0 / 256.0)
        qc = e * LANES + (1 - e) * HEAD_DIM
        kc = (2 + e) * LANES + (1 - e) * HEAD_DIM
        entries += [(0, qc, -64.0 * slope), (1, qc + 1, -slope), (2, qc + 2, 64.0 * slope), (2, qc + 3, slope),
                    (2, kc, 1.0), (2, kc + 1, 1.0), (0, kc + 2, 1.0), (1, kc + 3, 1.0)]
    aug = jnp.dot(pos_terms, _placement(entries, 4 * LANES), preferred_element_type=F32)
    q = q_ref[0, 0].astype(F32) * ATTN_SCALE
    k = k_ref[0, 0].astype(F32)
    for e in range(2):
        data = _lane_row(e * HEAD_DIM, (e + 1) * HEAD_DIM)
        q_aug = q * data + aug[:, e * LANES:(e + 1) * LANES]
        k_aug = k * data + aug[:, (2 + e) * LANES:(3 + e) * LANES]
        qT_ref[0, e] = q_aug.T.astype(BF16)
        ka_ref[0, e] = k_aug.astype(BF16)
    _store_value_chunks(vT_ref, v_ref[0, 0])
    iqT_ref[0, 0] = iq_ref[0, 0].astype(F32).T.astype(BF16)
    sT = s_ref[0].T
    iwT_ref[0] = sT[_SM_IW:_SM_IW + HEADS, :] * IDX_SCALE


def _dsa_prep(p4, small):
    B, _, L, _ = p4.shape
    nk = L // TK

    def pair_spec(cb):
        return pl.BlockSpec((1, 1, L, LANES), lambda b, p: (b, cb + p, 0, 0))

    return pl.pallas_call(
        _dsa_prep_kernel,
        grid=(B, PAIRS),
        in_specs=[pair_spec(_CB_DQ), pair_spec(_CB_DK), pair_spec(_CB_DV), pair_spec(_CB_IQ),
                  pl.BlockSpec((1, L, LANES), lambda b, p: (b, 0, 0))],
        out_specs=[pl.BlockSpec((1, 2, LANES, L), lambda b, p: (b, p, 0, 0)),
                   pl.BlockSpec((1, 2, L, LANES), lambda b, p: (b, p, 0, 0)),
                   pl.BlockSpec((1, 2, nk, V_ROWS, TK), lambda b, p: (b, p, 0, 0, 0)),
                   pl.BlockSpec((1, 1, LANES, L), lambda b, p: (b, p, 0, 0)),
                   pl.BlockSpec((1, HEADS, L), lambda b, p: (b, 0, 0))],
        out_shape=[jax.ShapeDtypeStruct((B, HEADS, LANES, L), BF16),
                   jax.ShapeDtypeStruct((B, HEADS, L, LANES), BF16),
                   jax.ShapeDtypeStruct((B, HEADS, nk, V_ROWS, TK), BF16),
                   jax.ShapeDtypeStruct((B, PAIRS, LANES, L), BF16),
                   jax.ShapeDtypeStruct((B, HEADS, L), F32)],
        compiler_params=_params(("parallel", "arbitrary")),
    )(p4, p4, p4, p4, small)


def _fold8(x, op):
    return op(x.reshape(x.shape[0] // 8, 8, x.shape[1]), axis=0)


def _causal(shape):
    return lax.broadcasted_iota(I32, shape, 0) <= lax.broadcasted_iota(I32, shape, 1)


L_SUM_MIN, L_SUM_MAX = 1e-30, 1e30


def _attend_heads(n_full, bias_fn, ref_score_fn, qT_ref, k_ref, vT_ref, g_sc, acc_sc, o_ref):
    def for_chunks(chunk):
        def body(j, carry):
            chunk(j, False)
            return carry

        lax.fori_loop(0, n_full, body, 0)
        chunk(n_full, True)

    def scores(h, rows, diag, raw):
        s = raw * LOG2E
        bias = bias_fn(rows, diag)
        return s if bias is None else s + bias

    def sweep():
        acc_sc[...] = jnp.zeros(acc_sc.shape, F32)

        def chunk(j, diag):
            rows = pl.ds(pl.multiple_of(j * TK, TK), TK)

            def raw_scores(h):
                return jnp.dot(k_ref[0, h, rows, :], qT_ref[0, h], preferred_element_type=F32)

            nxt = raw_scores(0)
            for h in range(HEADS):
                raw = nxt
                if h + 1 < HEADS:
                    nxt = raw_scores(h + 1)
                p = jnp.exp2(scores(h, rows, diag, raw) - g_sc[h]).astype(BF16)
                acc_sc[h] += jnp.dot(vT_ref[0, h, j], p, preferred_element_type=F32)

        for_chunks(chunk)
        bad = jnp.zeros((1, TQ), I32)
        for p in range(PAIRS):
            halves = []
            for e in range(2):
                acc = acc_sc[2 * p + e]
                l = acc[HEAD_DIM:HEAD_DIM + 1]
                bad = jnp.maximum(bad, jnp.where((l > L_SUM_MIN) & (l < L_SUM_MAX), 0, 1))
                halves.append(acc[:HEAD_DIM] / l)
            o_ref[0, :, p * LANES:(p + 1) * LANES] = jnp.concatenate(halves, axis=0).T.astype(o_ref.dtype)
        return jnp.max(bad) > 0

    def exact_row_maxima():
        acc_sc[:, 0:8, :] = jnp.full((HEADS, 8, TQ), NEG_INF, F32)

        def chunk(j, diag):
            rows = pl.ds(pl.multiple_of(j * TK, TK), TK)
            for h in range(HEADS):
                raw = jnp.dot(k_ref[0, h, rows, :], qT_ref[0, h], preferred_element_type=F32)
                acc_sc[h, 0:8, :] = jnp.maximum(acc_sc[h, 0:8, :], _fold8(scores(h, rows, diag, raw), jnp.max))

        for_chunks(chunk)
        for h in range(HEADS):
            g_sc[h] = jnp.max(acc_sc[h, 0:8, :], axis=0, keepdims=True)

    for h in range(HEADS):
        g_sc[h] = ref_score_fn(h)
    unsafe = sweep()

    @pl.when(unsafe)
    def _redo_with_exact_maxima():
        exact_row_maxima()
        sweep()


def _head_specs(L, index):
    nk = L // TK
    return [pl.BlockSpec((1, HEADS, L, LANES), lambda *g: (index(*g), 0, 0, 0)),
            pl.BlockSpec((1, HEADS, nk, V_ROWS, TK), lambda *g: (index(*g), 0, 0, 0, 0))]


def _head_scratch():
    return [pltpu.VMEM((HEADS, 1, TQ), F32), pltpu.VMEM((HEADS, V_ROWS, TQ), F32)]


def _fox_kernel(qT_ref, own_ref, k_ref, vT_ref, o_ref, g_sc, acc_sc):
    i = pl.program_id(1)

    def bias_fn(rows, diag):
        return jnp.where(_causal((TK, TQ)), 0.0, NEG_INF) if diag else None

    _attend_heads(i, bias_fn, lambda h: own_ref[0, h], qT_ref, k_ref, vT_ref, g_sc, acc_sc, o_ref)


def _fox_attention(qT, own, ka, vT):
    B, _, _, L = qT.shape
    return pl.pallas_call(
        _fox_kernel,
        grid=(B, L // TQ),
        in_specs=[pl.BlockSpec((1, HEADS, LANES, TQ), lambda b, i: (b, 0, 0, i)),
                  pl.BlockSpec((1, HEADS, 1, TQ), lambda b, i: (b, 0, 0, i))] + _head_specs(L, lambda b, i: b),
        out_specs=pl.BlockSpec((1, TQ, PAIRS * LANES), lambda b, i: (b, i, 0)),
        out_shape=jax.ShapeDtypeStruct((B, L, PAIRS * LANES), BF16),
        scratch_shapes=_head_scratch(),
        compiler_params=_params(("parallel", "arbitrary")),
    )(qT, own, ka, vT)


def _count_rows(mask):
    return _fold8(jnp.where(mask, 1, 0).astype(I32), jnp.sum)


def _dsa_kernel(qT_ref, k_ref, vT_ref, iqT_ref, iwT_ref, s_ref, o_ref,
                keys_sc, hi_sc, lo_sc, g_sc, acc_sc):
    i = pl.program_id(1)
    n_chunks = i + 1

    def _select():
        wts = iwT_ref[0]

        def score_chunk(j, masked):
            ks = pl.multiple_of(j * TK, TK)
            ik = s_ref[0, pl.ds(ks, TK), :][:, _SM_IK:_SM_IK + HEAD_DIM].astype(BF16)
            acc = jnp.zeros((TK, TQ), F32)
            for h in range(HEADS):
                iq = iqT_ref[0, h // 2, (h % 2) * HEAD_DIM:(h % 2 + 1) * HEAD_DIM, :]
                rel = jnp.maximum(jnp.dot(ik, iq, preferred_element_type=F32), 0.0)
                acc = acc + wts[h:h + 1, :] * rel
            if masked:
                acc = jnp.where(_causal(acc.shape), acc, NEG_INF)
            acc = jnp.where(acc == 0.0, 0.0, acc)
            bits = pltpu.bitcast(acc, I32)
            key = bits ^ (lax.shift_right_arithmetic(bits, 31) & 0x7FFFFFFF)
            keys_sc[pl.ds(ks, TK), :] = key
            hi_sc[pl.ds(ks, TK), :] = lax.shift_right_arithmetic(key, 16).astype(I16)
            lo_sc[pl.ds(ks, TK), :] = ((key & 0xFFFF) - 2 ** 15).astype(I16)

        def score_body(j, carry):
            score_chunk(j, False)
            return carry

        lax.fori_loop(0, i, score_body, 0)
        score_chunk(i, True)

        def count(pred):
            def body(j, cnt):
                ks = pl.multiple_of(j * TK, TK)
                return cnt + _count_rows(pred(keys_sc[pl.ds(ks, TK), :]))
            cnt8 = lax.fori_loop(0, n_chunks, body, jnp.zeros((8, TQ), I32))
            return jnp.sum(cnt8, axis=0, keepdims=True)

        def count16(ref, pred):
            one, nil = jnp.ones((), I16), jnp.zeros((), I16)

            def body(j, cnt):
                ks = pl.multiple_of(j * TK, TK)
                hit = jnp.where(pred(ref[pl.ds(ks, TK), :]), one, nil)
                for r in range(TK // 16):
                    cnt = cnt + hit[r * 16:(r + 1) * 16, :]
                return cnt
            cnt16 = lax.fori_loop(0, n_chunks, body, jnp.zeros((16, TQ), I16))
            return jnp.sum(cnt16.astype(I32), axis=0, keepdims=True)

        def select16(ref, k):
            zero = jnp.zeros((1, TQ), I16)
            t0 = jnp.where(count16(ref, lambda v: v >= zero) >= k, 0, -(2 ** 15)).astype(I32)

            def bit_body(bi, t):
                cand = t | jnp.left_shift(jnp.ones((1, TQ), I32), 14 - bi)
                cand16 = cand.astype(I16)
                return jnp.where(count16(ref, lambda v: v >= cand16) >= k, cand, t)

            return lax.fori_loop(0, 15, bit_body, t0)

        t_hi = select16(hi_sc, TOPK)
        t_hi16 = t_hi.astype(I16)
        n_above = count16(hi_sc, lambda v: v > t_hi16)
        k_lo = TOPK - n_above

        def restrict_body(j, carry):
            ks = pl.multiple_of(j * TK, TK)
            in_bucket = hi_sc[pl.ds(ks, TK), :] == t_hi16
            lo_sc[pl.ds(ks, TK), :] = jnp.where(in_bucket, lo_sc[pl.ds(ks, TK), :], jnp.full((), -(2 ** 15), I16))
            return carry

        lax.fori_loop(0, n_chunks, restrict_body, 0)
        t_lo = select16(lo_sc, k_lo)
        thr = jnp.left_shift(t_hi, 16) | (t_lo + 2 ** 15)

        t_lo16 = t_lo.astype(I16)
        n_ge = n_above + count16(lo_sc, lambda v: v >= t_lo16)
        excess = jnp.where((n_ge > TOPK) & (thr > NEG_INF_KEY), 1, 0)

        @pl.when(jnp.max(excess) > 0)
        def _break_ties():
            need = TOPK - count(lambda kk: kk > thr)
            r = lax.broadcasted_iota(I32, (TK, TK), 0)
            c = lax.broadcasted_iota(I32, (TK, TK), 1)
            strict_lower = jnp.where(c < r, 1.0, 0.0).astype(BF16)

            def body(j, seen):
                ks = pl.multiple_of(j * TK, TK)
                kk = keys_sc[pl.ds(ks, TK), :]
                tie = kk == thr
                tie_f = jnp.where(tie, 1.0, 0.0)
                before = jnp.dot(strict_lower, tie_f.astype(BF16), preferred_element_type=F32) + seen
                drop = tie & (before >= need.astype(F32))
                keys_sc[pl.ds(ks, TK), :] = jnp.where(drop, INT_MIN, kk)
                return seen + jnp.sum(tie_f, axis=0, keepdims=True)

            lax.fori_loop(0, n_chunks, body, jnp.zeros((1, TQ), F32))

        def bias_chunk(j, masked, near8):
            ks = pl.multiple_of(j * TK, TK)
            bias = jnp.where(keys_sc[pl.ds(ks, TK), :] >= thr, 0.0, NEG_INF)
            if masked:
                bias = jnp.where(_causal(bias.shape), bias, NEG_INF)
            keys_sc[pl.ds(ks, TK), :] = pltpu.bitcast(bias, I32)
            position = lax.broadcasted_iota(I32, (TK, TQ), 0) + ks
            return jnp.maximum(near8, _fold8(jnp.where(bias == 0.0, position, -1), jnp.max))

        near8 = lax.fori_loop(0, i, lambda j, n8: bias_chunk(j, False, n8), jnp.full((8, TQ), -1, I32))
        return jnp.max(bias_chunk(i, True, near8), axis=0, keepdims=True)

    nearest = _select()

    def stored_bias(rows, diag):
        return pltpu.bitcast(keys_sc[rows, :], F32)

    def nearest_key_bias(h):
        query = i * TQ + lax.broadcasted_iota(I32, (1, TQ), 1)
        return (query - nearest).astype(F32) * (-(2.0 ** -(h + 1)) * LOG2E)

    _attend_heads(i, stored_bias, nearest_key_bias, qT_ref, k_ref, vT_ref, g_sc, acc_sc, o_ref)


def _dsa_attention(qT, ka, vT, iqT, iwT, small):
    B, _, _, L = qT.shape
    return pl.pallas_call(
        _dsa_kernel,
        grid=(B, L // TQ),
        in_specs=[pl.BlockSpec((1, HEADS, LANES, TQ), lambda b, i: (b, 0, 0, i))] + _head_specs(L, lambda b, i: b)
        + [pl.BlockSpec((1, PAIRS, LANES, TQ), lambda b, i: (b, 0, 0, i)),
           pl.BlockSpec((1, HEADS, TQ), lambda b, i: (b, 0, i)),
           pl.BlockSpec((1, L, LANES), lambda b, i: (b, 0, 0))],
        out_specs=pl.BlockSpec((1, TQ, PAIRS * LANES), lambda b, i: (b, i, 0)),
        out_shape=jax.ShapeDtypeStruct((B, L, PAIRS * LANES), BF16),
        scratch_shapes=[pltpu.VMEM((L, TQ), I32), pltpu.VMEM((L, TQ), I16), pltpu.VMEM((L, TQ), I16)]
        + _head_scratch(),
        compiler_params=_params(("parallel", "arbitrary")),
    )(qT, ka, vT, iqT, iwT, small)


def _layer_norm(z, g, b):
    mu = jnp.mean(z, axis=-1, keepdims=True)
    zc = z - mu
    var = jnp.mean(zc * zc, axis=-1, keepdims=True)
    return zc * lax.rsqrt(var + LN_EPS) * g + b


def _merge_kernel(x_ref, oa_ref, ob_ref, ga_ref, gb_ref, wa_ref, wb_ref, wo_ref, g_ref, b_ref,
                  out_ref, *, alpha):
    a = jnp.dot(oa_ref[0], wa_ref[...], preferred_element_type=F32)
    b = jnp.dot(ob_ref[0], wb_ref[...], preferred_element_type=F32)
    n_gate_blocks = D_MODEL // LANES
    gate_a = jnp.concatenate([ga_ref[0, c] for c in range(n_gate_blocks)], axis=1).astype(F32)
    gate_b = jnp.concatenate([gb_ref[0, c] for c in range(n_gate_blocks)], axis=1).astype(F32)
    merged = jax.nn.sigmoid(gate_a) * a + jax.nn.sigmoid(gate_b) * b
    y = jnp.dot(merged.astype(BF16), wo_ref[...], preferred_element_type=F32)
    out_ref[0] = _layer_norm(alpha * x_ref[0] + y, g_ref[...], b_ref[...])


def _merge(x, oa, ob, p4, wa, wb, wo, g, b, alpha):
    B, L, D = x.shape
    tm = ROW_TILE
    nb = D // LANES
    row = lambda b_, i: (b_, i, 0)
    return pl.pallas_call(
        functools.partial(_merge_kernel, alpha=alpha),
        grid=(B, L // tm),
        in_specs=[pl.BlockSpec((1, tm, D), row),
                  pl.BlockSpec((1, tm, PAIRS * LANES), row),
                  pl.BlockSpec((1, tm, PAIRS * LANES), row),
                  pl.BlockSpec((1, nb, tm, LANES), lambda b_, i: (b_, _CB_GA // nb, i, 0)),
                  pl.BlockSpec((1, nb, tm, LANES), lambda b_, i: (b_, _CB_GB // nb, i, 0)),
                  _const_spec(wa.shape), _const_spec(wb.shape), _const_spec(wo.shape),
                  _const_spec(g.shape), _const_spec(b.shape)],
        out_specs=pl.BlockSpec((1, tm, D), row),
        out_shape=jax.ShapeDtypeStruct((B, L, D), F32),
        compiler_params=_params(("parallel", "parallel")),
    )(x, oa, ob, p4, p4, wa, wb, wo, g, b)


FF_CHUNK = 256


def _ffn_kernel(x_ref, wi_ref, wo_ref, g_ref, b_ref, out_ref, h_sc, *, alpha):
    x = x_ref[0]
    xb = x.astype(BF16)
    for c in range(D_FF // FF_CHUNK):
        lo = c * FF_CHUNK
        gate = jnp.dot(xb, wi_ref[:, lo:lo + FF_CHUNK], preferred_element_type=F32)
        up = jnp.dot(xb, wi_ref[:, D_FF + lo:D_FF + lo + FF_CHUNK], preferred_element_type=F32)
        h_sc[:, lo:lo + FF_CHUNK] = (jax.nn.silu(gate) * up).astype(BF16)
    y = jnp.dot(h_sc[...], wo_ref[...], preferred_element_type=F32)
    out_ref[0] = _layer_norm(alpha * x + y, g_ref[...], b_ref[...])


def _ffn(x, wi, wo, g, b, alpha):
    B, L, D = x.shape
    tm = ROW_TILE
    row = lambda b_, i: (b_, i, 0)
    return pl.pallas_call(
        functools.partial(_ffn_kernel, alpha=alpha),
        grid=(B, L // tm),
        in_specs=[pl.BlockSpec((1, tm, D), row),
                  _const_spec(wi.shape), _const_spec(wo.shape),
                  _const_spec(g.shape), _const_spec(b.shape)],
        out_specs=pl.BlockSpec((1, tm, D), row),
        out_shape=jax.ShapeDtypeStruct((B, L, D), F32),
        scratch_shapes=[pltpu.VMEM((tm, D_FF), BF16)],
        compiler_params=_params(("parallel", "parallel")),
    )(x, wi, wo, g, b)


def _split_w_in(w):
    fox = w[:, 0:_OFF_FLOGIT]
    dsa = w[:, _OFF_DQ:_OFF_DQ + 3 * 512]
    iq = w[:, _OFF_DQ + 3 * 512:_OFF_IK]
    gates = w[:, _OFF_GA:_N_IN]
    w_main = jnp.concatenate([fox, dsa, gates, iq], axis=1).astype(BF16)
    pad = jnp.zeros((w.shape[0], LANES - HEAD_DIM - 2 * HEADS), w.dtype)
    w_small = jnp.concatenate([w[:, _OFF_IK:_OFF_IW], w[:, _OFF_IW:_OFF_GA],
                               w[:, _OFF_FLOGIT:_OFF_DQ], pad], axis=1).astype(BF16)
    return w_main, w_small


def kernel(x, w_in, b_forget, w_branch_a, w_branch_b, w_out, ln1_g, ln1_b, w_ffn_in, w_ffn_out, ln2_g, ln2_b):
    depth = w_in.shape[0]
    alpha = (2.0 * depth) ** 0.25
    for l in range(depth):
        w_main, w_small = _split_w_in(w_in[l])
        bias_row = jnp.zeros((1, LANES), F32).at[0, _SM_FL:_SM_FL + HEADS].set(b_forget[l])
        p4, small = _project(x, w_main, w_small)
        fqT, fka, fvT, f_own = _fox_prep(p4, small, bias_row)
        dqT, dka, dvT, iqT, iwT = _dsa_prep(p4, small)
        o_a = _fox_attention(fqT, f_own, fka, fvT)
        o_b = _dsa_attention(dqT, dka, dvT, iqT, iwT, small)
        x = _merge(x, o_a, o_b, p4,
                   w_branch_a[l].astype(BF16), w_branch_b[l].astype(BF16), w_out[l].astype(BF16),
                   ln1_g[l][None, :], ln1_b[l][None, :], alpha)
        x = _ffn(x, w_ffn_in[l].astype(BF16), w_ffn_out[l].astype(BF16),
                 ln2_g[l][None, :], ln2_b[l][None, :], alpha)
    return x
```

```python
import functools

import numpy as np
import jax
import jax.numpy as jnp
from jax import lax
from jax.experimental import pallas as pl
from jax.experimental.pallas import tpu as pltpu

F32 = jnp.float32
BF16 = jnp.bfloat16
I32 = jnp.int32
I16 = jnp.int16

D_MODEL = 1024
HEADS = 8
HEAD_DIM = 64
PAIRS = HEADS // 2
D_FF = 2816
TOPK = 256
LANES = 128
NEG_INF = -1e30
LN_EPS = 1e-5
ATTN_SCALE = HEAD_DIM ** -0.5
IDX_SCALE = (HEADS ** -0.5) * (HEAD_DIM ** -0.5)

_OFF_FLOGIT = 1536
_OFF_DQ = 1544
_OFF_IK = 3592
_OFF_IW = 3656
_OFF_GA = 3664
_N_IN = 5712

_CB_FQ, _CB_FK, _CB_FV, _CB_DQ, _CB_DK, _CB_DV, _CB_GA, _CB_GB, _CB_IQ = 0, 4, 8, 12, 16, 20, 24, 32, 40
N_MAIN_BLOCKS = 44
N_MAIN = N_MAIN_BLOCKS * LANES
_SM_IK, _SM_IW, _SM_FL = 0, 64, 72

ROW_TILE = 512
TQ = 512
TK = 512
CUMSUM_BLOCK = 256
V_ROWS = HEAD_DIM + 16
LOG2E = 1.4426950408889634

INT_MIN = -(2 ** 31)


def _monotone_key_of(value):
    b = int(np.float32(value).view(np.int32))
    return b ^ ((b >> 31) & 0x7FFFFFFF)


NEG_INF_KEY = _monotone_key_of(NEG_INF)
VMEM_LIMIT = 60 * 1024 * 1024


def _params(sem):
    return pltpu.CompilerParams(dimension_semantics=sem, vmem_limit_bytes=VMEM_LIMIT)


def _const_spec(shape):
    nd = len(shape)
    return pl.BlockSpec(shape, lambda *_: (0,) * nd, pipeline_mode=pl.Buffered(1))


def _proj_kernel(x_ref, wm_ref, ws_ref, om_ref, os_ref):
    xb = x_ref[0].astype(BF16)
    os_ref[0] = jnp.dot(xb, ws_ref[...], preferred_element_type=F32)
    group = 4
    for c in range(N_MAIN_BLOCKS // group):
        w = wm_ref[:, c * group * LANES:(c + 1) * group * LANES]
        r = jnp.dot(xb, w, preferred_element_type=F32).astype(BF16)
        for g in range(group):
            om_ref[0, c * group + g] = r[:, g * LANES:(g + 1) * LANES]


def _project(x, w_main, w_small):
    B, L, D = x.shape
    tm = ROW_TILE
    return pl.pallas_call(
        _proj_kernel,
        grid=(B, L // tm),
        in_specs=[pl.BlockSpec((1, tm, D), lambda b, i: (b, i, 0)),
                  _const_spec((D, N_MAIN)),
                  _const_spec((D, LANES))],
        out_specs=[pl.BlockSpec((1, N_MAIN_BLOCKS, tm, LANES), lambda b, i: (b, 0, i, 0)),
                   pl.BlockSpec((1, tm, LANES), lambda b, i: (b, i, 0))],
        out_shape=[jax.ShapeDtypeStruct((B, N_MAIN_BLOCKS, L, LANES), BF16),
                   jax.ShapeDtypeStruct((B, L, LANES), F32)],
        compiler_params=_params(("parallel", "parallel")),
    )(x, w_main, w_small)


def _split3(v):
    hi = v.astype(BF16).astype(F32)
    r = v - hi
    mid = r.astype(BF16).astype(F32)
    lo = (r - mid).astype(BF16).astype(F32)
    return hi, mid, lo


def _cumsum_rows(x):
    n = CUMSUM_BLOCK
    L = x.shape[0]
    r = lax.broadcasted_iota(I32, (n, n), 0)
    c = lax.broadcasted_iota(I32, (n, n), 1)
    tri = jnp.where(c <= r, 1.0, 0.0).astype(BF16)
    carry = jnp.zeros((1, x.shape[1]), F32)
    outs = []
    w = x.shape[1]
    for blk in range(L // n):
        parts = jnp.concatenate([p.astype(BF16) for p in _split3(x[blk * n:(blk + 1) * n])], axis=1)
        y3 = jnp.dot(tri, parts, preferred_element_type=F32)
        y = (y3[:, :w] + y3[:, w:2 * w] + y3[:, 2 * w:]) + carry
        carry = y[n - 1:n, :]
        outs.append(y)
    return jnp.concatenate(outs, axis=0)


def _placement(entries, n_cols):
    r = lax.broadcasted_iota(I32, (LANES, n_cols), 0)
    c = lax.broadcasted_iota(I32, (LANES, n_cols), 1)
    out = jnp.zeros((LANES, n_cols), F32)
    for src, dst, coef in entries:
        out = jnp.where((r == src) & (c == dst), coef, out)
    return out.astype(BF16)


def _lane_row(lo, hi, value=1.0):
    lane = lax.broadcasted_iota(I32, (1, LANES), 1)
    return jnp.where((lane >= lo) & (lane < hi), value, 0.0)


def _store_value_chunks(ref, v_pair):
    L = v_pair.shape[0]
    vT = v_pair.astype(F32).T.astype(BF16)
    ones = jnp.ones((V_ROWS - HEAD_DIM, L), BF16)
    for e in range(2):
        rows = jnp.concatenate([vT[e * HEAD_DIM:(e + 1) * HEAD_DIM], ones], axis=0)
        for j in range(L // TK):
            ref[0, e, j] = rows[:, j * TK:(j + 1) * TK]


def _fox_prep_kernel(q_ref, k_ref, v_ref, s_ref, bias_ref, qT_ref, ka_ref, vT_ref, own_ref, parts_sc):
    p = pl.program_id(1)
    L = q_ref.shape[2]

    @pl.when(p == 0)
    def _cumulative_log_forget():
        z = s_ref[0] + bias_ref[...]
        c = _cumsum_rows(jax.nn.log_sigmoid(z))
        hi, mid, lo = _split3(c)
        lane = lax.broadcasted_iota(I32, (L, LANES), 1)
        packed = jnp.where(lane < _SM_FL + HEADS, hi,
                           jnp.where(lane < _SM_FL + 2 * HEADS, pltpu.roll(mid, HEADS, 1), pltpu.roll(lo, 2 * HEADS, 1)))
        parts_sc[...] = packed.astype(BF16)

    q_entries, k_entries = [], []
    for e in range(2):
        base = (1 - e) * HEAD_DIM
        for part in range(3):
            src = _SM_FL + part * HEADS + 2 * p + e
            q_entries.append((src, e * LANES + base + part, 1.0))
            k_entries.append((src, (2 + e) * LANES + base + 3 + part, -1.0))
    aug = jnp.dot(parts_sc[...], _placement(q_entries + k_entries, 4 * LANES), preferred_element_type=F32)
    q = q_ref[0, 0].astype(F32) * ATTN_SCALE
    k = k_ref[0, 0].astype(F32)
    for e in range(2):
        data = _lane_row(e * HEAD_DIM, (e + 1) * HEAD_DIM)
        base = (1 - e) * HEAD_DIM
        q_aug = q * data + aug[:, e * LANES:(e + 1) * LANES] + _lane_row(base + 3, base + 6)
        k_aug = k * data + aug[:, (2 + e) * LANES:(3 + e) * LANES] + _lane_row(base, base + 3)
        qT_ref[0, e] = q_aug.T.astype(BF16)
        ka_ref[0, e] = k_aug.astype(BF16)
        own_ref[0, e] = jnp.sum((q_aug * k_aug).T, axis=0, keepdims=True) * LOG2E
    _store_value_chunks(vT_ref, v_ref[0, 0])


def _fox_prep(p4, small, bias_row):
    B, _, L, _ = p4.shape
    nk = L // TK

    def pair_spec(cb):
        return pl.BlockSpec((1, 1, L, LANES), lambda b, p: (b, cb + p, 0, 0))

    return pl.pallas_call(
        _fox_prep_kernel,
        grid=(B, PAIRS),
        in_specs=[pair_spec(_CB_FQ), pair_spec(_CB_FK), pair_spec(_CB_FV),
                  pl.BlockSpec((1, L, LANES), lambda b, p: (b, 0, 0)),
                  pl.BlockSpec((1, LANES), lambda b, p: (0, 0))],
        out_specs=[pl.BlockSpec((1, 2, LANES, L), lambda b, p: (b, p, 0, 0)),
                   pl.BlockSpec((1, 2, L, LANES), lambda b, p: (b, p, 0, 0)),
                   pl.BlockSpec((1, 2, nk, V_ROWS, TK), lambda b, p: (b, p, 0, 0, 0)),
                   pl.BlockSpec((1, 2, 1, L), lambda b, p: (b, p, 0, 0))],
        out_shape=[jax.ShapeDtypeStruct((B, HEADS, LANES, L), BF16),
                   jax.ShapeDtypeStruct((B, HEADS, L, LANES), BF16),
                   jax.ShapeDtypeStruct((B, HEADS, nk, V_ROWS, TK), BF16),
                   jax.ShapeDtypeStruct((B, HEADS, 1, L), F32)],
        scratch_shapes=[pltpu.VMEM((L, LANES), BF16)],
        compiler_params=_params(("parallel", "arbitrary")),
    )(p4, p4, p4, small, bias_row)


def _dsa_prep_kernel(q_ref, k_ref, v_ref, iq_ref, s_ref, qT_ref, ka_ref, vT_ref, iqT_ref, iwT_ref):
    p = pl.program_id(1)
    L = q_ref.shape[2]
    lane = lax.broadcasted_iota(I32, (L, LANES), 1)
    pos = lax.broadcasted_iota(I32, (L, LANES), 0)
    pos_terms = jnp.where(lane == 0, lax.shift_right_logical(pos, 6),
                          jnp.where(lane == 1, pos & 63, jnp.where(lane == 2, 1, 0))).astype(F32).astype(BF16)
    entries = []
    for e in range(2):
        h = jnp.full((1, 4 * LANES), 2 * p + e, I32)
        slope = lax.shift_left(jnp.ones((1, 4 * LANES), I32), 7 - h).astype(F32) * (1.0 / 256.0)
        qc = e * LANES + (1 - e) * HEAD_DIM
        kc = (2 + e) * LANES + (1 - e) * HEAD_DIM
        entries += [(0, qc, -64.0 * slope), (1, qc + 1, -slope), (2, qc + 2, 64.0 * slope), (2, qc + 3, slope),
                    (2, kc, 1.0), (2, kc + 1, 1.0), (0, kc + 2, 1.0), (1, kc + 3, 1.0)]
    aug = jnp.dot(pos_terms, _placement(entries, 4 * LANES), preferred_element_type=F32)
    q = q_ref[0, 0].astype(F32) * ATTN_SCALE
    k = k_ref[0, 0].astype(F32)
    for e in range(2):
        data = _lane_row(e * HEAD_DIM, (e + 1) * HEAD_DIM)
        q_aug = q * data + aug[:, e * LANES:(e + 1) * LANES]
        k_aug = k * data + aug[:, (2 + e) * LANES:(3 + e) * LANES]
        qT_ref[0, e] = q_aug.T.astype(BF16)
        ka_ref[0, e] = k_aug.astype(BF16)
    _store_value_chunks(vT_ref, v_ref[0, 0])
    iqT_ref[0, 0] = iq_ref[0, 0].astype(F32).T.astype(BF16)
    sT = s_ref[0].T
    iwT_ref[0] = sT[_SM_IW:_SM_IW + HEADS, :] * IDX_SCALE


def _dsa_prep(p4, small):
    B, _, L, _ = p4.shape
    nk = L // TK

    def pair_spec(cb):
        return pl.BlockSpec((1, 1, L, LANES), lambda b, p: (b, cb + p, 0, 0))

    return pl.pallas_call(
        _dsa_prep_kernel,
        grid=(B, PAIRS),
        in_specs=[pair_spec(_CB_DQ), pair_spec(_CB_DK), pair_spec(_CB_DV), pair_spec(_CB_IQ),
                  pl.BlockSpec((1, L, LANES), lambda b, p: (b, 0, 0))],
        out_specs=[pl.BlockSpec((1, 2, LANES, L), lambda b, p: (b, p, 0, 0)),
                   pl.BlockSpec((1, 2, L, LANES), lambda b, p: (b, p, 0, 0)),
                   pl.BlockSpec((1, 2, nk, V_ROWS, TK), lambda b, p: (b, p, 0, 0, 0)),
                   pl.BlockSpec((1, 1, LANES, L), lambda b, p: (b, p, 0, 0)),
                   pl.BlockSpec((1, HEADS, L), lambda b, p: (b, 0, 0))],
        out_shape=[jax.ShapeDtypeStruct((B, HEADS, LANES, L), BF16),
                   jax.ShapeDtypeStruct((B, HEADS, L, LANES), BF16),
                   jax.ShapeDtypeStruct((B, HEADS, nk, V_ROWS, TK), BF16),
                   jax.ShapeDtypeStruct((B, PAIRS, LANES, L), BF16),
                   jax.ShapeDtypeStruct((B, HEADS, L), F32)],
        compiler_params=_params(("parallel", "arbitrary")),
    )(p4, p4, p4, p4, small)


def _fold8(x, op):
    return op(x.reshape(x.shape[0] // 8, 8, x.shape[1]), axis=0)


def _causal(shape):
    return lax.broadcasted_iota(I32, shape, 0) <= lax.broadcasted_iota(I32, shape, 1)


L_SUM_MIN, L_SUM_MAX = 1e-30, 1e30


def _attend_heads(n_full, bias_fn, ref_score_fn, qT_ref, k_ref, vT_ref, g_sc, acc_sc, o_ref):
    def for_chunks(chunk):
        def body(j, carry):
            chunk(j, False)
            return carry

        lax.fori_loop(0, n_full, body, 0)
        chunk(n_full, True)

    def scores(h, rows, diag, raw):
        s = raw * LOG2E
        bias = bias_fn(rows, diag)
        return s if bias is None else s + bias

    def sweep():
        acc_sc[...] = jnp.zeros(acc_sc.shape, F32)

        def chunk(j, diag):
            rows = pl.ds(pl.multiple_of(j * TK, TK), TK)

            def raw_scores(h):
                return jnp.dot(k_ref[0, h, rows, :], qT_ref[0, h], preferred_element_type=F32)

            nxt = raw_scores(0)
            for h in range(HEADS):
                raw = nxt
                if h + 1 < HEADS:
                    nxt = raw_scores(h + 1)
                p = jnp.exp2(scores(h, rows, diag, raw) - g_sc[h]).astype(BF16)
                acc_sc[h] += jnp.dot(vT_ref[0, h, j], p, preferred_element_type=F32)

        for_chunks(chunk)
        bad = jnp.zeros((1, TQ), I32)
        for p in range(PAIRS):
            halves = []
            for e in range(2):
                acc = acc_sc[2 * p + e]
                l = acc[HEAD_DIM:HEAD_DIM + 1]
                bad = jnp.maximum(bad, jnp.where((l > L_SUM_MIN) & (l < L_SUM_MAX), 0, 1))
                halves.append(acc[:HEAD_DIM] / l)
            o_ref[0, :, p * LANES:(p + 1) * LANES] = jnp.concatenate(halves, axis=0).T.astype(o_ref.dtype)
        return jnp.max(bad) > 0

    def exact_row_maxima():
        acc_sc[:, 0:8, :] = jnp.full((HEADS, 8, TQ), NEG_INF, F32)

        def chunk(j, diag):
            rows = pl.ds(pl.multiple_of(j * TK, TK), TK)
            for h in range(HEADS):
                raw = jnp.dot(k_ref[0, h, rows, :], qT_ref[0, h], preferred_element_type=F32)
                acc_sc[h, 0:8, :] = jnp.maximum(acc_sc[h, 0:8, :], _fold8(scores(h, rows, diag, raw), jnp.max))

        for_chunks(chunk)
        for h in range(HEADS):
            g_sc[h] = jnp.max(acc_sc[h, 0:8, :], axis=0, keepdims=True)

    for h in range(HEADS):
        g_sc[h] = ref_score_fn(h)
    unsafe = sweep()

    @pl.when(unsafe)
    def _redo_with_exact_maxima():
        exact_row_maxima()
        sweep()


def _head_specs(L, index):
    nk = L // TK
    return [pl.BlockSpec((1, HEADS, L, LANES), lambda *g: (index(*g), 0, 0, 0)),
            pl.BlockSpec((1, HEADS, nk, V_ROWS, TK), lambda *g: (index(*g), 0, 0, 0, 0))]


def _head_scratch():
    return [pltpu.VMEM((HEADS, 1, TQ), F32), pltpu.VMEM((HEADS, V_ROWS, TQ), F32)]


def _fox_kernel(qT_ref, own_ref, k_ref, vT_ref, o_ref, g_sc, acc_sc):
    i = pl.program_id(1)

    def bias_fn(rows, diag):
        return jnp.where(_causal((TK, TQ)), 0.0, NEG_INF) if diag else None

    _attend_heads(i, bias_fn, lambda h: own_ref[0, h], qT_ref, k_ref, vT_ref, g_sc, acc_sc, o_ref)


def _fox_attention(qT, own, ka, vT):
    B, _, _, L = qT.shape
    return pl.pallas_call(
        _fox_kernel,
        grid=(B, L // TQ),
        in_specs=[pl.BlockSpec((1, HEADS, LANES, TQ), lambda b, i: (b, 0, 0, i)),
                  pl.BlockSpec((1, HEADS, 1, TQ), lambda b, i: (b, 0, 0, i))] + _head_specs(L, lambda b, i: b),
        out_specs=pl.BlockSpec((1, TQ, PAIRS * LANES), lambda b, i: (b, i, 0)),
        out_shape=jax.ShapeDtypeStruct((B, L, PAIRS * LANES), BF16),
        scratch_shapes=_head_scratch(),
        compiler_params=_params(("parallel", "arbitrary")),
    )(qT, own, ka, vT)


def _count_rows(mask):
    return _fold8(jnp.where(mask, 1, 0).astype(I32), jnp.sum)


def _dsa_kernel(qT_ref, k_ref, vT_ref, iqT_ref, iwT_ref, s_ref, o_ref,
                keys_sc, hi_sc, lo_sc, g_sc, acc_sc):
    i = pl.program_id(1)
    n_chunks = i + 1

    def _select():
        wts = iwT_ref[0]

        def score_chunk(j, masked):
            ks = pl.multiple_of(j * TK, TK)
            ik = s_ref[0, pl.ds(ks, TK), :][:, _SM_IK:_SM_IK + HEAD_DIM].astype(BF16)
            acc = jnp.zeros((TK, TQ), F32)
            for h in range(HEADS):
                iq = iqT_ref[0, h // 2, (h % 2) * HEAD_DIM:(h % 2 + 1) * HEAD_DIM, :]
                rel = jnp.maximum(jnp.dot(ik, iq, preferred_element_type=F32), 0.0)
                acc = acc + wts[h:h + 1, :] * rel
            if masked:
                acc = jnp.where(_causal(acc.shape), acc, NEG_INF)
            acc = jnp.where(acc == 0.0, 0.0, acc)
            bits = pltpu.bitcast(acc, I32)
            key = bits ^ (lax.shift_right_arithmetic(bits, 31) & 0x7FFFFFFF)
            keys_sc[pl.ds(ks, TK), :] = key
            hi_sc[pl.ds(ks, TK), :] = lax.shift_right_arithmetic(key, 16).astype(I16)
            lo_sc[pl.ds(ks, TK), :] = ((key & 0xFFFF) - 2 ** 15).astype(I16)

        def score_body(j, carry):
            score_chunk(j, False)
            return carry

        lax.fori_loop(0, i, score_body, 0)
        score_chunk(i, True)

        def count(pred):
            def body(j, cnt):
                ks = pl.multiple_of(j * TK, TK)
                return cnt + _count_rows(pred(keys_sc[pl.ds(ks, TK), :]))
            cnt8 = lax.fori_loop(0, n_chunks, body, jnp.zeros((8, TQ), I32))
            return jnp.sum(cnt8, axis=0, keepdims=True)

        def count16(ref, pred):
            one, nil = jnp.ones((), I16), jnp.zeros((), I16)

            def body(j, cnt):
                ks = pl.multiple_of(j * TK, TK)
                hit = jnp.where(pred(ref[pl.ds(ks, TK), :]), one, nil)
                for r in range(TK // 16):
                    cnt = cnt + hit[r * 16:(r + 1) * 16, :]
                return cnt
            cnt16 = lax.fori_loop(0, n_chunks, body, jnp.zeros((16, TQ), I16))
            return jnp.sum(cnt16.astype(I32), axis=0, keepdims=True)

        def select16(ref, k):
            zero = jnp.zeros((1, TQ), I16)
            c0 = count16(ref, lambda v: v >= zero)
            ok0 = c0 >= k
            init = (jnp.where(ok0, 0, -(2 ** 15)).astype(I32),
                    jnp.where(ok0, c0, n_chunks * TK),
                    jnp.where(ok0, 0, c0))

            def bit_body(bi, carry):
                t, n_ge, n_gt = carry
                cand = t | jnp.left_shift(jnp.ones((1, TQ), I32), 14 - bi)
                cand16 = cand.astype(I16)
                c = count16(ref, lambda v: v >= cand16)
                ok = c >= k
                return jnp.where(ok, cand, t), jnp.where(ok, c, n_ge), jnp.where(ok, n_gt, c)

            return lax.fori_loop(0, 15, bit_body, init)

        t_hi, _, n_above = select16(hi_sc, TOPK)
        t_hi16 = t_hi.astype(I16)
        k_lo = TOPK - n_above

        def restrict_body(j, carry):
            ks = pl.multiple_of(j * TK, TK)
            in_bucket = hi_sc[pl.ds(ks, TK), :] == t_hi16
            lo_sc[pl.ds(ks, TK), :] = jnp.where(in_bucket, lo_sc[pl.ds(ks, TK), :], jnp.full((), -(2 ** 15), I16))
            return carry

        lax.fori_loop(0, n_chunks, restrict_body, 0)
        t_lo, n_ge_lo, _ = select16(lo_sc, k_lo)
        thr = jnp.left_shift(t_hi, 16) | (t_lo + 2 ** 15)

        excess = jnp.where((n_above + n_ge_lo > TOPK) & (thr > NEG_INF_KEY), 1, 0)

        @pl.when(jnp.max(excess) > 0)
        def _break_ties():
            need = TOPK - count(lambda kk: kk > thr)
            r = lax.broadcasted_iota(I32, (TK, TK), 0)
            c = lax.broadcasted_iota(I32, (TK, TK), 1)
            strict_lower = jnp.where(c < r, 1.0, 0.0).astype(BF16)

            def body(j, seen):
                ks = pl.multiple_of(j * TK, TK)
                kk = keys_sc[pl.ds(ks, TK), :]
                tie = kk == thr
                tie_f = jnp.where(tie, 1.0, 0.0)
                before = jnp.dot(strict_lower, tie_f.astype(BF16), preferred_element_type=F32) + seen
                drop = tie & (before >= need.astype(F32))
                keys_sc[pl.ds(ks, TK), :] = jnp.where(drop, INT_MIN, kk)
                return seen + jnp.sum(tie_f, axis=0, keepdims=True)

            lax.fori_loop(0, n_chunks, body, jnp.zeros((1, TQ), F32))

        def bias_chunk(j, masked, near8):
            ks = pl.multiple_of(j * TK, TK)
            bias = jnp.where(keys_sc[pl.ds(ks, TK), :] >= thr, 0.0, NEG_INF)
            if masked:
                bias = jnp.where(_causal(bias.shape), bias, NEG_INF)
            keys_sc[pl.ds(ks, TK), :] = pltpu.bitcast(bias, I32)
            position = lax.broadcasted_iota(I32, (TK, TQ), 0) + ks
            return jnp.maximum(near8, _fold8(jnp.where(bias == 0.0, position, -1), jnp.max))

        near8 = lax.fori_loop(0, i, lambda j, n8: bias_chunk(j, False, n8), jnp.full((8, TQ), -1, I32))
        return jnp.max(bias_chunk(i, True, near8), axis=0, keepdims=True)

    nearest = _select()

    def stored_bias(rows, diag):
        return pltpu.bitcast(keys_sc[rows, :], F32)

    def nearest_key_bias(h):
        query = i * TQ + lax.broadcasted_iota(I32, (1, TQ), 1)
        return (query - nearest).astype(F32) * (-(2.0 ** -(h + 1)) * LOG2E)

    _attend_heads(i, stored_bias, nearest_key_bias, qT_ref, k_ref, vT_ref, g_sc, acc_sc, o_ref)


def _dsa_attention(qT, ka, vT, iqT, iwT, small):
    B, _, _, L = qT.shape
    return pl.pallas_call(
        _dsa_kernel,
        grid=(B, L // TQ),
        in_specs=[pl.BlockSpec((1, HEADS, LANES, TQ), lambda b, i: (b, 0, 0, i))] + _head_specs(L, lambda b, i: b)
        + [pl.BlockSpec((1, PAIRS, LANES, TQ), lambda b, i: (b, 0, 0, i)),
           pl.BlockSpec((1, HEADS, TQ), lambda b, i: (b, 0, i)),
           pl.BlockSpec((1, L, LANES), lambda b, i: (b, 0, 0))],
        out_specs=pl.BlockSpec((1, TQ, PAIRS * LANES), lambda b, i: (b, i, 0)),
        out_shape=jax.ShapeDtypeStruct((B, L, PAIRS * LANES), BF16),
        scratch_shapes=[pltpu.VMEM((L, TQ), I32), pltpu.VMEM((L, TQ), I16), pltpu.VMEM((L, TQ), I16)]
        + _head_scratch(),
        compiler_params=_params(("parallel", "arbitrary")),
    )(qT, ka, vT, iqT, iwT, small)


def _layer_norm(z, g, b):
    mu = jnp.mean(z, axis=-1, keepdims=True)
    zc = z - mu
    var = jnp.mean(zc * zc, axis=-1, keepdims=True)
    return zc * lax.rsqrt(var + LN_EPS) * g + b


def _merge_kernel(x_ref, oa_ref, ob_ref, ga_ref, gb_ref, wa_ref, wb_ref, wo_ref, g_ref, b_ref,
                  out_ref, *, alpha):
    a = jnp.dot(oa_ref[0], wa_ref[...], preferred_element_type=F32)
    b = jnp.dot(ob_ref[0], wb_ref[...], preferred_element_type=F32)
    n_gate_blocks = D_MODEL // LANES
    gate_a = jnp.concatenate([ga_ref[0, c] for c in range(n_gate_blocks)], axis=1).astype(F32)
    gate_b = jnp.concatenate([gb_ref[0, c] for c in range(n_gate_blocks)], axis=1).astype(F32)
    merged = jax.nn.sigmoid(gate_a) * a + jax.nn.sigmoid(gate_b) * b
    y = jnp.dot(merged.astype(BF16), wo_ref[...], preferred_element_type=F32)
    out_ref[0] = _layer_norm(alpha * x_ref[0] + y, g_ref[...], b_ref[...])


def _merge(x, oa, ob, p4, wa, wb, wo, g, b, alpha):
    B, L, D = x.shape
    tm = ROW_TILE
    nb = D // LANES
    row = lambda b_, i: (b_, i, 0)
    return pl.pallas_call(
        functools.partial(_merge_kernel, alpha=alpha),
        grid=(B, L // tm),
        in_specs=[pl.BlockSpec((1, tm, D), row),
                  pl.BlockSpec((1, tm, PAIRS * LANES), row),
                  pl.BlockSpec((1, tm, PAIRS * LANES), row),
                  pl.BlockSpec((1, nb, tm, LANES), lambda b_, i: (b_, _CB_GA // nb, i, 0)),
                  pl.BlockSpec((1, nb, tm, LANES), lambda b_, i: (b_, _CB_GB // nb, i, 0)),
                  _const_spec(wa.shape), _const_spec(wb.shape), _const_spec(wo.shape),
                  _const_spec(g.shape), _const_spec(b.shape)],
        out_specs=pl.BlockSpec((1, tm, D), row),
        out_shape=jax.ShapeDtypeStruct((B, L, D), F32),
        compiler_params=_params(("parallel", "parallel")),
    )(x, oa, ob, p4, p4, wa, wb, wo, g, b)


FF_CHUNK = 256


def _ffn_kernel(x_ref, wi_ref, wo_ref, g_ref, b_ref, out_ref, h_sc, *, alpha):
    x = x_ref[0]
    xb = x.astype(BF16)
    for c in range(D_FF // FF_CHUNK):
        lo = c * FF_CHUNK
        gate = jnp.dot(xb, wi_ref[:, lo:lo + FF_CHUNK], preferred_element_type=F32)
        up = jnp.dot(xb, wi_ref[:, D_FF + lo:D_FF + lo + FF_CHUNK], preferred_element_type=F32)
        h_sc[:, lo:lo + FF_CHUNK] = (jax.nn.silu(gate) * up).astype(BF16)
    y = jnp.dot(h_sc[...], wo_ref[...], preferred_element_type=F32)
    out_ref[0] = _layer_norm(alpha * x + y, g_ref[...], b_ref[...])


def _ffn(x, wi, wo, g, b, alpha):
    B, L, D = x.shape
    tm = ROW_TILE
    row = lambda b_, i: (b_, i, 0)
    return pl.pallas_call(
        functools.partial(_ffn_kernel, alpha=alpha),
        grid=(B, L // tm),
        in_specs=[pl.BlockSpec((1, tm, D), row),
                  _const_spec(wi.shape), _const_spec(wo.shape),
                  _const_spec(g.shape), _const_spec(b.shape)],
        out_specs=pl.BlockSpec((1, tm, D), row),
        out_shape=jax.ShapeDtypeStruct((B, L, D), F32),
        scratch_shapes=[pltpu.VMEM((tm, D_FF), BF16)],
        compiler_params=_params(("parallel", "parallel")),
    )(x, wi, wo, g, b)


def _split_w_in(w):
    fox = w[:, 0:_OFF_FLOGIT]
    dsa = w[:, _OFF_DQ:_OFF_DQ + 3 * 512]
    iq = w[:, _OFF_DQ + 3 * 512:_OFF_IK]
    gates = w[:, _OFF_GA:_N_IN]
    w_main = jnp.concatenate([fox, dsa, gates, iq], axis=1).astype(BF16)
    pad = jnp.zeros((w.shape[0], LANES - HEAD_DIM - 2 * HEADS), w.dtype)
    w_small = jnp.concatenate([w[:, _OFF_IK:_OFF_IW], w[:, _OFF_IW:_OFF_GA],
                               w[:, _OFF_FLOGIT:_OFF_DQ], pad], axis=1).astype(BF16)
    return w_main, w_small


def kernel(x, w_in, b_forget, w_branch_a, w_branch_b, w_out, ln1_g, ln1_b, w_ffn_in, w_ffn_out, ln2_g, ln2_b):
    depth = w_in.shape[0]
    alpha = (2.0 * depth) ** 0.25
    for l in range(depth):
        w_main, w_small = _split_w_in(w_in[l])
        bias_row = jnp.zeros((1, LANES), F32).at[0, _SM_FL:_SM_FL + HEADS].set(b_forget[l])
        p4, small = _project(x, w_main, w_small)
        fqT, fka, fvT, f_own = _fox_prep(p4, small, bias_row)
        dqT, dka, dvT, iqT, iwT = _dsa_prep(p4, small)
        o_a = _fox_attention(fqT, f_own, fka, fvT)
        o_b = _dsa_attention(dqT, dka, dvT, iqT, iwT, small)
        x = _merge(x, o_a, o_b, p4,
                   w_branch_a[l].astype(BF16), w_branch_b[l].astype(BF16), w_out[l].astype(BF16),
                   ln1_g[l][None, :], ln1_b[l][None, :], alpha)
        x = _ffn(x, w_ffn_in[l].astype(BF16), w_ffn_out[l].astype(BF16),
                 ln2_g[l][None, :], ln2_b[l][None, :], alpha)
    return x
```

```python
import functools

import numpy as np
import jax
import jax.numpy as jnp
from jax import lax
from jax.experimental import pallas as pl
from jax.experimental.pallas import tpu as pltpu

F32 = jnp.float32
BF16 = jnp.bfloat16
I32 = jnp.int32
I16 = jnp.int16

D_MODEL = 1024
HEADS = 8
HEAD_DIM = 64
PAIRS = HEADS // 2
D_FF = 2816
TOPK = 256
LANES = 128
NEG_INF = -1e30
LN_EPS = 1e-5
ATTN_SCALE = HEAD_DIM ** -0.5
IDX_SCALE = (HEADS ** -0.5) * (HEAD_DIM ** -0.5)

_OFF_FLOGIT = 1536
_OFF_DQ = 1544
_OFF_IK = 3592
_OFF_IW = 3656
_OFF_GA = 3664
_N_IN = 5712

_CB_FQ, _CB_FK, _CB_FV, _CB_DQ, _CB_DK, _CB_DV, _CB_GA, _CB_GB, _CB_IQ = 0, 4, 8, 12, 16, 20, 24, 32, 40
N_MAIN_BLOCKS = 44
N_MAIN = N_MAIN_BLOCKS * LANES
_SM_IK, _SM_IW, _SM_FL = 0, 64, 72

ROW_TILE = 512
TQ = 512
TK = 512
CUMSUM_BLOCK = 256
V_ROWS = HEAD_DIM + 16
LOG2E = 1.4426950408889634

INT_MIN = -(2 ** 31)


def _monotone_key_of(value):
    b = int(np.float32(value).view(np.int32))
    return (b ^ ((b >> 31) & 0x7FFFFFFF)) - (b >> 31)


NEG_INF_KEY = _monotone_key_of(NEG_INF)
VMEM_LIMIT = 60 * 1024 * 1024


def _params(sem):
    return pltpu.CompilerParams(dimension_semantics=sem, vmem_limit_bytes=VMEM_LIMIT)


def _const_spec(shape):
    nd = len(shape)
    return pl.BlockSpec(shape, lambda *_: (0,) * nd, pipeline_mode=pl.Buffered(1))


def _proj_kernel(x_ref, wm_ref, ws_ref, om_ref, os_ref):
    xb = x_ref[0].astype(BF16)
    os_ref[0] = jnp.dot(xb, ws_ref[...], preferred_element_type=F32)
    group = 4
    for c in range(N_MAIN_BLOCKS // group):
        w = wm_ref[:, c * group * LANES:(c + 1) * group * LANES]
        r = jnp.dot(xb, w, preferred_element_type=F32).astype(BF16)
        for g in range(group):
            om_ref[0, c * group + g] = r[:, g * LANES:(g + 1) * LANES]


def _project(x, w_main, w_small):
    B, L, D = x.shape
    tm = ROW_TILE
    return pl.pallas_call(
        _proj_kernel,
        grid=(B, L // tm),
        in_specs=[pl.BlockSpec((1, tm, D), lambda b, i: (b, i, 0)),
                  _const_spec((D, N_MAIN)),
                  _const_spec((D, LANES))],
        out_specs=[pl.BlockSpec((1, N_MAIN_BLOCKS, tm, LANES), lambda b, i: (b, 0, i, 0)),
                   pl.BlockSpec((1, tm, LANES), lambda b, i: (b, i, 0))],
        out_shape=[jax.ShapeDtypeStruct((B, N_MAIN_BLOCKS, L, LANES), BF16),
                   jax.ShapeDtypeStruct((B, L, LANES), F32)],
        compiler_params=_params(("parallel", "parallel")),
    )(x, w_main, w_small)


def _split3(v):
    hi = v.astype(BF16).astype(F32)
    r = v - hi
    mid = r.astype(BF16).astype(F32)
    lo = (r - mid).astype(BF16).astype(F32)
    return hi, mid, lo


def _cumsum_rows(x):
    n = CUMSUM_BLOCK
    L = x.shape[0]
    r = lax.broadcasted_iota(I32, (n, n), 0)
    c = lax.broadcasted_iota(I32, (n, n), 1)
    tri = jnp.where(c <= r, 1.0, 0.0).astype(BF16)
    carry = jnp.zeros((1, x.shape[1]), F32)
    outs = []
    w = x.shape[1]
    for blk in range(L // n):
        parts = jnp.concatenate([p.astype(BF16) for p in _split3(x[blk * n:(blk + 1) * n])], axis=1)
        y3 = jnp.dot(tri, parts, preferred_element_type=F32)
        y = (y3[:, :w] + y3[:, w:2 * w] + y3[:, 2 * w:]) + carry
        carry = y[n - 1:n, :]
        outs.append(y)
    return jnp.concatenate(outs, axis=0)


def _placement(entries, n_cols):
    r = lax.broadcasted_iota(I32, (LANES, n_cols), 0)
    c = lax.broadcasted_iota(I32, (LANES, n_cols), 1)
    out = jnp.zeros((LANES, n_cols), F32)
    for src, dst, coef in entries:
        out = jnp.where((r == src) & (c == dst), coef, out)
    return out.astype(BF16)


def _lane_row(lo, hi, value=1.0):
    lane = lax.broadcasted_iota(I32, (1, LANES), 1)
    return jnp.where((lane >= lo) & (lane < hi), value, 0.0)


def _store_value_chunks(ref, v_pair):
    L = v_pair.shape[0]
    vT = v_pair.astype(F32).T.astype(BF16)
    ones = jnp.ones((V_ROWS - HEAD_DIM, L), BF16)
    for e in range(2):
        rows = jnp.concatenate([vT[e * HEAD_DIM:(e + 1) * HEAD_DIM], ones], axis=0)
        for j in range(L // TK):
            ref[0, e, j] = rows[:, j * TK:(j + 1) * TK]


def _fox_prep_kernel(q_ref, k_ref, v_ref, s_ref, bias_ref, qT_ref, ka_ref, vT_ref, own_ref, parts_sc):
    p = pl.program_id(1)
    L = q_ref.shape[2]

    @pl.when(p == 0)
    def _cumulative_log_forget():
        z = s_ref[0] + bias_ref[...]
        c = _cumsum_rows(jax.nn.log_sigmoid(z))
        hi, mid, lo = _split3(c)
        lane = lax.broadcasted_iota(I32, (L, LANES), 1)
        packed = jnp.where(lane < _SM_FL + HEADS, hi,
                           jnp.where(lane < _SM_FL + 2 * HEADS, pltpu.roll(mid, HEADS, 1), pltpu.roll(lo, 2 * HEADS, 1)))
        parts_sc[...] = packed.astype(BF16)

    q_entries, k_entries = [], []
    for e in range(2):
        base = (1 - e) * HEAD_DIM
        for part in range(3):
            src = _SM_FL + part * HEADS + 2 * p + e
            q_entries.append((src, e * LANES + base + part, 1.0))
            k_entries.append((src, (2 + e) * LANES + base + 3 + part, -1.0))
    aug = jnp.dot(parts_sc[...], _placement(q_entries + k_entries, 4 * LANES), preferred_element_type=F32)
    q = q_ref[0, 0].astype(F32) * ATTN_SCALE
    k = k_ref[0, 0].astype(F32)
    for e in range(2):
        data = _lane_row(e * HEAD_DIM, (e + 1) * HEAD_DIM)
        base = (1 - e) * HEAD_DIM
        q_aug = q * data + aug[:, e * LANES:(e + 1) * LANES] + _lane_row(base + 3, base + 6)
        k_aug = k * data + aug[:, (2 + e) * LANES:(3 + e) * LANES] + _lane_row(base, base + 3)
        qT_ref[0, e] = q_aug.T.astype(BF16)
        ka_ref[0, e] = k_aug.astype(BF16)
        own_ref[0, e] = jnp.sum((q_aug * k_aug).T, axis=0, keepdims=True) * LOG2E
    _store_value_chunks(vT_ref, v_ref[0, 0])


def _fox_prep(p4, small, bias_row):
    B, _, L, _ = p4.shape
    nk = L // TK

    def pair_spec(cb):
        return pl.BlockSpec((1, 1, L, LANES), lambda b, p: (b, cb + p, 0, 0))

    return pl.pallas_call(
        _fox_prep_kernel,
        grid=(B, PAIRS),
        in_specs=[pair_spec(_CB_FQ), pair_spec(_CB_FK), pair_spec(_CB_FV),
                  pl.BlockSpec((1, L, LANES), lambda b, p: (b, 0, 0)),
                  pl.BlockSpec((1, LANES), lambda b, p: (0, 0))],
        out_specs=[pl.BlockSpec((1, 2, LANES, L), lambda b, p: (b, p, 0, 0)),
                   pl.BlockSpec((1, 2, L, LANES), lambda b, p: (b, p, 0, 0)),
                   pl.BlockSpec((1, 2, nk, V_ROWS, TK), lambda b, p: (b, p, 0, 0, 0)),
                   pl.BlockSpec((1, 2, 1, L), lambda b, p: (b, p, 0, 0))],
        out_shape=[jax.ShapeDtypeStruct((B, HEADS, LANES, L), BF16),
                   jax.ShapeDtypeStruct((B, HEADS, L, LANES), BF16),
                   jax.ShapeDtypeStruct((B, HEADS, nk, V_ROWS, TK), BF16),
                   jax.ShapeDtypeStruct((B, HEADS, 1, L), F32)],
        scratch_shapes=[pltpu.VMEM((L, LANES), BF16)],
        compiler_params=_params(("parallel", "arbitrary")),
    )(p4, p4, p4, small, bias_row)


def _dsa_prep_kernel(q_ref, k_ref, v_ref, iq_ref, s_ref, qT_ref, ka_ref, vT_ref, iqT_ref, iwT_ref):
    p = pl.program_id(1)
    L = q_ref.shape[2]
    lane = lax.broadcasted_iota(I32, (L, LANES), 1)
    pos = lax.broadcasted_iota(I32, (L, LANES), 0)
    pos_terms = jnp.where(lane == 0, lax.shift_right_logical(pos, 6),
                          jnp.where(lane == 1, pos & 63, jnp.where(lane == 2, 1, 0))).astype(F32).astype(BF16)
    entries = []
    for e in range(2):
        h = jnp.full((1, 4 * LANES), 2 * p + e, I32)
        slope = lax.shift_left(jnp.ones((1, 4 * LANES), I32), 7 - h).astype(F32) * (1.0 / 256.0)
        qc = e * LANES + (1 - e) * HEAD_DIM
        kc = (2 + e) * LANES + (1 - e) * HEAD_DIM
        entries += [(0, qc, -64.0 * slope), (1, qc + 1, -slope), (2, qc + 2, 64.0 * slope), (2, qc + 3, slope),
                    (2, kc, 1.0), (2, kc + 1, 1.0), (0, kc + 2, 1.0), (1, kc + 3, 1.0)]
    aug = jnp.dot(pos_terms, _placement(entries, 4 * LANES), preferred_element_type=F32)
    q = q_ref[0, 0].astype(F32) * ATTN_SCALE
    k = k_ref[0, 0].astype(F32)
    for e in range(2):
        data = _lane_row(e * HEAD_DIM, (e + 1) * HEAD_DIM)
        q_aug = q * data + aug[:, e * LANES:(e + 1) * LANES]
        k_aug = k * data + aug[:, (2 + e) * LANES:(3 + e) * LANES]
        qT_ref[0, e] = q_aug.T.astype(BF16)
        ka_ref[0, e] = k_aug.astype(BF16)
    _store_value_chunks(vT_ref, v_ref[0, 0])
    iqT_ref[0, 0] = iq_ref[0, 0].astype(F32).T.astype(BF16)
    sT = s_ref[0].T
    iwT_ref[0] = sT[_SM_IW:_SM_IW + HEADS, :] * IDX_SCALE


def _dsa_prep(p4, small):
    B, _, L, _ = p4.shape
    nk = L // TK

    def pair_spec(cb):
        return pl.BlockSpec((1, 1, L, LANES), lambda b, p: (b, cb + p, 0, 0))

    return pl.pallas_call(
        _dsa_prep_kernel,
        grid=(B, PAIRS),
        in_specs=[pair_spec(_CB_DQ), pair_spec(_CB_DK), pair_spec(_CB_DV), pair_spec(_CB_IQ),
                  pl.BlockSpec((1, L, LANES), lambda b, p: (b, 0, 0))],
        out_specs=[pl.BlockSpec((1, 2, LANES, L), lambda b, p: (b, p, 0, 0)),
                   pl.BlockSpec((1, 2, L, LANES), lambda b, p: (b, p, 0, 0)),
                   pl.BlockSpec((1, 2, nk, V_ROWS, TK), lambda b, p: (b, p, 0, 0, 0)),
                   pl.BlockSpec((1, 1, LANES, L), lambda b, p: (b, p, 0, 0)),
                   pl.BlockSpec((1, HEADS, L), lambda b, p: (b, 0, 0))],
        out_shape=[jax.ShapeDtypeStruct((B, HEADS, LANES, L), BF16),
                   jax.ShapeDtypeStruct((B, HEADS, L, LANES), BF16),
                   jax.ShapeDtypeStruct((B, HEADS, nk, V_ROWS, TK), BF16),
                   jax.ShapeDtypeStruct((B, PAIRS, LANES, L), BF16),
                   jax.ShapeDtypeStruct((B, HEADS, L), F32)],
        compiler_params=_params(("parallel", "arbitrary")),
    )(p4, p4, p4, p4, small)


def _fold8(x, op):
    return op(x.reshape(x.shape[0] // 8, 8, x.shape[1]), axis=0)


def _causal(shape):
    return lax.broadcasted_iota(I32, shape, 0) <= lax.broadcasted_iota(I32, shape, 1)


L_SUM_MIN, L_SUM_MAX = 1e-30, 1e30
NEAR_BLOCK = 64


def _attend_heads(n_full, bias_fn, ref_score_fn, qT_ref, k_ref, vT_ref, g_sc, acc_sc, o_ref):
    def for_chunks(chunk):
        def body(j, carry):
            chunk(j, False)
            return carry

        lax.fori_loop(0, n_full, body, 0)
        chunk(n_full, True)

    def scores(h, rows, diag, raw):
        s = raw * LOG2E
        bias = bias_fn(rows, diag)
        return s if bias is None else s + bias

    def sweep():
        acc_sc[...] = jnp.zeros(acc_sc.shape, F32)

        def chunk(j, diag):
            rows = pl.ds(pl.multiple_of(j * TK, TK), TK)

            def raw_scores(h):
                return jnp.dot(k_ref[0, h, rows, :], qT_ref[0, h], preferred_element_type=F32)

            nxt = raw_scores(0)
            for h in range(HEADS):
                raw = nxt
                if h + 1 < HEADS:
                    nxt = raw_scores(h + 1)
                p = jnp.exp2(scores(h, rows, diag, raw) - g_sc[h]).astype(BF16)
                acc_sc[h] += jnp.dot(vT_ref[0, h, j], p, preferred_element_type=F32)

        for_chunks(chunk)
        bad = jnp.zeros((1, TQ), I32)
        for p in range(PAIRS):
            halves = []
            for e in range(2):
                acc = acc_sc[2 * p + e]
                l = acc[HEAD_DIM:HEAD_DIM + 1]
                bad = jnp.maximum(bad, jnp.where((l > L_SUM_MIN) & (l < L_SUM_MAX), 0, 1))
                halves.append(acc[:HEAD_DIM] / l)
            o_ref[0, :, p * LANES:(p + 1) * LANES] = jnp.concatenate(halves, axis=0).T.astype(o_ref.dtype)
        return jnp.max(bad) > 0

    def exact_row_maxima():
        acc_sc[:, 0:8, :] = jnp.full((HEADS, 8, TQ), NEG_INF, F32)

        def chunk(j, diag):
            rows = pl.ds(pl.multiple_of(j * TK, TK), TK)
            for h in range(HEADS):
                raw = jnp.dot(k_ref[0, h, rows, :], qT_ref[0, h], preferred_element_type=F32)
                acc_sc[h, 0:8, :] = jnp.maximum(acc_sc[h, 0:8, :], _fold8(scores(h, rows, diag, raw), jnp.max))

        for_chunks(chunk)
        for h in range(HEADS):
            g_sc[h] = jnp.max(acc_sc[h, 0:8, :], axis=0, keepdims=True)

    for h in range(HEADS):
        g_sc[h] = ref_score_fn(h)
    unsafe = sweep()

    @pl.when(unsafe)
    def _redo_with_exact_maxima():
        exact_row_maxima()
        sweep()


def _head_specs(L, index):
    nk = L // TK
    return [pl.BlockSpec((1, HEADS, L, LANES), lambda *g: (index(*g), 0, 0, 0)),
            pl.BlockSpec((1, HEADS, nk, V_ROWS, TK), lambda *g: (index(*g), 0, 0, 0, 0))]


def _head_scratch():
    return [pltpu.VMEM((HEADS, 1, TQ), F32), pltpu.VMEM((HEADS, V_ROWS, TQ), F32)]


def _fox_kernel(qT_ref, own_ref, k_ref, vT_ref, o_ref, g_sc, acc_sc):
    i = pl.program_id(1)

    def bias_fn(rows, diag):
        return jnp.where(_causal((TK, TQ)), 0.0, NEG_INF) if diag else None

    _attend_heads(i, bias_fn, lambda h: own_ref[0, h], qT_ref, k_ref, vT_ref, g_sc, acc_sc, o_ref)


def _fox_attention(qT, own, ka, vT):
    B, _, _, L = qT.shape
    return pl.pallas_call(
        _fox_kernel,
        grid=(B, L // TQ),
        in_specs=[pl.BlockSpec((1, HEADS, LANES, TQ), lambda b, i: (b, 0, 0, i)),
                  pl.BlockSpec((1, HEADS, 1, TQ), lambda b, i: (b, 0, 0, i))] + _head_specs(L, lambda b, i: b),
        out_specs=pl.BlockSpec((1, TQ, PAIRS * LANES), lambda b, i: (b, i, 0)),
        out_shape=jax.ShapeDtypeStruct((B, L, PAIRS * LANES), BF16),
        scratch_shapes=_head_scratch(),
        compiler_params=_params(("parallel", "arbitrary")),
    )(qT, own, ka, vT)


def _count_rows(mask):
    return _fold8(jnp.where(mask, 1, 0).astype(I32), jnp.sum)


def _dsa_kernel(qT_ref, k_ref, vT_ref, iqT_ref, iwT_ref, s_ref, o_ref,
                keys_sc, hi_sc, lo_sc, g_sc, acc_sc):
    i = pl.program_id(1)
    n_chunks = i + 1

    def _select():
        wts = iwT_ref[0]

        def score_chunk(j, masked):
            ks = pl.multiple_of(j * TK, TK)
            ik = s_ref[0, pl.ds(ks, TK), :][:, _SM_IK:_SM_IK + HEAD_DIM].astype(BF16)
            acc = jnp.zeros((TK, TQ), F32)
            for h in range(HEADS):
                iq = iqT_ref[0, h // 2, (h % 2) * HEAD_DIM:(h % 2 + 1) * HEAD_DIM, :]
                rel = jnp.maximum(jnp.dot(ik, iq, preferred_element_type=F32), 0.0)
                acc = acc + wts[h:h + 1, :] * rel
            if masked:
                acc = jnp.where(_causal(acc.shape), acc, NEG_INF)
            bits = pltpu.bitcast(acc, I32)
            sign = lax.shift_right_arithmetic(bits, 31)
            key = (bits ^ (sign & 0x7FFFFFFF)) - sign
            keys_sc[pl.ds(ks, TK), :] = key
            hi_sc[pl.ds(ks, TK), :] = lax.shift_right_arithmetic(key, 16).astype(I16)
            lo_sc[pl.ds(ks, TK), :] = key.astype(I16) ^ jnp.full((), -(2 ** 15), I16)

        def score_body(j, carry):
            score_chunk(j, False)
            return carry

        lax.fori_loop(0, i, score_body, 0)
        score_chunk(i, True)

        def count(pred):
            def body(j, cnt):
                ks = pl.multiple_of(j * TK, TK)
                return cnt + _count_rows(pred(keys_sc[pl.ds(ks, TK), :]))
            cnt8 = lax.fori_loop(0, n_chunks, body, jnp.zeros((8, TQ), I32))
            return jnp.sum(cnt8, axis=0, keepdims=True)

        def count16(ref, pred):
            one, nil = jnp.ones((), I16), jnp.zeros((), I16)

            def body(j, cnt):
                ks = pl.multiple_of(j * TK, TK)
                hit = jnp.where(pred(ref[pl.ds(ks, TK), :]), one, nil)
                for r in range(TK // 16):
                    cnt = cnt + hit[r * 16:(r + 1) * 16, :]
                return cnt
            cnt16 = lax.fori_loop(0, n_chunks, body, jnp.zeros((16, TQ), I16))
            return jnp.sum(cnt16.astype(I32), axis=0, keepdims=True)

        def select16(ref, k):
            zero = jnp.zeros((1, TQ), I16)
            c0 = count16(ref, lambda v: v >= zero)
            ok0 = c0 >= k
            init = (jnp.where(ok0, 0, -(2 ** 15)).astype(I32),
                    jnp.where(ok0, c0, n_chunks * TK),
                    jnp.where(ok0, 0, c0))

            def bit_body(bi, carry):
                t, n_ge, n_gt = carry
                cand = t | jnp.left_shift(jnp.ones((1, TQ), I32), 14 - bi)
                cand16 = cand.astype(I16)
                c = count16(ref, lambda v: v >= cand16)
                ok = c >= k
                return jnp.where(ok, cand, t), jnp.where(ok, c, n_ge), jnp.where(ok, n_gt, c)

            return lax.fori_loop(0, 15, bit_body, init)

        t_hi, _, n_above = select16(hi_sc, TOPK)
        t_hi16 = t_hi.astype(I16)
        k_lo = TOPK - n_above

        def restrict_body(j, carry):
            ks = pl.multiple_of(j * TK, TK)
            in_bucket = hi_sc[pl.ds(ks, TK), :] == t_hi16
            lo_sc[pl.ds(ks, TK), :] = jnp.where(in_bucket, lo_sc[pl.ds(ks, TK), :], jnp.full((), -(2 ** 15), I16))
            return carry

        lax.fori_loop(0, n_chunks, restrict_body, 0)
        t_lo, n_ge_lo, _ = select16(lo_sc, k_lo)
        thr = jnp.left_shift(t_hi, 16) | (t_lo + 2 ** 15)

        excess = jnp.where((n_above + n_ge_lo > TOPK) & (thr > NEG_INF_KEY), 1, 0)

        @pl.when(jnp.max(excess) > 0)
        def _break_ties():
            need = TOPK - count(lambda kk: kk > thr)
            r = lax.broadcasted_iota(I32, (TK, TK), 0)
            c = lax.broadcasted_iota(I32, (TK, TK), 1)
            strict_lower = jnp.where(c < r, 1.0, 0.0).astype(BF16)

            def body(j, seen):
                ks = pl.multiple_of(j * TK, TK)
                kk = keys_sc[pl.ds(ks, TK), :]
                tie = kk == thr
                tie_f = jnp.where(tie, 1.0, 0.0)
                before = jnp.dot(strict_lower, tie_f.astype(BF16), preferred_element_type=F32) + seen
                drop = tie & (before >= need.astype(F32))
                keys_sc[pl.ds(ks, TK), :] = jnp.where(drop, INT_MIN, kk)
                return seen + jnp.sum(tie_f, axis=0, keepdims=True)

            lax.fori_loop(0, n_chunks, body, jnp.zeros((1, TQ), F32))

        def bias_chunk(j, masked, near8):
            ks = pl.multiple_of(j * TK, TK)
            bias = jnp.where(keys_sc[pl.ds(ks, TK), :] >= thr, 0.0, NEG_INF)
            if masked:
                bias = jnp.where(_causal(bias.shape), bias, NEG_INF)
            keys_sc[pl.ds(ks, TK), :] = pltpu.bitcast(bias, I32)
            for blk in range(TK // NEAR_BLOCK):
                any_selected = _fold8(bias[blk * NEAR_BLOCK:(blk + 1) * NEAR_BLOCK], jnp.max) == 0.0
                near8 = jnp.maximum(near8, jnp.where(any_selected, ks + (blk + 1) * NEAR_BLOCK - 1, -1))
            return near8

        near8 = lax.fori_loop(0, i, lambda j, n8: bias_chunk(j, False, n8), jnp.full((8, TQ), -1, I32))
        return jnp.max(bias_chunk(i, True, near8), axis=0, keepdims=True)

    nearest = _select()

    def stored_bias(rows, diag):
        return pltpu.bitcast(keys_sc[rows, :], F32)

    def nearest_key_bias(h):
        query = i * TQ + lax.broadcasted_iota(I32, (1, TQ), 1)
        return (query - nearest).astype(F32) * (-(2.0 ** -(h + 1)) * LOG2E)

    _attend_heads(i, stored_bias, nearest_key_bias, qT_ref, k_ref, vT_ref, g_sc, acc_sc, o_ref)


def _dsa_attention(qT, ka, vT, iqT, iwT, small):
    B, _, _, L = qT.shape
    return pl.pallas_call(
        _dsa_kernel,
        grid=(B, L // TQ),
        in_specs=[pl.BlockSpec((1, HEADS, LANES, TQ), lambda b, i: (b, 0, 0, i))] + _head_specs(L, lambda b, i: b)
        + [pl.BlockSpec((1, PAIRS, LANES, TQ), lambda b, i: (b, 0, 0, i)),
           pl.BlockSpec((1, HEADS, TQ), lambda b, i: (b, 0, i)),
           pl.BlockSpec((1, L, LANES), lambda b, i: (b, 0, 0))],
        out_specs=pl.BlockSpec((1, TQ, PAIRS * LANES), lambda b, i: (b, i, 0)),
        out_shape=jax.ShapeDtypeStruct((B, L, PAIRS * LANES), BF16),
        scratch_shapes=[pltpu.VMEM((L, TQ), I32), pltpu.VMEM((L, TQ), I16), pltpu.VMEM((L, TQ), I16)]
        + _head_scratch(),
        compiler_params=_params(("parallel", "arbitrary")),
    )(qT, ka, vT, iqT, iwT, small)


def _layer_norm(z, g, b):
    mu = jnp.mean(z, axis=-1, keepdims=True)
    zc = z - mu
    var = jnp.mean(zc * zc, axis=-1, keepdims=True)
    return zc * lax.rsqrt(var + LN_EPS) * g + b


def _merge_kernel(x_ref, oa_ref, ob_ref, ga_ref, gb_ref, wa_ref, wb_ref, wo_ref, g_ref, b_ref,
                  out_ref, *, alpha):
    a = jnp.dot(oa_ref[0], wa_ref[...], preferred_element_type=F32)
    b = jnp.dot(ob_ref[0], wb_ref[...], preferred_element_type=F32)
    n_gate_blocks = D_MODEL // LANES
    gate_a = jnp.concatenate([ga_ref[0, c] for c in range(n_gate_blocks)], axis=1).astype(F32)
    gate_b = jnp.concatenate([gb_ref[0, c] for c in range(n_gate_blocks)], axis=1).astype(F32)
    merged = jax.nn.sigmoid(gate_a) * a + jax.nn.sigmoid(gate_b) * b
    y = jnp.dot(merged.astype(BF16), wo_ref[...], preferred_element_type=F32)
    out_ref[0] = _layer_norm(alpha * x_ref[0] + y, g_ref[...], b_ref[...])


def _merge(x, oa, ob, p4, wa, wb, wo, g, b, alpha):
    B, L, D = x.shape
    tm = ROW_TILE
    nb = D // LANES
    row = lambda b_, i: (b_, i, 0)
    return pl.pallas_call(
        functools.partial(_merge_kernel, alpha=alpha),
        grid=(B, L // tm),
        in_specs=[pl.BlockSpec((1, tm, D), row),
                  pl.BlockSpec((1, tm, PAIRS * LANES), row),
                  pl.BlockSpec((1, tm, PAIRS * LANES), row),
                  pl.BlockSpec((1, nb, tm, LANES), lambda b_, i: (b_, _CB_GA // nb, i, 0)),
                  pl.BlockSpec((1, nb, tm, LANES), lambda b_, i: (b_, _CB_GB // nb, i, 0)),
                  _const_spec(wa.shape), _const_spec(wb.shape), _const_spec(wo.shape),
                  _const_spec(g.shape), _const_spec(b.shape)],
        out_specs=pl.BlockSpec((1, tm, D), row),
        out_shape=jax.ShapeDtypeStruct((B, L, D), F32),
        compiler_params=_params(("parallel", "parallel")),
    )(x, oa, ob, p4, p4, wa, wb, wo, g, b)


FF_CHUNK = 256


def _ffn_kernel(x_ref, wi_ref, wo_ref, g_ref, b_ref, out_ref, h_sc, *, alpha):
    x = x_ref[0]
    xb = x.astype(BF16)
    for c in range(D_FF // FF_CHUNK):
        lo = c * FF_CHUNK
        gate = jnp.dot(xb, wi_ref[:, lo:lo + FF_CHUNK], preferred_element_type=F32)
        up = jnp.dot(xb, wi_ref[:, D_FF + lo:D_FF + lo + FF_CHUNK], preferred_element_type=F32)
        h_sc[:, lo:lo + FF_CHUNK] = (jax.nn.silu(gate) * up).astype(BF16)
    y = jnp.dot(h_sc[...], wo_ref[...], preferred_element_type=F32)
    out_ref[0] = _layer_norm(alpha * x + y, g_ref[...], b_ref[...])


def _ffn(x, wi, wo, g, b, alpha):
    B, L, D = x.shape
    tm = ROW_TILE
    row = lambda b_, i: (b_, i, 0)
    return pl.pallas_call(
        functools.partial(_ffn_kernel, alpha=alpha),
        grid=(B, L // tm),
        in_specs=[pl.BlockSpec((1, tm, D), row),
                  _const_spec(wi.shape), _const_spec(wo.shape),
                  _const_spec(g.shape), _const_spec(b.shape)],
        out_specs=pl.BlockSpec((1, tm, D), row),
        out_shape=jax.ShapeDtypeStruct((B, L, D), F32),
        scratch_shapes=[pltpu.VMEM((tm, D_FF), BF16)],
        compiler_params=_params(("parallel", "parallel")),
    )(x, wi, wo, g, b)


def _split_w_in(w):
    fox = w[:, 0:_OFF_FLOGIT]
    dsa = w[:, _OFF_DQ:_OFF_DQ + 3 * 512]
    iq = w[:, _OFF_DQ + 3 * 512:_OFF_IK]
    gates = w[:, _OFF_GA:_N_IN]
    w_main = jnp.concatenate([fox, dsa, gates, iq], axis=1).astype(BF16)
    pad = jnp.zeros((w.shape[0], LANES - HEAD_DIM - 2 * HEADS), w.dtype)
    w_small = jnp.concatenate([w[:, _OFF_IK:_OFF_IW], w[:, _OFF_IW:_OFF_GA],
                               w[:, _OFF_FLOGIT:_OFF_DQ], pad], axis=1).astype(BF16)
    return w_main, w_small


def kernel(x, w_in, b_forget, w_branch_a, w_branch_b, w_out, ln1_g, ln1_b, w_ffn_in, w_ffn_out, ln2_g, ln2_b):
    depth = w_in.shape[0]
    alpha = (2.0 * depth) ** 0.25
    for l in range(depth):
        w_main, w_small = _split_w_in(w_in[l])
        bias_row = jnp.zeros((1, LANES), F32).at[0, _SM_FL:_SM_FL + HEADS].set(b_forget[l])
        p4, small = _project(x, w_main, w_small)
        fqT, fka, fvT, f_own = _fox_prep(p4, small, bias_row)
        dqT, dka, dvT, iqT, iwT = _dsa_prep(p4, small)
        o_a = _fox_attention(fqT, f_own, fka, fvT)
        o_b = _dsa_attention(dqT, dka, dvT, iqT, iwT, small)
        x = _merge(x, o_a, o_b, p4,
                   w_branch_a[l].astype(BF16), w_branch_b[l].astype(BF16), w_out[l].astype(BF16),
                   ln1_g[l][None, :], ln1_b[l][None, :], alpha)
        x = _ffn(x, w_ffn_in[l].astype(BF16), w_ffn_out[l].astype(BF16),
                 ln2_g[l][None, :], ln2_b[l][None, :], alpha)
    return x
```

```python
import functools

import numpy as np
import jax
import jax.numpy as jnp
from jax import lax
from jax.experimental import pallas as pl
from jax.experimental.pallas import tpu as pltpu

F32 = jnp.float32
BF16 = jnp.bfloat16
I32 = jnp.int32
I16 = jnp.int16

D_MODEL = 1024
HEADS = 8
HEAD_DIM = 64
PAIRS = HEADS // 2
D_FF = 2816
TOPK = 256
LANES = 128
NEG_INF = -1e30
LN_EPS = 1e-5
ATTN_SCALE = HEAD_DIM ** -0.5
IDX_SCALE = (HEADS ** -0.5) * (HEAD_DIM ** -0.5)

_OFF_FLOGIT = 1536
_OFF_DQ = 1544
_OFF_IK = 3592
_OFF_IW = 3656
_OFF_GA = 3664
_N_IN = 5712

_CB_FQ, _CB_FK, _CB_GA, _CB_GB = 0, 4, 8, 16
N_KEPT_BLOCKS = 24
_G_FV, _G_DQ, _G_DK, _G_DV, _G_IQ = 6, 7, 8, 9, 10
GROUP = 4 * LANES
N_MAIN = 11 * GROUP
_SM_IK, _SM_IW, _SM_FL = 0, 64, 72

ROW_TILE = 512
TQ = 512
TK = 512
CUMSUM_BLOCK = 256
V_ROWS = HEAD_DIM + 16
LOG2E = 1.4426950408889634

INT_MIN = -(2 ** 31)


def _monotone_key_of(value):
    b = int(np.float32(value).view(np.int32))
    return (b ^ ((b >> 31) & 0x7FFFFFFF)) - (b >> 31)


NEG_INF_KEY = _monotone_key_of(NEG_INF)
VMEM_LIMIT = 60 * 1024 * 1024


def _params(sem):
    return pltpu.CompilerParams(dimension_semantics=sem, vmem_limit_bytes=VMEM_LIMIT)


def _const_spec(shape):
    nd = len(shape)
    return pl.BlockSpec(shape, lambda *_: (0,) * nd, pipeline_mode=pl.Buffered(1))


def _lane_row(lo, hi, value=1.0):
    lane = lax.broadcasted_iota(I32, (1, LANES), 1)
    return jnp.where((lane >= lo) & (lane < hi), value, 0.0)


def _alibi_placement():
    put = np.zeros((LANES, 2 * HEADS * LANES), np.float32)
    for h in range(HEADS):
        slope = 2.0 ** -(h + 1)
        base = (1 - h % 2) * HEAD_DIM
        qc, kc = h * LANES + base, (HEADS + h) * LANES + base
        put[0, qc], put[1, qc + 1], put[2, qc + 2], put[2, qc + 3] = -64.0 * slope, -slope, 64.0 * slope, slope
        put[2, kc], put[2, kc + 1], put[0, kc + 2], put[1, kc + 3] = 1.0, 1.0, 1.0, 1.0
    return jnp.asarray(put, BF16)


def _proj_kernel(x_ref, wm_ref, ws_ref, put_ref,
                 kept_ref, os_ref, fvT_ref, dqT_ref, dka_ref, dvT_ref, iqT_ref, iwT_ref):
    tm = x_ref.shape[1]
    xb = x_ref[0].astype(BF16)
    small = jnp.dot(xb, ws_ref[...], preferred_element_type=F32)
    os_ref[0] = small
    iwT_ref[0] = small.T[_SM_IW:_SM_IW + HEADS, :] * IDX_SCALE

    def group(g):
        return jnp.dot(xb, wm_ref[:, g * GROUP:(g + 1) * GROUP], preferred_element_type=F32)

    for g in range(N_KEPT_BLOCKS // 4):
        r = group(g).astype(BF16)
        for c in range(4):
            kept_ref[0, 4 * g + c] = r[:, c * LANES:(c + 1) * LANES]

    def store_values(ref, r):
        ones = jnp.ones((V_ROWS - HEAD_DIM, tm), BF16)
        for p in range(PAIRS):
            vT = r[:, p * LANES:(p + 1) * LANES].T.astype(BF16)
            for e in range(2):
                ref[0, 2 * p + e, 0] = jnp.concatenate([vT[e * HEAD_DIM:(e + 1) * HEAD_DIM], ones], axis=0)

    store_values(fvT_ref, group(_G_FV))
    store_values(dvT_ref, group(_G_DV))

    lane = lax.broadcasted_iota(I32, (tm, LANES), 1)
    pos = lax.broadcasted_iota(I32, (tm, LANES), 0) + pl.program_id(1) * tm
    pos_terms = jnp.where(lane == 0, lax.shift_right_logical(pos, 6),
                          jnp.where(lane == 1, pos & 63, jnp.where(lane == 2, 1, 0))).astype(F32).astype(BF16)
    aug = jnp.dot(pos_terms, put_ref[...], preferred_element_type=F32)
    q = group(_G_DQ) * ATTN_SCALE
    k = group(_G_DK)
    for h in range(HEADS):
        p, e = divmod(h, 2)
        data = _lane_row(e * HEAD_DIM, (e + 1) * HEAD_DIM)
        q_aug = q[:, p * LANES:(p + 1) * LANES] * data + aug[:, h * LANES:(h + 1) * LANES]
        k_aug = k[:, p * LANES:(p + 1) * LANES] * data + aug[:, (HEADS + h) * LANES:(HEADS + h + 1) * LANES]
        dqT_ref[0, h] = q_aug.T.astype(BF16)
        dka_ref[0, h] = k_aug.astype(BF16)
    iq = group(_G_IQ)
    for p in range(PAIRS):
        iqT_ref[0, p] = iq[:, p * LANES:(p + 1) * LANES].T.astype(BF16)


def _project(x, w_main, w_small):
    B, L, D = x.shape
    tm = TK
    nk = L // TK
    put = _alibi_placement()
    return pl.pallas_call(
        _proj_kernel,
        grid=(B, nk),
        in_specs=[pl.BlockSpec((1, tm, D), lambda b, i: (b, i, 0)),
                  _const_spec((D, N_MAIN)), _const_spec((D, LANES)), _const_spec(put.shape)],
        out_specs=[pl.BlockSpec((1, N_KEPT_BLOCKS, tm, LANES), lambda b, i: (b, 0, i, 0)),
                   pl.BlockSpec((1, tm, LANES), lambda b, i: (b, i, 0)),
                   pl.BlockSpec((1, HEADS, 1, V_ROWS, TK), lambda b, i: (b, 0, i, 0, 0)),
                   pl.BlockSpec((1, HEADS, LANES, tm), lambda b, i: (b, 0, 0, i)),
                   pl.BlockSpec((1, HEADS, tm, LANES), lambda b, i: (b, 0, i, 0)),
                   pl.BlockSpec((1, HEADS, 1, V_ROWS, TK), lambda b, i: (b, 0, i, 0, 0)),
                   pl.BlockSpec((1, PAIRS, LANES, tm), lambda b, i: (b, 0, 0, i)),
                   pl.BlockSpec((1, HEADS, tm), lambda b, i: (b, 0, i))],
        out_shape=[jax.ShapeDtypeStruct((B, N_KEPT_BLOCKS, L, LANES), BF16),
                   jax.ShapeDtypeStruct((B, L, LANES), F32),
                   jax.ShapeDtypeStruct((B, HEADS, nk, V_ROWS, TK), BF16),
                   jax.ShapeDtypeStruct((B, HEADS, LANES, L), BF16),
                   jax.ShapeDtypeStruct((B, HEADS, L, LANES), BF16),
                   jax.ShapeDtypeStruct((B, HEADS, nk, V_ROWS, TK), BF16),
                   jax.ShapeDtypeStruct((B, PAIRS, LANES, L), BF16),
                   jax.ShapeDtypeStruct((B, HEADS, L), F32)],
        compiler_params=_params(("parallel", "parallel")),
    )(x, w_main, w_small, put)


def _split3(v):
    hi = v.astype(BF16).astype(F32)
    r = v - hi
    mid = r.astype(BF16).astype(F32)
    lo = (r - mid).astype(BF16).astype(F32)
    return hi, mid, lo


def _cumsum_rows(x):
    n = CUMSUM_BLOCK
    L = x.shape[0]
    r = lax.broadcasted_iota(I32, (n, n), 0)
    c = lax.broadcasted_iota(I32, (n, n), 1)
    tri = jnp.where(c <= r, 1.0, 0.0).astype(BF16)
    carry = jnp.zeros((1, x.shape[1]), F32)
    outs = []
    w = x.shape[1]
    for blk in range(L // n):
        parts = jnp.concatenate([p.astype(BF16) for p in _split3(x[blk * n:(blk + 1) * n])], axis=1)
        y3 = jnp.dot(tri, parts, preferred_element_type=F32)
        y = (y3[:, :w] + y3[:, w:2 * w] + y3[:, 2 * w:]) + carry
        carry = y[n - 1:n, :]
        outs.append(y)
    return jnp.concatenate(outs, axis=0)


def _placement(entries, n_cols):
    r = lax.broadcasted_iota(I32, (LANES, n_cols), 0)
    c = lax.broadcasted_iota(I32, (LANES, n_cols), 1)
    out = jnp.zeros((LANES, n_cols), F32)
    for src, dst, coef in entries:
        out = jnp.where((r == src) & (c == dst), coef, out)
    return out.astype(BF16)


def _fox_prep_kernel(q_ref, k_ref, s_ref, bias_ref, qT_ref, ka_ref, own_ref, parts_sc):
    p = pl.program_id(1)
    L = q_ref.shape[2]

    @pl.when(p == 0)
    def _cumulative_log_forget():
        z = s_ref[0] + bias_ref[...]
        c = _cumsum_rows(jax.nn.log_sigmoid(z))
        hi, mid, lo = _split3(c)
        lane = lax.broadcasted_iota(I32, (L, LANES), 1)
        packed = jnp.where(lane < _SM_FL + HEADS, hi,
                           jnp.where(lane < _SM_FL + 2 * HEADS, pltpu.roll(mid, HEADS, 1), pltpu.roll(lo, 2 * HEADS, 1)))
        parts_sc[...] = packed.astype(BF16)

    q_entries, k_entries = [], []
    for e in range(2):
        base = (1 - e) * HEAD_DIM
        for part in range(3):
            src = _SM_FL + part * HEADS + 2 * p + e
            q_entries.append((src, e * LANES + base + part, 1.0))
            k_entries.append((src, (2 + e) * LANES + base + 3 + part, -1.0))
    aug = jnp.dot(parts_sc[...], _placement(q_entries + k_entries, 4 * LANES), preferred_element_type=F32)
    q = q_ref[0, 0].astype(F32) * ATTN_SCALE
    k = k_ref[0, 0].astype(F32)
    for e in range(2):
        data = _lane_row(e * HEAD_DIM, (e + 1) * HEAD_DIM)
        base = (1 - e) * HEAD_DIM
        q_aug = q * data + aug[:, e * LANES:(e + 1) * LANES] + _lane_row(base + 3, base + 6)
        k_aug = k * data + aug[:, (2 + e) * LANES:(3 + e) * LANES] + _lane_row(base, base + 3)
        qT_ref[0, e] = q_aug.T.astype(BF16)
        ka_ref[0, e] = k_aug.astype(BF16)
        own_ref[0, e] = jnp.sum((q_aug * k_aug).T, axis=0, keepdims=True) * LOG2E


def _fox_prep(kept, small, bias_row):
    B, _, L, _ = kept.shape

    def pair_spec(cb):
        return pl.BlockSpec((1, 1, L, LANES), lambda b, p: (b, cb + p, 0, 0))

    return pl.pallas_call(
        _fox_prep_kernel,
        grid=(B, PAIRS),
        in_specs=[pair_spec(_CB_FQ), pair_spec(_CB_FK),
                  pl.BlockSpec((1, L, LANES), lambda b, p: (b, 0, 0)),
                  pl.BlockSpec((1, LANES), lambda b, p: (0, 0))],
        out_specs=[pl.BlockSpec((1, 2, LANES, L), lambda b, p: (b, p, 0, 0)),
                   pl.BlockSpec((1, 2, L, LANES), lambda b, p: (b, p, 0, 0)),
                   pl.BlockSpec((1, 2, 1, L), lambda b, p: (b, p, 0, 0))],
        out_shape=[jax.ShapeDtypeStruct((B, HEADS, LANES, L), BF16),
                   jax.ShapeDtypeStruct((B, HEADS, L, LANES), BF16),
                   jax.ShapeDtypeStruct((B, HEADS, 1, L), F32)],
        scratch_shapes=[pltpu.VMEM((L, LANES), BF16)],
        compiler_params=_params(("parallel", "arbitrary")),
    )(kept, kept, small, bias_row)


def _fold8(x, op):
    return op(x.reshape(x.shape[0] // 8, 8, x.shape[1]), axis=0)


def _causal(shape):
    return lax.broadcasted_iota(I32, shape, 0) <= lax.broadcasted_iota(I32, shape, 1)


L_SUM_MIN, L_SUM_MAX = 1e-30, 1e30
NEAR_BLOCK = 64


def _attend_heads(n_full, bias_fn, ref_score_fn, qT_ref, k_ref, vT_ref, g_sc, acc_sc, o_ref):
    def for_chunks(chunk):
        def body(j, carry):
            chunk(j, False)
            return carry

        lax.fori_loop(0, n_full, body, 0)
        chunk(n_full, True)

    def scores(h, rows, diag, raw):
        s = raw * LOG2E
        bias = bias_fn(rows, diag)
        return s if bias is None else s + bias

    def sweep():
        acc_sc[...] = jnp.zeros(acc_sc.shape, F32)

        def chunk(j, diag):
            rows = pl.ds(pl.multiple_of(j * TK, TK), TK)

            def raw_scores(h):
                return jnp.dot(k_ref[0, h, rows, :], qT_ref[0, h], preferred_element_type=F32)

            nxt = raw_scores(0)
            for h in range(HEADS):
                raw = nxt
                if h + 1 < HEADS:
                    nxt = raw_scores(h + 1)
                p = jnp.exp2(scores(h, rows, diag, raw) - g_sc[h]).astype(BF16)
                acc_sc[h] += jnp.dot(vT_ref[0, h, j], p, preferred_element_type=F32)

        for_chunks(chunk)
        bad = jnp.zeros((1, TQ), I32)
        for p in range(PAIRS):
            halves = []
            for e in range(2):
                acc = acc_sc[2 * p + e]
                l = acc[HEAD_DIM:HEAD_DIM + 1]
                bad = jnp.maximum(bad, jnp.where((l > L_SUM_MIN) & (l < L_SUM_MAX), 0, 1))
                halves.append(acc[:HEAD_DIM] / l)
            o_ref[0, :, p * LANES:(p + 1) * LANES] = jnp.concatenate(halves, axis=0).T.astype(o_ref.dtype)
        return jnp.max(bad) > 0

    def exact_row_maxima():
        acc_sc[:, 0:8, :] = jnp.full((HEADS, 8, TQ), NEG_INF, F32)

        def chunk(j, diag):
            rows = pl.ds(pl.multiple_of(j * TK, TK), TK)
            for h in range(HEADS):
                raw = jnp.dot(k_ref[0, h, rows, :], qT_ref[0, h], preferred_element_type=F32)
                acc_sc[h, 0:8, :] = jnp.maximum(acc_sc[h, 0:8, :], _fold8(scores(h, rows, diag, raw), jnp.max))

        for_chunks(chunk)
        for h in range(HEADS):
            g_sc[h] = jnp.max(acc_sc[h, 0:8, :], axis=0, keepdims=True)

    for h in range(HEADS):
        g_sc[h] = ref_score_fn(h)
    unsafe = sweep()

    @pl.when(unsafe)
    def _redo_with_exact_maxima():
        exact_row_maxima()
        sweep()


def _head_specs(L, index):
    nk = L // TK
    return [pl.BlockSpec((1, HEADS, L, LANES), lambda *g: (index(*g), 0, 0, 0)),
            pl.BlockSpec((1, HEADS, nk, V_ROWS, TK), lambda *g: (index(*g), 0, 0, 0, 0))]


def _head_scratch():
    return [pltpu.VMEM((HEADS, 1, TQ), F32), pltpu.VMEM((HEADS, V_ROWS, TQ), F32)]


def _fox_kernel(qT_ref, own_ref, k_ref, vT_ref, o_ref, g_sc, acc_sc):
    i = pl.program_id(1)

    def bias_fn(rows, diag):
        return jnp.where(_causal((TK, TQ)), 0.0, NEG_INF) if diag else None

    _attend_heads(i, bias_fn, lambda h: own_ref[0, h], qT_ref, k_ref, vT_ref, g_sc, acc_sc, o_ref)


def _fox_attention(qT, own, ka, vT):
    B, _, _, L = qT.shape
    return pl.pallas_call(
        _fox_kernel,
        grid=(B, L // TQ),
        in_specs=[pl.BlockSpec((1, HEADS, LANES, TQ), lambda b, i: (b, 0, 0, i)),
                  pl.BlockSpec((1, HEADS, 1, TQ), lambda b, i: (b, 0, 0, i))] + _head_specs(L, lambda b, i: b),
        out_specs=pl.BlockSpec((1, TQ, PAIRS * LANES), lambda b, i: (b, i, 0)),
        out_shape=jax.ShapeDtypeStruct((B, L, PAIRS * LANES), BF16),
        scratch_shapes=_head_scratch(),
        compiler_params=_params(("parallel", "arbitrary")),
    )(qT, own, ka, vT)


def _count_rows(mask):
    return _fold8(jnp.where(mask, 1, 0).astype(I32), jnp.sum)


def _dsa_kernel(qT_ref, k_ref, vT_ref, iqT_ref, iwT_ref, s_ref, o_ref,
                keys_sc, hi_sc, lo_sc, g_sc, acc_sc):
    i = pl.program_id(1)
    n_chunks = i + 1

    def _select():
        wts = iwT_ref[0]

        def score_chunk(j, masked):
            ks = pl.multiple_of(j * TK, TK)
            ik = s_ref[0, pl.ds(ks, TK), :][:, _SM_IK:_SM_IK + HEAD_DIM].astype(BF16)
            acc = jnp.zeros((TK, TQ), F32)
            for h in range(HEADS):
                iq = iqT_ref[0, h // 2, (h % 2) * HEAD_DIM:(h % 2 + 1) * HEAD_DIM, :]
                rel = jnp.maximum(jnp.dot(ik, iq, preferred_element_type=F32), 0.0)
                acc = acc + wts[h:h + 1, :] * rel
            if masked:
                acc = jnp.where(_causal(acc.shape), acc, NEG_INF)
            bits = pltpu.bitcast(acc, I32)
            sign = lax.shift_right_arithmetic(bits, 31)
            key = (bits ^ (sign & 0x7FFFFFFF)) - sign
            keys_sc[pl.ds(ks, TK), :] = key
            hi_sc[pl.ds(ks, TK), :] = lax.shift_right_arithmetic(key, 16).astype(I16)
            lo_sc[pl.ds(ks, TK), :] = key.astype(I16) ^ jnp.full((), -(2 ** 15), I16)

        def score_body(j, carry):
            score_chunk(j, False)
            return carry

        lax.fori_loop(0, i, score_body, 0)
        score_chunk(i, True)

        def count(pred):
            def body(j, cnt):
                ks = pl.multiple_of(j * TK, TK)
                return cnt + _count_rows(pred(keys_sc[pl.ds(ks, TK), :]))
            cnt8 = lax.fori_loop(0, n_chunks, body, jnp.zeros((8, TQ), I32))
            return jnp.sum(cnt8, axis=0, keepdims=True)

        def count16(ref, pred):
            one, nil = jnp.ones((), I16), jnp.zeros((), I16)

            def body(j, cnt):
                ks = pl.multiple_of(j * TK, TK)
                hit = jnp.where(pred(ref[pl.ds(ks, TK), :]), one, nil)
                for r in range(TK // 16):
                    cnt = cnt + hit[r * 16:(r + 1) * 16, :]
                return cnt
            cnt16 = lax.fori_loop(0, n_chunks, body, jnp.zeros((16, TQ), I16))
            return jnp.sum(cnt16.astype(I32), axis=0, keepdims=True)

        def select16(ref, k):
            zero = jnp.zeros((1, TQ), I16)
            c0 = count16(ref, lambda v: v >= zero)
            ok0 = c0 >= k
            init = (jnp.where(ok0, 0, -(2 ** 15)).astype(I32),
                    jnp.where(ok0, c0, n_chunks * TK),
                    jnp.where(ok0, 0, c0))

            def bit_body(bi, carry):
                t, n_ge, n_gt = carry
                cand = t | jnp.left_shift(jnp.ones((1, TQ), I32), 14 - bi)
                cand16 = cand.astype(I16)
                c = count16(ref, lambda v: v >= cand16)
                ok = c >= k
                return jnp.where(ok, cand, t), jnp.where(ok, c, n_ge), jnp.where(ok, n_gt, c)

            return lax.fori_loop(0, 15, bit_body, init)

        t_hi, _, n_above = select16(hi_sc, TOPK)
        t_hi16 = t_hi.astype(I16)
        k_lo = TOPK - n_above

        def restrict_body(j, carry):
            ks = pl.multiple_of(j * TK, TK)
            in_bucket = hi_sc[pl.ds(ks, TK), :] == t_hi16
            lo_sc[pl.ds(ks, TK), :] = jnp.where(in_bucket, lo_sc[pl.ds(ks, TK), :], jnp.full((), -(2 ** 15), I16))
            return carry

        lax.fori_loop(0, n_chunks, restrict_body, 0)
        t_lo, n_ge_lo, _ = select16(lo_sc, k_lo)
        thr = jnp.left_shift(t_hi, 16) | (t_lo + 2 ** 15)

        excess = jnp.where((n_above + n_ge_lo > TOPK) & (thr > NEG_INF_KEY), 1, 0)

        @pl.when(jnp.max(excess) > 0)
        def _break_ties():
            need = TOPK - count(lambda kk: kk > thr)
            r = lax.broadcasted_iota(I32, (TK, TK), 0)
            c = lax.broadcasted_iota(I32, (TK, TK), 1)
            strict_lower = jnp.where(c < r, 1.0, 0.0).astype(BF16)

            def body(j, seen):
                ks = pl.multiple_of(j * TK, TK)
                kk = keys_sc[pl.ds(ks, TK), :]
                tie = kk == thr
                tie_f = jnp.where(tie, 1.0, 0.0)
                before = jnp.dot(strict_lower, tie_f.astype(BF16), preferred_element_type=F32) + seen
                drop = tie & (before >= need.astype(F32))
                keys_sc[pl.ds(ks, TK), :] = jnp.where(drop, INT_MIN, kk)
                return seen + jnp.sum(tie_f, axis=0, keepdims=True)

            lax.fori_loop(0, n_chunks, body, jnp.zeros((1, TQ), F32))

        def bias_chunk(j, masked, near8):
            ks = pl.multiple_of(j * TK, TK)
            bias = jnp.where(keys_sc[pl.ds(ks, TK), :] >= thr, 0.0, NEG_INF)
            if masked:
                bias = jnp.where(_causal(bias.shape), bias, NEG_INF)
            keys_sc[pl.ds(ks, TK), :] = pltpu.bitcast(bias, I32)
            for blk in range(TK // NEAR_BLOCK):
                any_selected = _fold8(bias[blk * NEAR_BLOCK:(blk + 1) * NEAR_BLOCK], jnp.max) == 0.0
                near8 = jnp.maximum(near8, jnp.where(any_selected, ks + (blk + 1) * NEAR_BLOCK - 1, -1))
            return near8

        near8 = lax.fori_loop(0, i, lambda j, n8: bias_chunk(j, False, n8), jnp.full((8, TQ), -1, I32))
        return jnp.max(bias_chunk(i, True, near8), axis=0, keepdims=True)

    nearest = _select()

    def stored_bias(rows, diag):
        return pltpu.bitcast(keys_sc[rows, :], F32)

    def nearest_key_bias(h):
        query = i * TQ + lax.broadcasted_iota(I32, (1, TQ), 1)
        return (query - nearest).astype(F32) * (-(2.0 ** -(h + 1)) * LOG2E)

    _attend_heads(i, stored_bias, nearest_key_bias, qT_ref, k_ref, vT_ref, g_sc, acc_sc, o_ref)


def _dsa_attention(qT, ka, vT, iqT, iwT, small):
    B, _, _, L = qT.shape
    return pl.pallas_call(
        _dsa_kernel,
        grid=(B, L // TQ),
        in_specs=[pl.BlockSpec((1, HEADS, LANES, TQ), lambda b, i: (b, 0, 0, i))] + _head_specs(L, lambda b, i: b)
        + [pl.BlockSpec((1, PAIRS, LANES, TQ), lambda b, i: (b, 0, 0, i)),
           pl.BlockSpec((1, HEADS, TQ), lambda b, i: (b, 0, i)),
           pl.BlockSpec((1, L, LANES), lambda b, i: (b, 0, 0))],
        out_specs=pl.BlockSpec((1, TQ, PAIRS * LANES), lambda b, i: (b, i, 0)),
        out_shape=jax.ShapeDtypeStruct((B, L, PAIRS * LANES), BF16),
        scratch_shapes=[pltpu.VMEM((L, TQ), I32), pltpu.VMEM((L, TQ), I16), pltpu.VMEM((L, TQ), I16)]
        + _head_scratch(),
        compiler_params=_params(("parallel", "arbitrary")),
    )(qT, ka, vT, iqT, iwT, small)


def _layer_norm(z, g, b):
    mu = jnp.mean(z, axis=-1, keepdims=True)
    zc = z - mu
    var = jnp.mean(zc * zc, axis=-1, keepdims=True)
    return zc * lax.rsqrt(var + LN_EPS) * g + b


def _merge_kernel(x_ref, oa_ref, ob_ref, ga_ref, gb_ref, wa_ref, wb_ref, wo_ref, g_ref, b_ref,
                  out_ref, *, alpha):
    a = jnp.dot(oa_ref[0], wa_ref[...], preferred_element_type=F32)
    b = jnp.dot(ob_ref[0], wb_ref[...], preferred_element_type=F32)
    n_gate_blocks = D_MODEL // LANES
    gate_a = jnp.concatenate([ga_ref[0, c] for c in range(n_gate_blocks)], axis=1).astype(F32)
    gate_b = jnp.concatenate([gb_ref[0, c] for c in range(n_gate_blocks)], axis=1).astype(F32)
    merged = jax.nn.sigmoid(gate_a) * a + jax.nn.sigmoid(gate_b) * b
    y = jnp.dot(merged.astype(BF16), wo_ref[...], preferred_element_type=F32)
    out_ref[0] = _layer_norm(alpha * x_ref[0] + y, g_ref[...], b_ref[...])


def _merge(x, oa, ob, p4, wa, wb, wo, g, b, alpha):
    B, L, D = x.shape
    tm = ROW_TILE
    nb = D // LANES
    row = lambda b_, i: (b_, i, 0)
    return pl.pallas_call(
        functools.partial(_merge_kernel, alpha=alpha),
        grid=(B, L // tm),
        in_specs=[pl.BlockSpec((1, tm, D), row),
                  pl.BlockSpec((1, tm, PAIRS * LANES), row),
                  pl.BlockSpec((1, tm, PAIRS * LANES), row),
                  pl.BlockSpec((1, nb, tm, LANES), lambda b_, i: (b_, _CB_GA // nb, i, 0)),
                  pl.BlockSpec((1, nb, tm, LANES), lambda b_, i: (b_, _CB_GB // nb, i, 0)),
                  _const_spec(wa.shape), _const_spec(wb.shape), _const_spec(wo.shape),
                  _const_spec(g.shape), _const_spec(b.shape)],
        out_specs=pl.BlockSpec((1, tm, D), row),
        out_shape=jax.ShapeDtypeStruct((B, L, D), F32),
        compiler_params=_params(("parallel", "parallel")),
    )(x, oa, ob, p4, p4, wa, wb, wo, g, b)


FF_CHUNK = 256


def _ffn_kernel(x_ref, wi_ref, wo_ref, g_ref, b_ref, out_ref, h_sc, *, alpha):
    x = x_ref[0]
    xb = x.astype(BF16)
    for c in range(D_FF // FF_CHUNK):
        lo = c * FF_CHUNK
        gate = jnp.dot(xb, wi_ref[:, lo:lo + FF_CHUNK], preferred_element_type=F32)
        up = jnp.dot(xb, wi_ref[:, D_FF + lo:D_FF + lo + FF_CHUNK], preferred_element_type=F32)
        h_sc[:, lo:lo + FF_CHUNK] = (jax.nn.silu(gate) * up).astype(BF16)
    y = jnp.dot(h_sc[...], wo_ref[...], preferred_element_type=F32)
    out_ref[0] = _layer_norm(alpha * x + y, g_ref[...], b_ref[...])


def _ffn(x, wi, wo, g, b, alpha):
    B, L, D = x.shape
    tm = ROW_TILE
    row = lambda b_, i: (b_, i, 0)
    return pl.pallas_call(
        functools.partial(_ffn_kernel, alpha=alpha),
        grid=(B, L // tm),
        in_specs=[pl.BlockSpec((1, tm, D), row),
                  _const_spec(wi.shape), _const_spec(wo.shape),
                  _const_spec(g.shape), _const_spec(b.shape)],
        out_specs=pl.BlockSpec((1, tm, D), row),
        out_shape=jax.ShapeDtypeStruct((B, L, D), F32),
        scratch_shapes=[pltpu.VMEM((tm, D_FF), BF16)],
        compiler_params=_params(("parallel", "parallel")),
    )(x, wi, wo, g, b)


def _split_w_in(w):
    fq_fk = w[:, 0:2 * GROUP]
    fv = w[:, 2 * GROUP:_OFF_FLOGIT]
    dsa_iq = w[:, _OFF_DQ:_OFF_IK]
    gates = w[:, _OFF_GA:_N_IN]
    w_main = jnp.concatenate([fq_fk, gates, fv, dsa_iq], axis=1).astype(BF16)
    pad = jnp.zeros((w.shape[0], LANES - HEAD_DIM - 2 * HEADS), w.dtype)
    w_small = jnp.concatenate([w[:, _OFF_IK:_OFF_IW], w[:, _OFF_IW:_OFF_GA],
                               w[:, _OFF_FLOGIT:_OFF_DQ], pad], axis=1).astype(BF16)
    return w_main, w_small


def kernel(x, w_in, b_forget, w_branch_a, w_branch_b, w_out, ln1_g, ln1_b, w_ffn_in, w_ffn_out, ln2_g, ln2_b):
    depth = w_in.shape[0]
    alpha = (2.0 * depth) ** 0.25
    for l in range(depth):
        w_main, w_small = _split_w_in(w_in[l])
        bias_row = jnp.zeros((1, LANES), F32).at[0, _SM_FL:_SM_FL + HEADS].set(b_forget[l])
        kept, small, fvT, dqT, dka, dvT, iqT, iwT = _project(x, w_main, w_small)
        fqT, fka, f_own = _fox_prep(kept, small, bias_row)
        o_a = _fox_attention(fqT, f_own, fka, fvT)
        o_b = _dsa_attention(dqT, dka, dvT, iqT, iwT, small)
        x = _merge(x, o_a, o_b, kept,
                   w_branch_a[l].astype(BF16), w_branch_b[l].astype(BF16), w_out[l].astype(BF16),
                   ln1_g[l][None, :], ln1_b[l][None, :], alpha)
        x = _ffn(x, w_ffn_in[l].astype(BF16), w_ffn_out[l].astype(BF16),
                 ln2_g[l][None, :], ln2_b[l][None, :], alpha)
    return x
```

```python
import functools

import numpy as np
import jax
import jax.numpy as jnp
from jax import lax
from jax.experimental import pallas as pl
from jax.experimental.pallas import tpu as pltpu

F32 = jnp.float32
BF16 = jnp.bfloat16
I32 = jnp.int32
I16 = jnp.int16

D_MODEL = 1024
HEADS = 8
HEAD_DIM = 64
PAIRS = HEADS // 2
D_FF = 2816
TOPK = 256
LANES = 128
NEG_INF = -1e30
LN_EPS = 1e-5
ATTN_SCALE = HEAD_DIM ** -0.5
IDX_SCALE = (HEADS ** -0.5) * (HEAD_DIM ** -0.5)

_OFF_FLOGIT = 1536
_OFF_DQ = 1544
_OFF_IK = 3592
_OFF_IW = 3656
_OFF_GA = 3664
_N_IN = 5712

_G_FQ, _G_FK, _G_GATES, _G_FV, _G_DQ, _G_DK, _G_DV, _G_IQ = 0, 1, (2, 3, 4, 5), 6, 7, 8, 9, 10
_CB_GA, _CB_GB = 0, 8
N_GATE_BLOCKS = 16
GROUP = 4 * LANES
N_MAIN = 11 * GROUP
_SM_IK, _SM_IW, _SM_FL = 0, 64, 72

ROW_TILE = 512
TQ = 512
TK = 512
CUMSUM_BLOCK = 256
V_ROWS = HEAD_DIM + 16
LOG2E = 1.4426950408889634

INT_MIN = -(2 ** 31)


def _monotone_key_of(value):
    b = int(np.float32(value).view(np.int32))
    return (b ^ ((b >> 31) & 0x7FFFFFFF)) - (b >> 31)


NEG_INF_KEY = _monotone_key_of(NEG_INF)
VMEM_LIMIT = 60 * 1024 * 1024


def _params(sem):
    return pltpu.CompilerParams(dimension_semantics=sem, vmem_limit_bytes=VMEM_LIMIT)


def _const_spec(shape):
    nd = len(shape)
    return pl.BlockSpec(shape, lambda *_: (0,) * nd, pipeline_mode=pl.Buffered(1))


def _lane_row(lo, hi, value=1.0):
    lane = lax.broadcasted_iota(I32, (1, LANES), 1)
    return jnp.where((lane >= lo) & (lane < hi), value, 0.0)


def _alibi_placement():
    put = np.zeros((LANES, 2 * HEADS * LANES), np.float32)
    for h in range(HEADS):
        slope = 2.0 ** -(h + 1)
        base = (1 - h % 2) * HEAD_DIM
        qc, kc = h * LANES + base, (HEADS + h) * LANES + base
        put[0, qc], put[1, qc + 1], put[2, qc + 2], put[2, qc + 3] = -64.0 * slope, -slope, 64.0 * slope, slope
        put[2, kc], put[2, kc + 1], put[0, kc + 2], put[1, kc + 3] = 1.0, 1.0, 1.0, 1.0
    return jnp.asarray(put, BF16)


def _forget_placement():
    put = np.zeros((LANES, 2 * HEADS * LANES), np.float32)
    for h in range(HEADS):
        base = (1 - h % 2) * HEAD_DIM
        for part in range(3):
            src = _SM_FL + part * HEADS + h
            put[src, h * LANES + base + part] = 1.0
            put[src, (HEADS + h) * LANES + base + 3 + part] = -1.0
    return jnp.asarray(put, BF16)


def _proj_kernel(x_ref, wm_ref, ws_ref, alibi_ref, forget_ref, bias_ref,
                 gates_ref, os_ref, fqT_ref, fka_ref, own_ref, fvT_ref, dqT_ref, dka_ref, dvT_ref, iqT_ref, iwT_ref,
                 carry_sc):
    tm = x_ref.shape[1]
    xb = x_ref[0].astype(BF16)
    small = jnp.dot(xb, ws_ref[...], preferred_element_type=F32)
    os_ref[0] = small
    iwT_ref[0] = small.T[_SM_IW:_SM_IW + HEADS, :] * IDX_SCALE

    def group(g):
        return jnp.dot(xb, wm_ref[:, g * GROUP:(g + 1) * GROUP], preferred_element_type=F32)

    for n, g in enumerate(_G_GATES):
        r = group(g).astype(BF16)
        for c in range(4):
            gates_ref[0, 4 * n + c] = r[:, c * LANES:(c + 1) * LANES]

    def pair_and_half(h):
        p, e = divmod(h, 2)
        return slice(p * LANES, (p + 1) * LANES), _lane_row(e * HEAD_DIM, (e + 1) * HEAD_DIM), (1 - e) * HEAD_DIM

    @pl.when(pl.program_id(1) == 0)
    def _new_sequence():
        carry_sc[...] = jnp.zeros(carry_sc.shape, F32)

    c = _cumsum_rows(jax.nn.log_sigmoid(small + bias_ref[...])) + carry_sc[...]
    carry_sc[...] = c[tm - 1:tm, :]
    hi, mid, lo = _split3(c)
    lane = lax.broadcasted_iota(I32, (tm, LANES), 1)
    parts = jnp.where(lane < _SM_FL + HEADS, hi,
                      jnp.where(lane < _SM_FL + 2 * HEADS, pltpu.roll(mid, HEADS, 1), pltpu.roll(lo, 2 * HEADS, 1)))
    aug = jnp.dot(parts.astype(BF16), forget_ref[...], preferred_element_type=F32)
    q = group(_G_FQ) * ATTN_SCALE
    k = group(_G_FK)
    for h in range(HEADS):
        pair, data, base = pair_and_half(h)
        q_aug = q[:, pair] * data + aug[:, h * LANES:(h + 1) * LANES] + _lane_row(base + 3, base + 6)
        k_aug = k[:, pair] * data + aug[:, (HEADS + h) * LANES:(HEADS + h + 1) * LANES] + _lane_row(base, base + 3)
        fqT_ref[0, h] = q_aug.T.astype(BF16)
        fka_ref[0, h] = k_aug.astype(BF16)
        own_ref[0, h] = jnp.sum((q_aug * k_aug).T, axis=0, keepdims=True) * LOG2E

    def store_values(ref, r):
        ones = jnp.ones((V_ROWS - HEAD_DIM, tm), BF16)
        for p in range(PAIRS):
            vT = r[:, p * LANES:(p + 1) * LANES].T.astype(BF16)
            for e in range(2):
                ref[0, 2 * p + e, 0] = jnp.concatenate([vT[e * HEAD_DIM:(e + 1) * HEAD_DIM], ones], axis=0)

    store_values(fvT_ref, group(_G_FV))
    store_values(dvT_ref, group(_G_DV))

    lane = lax.broadcasted_iota(I32, (tm, LANES), 1)
    pos = lax.broadcasted_iota(I32, (tm, LANES), 0) + pl.program_id(1) * tm
    pos_terms = jnp.where(lane == 0, lax.shift_right_logical(pos, 6),
                          jnp.where(lane == 1, pos & 63, jnp.where(lane == 2, 1, 0))).astype(F32).astype(BF16)
    aug = jnp.dot(pos_terms, alibi_ref[...], preferred_element_type=F32)
    q = group(_G_DQ) * ATTN_SCALE
    k = group(_G_DK)
    for h in range(HEADS):
        pair, data, _ = pair_and_half(h)
        q_aug = q[:, pair] * data + aug[:, h * LANES:(h + 1) * LANES]
        k_aug = k[:, pair] * data + aug[:, (HEADS + h) * LANES:(HEADS + h + 1) * LANES]
        dqT_ref[0, h] = q_aug.T.astype(BF16)
        dka_ref[0, h] = k_aug.astype(BF16)
    iq = group(_G_IQ)
    for p in range(PAIRS):
        iqT_ref[0, p] = iq[:, p * LANES:(p + 1) * LANES].T.astype(BF16)


def _project(x, w_main, w_small, bias_row):
    B, L, D = x.shape
    tm = TK
    nk = L // TK
    alibi, forget = _alibi_placement(), _forget_placement()
    q_t = pl.BlockSpec((1, HEADS, LANES, tm), lambda b, i: (b, 0, 0, i))
    k_aug = pl.BlockSpec((1, HEADS, tm, LANES), lambda b, i: (b, 0, i, 0))
    v_t = pl.BlockSpec((1, HEADS, 1, V_ROWS, TK), lambda b, i: (b, 0, i, 0, 0))
    q_t_shape = jax.ShapeDtypeStruct((B, HEADS, LANES, L), BF16)
    k_aug_shape = jax.ShapeDtypeStruct((B, HEADS, L, LANES), BF16)
    v_t_shape = jax.ShapeDtypeStruct((B, HEADS, nk, V_ROWS, TK), BF16)
    return pl.pallas_call(
        _proj_kernel,
        grid=(B, nk),
        in_specs=[pl.BlockSpec((1, tm, D), lambda b, i: (b, i, 0)),
                  _const_spec((D, N_MAIN)), _const_spec((D, LANES)),
                  _const_spec(alibi.shape), _const_spec(forget.shape), _const_spec((1, LANES))],
        out_specs=[pl.BlockSpec((1, N_GATE_BLOCKS, tm, LANES), lambda b, i: (b, 0, i, 0)),
                   pl.BlockSpec((1, tm, LANES), lambda b, i: (b, i, 0)),
                   q_t, k_aug, pl.BlockSpec((1, HEADS, 1, tm), lambda b, i: (b, 0, 0, i)), v_t,
                   q_t, k_aug, v_t,
                   pl.BlockSpec((1, PAIRS, LANES, tm), lambda b, i: (b, 0, 0, i)),
                   pl.BlockSpec((1, HEADS, tm), lambda b, i: (b, 0, i))],
        out_shape=[jax.ShapeDtypeStruct((B, N_GATE_BLOCKS, L, LANES), BF16),
                   jax.ShapeDtypeStruct((B, L, LANES), F32),
                   q_t_shape, k_aug_shape, jax.ShapeDtypeStruct((B, HEADS, 1, L), F32), v_t_shape,
                   q_t_shape, k_aug_shape, v_t_shape,
                   jax.ShapeDtypeStruct((B, PAIRS, LANES, L), BF16),
                   jax.ShapeDtypeStruct((B, HEADS, L), F32)],
        scratch_shapes=[pltpu.VMEM((1, LANES), F32)],
        compiler_params=_params(("parallel", "arbitrary")),
    )(x, w_main, w_small, alibi, forget, bias_row)


def _split3(v):
    hi = v.astype(BF16).astype(F32)
    r = v - hi
    mid = r.astype(BF16).astype(F32)
    lo = (r - mid).astype(BF16).astype(F32)
    return hi, mid, lo


def _cumsum_rows(x):
    n = CUMSUM_BLOCK
    L = x.shape[0]
    r = lax.broadcasted_iota(I32, (n, n), 0)
    c = lax.broadcasted_iota(I32, (n, n), 1)
    tri = jnp.where(c <= r, 1.0, 0.0).astype(BF16)
    carry = jnp.zeros((1, x.shape[1]), F32)
    outs = []
    w = x.shape[1]
    for blk in range(L // n):
        parts = jnp.concatenate([p.astype(BF16) for p in _split3(x[blk * n:(blk + 1) * n])], axis=1)
        y3 = jnp.dot(tri, parts, preferred_element_type=F32)
        y = (y3[:, :w] + y3[:, w:2 * w] + y3[:, 2 * w:]) + carry
        carry = y[n - 1:n, :]
        outs.append(y)
    return jnp.concatenate(outs, axis=0)


def _fold8(x, op):
    return op(x.reshape(x.shape[0] // 8, 8, x.shape[1]), axis=0)


def _causal(shape):
    return lax.broadcasted_iota(I32, shape, 0) <= lax.broadcasted_iota(I32, shape, 1)


L_SUM_MIN, L_SUM_MAX = 1e-30, 1e30
NEAR_BLOCK = 64


def _attend_heads(n_full, bias_fn, ref_score_fn, qT_ref, k_ref, vT_ref, g_sc, acc_sc, o_ref):
    def for_chunks(chunk):
        def body(j, carry):
            chunk(j, False)
            return carry

        lax.fori_loop(0, n_full, body, 0)
        chunk(n_full, True)

    def scores(h, rows, diag, raw):
        s = raw * LOG2E
        bias = bias_fn(rows, diag)
        return s if bias is None else s + bias

    def sweep():
        acc_sc[...] = jnp.zeros(acc_sc.shape, F32)

        def chunk(j, diag):
            rows = pl.ds(pl.multiple_of(j * TK, TK), TK)

            def raw_scores(h):
                return jnp.dot(k_ref[0, h, rows, :], qT_ref[0, h], preferred_element_type=F32)

            nxt = raw_scores(0)
            for h in range(HEADS):
                raw = nxt
                if h + 1 < HEADS:
                    nxt = raw_scores(h + 1)
                p = jnp.exp2(scores(h, rows, diag, raw) - g_sc[h]).astype(BF16)
                acc_sc[h] += jnp.dot(vT_ref[0, h, j], p, preferred_element_type=F32)

        for_chunks(chunk)
        bad = jnp.zeros((1, TQ), I32)
        for p in range(PAIRS):
            halves = []
            for e in range(2):
                acc = acc_sc[2 * p + e]
                l = acc[HEAD_DIM:HEAD_DIM + 1]
                bad = jnp.maximum(bad, jnp.where((l > L_SUM_MIN) & (l < L_SUM_MAX), 0, 1))
                halves.append(acc[:HEAD_DIM] / l)
            o_ref[0, :, p * LANES:(p + 1) * LANES] = jnp.concatenate(halves, axis=0).T.astype(o_ref.dtype)
        return jnp.max(bad) > 0

    def exact_row_maxima():
        acc_sc[:, 0:8, :] = jnp.full((HEADS, 8, TQ), NEG_INF, F32)

        def chunk(j, diag):
            rows = pl.ds(pl.multiple_of(j * TK, TK), TK)
            for h in range(HEADS):
                raw = jnp.dot(k_ref[0, h, rows, :], qT_ref[0, h], preferred_element_type=F32)
                acc_sc[h, 0:8, :] = jnp.maximum(acc_sc[h, 0:8, :], _fold8(scores(h, rows, diag, raw), jnp.max))

        for_chunks(chunk)
        for h in range(HEADS):
            g_sc[h] = jnp.max(acc_sc[h, 0:8, :], axis=0, keepdims=True)

    for h in range(HEADS):
        g_sc[h] = ref_score_fn(h)
    unsafe = sweep()

    @pl.when(unsafe)
    def _redo_with_exact_maxima():
        exact_row_maxima()
        sweep()


def _head_specs(L, index):
    nk = L // TK
    return [pl.BlockSpec((1, HEADS, L, LANES), lambda *g: (index(*g), 0, 0, 0)),
            pl.BlockSpec((1, HEADS, nk, V_ROWS, TK), lambda *g: (index(*g), 0, 0, 0, 0))]


def _head_scratch():
    return [pltpu.VMEM((HEADS, 1, TQ), F32), pltpu.VMEM((HEADS, V_ROWS, TQ), F32)]


def _fox_kernel(qT_ref, own_ref, k_ref, vT_ref, o_ref, g_sc, acc_sc):
    i = pl.program_id(1)

    def bias_fn(rows, diag):
        return jnp.where(_causal((TK, TQ)), 0.0, NEG_INF) if diag else None

    _attend_heads(i, bias_fn, lambda h: own_ref[0, h], qT_ref, k_ref, vT_ref, g_sc, acc_sc, o_ref)


def _fox_attention(qT, own, ka, vT):
    B, _, _, L = qT.shape
    return pl.pallas_call(
        _fox_kernel,
        grid=(B, L // TQ),
        in_specs=[pl.BlockSpec((1, HEADS, LANES, TQ), lambda b, i: (b, 0, 0, i)),
                  pl.BlockSpec((1, HEADS, 1, TQ), lambda b, i: (b, 0, 0, i))] + _head_specs(L, lambda b, i: b),
        out_specs=pl.BlockSpec((1, TQ, PAIRS * LANES), lambda b, i: (b, i, 0)),
        out_shape=jax.ShapeDtypeStruct((B, L, PAIRS * LANES), BF16),
        scratch_shapes=_head_scratch(),
        compiler_params=_params(("parallel", "arbitrary")),
    )(qT, own, ka, vT)


def _count_rows(mask):
    return _fold8(jnp.where(mask, 1, 0).astype(I32), jnp.sum)


def _dsa_kernel(qT_ref, k_ref, vT_ref, iqT_ref, iwT_ref, s_ref, o_ref,
                keys_sc, hi_sc, lo_sc, g_sc, acc_sc):
    i = pl.program_id(1)
    n_chunks = i + 1

    def _select():
        wts = iwT_ref[0]

        def score_chunk(j, masked):
            ks = pl.multiple_of(j * TK, TK)
            ik = s_ref[0, pl.ds(ks, TK), :][:, _SM_IK:_SM_IK + HEAD_DIM].astype(BF16)
            acc = jnp.zeros((TK, TQ), F32)
            for h in range(HEADS):
                iq = iqT_ref[0, h // 2, (h % 2) * HEAD_DIM:(h % 2 + 1) * HEAD_DIM, :]
                rel = jnp.maximum(jnp.dot(ik, iq, preferred_element_type=F32), 0.0)
                acc = acc + wts[h:h + 1, :] * rel
            if masked:
                acc = jnp.where(_causal(acc.shape), acc, NEG_INF)
            bits = pltpu.bitcast(acc, I32)
            sign = lax.shift_right_arithmetic(bits, 31)
            key = (bits ^ (sign & 0x7FFFFFFF)) - sign
            keys_sc[pl.ds(ks, TK), :] = key
            hi_sc[pl.ds(ks, TK), :] = lax.shift_right_arithmetic(key, 16).astype(I16)
            lo_sc[pl.ds(ks, TK), :] = key.astype(I16) ^ jnp.full((), -(2 ** 15), I16)

        def score_body(j, carry):
            score_chunk(j, False)
            return carry

        lax.fori_loop(0, i, score_body, 0)
        score_chunk(i, True)

        def count(pred):
            def body(j, cnt):
                ks = pl.multiple_of(j * TK, TK)
                return cnt + _count_rows(pred(keys_sc[pl.ds(ks, TK), :]))
            cnt8 = lax.fori_loop(0, n_chunks, body, jnp.zeros((8, TQ), I32))
            return jnp.sum(cnt8, axis=0, keepdims=True)

        def count16(ref, pred):
            one, nil = jnp.ones((), I16), jnp.zeros((), I16)

            def body(j, cnt):
                ks = pl.multiple_of(j * TK, TK)
                hit = jnp.where(pred(ref[pl.ds(ks, TK), :]), one, nil)
                for r in range(TK // 16):
                    cnt = cnt + hit[r * 16:(r + 1) * 16, :]
                return cnt
            cnt16 = lax.fori_loop(0, n_chunks, body, jnp.zeros((16, TQ), I16))
            return jnp.sum(cnt16.astype(I32), axis=0, keepdims=True)

        def select16(ref, k):
            zero = jnp.zeros((1, TQ), I16)
            c0 = count16(ref, lambda v: v >= zero)
            ok0 = c0 >= k
            init = (jnp.where(ok0, 0, -(2 ** 15)).astype(I32),
                    jnp.where(ok0, c0, n_chunks * TK),
                    jnp.where(ok0, 0, c0))

            def bit_body(bi, carry):
                t, n_ge, n_gt = carry
                cand = t | jnp.left_shift(jnp.ones((1, TQ), I32), 14 - bi)
                cand16 = cand.astype(I16)
                c = count16(ref, lambda v: v >= cand16)
                ok = c >= k
                return jnp.where(ok, cand, t), jnp.where(ok, c, n_ge), jnp.where(ok, n_gt, c)

            return lax.fori_loop(0, 15, bit_body, init)

        t_hi, _, n_above = select16(hi_sc, TOPK)
        t_hi16 = t_hi.astype(I16)
        k_lo = TOPK - n_above

        def restrict_body(j, carry):
            ks = pl.multiple_of(j * TK, TK)
            in_bucket = hi_sc[pl.ds(ks, TK), :] == t_hi16
            lo_sc[pl.ds(ks, TK), :] = jnp.where(in_bucket, lo_sc[pl.ds(ks, TK), :], jnp.full((), -(2 ** 15), I16))
            return carry

        lax.fori_loop(0, n_chunks, restrict_body, 0)
        t_lo, n_ge_lo, _ = select16(lo_sc, k_lo)
        thr = jnp.left_shift(t_hi, 16) | (t_lo + 2 ** 15)

        excess = jnp.where((n_above + n_ge_lo > TOPK) & (thr > NEG_INF_KEY), 1, 0)

        @pl.when(jnp.max(excess) > 0)
        def _break_ties():
            need = TOPK - count(lambda kk: kk > thr)
            r = lax.broadcasted_iota(I32, (TK, TK), 0)
            c = lax.broadcasted_iota(I32, (TK, TK), 1)
            strict_lower = jnp.where(c < r, 1.0, 0.0).astype(BF16)

            def body(j, seen):
                ks = pl.multiple_of(j * TK, TK)
                kk = keys_sc[pl.ds(ks, TK), :]
                tie = kk == thr
                tie_f = jnp.where(tie, 1.0, 0.0)
                before = jnp.dot(strict_lower, tie_f.astype(BF16), preferred_element_type=F32) + seen
                drop = tie & (before >= need.astype(F32))
                keys_sc[pl.ds(ks, TK), :] = jnp.where(drop, INT_MIN, kk)
                return seen + jnp.sum(tie_f, axis=0, keepdims=True)

            lax.fori_loop(0, n_chunks, body, jnp.zeros((1, TQ), F32))

        def bias_chunk(j, masked, near8):
            ks = pl.multiple_of(j * TK, TK)
            bias = jnp.where(keys_sc[pl.ds(ks, TK), :] >= thr, 0.0, NEG_INF)
            if masked:
                bias = jnp.where(_causal(bias.shape), bias, NEG_INF)
            keys_sc[pl.ds(ks, TK), :] = pltpu.bitcast(bias, I32)
            for blk in range(TK // NEAR_BLOCK):
                any_selected = _fold8(bias[blk * NEAR_BLOCK:(blk + 1) * NEAR_BLOCK], jnp.max) == 0.0
                near8 = jnp.maximum(near8, jnp.where(any_selected, ks + (blk + 1) * NEAR_BLOCK - 1, -1))
            return near8

        near8 = lax.fori_loop(0, i, lambda j, n8: bias_chunk(j, False, n8), jnp.full((8, TQ), -1, I32))
        return jnp.max(bias_chunk(i, True, near8), axis=0, keepdims=True)

    nearest = _select()

    def stored_bias(rows, diag):
        return pltpu.bitcast(keys_sc[rows, :], F32)

    def nearest_key_bias(h):
        query = i * TQ + lax.broadcasted_iota(I32, (1, TQ), 1)
        return (query - nearest).astype(F32) * (-(2.0 ** -(h + 1)) * LOG2E)

    _attend_heads(i, stored_bias, nearest_key_bias, qT_ref, k_ref, vT_ref, g_sc, acc_sc, o_ref)


def _dsa_attention(qT, ka, vT, iqT, iwT, small):
    B, _, _, L = qT.shape
    return pl.pallas_call(
        _dsa_kernel,
        grid=(B, L // TQ),
        in_specs=[pl.BlockSpec((1, HEADS, LANES, TQ), lambda b, i: (b, 0, 0, i))] + _head_specs(L, lambda b, i: b)
        + [pl.BlockSpec((1, PAIRS, LANES, TQ), lambda b, i: (b, 0, 0, i)),
           pl.BlockSpec((1, HEADS, TQ), lambda b, i: (b, 0, i)),
           pl.BlockSpec((1, L, LANES), lambda b, i: (b, 0, 0))],
        out_specs=pl.BlockSpec((1, TQ, PAIRS * LANES), lambda b, i: (b, i, 0)),
        out_shape=jax.ShapeDtypeStruct((B, L, PAIRS * LANES), BF16),
        scratch_shapes=[pltpu.VMEM((L, TQ), I32), pltpu.VMEM((L, TQ), I16), pltpu.VMEM((L, TQ), I16)]
        + _head_scratch(),
        compiler_params=_params(("parallel", "arbitrary")),
    )(qT, ka, vT, iqT, iwT, small)


def _layer_norm(z, g, b):
    mu = jnp.mean(z, axis=-1, keepdims=True)
    zc = z - mu
    var = jnp.mean(zc * zc, axis=-1, keepdims=True)
    return zc * lax.rsqrt(var + LN_EPS) * g + b


def _merge_kernel(x_ref, oa_ref, ob_ref, ga_ref, gb_ref, wa_ref, wb_ref, wo_ref, g_ref, b_ref,
                  out_ref, *, alpha):
    a = jnp.dot(oa_ref[0], wa_ref[...], preferred_element_type=F32)
    b = jnp.dot(ob_ref[0], wb_ref[...], preferred_element_type=F32)
    n_gate_blocks = D_MODEL // LANES
    gate_a = jnp.concatenate([ga_ref[0, c] for c in range(n_gate_blocks)], axis=1).astype(F32)
    gate_b = jnp.concatenate([gb_ref[0, c] for c in range(n_gate_blocks)], axis=1).astype(F32)
    merged = jax.nn.sigmoid(gate_a) * a + jax.nn.sigmoid(gate_b) * b
    y = jnp.dot(merged.astype(BF16), wo_ref[...], preferred_element_type=F32)
    out_ref[0] = _layer_norm(alpha * x_ref[0] + y, g_ref[...], b_ref[...])


def _merge(x, oa, ob, gates, wa, wb, wo, g, b, alpha):
    B, L, D = x.shape
    tm = ROW_TILE
    nb = D // LANES
    row = lambda b_, i: (b_, i, 0)
    return pl.pallas_call(
        functools.partial(_merge_kernel, alpha=alpha),
        grid=(B, L // tm),
        in_specs=[pl.BlockSpec((1, tm, D), row),
                  pl.BlockSpec((1, tm, PAIRS * LANES), row),
                  pl.BlockSpec((1, tm, PAIRS * LANES), row),
                  pl.BlockSpec((1, nb, tm, LANES), lambda b_, i: (b_, _CB_GA // nb, i, 0)),
                  pl.BlockSpec((1, nb, tm, LANES), lambda b_, i: (b_, _CB_GB // nb, i, 0)),
                  _const_spec(wa.shape), _const_spec(wb.shape), _const_spec(wo.shape),
                  _const_spec(g.shape), _const_spec(b.shape)],
        out_specs=pl.BlockSpec((1, tm, D), row),
        out_shape=jax.ShapeDtypeStruct((B, L, D), F32),
        compiler_params=_params(("parallel", "parallel")),
    )(x, oa, ob, gates, gates, wa, wb, wo, g, b)


FF_CHUNK = 256


def _ffn_kernel(x_ref, wi_ref, wo_ref, g_ref, b_ref, out_ref, h_sc, *, alpha):
    x = x_ref[0]
    xb = x.astype(BF16)
    for c in range(D_FF // FF_CHUNK):
        lo = c * FF_CHUNK
        gate = jnp.dot(xb, wi_ref[:, lo:lo + FF_CHUNK], preferred_element_type=F32)
        up = jnp.dot(xb, wi_ref[:, D_FF + lo:D_FF + lo + FF_CHUNK], preferred_element_type=F32)
        h_sc[:, lo:lo + FF_CHUNK] = (jax.nn.silu(gate) * up).astype(BF16)
    y = jnp.dot(h_sc[...], wo_ref[...], preferred_element_type=F32)
    out_ref[0] = _layer_norm(alpha * x + y, g_ref[...], b_ref[...])


def _ffn(x, wi, wo, g, b, alpha):
    B, L, D = x.shape
    tm = ROW_TILE
    row = lambda b_, i: (b_, i, 0)
    return pl.pallas_call(
        functools.partial(_ffn_kernel, alpha=alpha),
        grid=(B, L // tm),
        in_specs=[pl.BlockSpec((1, tm, D), row),
                  _const_spec(wi.shape), _const_spec(wo.shape),
                  _const_spec(g.shape), _const_spec(b.shape)],
        out_specs=pl.BlockSpec((1, tm, D), row),
        out_shape=jax.ShapeDtypeStruct((B, L, D), F32),
        scratch_shapes=[pltpu.VMEM((tm, D_FF), BF16)],
        compiler_params=_params(("parallel", "parallel")),
    )(x, wi, wo, g, b)


def _split_w_in(w):
    fq_fk = w[:, 0:2 * GROUP]
    fv = w[:, 2 * GROUP:_OFF_FLOGIT]
    dsa_iq = w[:, _OFF_DQ:_OFF_IK]
    gates = w[:, _OFF_GA:_N_IN]
    w_main = jnp.concatenate([fq_fk, gates, fv, dsa_iq], axis=1).astype(BF16)
    pad = jnp.zeros((w.shape[0], LANES - HEAD_DIM - 2 * HEADS), w.dtype)
    w_small = jnp.concatenate([w[:, _OFF_IK:_OFF_IW], w[:, _OFF_IW:_OFF_GA],
                               w[:, _OFF_FLOGIT:_OFF_DQ], pad], axis=1).astype(BF16)
    return w_main, w_small


def kernel(x, w_in, b_forget, w_branch_a, w_branch_b, w_out, ln1_g, ln1_b, w_ffn_in, w_ffn_out, ln2_g, ln2_b):
    depth = w_in.shape[0]
    alpha = (2.0 * depth) ** 0.25
    for l in range(depth):
        w_main, w_small = _split_w_in(w_in[l])
        bias_row = jnp.zeros((1, LANES), F32).at[0, _SM_FL:_SM_FL + HEADS].set(b_forget[l])
        gates, small, fqT, fka, f_own, fvT, dqT, dka, dvT, iqT, iwT = _project(x, w_main, w_small, bias_row)
        o_a = _fox_attention(fqT, f_own, fka, fvT)
        o_b = _dsa_attention(dqT, dka, dvT, iqT, iwT, small)
        x = _merge(x, o_a, o_b, gates,
                   w_branch_a[l].astype(BF16), w_branch_b[l].astype(BF16), w_out[l].astype(BF16),
                   ln1_g[l][None, :], ln1_b[l][None, :], alpha)
        x = _ffn(x, w_ffn_in[l].astype(BF16), w_ffn_out[l].astype(BF16),
                 ln2_g[l][None, :], ln2_b[l][None, :], alpha)
    return x
```

```python
import functools

import numpy as np
import jax
import jax.numpy as jnp
from jax import lax
from jax.experimental import pallas as pl
from jax.experimental.pallas import tpu as pltpu

F32 = jnp.float32
BF16 = jnp.bfloat16
I32 = jnp.int32
I16 = jnp.int16

D_MODEL = 1024
HEADS = 8
HEAD_DIM = 64
PAIRS = HEADS // 2
D_FF = 2816
TOPK = 256
LANES = 128
NEG_INF = -1e30
LN_EPS = 1e-5
ATTN_SCALE = HEAD_DIM ** -0.5
IDX_SCALE = (HEADS ** -0.5) * (HEAD_DIM ** -0.5)

_OFF_FLOGIT = 1536
_OFF_DQ = 1544
_OFF_IK = 3592
_OFF_IW = 3656
_OFF_GA = 3664
_N_IN = 5712

_G_FQ, _G_FK, _G_GATES, _G_FV, _G_DQ, _G_DK, _G_DV, _G_IQ = 0, 1, (2, 3, 4, 5), 6, 7, 8, 9, 10
_CB_GA, _CB_GB = 0, 8
N_GATE_BLOCKS = 16
GROUP = 4 * LANES
N_MAIN = 11 * GROUP
_SM_IK, _SM_IW, _SM_FL = 0, 64, 72

ROW_TILE = 512
TQ = 512
TK = 512
CUMSUM_BLOCK = 256
V_ROWS = HEAD_DIM + 16
LOG2E = 1.4426950408889634

INT_MIN = -(2 ** 31)


def _monotone_key_of(value):
    b = int(np.float32(value).view(np.int32))
    return (b ^ ((b >> 31) & 0x7FFFFFFF)) - (b >> 31)


NEG_INF_KEY = _monotone_key_of(NEG_INF)
VMEM_LIMIT = 60 * 1024 * 1024


def _params(sem):
    return pltpu.CompilerParams(dimension_semantics=sem, vmem_limit_bytes=VMEM_LIMIT)


def _const_spec(shape):
    nd = len(shape)
    return pl.BlockSpec(shape, lambda *_: (0,) * nd, pipeline_mode=pl.Buffered(1))


def _lane_row(lo, hi, value=1.0):
    lane = lax.broadcasted_iota(I32, (1, LANES), 1)
    return jnp.where((lane >= lo) & (lane < hi), value, 0.0)


BIAS_COLS = 8


def _alibi_placement():
    put = np.zeros((LANES, 2 * LANES), np.float32)
    for h in range(HEADS):
        slope = 2.0 ** -(h + 1)
        qc, kc = BIAS_COLS * h, LANES + BIAS_COLS * h
        put[0, qc], put[1, qc + 1], put[2, qc + 2], put[2, qc + 3] = -64.0 * slope, -slope, 64.0 * slope, slope
        put[2, kc], put[2, kc + 1], put[0, kc + 2], put[1, kc + 3] = 1.0, 1.0, 1.0, 1.0
    return jnp.asarray(put, BF16)


def _forget_placement():
    put = np.zeros((LANES, 2 * LANES), np.float32)
    for h in range(HEADS):
        for part in range(3):
            src = _SM_FL + part * HEADS + h
            put[src, BIAS_COLS * h + part] = 1.0
            put[src, LANES + BIAS_COLS * h + 3 + part] = -1.0
    return jnp.asarray(put, BF16)


def _proj_kernel(x_ref, wm_ref, ws_ref, alibi_ref, forget_ref, bias_ref,
                 gates_ref, os_ref, fqT_ref, fka_ref, own_ref, fvT_ref, dqT_ref, dka_ref, dvT_ref, iqT_ref, iwT_ref,
                 carry_sc):
    tm = x_ref.shape[1]
    xb = x_ref[0].astype(BF16)
    small = jnp.dot(xb, ws_ref[...], preferred_element_type=F32)
    os_ref[0] = small
    iwT_ref[0] = small.T[_SM_IW:_SM_IW + HEADS, :] * IDX_SCALE

    def group(g):
        return jnp.dot(xb, wm_ref[:, g * GROUP:(g + 1) * GROUP], preferred_element_type=F32)

    def store_gates(n):
        r = group(_G_GATES[n]).astype(BF16)
        for c in range(4):
            gates_ref[0, 4 * n + c] = r[:, c * LANES:(c + 1) * LANES]

    def pair_and_half(h):
        p, e = divmod(h, 2)
        return slice(p * LANES, (p + 1) * LANES), _lane_row(e * HEAD_DIM, (e + 1) * HEAD_DIM), (1 - e) * HEAD_DIM

    def bias_columns(compact, h, first, n):
        base = (1 - h % 2) * HEAD_DIM
        moved = pltpu.roll(compact, (base - BIAS_COLS * h) % LANES, 1)
        return moved * _lane_row(base + first, base + first + n)

    @pl.when(pl.program_id(1) == 0)
    def _new_sequence():
        carry_sc[...] = jnp.zeros(carry_sc.shape, F32)

    store_gates(0)
    c = _cumsum_rows(jax.nn.log_sigmoid(small + bias_ref[...])) + carry_sc[...]
    carry_sc[...] = c[tm - 1:tm, :]
    hi, mid, lo = _split3(c)
    lane = lax.broadcasted_iota(I32, (tm, LANES), 1)
    parts = jnp.where(lane < _SM_FL + HEADS, hi,
                      jnp.where(lane < _SM_FL + 2 * HEADS, pltpu.roll(mid, HEADS, 1), pltpu.roll(lo, 2 * HEADS, 1)))
    store_gates(1)
    aug = jnp.dot(parts.astype(BF16), forget_ref[...], preferred_element_type=F32)
    q = group(_G_FQ) * ATTN_SCALE
    k = group(_G_FK)
    for h in range(HEADS):
        pair, data, base = pair_and_half(h)
        q_aug = q[:, pair] * data + bias_columns(aug[:, :LANES], h, 0, 3) + _lane_row(base + 3, base + 6)
        k_aug = k[:, pair] * data + bias_columns(aug[:, LANES:], h, 3, 3) + _lane_row(base, base + 3)
        fqT_ref[0, h] = q_aug.T.astype(BF16)
        fka_ref[0, h] = k_aug.astype(BF16)
        own_ref[0, h] = jnp.sum((q_aug * k_aug).T, axis=0, keepdims=True) * LOG2E

    def store_values(ref, r):
        ones = jnp.ones((V_ROWS - HEAD_DIM, tm), BF16)
        for p in range(PAIRS):
            vT = r[:, p * LANES:(p + 1) * LANES].T.astype(BF16)
            for e in range(2):
                ref[0, 2 * p + e, 0] = jnp.concatenate([vT[e * HEAD_DIM:(e + 1) * HEAD_DIM], ones], axis=0)

    store_values(fvT_ref, group(_G_FV))
    store_gates(2)
    store_values(dvT_ref, group(_G_DV))

    lane = lax.broadcasted_iota(I32, (tm, LANES), 1)
    pos = lax.broadcasted_iota(I32, (tm, LANES), 0) + pl.program_id(1) * tm
    pos_terms = jnp.where(lane == 0, lax.shift_right_logical(pos, 6),
                          jnp.where(lane == 1, pos & 63, jnp.where(lane == 2, 1, 0))).astype(F32).astype(BF16)
    aug = jnp.dot(pos_terms, alibi_ref[...], preferred_element_type=F32)
    q = group(_G_DQ) * ATTN_SCALE
    k = group(_G_DK)
    for h in range(HEADS):
        pair, data, _ = pair_and_half(h)
        q_aug = q[:, pair] * data + bias_columns(aug[:, :LANES], h, 0, 4)
        k_aug = k[:, pair] * data + bias_columns(aug[:, LANES:], h, 0, 4)
        dqT_ref[0, h] = q_aug.T.astype(BF16)
        dka_ref[0, h] = k_aug.astype(BF16)
    store_gates(3)
    iq = group(_G_IQ)
    for p in range(PAIRS):
        iqT_ref[0, p] = iq[:, p * LANES:(p + 1) * LANES].T.astype(BF16)


def _project(x, w_main, w_small, bias_row):
    B, L, D = x.shape
    tm = TK
    nk = L // TK
    alibi, forget = _alibi_placement(), _forget_placement()
    q_t = pl.BlockSpec((1, HEADS, LANES, tm), lambda b, i: (b, 0, 0, i))
    k_aug = pl.BlockSpec((1, HEADS, tm, LANES), lambda b, i: (b, 0, i, 0))
    v_t = pl.BlockSpec((1, HEADS, 1, V_ROWS, TK), lambda b, i: (b, 0, i, 0, 0))
    q_t_shape = jax.ShapeDtypeStruct((B, HEADS, LANES, L), BF16)
    k_aug_shape = jax.ShapeDtypeStruct((B, HEADS, L, LANES), BF16)
    v_t_shape = jax.ShapeDtypeStruct((B, HEADS, nk, V_ROWS, TK), BF16)
    return pl.pallas_call(
        _proj_kernel,
        grid=(B, nk),
        in_specs=[pl.BlockSpec((1, tm, D), lambda b, i: (b, i, 0)),
                  _const_spec((D, N_MAIN)), _const_spec((D, LANES)),
                  _const_spec(alibi.shape), _const_spec(forget.shape), _const_spec((1, LANES))],
        out_specs=[pl.BlockSpec((1, N_GATE_BLOCKS, tm, LANES), lambda b, i: (b, 0, i, 0)),
                   pl.BlockSpec((1, tm, LANES), lambda b, i: (b, i, 0)),
                   q_t, k_aug, pl.BlockSpec((1, HEADS, 1, tm), lambda b, i: (b, 0, 0, i)), v_t,
                   q_t, k_aug, v_t,
                   pl.BlockSpec((1, PAIRS, LANES, tm), lambda b, i: (b, 0, 0, i)),
                   pl.BlockSpec((1, HEADS, tm), lambda b, i: (b, 0, i))],
        out_shape=[jax.ShapeDtypeStruct((B, N_GATE_BLOCKS, L, LANES), BF16),
                   jax.ShapeDtypeStruct((B, L, LANES), F32),
                   q_t_shape, k_aug_shape, jax.ShapeDtypeStruct((B, HEADS, 1, L), F32), v_t_shape,
                   q_t_shape, k_aug_shape, v_t_shape,
                   jax.ShapeDtypeStruct((B, PAIRS, LANES, L), BF16),
                   jax.ShapeDtypeStruct((B, HEADS, L), F32)],
        scratch_shapes=[pltpu.VMEM((1, LANES), F32)],
        compiler_params=_params(("parallel", "arbitrary")),
    )(x, w_main, w_small, alibi, forget, bias_row)


def _split3(v):
    hi = v.astype(BF16).astype(F32)
    r = v - hi
    mid = r.astype(BF16).astype(F32)
    lo = (r - mid).astype(BF16).astype(F32)
    return hi, mid, lo


def _cumsum_rows(x):
    n = CUMSUM_BLOCK
    L = x.shape[0]
    r = lax.broadcasted_iota(I32, (n, n), 0)
    c = lax.broadcasted_iota(I32, (n, n), 1)
    tri = jnp.where(c <= r, 1.0, 0.0).astype(BF16)
    carry = jnp.zeros((1, x.shape[1]), F32)
    outs = []
    w = x.shape[1]
    for blk in range(L // n):
        parts = jnp.concatenate([p.astype(BF16) for p in _split3(x[blk * n:(blk + 1) * n])], axis=1)
        y3 = jnp.dot(tri, parts, preferred_element_type=F32)
        y = (y3[:, :w] + y3[:, w:2 * w] + y3[:, 2 * w:]) + carry
        carry = y[n - 1:n, :]
        outs.append(y)
    return jnp.concatenate(outs, axis=0)


def _fold8(x, op):
    return op(x.reshape(x.shape[0] // 8, 8, x.shape[1]), axis=0)


def _causal(shape):
    return lax.broadcasted_iota(I32, shape, 0) <= lax.broadcasted_iota(I32, shape, 1)


L_SUM_MIN, L_SUM_MAX = 1e-30, 1e30
NEAR_BLOCK = 64


def _attend_heads(n_full, bias_fn, ref_score_fn, qT_ref, k_ref, vT_ref, g_sc, acc_sc, o_ref):
    def for_chunks(chunk):
        def body(j, carry):
            chunk(j, False)
            return carry

        lax.fori_loop(0, n_full, body, 0)
        chunk(n_full, True)

    def scores(h, rows, diag, raw):
        s = raw * LOG2E
        bias = bias_fn(rows, diag)
        return s if bias is None else s + bias

    def sweep():
        acc_sc[...] = jnp.zeros(acc_sc.shape, F32)

        def chunk(j, diag):
            rows = pl.ds(pl.multiple_of(j * TK, TK), TK)

            def raw_scores(h):
                return jnp.dot(k_ref[0, h, rows, :], qT_ref[0, h], preferred_element_type=F32)

            nxt = raw_scores(0)
            for h in range(HEADS):
                raw = nxt
                if h + 1 < HEADS:
                    nxt = raw_scores(h + 1)
                p = jnp.exp2(scores(h, rows, diag, raw) - g_sc[h]).astype(BF16)
                acc_sc[h] += jnp.dot(vT_ref[0, h, j], p, preferred_element_type=F32)

        for_chunks(chunk)
        bad = jnp.zeros((1, TQ), I32)
        for p in range(PAIRS):
            halves = []
            for e in range(2):
                acc = acc_sc[2 * p + e]
                l = acc[HEAD_DIM:HEAD_DIM + 1]
                bad = jnp.maximum(bad, jnp.where((l > L_SUM_MIN) & (l < L_SUM_MAX), 0, 1))
                halves.append(acc[:HEAD_DIM] / l)
            o_ref[0, :, p * LANES:(p + 1) * LANES] = jnp.concatenate(halves, axis=0).T.astype(o_ref.dtype)
        return jnp.max(bad) > 0

    def exact_row_maxima():
        acc_sc[:, 0:8, :] = jnp.full((HEADS, 8, TQ), NEG_INF, F32)

        def chunk(j, diag):
            rows = pl.ds(pl.multiple_of(j * TK, TK), TK)
            for h in range(HEADS):
                raw = jnp.dot(k_ref[0, h, rows, :], qT_ref[0, h], preferred_element_type=F32)
                acc_sc[h, 0:8, :] = jnp.maximum(acc_sc[h, 0:8, :], _fold8(scores(h, rows, diag, raw), jnp.max))

        for_chunks(chunk)
        for h in range(HEADS):
            g_sc[h] = jnp.max(acc_sc[h, 0:8, :], axis=0, keepdims=True)

    for h in range(HEADS):
        g_sc[h] = ref_score_fn(h)
    unsafe = sweep()

    @pl.when(unsafe)
    def _redo_with_exact_maxima():
        exact_row_maxima()
        sweep()


def _head_specs(L, index):
    nk = L // TK
    return [pl.BlockSpec((1, HEADS, L, LANES), lambda *g: (index(*g), 0, 0, 0)),
            pl.BlockSpec((1, HEADS, nk, V_ROWS, TK), lambda *g: (index(*g), 0, 0, 0, 0))]


def _head_scratch():
    return [pltpu.VMEM((HEADS, 1, TQ), F32), pltpu.VMEM((HEADS, V_ROWS, TQ), F32)]


def _fox_kernel(qT_ref, own_ref, k_ref, vT_ref, o_ref, g_sc, acc_sc):
    i = pl.program_id(1)

    def bias_fn(rows, diag):
        return jnp.where(_causal((TK, TQ)), 0.0, NEG_INF) if diag else None

    _attend_heads(i, bias_fn, lambda h: own_ref[0, h], qT_ref, k_ref, vT_ref, g_sc, acc_sc, o_ref)


def _fox_attention(qT, own, ka, vT):
    B, _, _, L = qT.shape
    return pl.pallas_call(
        _fox_kernel,
        grid=(B, L // TQ),
        in_specs=[pl.BlockSpec((1, HEADS, LANES, TQ), lambda b, i: (b, 0, 0, i)),
                  pl.BlockSpec((1, HEADS, 1, TQ), lambda b, i: (b, 0, 0, i))] + _head_specs(L, lambda b, i: b),
        out_specs=pl.BlockSpec((1, TQ, PAIRS * LANES), lambda b, i: (b, i, 0)),
        out_shape=jax.ShapeDtypeStruct((B, L, PAIRS * LANES), BF16),
        scratch_shapes=_head_scratch(),
        compiler_params=_params(("parallel", "arbitrary")),
    )(qT, own, ka, vT)


def _count_rows(mask):
    return _fold8(jnp.where(mask, 1, 0).astype(I32), jnp.sum)


def _dsa_kernel(qT_ref, k_ref, vT_ref, iqT_ref, iwT_ref, s_ref, o_ref,
                keys_sc, hi_sc, lo_sc, g_sc, acc_sc):
    i = pl.program_id(1)
    n_chunks = i + 1

    def _select():
        wts = iwT_ref[0]

        def score_chunk(j, masked):
            ks = pl.multiple_of(j * TK, TK)
            ik = s_ref[0, pl.ds(ks, TK), :][:, _SM_IK:_SM_IK + HEAD_DIM].astype(BF16)
            acc = jnp.zeros((TK, TQ), F32)
            for h in range(HEADS):
                iq = iqT_ref[0, h // 2, (h % 2) * HEAD_DIM:(h % 2 + 1) * HEAD_DIM, :]
                rel = jnp.maximum(jnp.dot(ik, iq, preferred_element_type=F32), 0.0)
                acc = acc + wts[h:h + 1, :] * rel
            if masked:
                acc = jnp.where(_causal(acc.shape), acc, NEG_INF)
            bits = pltpu.bitcast(acc, I32)
            sign = lax.shift_right_arithmetic(bits, 31)
            key = (bits ^ (sign & 0x7FFFFFFF)) - sign
            keys_sc[pl.ds(ks, TK), :] = key
            hi_sc[pl.ds(ks, TK), :] = lax.shift_right_arithmetic(key, 16).astype(I16)
            lo_sc[pl.ds(ks, TK), :] = key.astype(I16) ^ jnp.full((), -(2 ** 15), I16)

        def score_body(j, carry):
            score_chunk(j, False)
            return carry

        lax.fori_loop(0, i, score_body, 0)
        score_chunk(i, True)

        def count(pred):
            def body(j, cnt):
                ks = pl.multiple_of(j * TK, TK)
                return cnt + _count_rows(pred(keys_sc[pl.ds(ks, TK), :]))
            cnt8 = lax.fori_loop(0, n_chunks, body, jnp.zeros((8, TQ), I32))
            return jnp.sum(cnt8, axis=0, keepdims=True)

        def count16(ref, pred):
            one, nil = jnp.ones((), I16), jnp.zeros((), I16)

            def body(j, cnt):
                ks = pl.multiple_of(j * TK, TK)
                hit = jnp.where(pred(ref[pl.ds(ks, TK), :]), one, nil)
                for r in range(TK // 16):
                    cnt = cnt + hit[r * 16:(r + 1) * 16, :]
                return cnt
            cnt16 = lax.fori_loop(0, n_chunks, body, jnp.zeros((16, TQ), I16))
            return jnp.sum(cnt16.astype(I32), axis=0, keepdims=True)

        def select16(ref, k):
            zero = jnp.zeros((1, TQ), I16)
            c0 = count16(ref, lambda v: v >= zero)
            ok0 = c0 >= k
            init = (jnp.where(ok0, 0, -(2 ** 15)).astype(I32),
                    jnp.where(ok0, c0, n_chunks * TK),
                    jnp.where(ok0, 0, c0))

            def bit_body(bi, carry):
                t, n_ge, n_gt = carry
                cand = t | jnp.left_shift(jnp.ones((1, TQ), I32), 14 - bi)
                cand16 = cand.astype(I16)
                c = count16(ref, lambda v: v >= cand16)
                ok = c >= k
                return jnp.where(ok, cand, t), jnp.where(ok, c, n_ge), jnp.where(ok, n_gt, c)

            return lax.fori_loop(0, 15, bit_body, init)

        t_hi, _, n_above = select16(hi_sc, TOPK)
        t_hi16 = t_hi.astype(I16)
        k_lo = TOPK - n_above

        def restrict_body(j, carry):
            ks = pl.multiple_of(j * TK, TK)
            in_bucket = hi_sc[pl.ds(ks, TK), :] == t_hi16
            lo_sc[pl.ds(ks, TK), :] = jnp.where(in_bucket, lo_sc[pl.ds(ks, TK), :], jnp.full((), -(2 ** 15), I16))
            return carry

        lax.fori_loop(0, n_chunks, restrict_body, 0)
        t_lo, n_ge_lo, _ = select16(lo_sc, k_lo)
        thr = jnp.left_shift(t_hi, 16) | (t_lo + 2 ** 15)

        excess = jnp.where((n_above + n_ge_lo > TOPK) & (thr > NEG_INF_KEY), 1, 0)

        @pl.when(jnp.max(excess) > 0)
        def _break_ties():
            need = TOPK - count(lambda kk: kk > thr)
            r = lax.broadcasted_iota(I32, (TK, TK), 0)
            c = lax.broadcasted_iota(I32, (TK, TK), 1)
            strict_lower = jnp.where(c < r, 1.0, 0.0).astype(BF16)

            def body(j, seen):
                ks = pl.multiple_of(j * TK, TK)
                kk = keys_sc[pl.ds(ks, TK), :]
                tie = kk == thr
                tie_f = jnp.where(tie, 1.0, 0.0)
                before = jnp.dot(strict_lower, tie_f.astype(BF16), preferred_element_type=F32) + seen
                drop = tie & (before >= need.astype(F32))
                keys_sc[pl.ds(ks, TK), :] = jnp.where(drop, INT_MIN, kk)
                return seen + jnp.sum(tie_f, axis=0, keepdims=True)

            lax.fori_loop(0, n_chunks, body, jnp.zeros((1, TQ), F32))

        def bias_chunk(j, masked, near8):
            ks = pl.multiple_of(j * TK, TK)
            bias = jnp.where(keys_sc[pl.ds(ks, TK), :] >= thr, 0.0, NEG_INF)
            if masked:
                bias = jnp.where(_causal(bias.shape), bias, NEG_INF)
            keys_sc[pl.ds(ks, TK), :] = pltpu.bitcast(bias, I32)
            for blk in range(TK // NEAR_BLOCK):
                any_selected = _fold8(bias[blk * NEAR_BLOCK:(blk + 1) * NEAR_BLOCK], jnp.max) == 0.0
                near8 = jnp.maximum(near8, jnp.where(any_selected, ks + (blk + 1) * NEAR_BLOCK - 1, -1))
            return near8

        near8 = lax.fori_loop(0, i, lambda j, n8: bias_chunk(j, False, n8), jnp.full((8, TQ), -1, I32))
        return jnp.max(bias_chunk(i, True, near8), axis=0, keepdims=True)

    nearest = _select()

    def stored_bias(rows, diag):
        return pltpu.bitcast(keys_sc[rows, :], F32)

    def nearest_key_bias(h):
        query = i * TQ + lax.broadcasted_iota(I32, (1, TQ), 1)
        return (query - nearest).astype(F32) * (-(2.0 ** -(h + 1)) * LOG2E)

    _attend_heads(i, stored_bias, nearest_key_bias, qT_ref, k_ref, vT_ref, g_sc, acc_sc, o_ref)


def _dsa_attention(qT, ka, vT, iqT, iwT, small):
    B, _, _, L = qT.shape
    return pl.pallas_call(
        _dsa_kernel,
        grid=(B, L // TQ),
        in_specs=[pl.BlockSpec((1, HEADS, LANES, TQ), lambda b, i: (b, 0, 0, i))] + _head_specs(L, lambda b, i: b)
        + [pl.BlockSpec((1, PAIRS, LANES, TQ), lambda b, i: (b, 0, 0, i)),
           pl.BlockSpec((1, HEADS, TQ), lambda b, i: (b, 0, i)),
           pl.BlockSpec((1, L, LANES), lambda b, i: (b, 0, 0))],
        out_specs=pl.BlockSpec((1, TQ, PAIRS * LANES), lambda b, i: (b, i, 0)),
        out_shape=jax.ShapeDtypeStruct((B, L, PAIRS * LANES), BF16),
        scratch_shapes=[pltpu.VMEM((L, TQ), I32), pltpu.VMEM((L, TQ), I16), pltpu.VMEM((L, TQ), I16)]
        + _head_scratch(),
        compiler_params=_params(("parallel", "arbitrary")),
    )(qT, ka, vT, iqT, iwT, small)


def _layer_norm(z, g, b):
    mu = jnp.mean(z, axis=-1, keepdims=True)
    zc = z - mu
    var = jnp.mean(zc * zc, axis=-1, keepdims=True)
    return zc * lax.rsqrt(var + LN_EPS) * g + b


def _merge_kernel(x_ref, oa_ref, ob_ref, ga_ref, gb_ref, wa_ref, wb_ref, wo_ref, g_ref, b_ref,
                  out_ref, *, alpha):
    a = jnp.dot(oa_ref[0], wa_ref[...], preferred_element_type=F32)
    b = jnp.dot(ob_ref[0], wb_ref[...], preferred_element_type=F32)
    n_gate_blocks = D_MODEL // LANES
    gate_a = jnp.concatenate([ga_ref[0, c] for c in range(n_gate_blocks)], axis=1).astype(F32)
    gate_b = jnp.concatenate([gb_ref[0, c] for c in range(n_gate_blocks)], axis=1).astype(F32)
    merged = jax.nn.sigmoid(gate_a) * a + jax.nn.sigmoid(gate_b) * b
    y = jnp.dot(merged.astype(BF16), wo_ref[...], preferred_element_type=F32)
    out_ref[0] = _layer_norm(alpha * x_ref[0] + y, g_ref[...], b_ref[...])


def _merge(x, oa, ob, gates, wa, wb, wo, g, b, alpha):
    B, L, D = x.shape
    tm = ROW_TILE
    nb = D // LANES
    row = lambda b_, i: (b_, i, 0)
    return pl.pallas_call(
        functools.partial(_merge_kernel, alpha=alpha),
        grid=(B, L // tm),
        in_specs=[pl.BlockSpec((1, tm, D), row),
                  pl.BlockSpec((1, tm, PAIRS * LANES), row),
                  pl.BlockSpec((1, tm, PAIRS * LANES), row),
                  pl.BlockSpec((1, nb, tm, LANES), lambda b_, i: (b_, _CB_GA // nb, i, 0)),
                  pl.BlockSpec((1, nb, tm, LANES), lambda b_, i: (b_, _CB_GB // nb, i, 0)),
                  _const_spec(wa.shape), _const_spec(wb.shape), _const_spec(wo.shape),
                  _const_spec(g.shape), _const_spec(b.shape)],
        out_specs=pl.BlockSpec((1, tm, D), row),
        out_shape=jax.ShapeDtypeStruct((B, L, D), F32),
        compiler_params=_params(("parallel", "parallel")),
    )(x, oa, ob, gates, gates, wa, wb, wo, g, b)


FF_CHUNK = 256


def _ffn_kernel(x_ref, wi_ref, wo_ref, g_ref, b_ref, out_ref, h_sc, *, alpha):
    x = x_ref[0]
    xb = x.astype(BF16)
    for c in range(D_FF // FF_CHUNK):
        lo = c * FF_CHUNK
        gate = jnp.dot(xb, wi_ref[:, lo:lo + FF_CHUNK], preferred_element_type=F32)
        up = jnp.dot(xb, wi_ref[:, D_FF + lo:D_FF + lo + FF_CHUNK], preferred_element_type=F32)
        h_sc[:, lo:lo + FF_CHUNK] = (jax.nn.silu(gate) * up).astype(BF16)
    y = jnp.dot(h_sc[...], wo_ref[...], preferred_element_type=F32)
    out_ref[0] = _layer_norm(alpha * x + y, g_ref[...], b_ref[...])


def _ffn(x, wi, wo, g, b, alpha):
    B, L, D = x.shape
    tm = ROW_TILE
    row = lambda b_, i: (b_, i, 0)
    return pl.pallas_call(
        functools.partial(_ffn_kernel, alpha=alpha),
        grid=(B, L // tm),
        in_specs=[pl.BlockSpec((1, tm, D), row),
                  _const_spec(wi.shape), _const_spec(wo.shape),
                  _const_spec(g.shape), _const_spec(b.shape)],
        out_specs=pl.BlockSpec((1, tm, D), row),
        out_shape=jax.ShapeDtypeStruct((B, L, D), F32),
        scratch_shapes=[pltpu.VMEM((tm, D_FF), BF16)],
        compiler_params=_params(("parallel", "parallel")),
    )(x, wi, wo, g, b)


def _split_w_in(w):
    fq_fk = w[:, 0:2 * GROUP]
    fv = w[:, 2 * GROUP:_OFF_FLOGIT]
    dsa_iq = w[:, _OFF_DQ:_OFF_IK]
    gates = w[:, _OFF_GA:_N_IN]
    w_main = jnp.concatenate([fq_fk, gates, fv, dsa_iq], axis=1).astype(BF16)
    pad = jnp.zeros((w.shape[0], LANES - HEAD_DIM - 2 * HEADS), w.dtype)
    w_small = jnp.concatenate([w[:, _OFF_IK:_OFF_IW], w[:, _OFF_IW:_OFF_GA],
                               w[:, _OFF_FLOGIT:_OFF_DQ], pad], axis=1).astype(BF16)
    return w_main, w_small


def kernel(x, w_in, b_forget, w_branch_a, w_branch_b, w_out, ln1_g, ln1_b, w_ffn_in, w_ffn_out, ln2_g, ln2_b):
    depth = w_in.shape[0]
    alpha = (2.0 * depth) ** 0.25
    for l in range(depth):
        w_main, w_small = _split_w_in(w_in[l])
        bias_row = jnp.zeros((1, LANES), F32).at[0, _SM_FL:_SM_FL + HEADS].set(b_forget[l])
        gates, small, fqT, fka, f_own, fvT, dqT, dka, dvT, iqT, iwT = _project(x, w_main, w_small, bias_row)
        o_a = _fox_attention(fqT, f_own, fka, fvT)
        o_b = _dsa_attention(dqT, dka, dvT, iqT, iwT, small)
        x = _merge(x, o_a, o_b, gates,
                   w_branch_a[l].astype(BF16), w_branch_b[l].astype(BF16), w_out[l].astype(BF16),
                   ln1_g[l][None, :], ln1_b[l][None, :], alpha)
        x = _ffn(x, w_ffn_in[l].astype(BF16), w_ffn_out[l].astype(BF16),
                 ln2_g[l][None, :], ln2_b[l][None, :], alpha)
    return x
```

```python
import functools

import numpy as np
import jax
import jax.numpy as jnp
from jax import lax
from jax.experimental import pallas as pl
from jax.experimental.pallas import tpu as pltpu

F32 = jnp.float32
BF16 = jnp.bfloat16
I32 = jnp.int32
I16 = jnp.int16

D_MODEL = 1024
HEADS = 8
HEAD_DIM = 64
PAIRS = HEADS // 2
D_FF = 2816
TOPK = 256
LANES = 128
NEG_INF = -1e30
LN_EPS = 1e-5
ATTN_SCALE = HEAD_DIM ** -0.5
IDX_SCALE = (HEADS ** -0.5) * (HEAD_DIM ** -0.5)

_OFF_FLOGIT = 1536
_OFF_DQ = 1544
_OFF_IK = 3592
_OFF_IW = 3656
_OFF_GA = 3664
_N_IN = 5712

_G_FQ, _G_FK, _G_GATES, _G_FV, _G_DQ, _G_DK, _G_DV, _G_IQ = 0, 1, (2, 3, 4, 5), 6, 7, 8, 9, 10
_CB_GA, _CB_GB = 0, 8
N_GATE_BLOCKS = 16
GROUP = 4 * LANES
N_MAIN = 11 * GROUP
_SM_IK, _SM_IW, _SM_FL = 0, 64, 72

ROW_TILE = 512
TQ = 512
TK = 512
CUMSUM_BLOCK = 256
V_ROWS = HEAD_DIM + 16
LOG2E = 1.4426950408889634

INT_MIN = -(2 ** 31)


def _monotone_key_of(value):
    b = int(np.float32(value).view(np.int32))
    return (b ^ ((b >> 31) & 0x7FFFFFFF)) - (b >> 31)


NEG_INF_KEY = _monotone_key_of(NEG_INF)
VMEM_LIMIT = 60 * 1024 * 1024


def _params(sem):
    return pltpu.CompilerParams(dimension_semantics=sem, vmem_limit_bytes=VMEM_LIMIT)


def _const_spec(shape):
    nd = len(shape)
    return pl.BlockSpec(shape, lambda *_: (0,) * nd, pipeline_mode=pl.Buffered(1))


def _lane_row(lo, hi, value=1.0):
    lane = lax.broadcasted_iota(I32, (1, LANES), 1)
    return jnp.where((lane >= lo) & (lane < hi), value, 0.0)


BIAS_COLS = 8


def _alibi_placement():
    put = np.zeros((LANES, 2 * LANES), np.float32)
    for h in range(HEADS):
        slope = 2.0 ** -(h + 1)
        qc, kc = BIAS_COLS * h, LANES + BIAS_COLS * h
        put[0, qc], put[1, qc + 1], put[2, qc + 2], put[2, qc + 3] = -64.0 * slope, -slope, 64.0 * slope, slope
        put[2, kc], put[2, kc + 1], put[0, kc + 2], put[1, kc + 3] = 1.0, 1.0, 1.0, 1.0
    return jnp.asarray(put, BF16)


def _forget_placement():
    put = np.zeros((LANES, 2 * LANES), np.float32)
    for h in range(HEADS):
        for part in range(3):
            src = _SM_FL + part * HEADS + h
            put[src, BIAS_COLS * h + part] = 1.0
            put[src, LANES + BIAS_COLS * h + 3 + part] = -1.0
    return jnp.asarray(put, BF16)


def _proj_kernel(x_ref, wm_ref, ws_ref, alibi_ref, forget_ref, bias_ref,
                 gates_ref, os_ref, fqT_ref, fka_ref, own_ref, fvT_ref, dqT_ref, dka_ref, dvT_ref, iqT_ref, iwT_ref,
                 carry_sc):
    tm = x_ref.shape[1]
    xb = x_ref[0].astype(BF16)

    def group(g):
        return jnp.dot(xb, wm_ref[:, g * GROUP:(g + 1) * GROUP], preferred_element_type=F32)

    def store_gates(n):
        r = group(_G_GATES[n]).astype(BF16)
        for c in range(4):
            gates_ref[0, 4 * n + c] = r[:, c * LANES:(c + 1) * LANES]

    def pair_and_half(h):
        p, e = divmod(h, 2)
        return slice(p * LANES, (p + 1) * LANES), _lane_row(e * HEAD_DIM, (e + 1) * HEAD_DIM), (1 - e) * HEAD_DIM

    def bias_columns(compact, h, first, n):
        base = (1 - h % 2) * HEAD_DIM
        moved = pltpu.roll(compact, (base - BIAS_COLS * h) % LANES, 1)
        return moved * _lane_row(base + first, base + first + n)

    @pl.when(pl.program_id(1) == 0)
    def _new_sequence():
        carry_sc[...] = jnp.zeros(carry_sc.shape, F32)

    store_gates(0)
    small = jnp.dot(xb, ws_ref[...], preferred_element_type=F32)
    os_ref[0] = small
    iwT_ref[0] = small.T[_SM_IW:_SM_IW + HEADS, :] * IDX_SCALE
    c = _cumsum_rows(jax.nn.log_sigmoid(small + bias_ref[...])) + carry_sc[...]
    carry_sc[...] = c[tm - 1:tm, :]
    hi, mid, lo = _split3(c)
    lane = lax.broadcasted_iota(I32, (tm, LANES), 1)
    parts = jnp.where(lane < _SM_FL + HEADS, hi,
                      jnp.where(lane < _SM_FL + 2 * HEADS, pltpu.roll(mid, HEADS, 1), pltpu.roll(lo, 2 * HEADS, 1)))
    store_gates(1)
    aug = jnp.dot(parts.astype(BF16), forget_ref[...], preferred_element_type=F32)
    q = group(_G_FQ) * ATTN_SCALE
    k = group(_G_FK)
    for h in range(HEADS):
        pair, data, base = pair_and_half(h)
        q_aug = q[:, pair] * data + bias_columns(aug[:, :LANES], h, 0, 3) + _lane_row(base + 3, base + 6)
        k_aug = k[:, pair] * data + bias_columns(aug[:, LANES:], h, 3, 3) + _lane_row(base, base + 3)
        fqT_ref[0, h] = q_aug.T.astype(BF16)
        fka_ref[0, h] = k_aug.astype(BF16)
        own_ref[0, h] = jnp.sum((q_aug * k_aug).T, axis=0, keepdims=True) * LOG2E

    def store_values(ref, r):
        ones = jnp.ones((V_ROWS - HEAD_DIM, tm), BF16)
        for p in range(PAIRS):
            vT = r[:, p * LANES:(p + 1) * LANES].T.astype(BF16)
            for e in range(2):
                ref[0, 2 * p + e, 0] = jnp.concatenate([vT[e * HEAD_DIM:(e + 1) * HEAD_DIM], ones], axis=0)

    store_values(fvT_ref, group(_G_FV))
    store_gates(2)
    store_values(dvT_ref, group(_G_DV))

    lane = lax.broadcasted_iota(I32, (tm, LANES), 1)
    pos = lax.broadcasted_iota(I32, (tm, LANES), 0) + pl.program_id(1) * tm
    pos_terms = jnp.where(lane == 0, lax.shift_right_logical(pos, 6),
                          jnp.where(lane == 1, pos & 63, jnp.where(lane == 2, 1, 0))).astype(F32).astype(BF16)
    aug = jnp.dot(pos_terms, alibi_ref[...], preferred_element_type=F32)
    q = group(_G_DQ) * ATTN_SCALE
    k = group(_G_DK)
    for h in range(HEADS):
        pair, data, _ = pair_and_half(h)
        q_aug = q[:, pair] * data + bias_columns(aug[:, :LANES], h, 0, 4)
        k_aug = k[:, pair] * data + bias_columns(aug[:, LANES:], h, 0, 4)
        dqT_ref[0, h] = q_aug.T.astype(BF16)
        dka_ref[0, h] = k_aug.astype(BF16)
    iq = group(_G_IQ)
    for p in range(PAIRS):
        iqT_ref[0, p] = iq[:, p * LANES:(p + 1) * LANES].T.astype(BF16)
    store_gates(3)


def _project(x, w_main, w_small, bias_row):
    B, L, D = x.shape
    tm = TK
    nk = L // TK
    alibi, forget = _alibi_placement(), _forget_placement()
    q_t = pl.BlockSpec((1, HEADS, LANES, tm), lambda b, i: (b, 0, 0, i))
    k_aug = pl.BlockSpec((1, HEADS, tm, LANES), lambda b, i: (b, 0, i, 0))
    v_t = pl.BlockSpec((1, HEADS, 1, V_ROWS, TK), lambda b, i: (b, 0, i, 0, 0))
    q_t_shape = jax.ShapeDtypeStruct((B, HEADS, LANES, L), BF16)
    k_aug_shape = jax.ShapeDtypeStruct((B, HEADS, L, LANES), BF16)
    v_t_shape = jax.ShapeDtypeStruct((B, HEADS, nk, V_ROWS, TK), BF16)
    return pl.pallas_call(
        _proj_kernel,
        grid=(B, nk),
        in_specs=[pl.BlockSpec((1, tm, D), lambda b, i: (b, i, 0)),
                  _const_spec((D, N_MAIN)), _const_spec((D, LANES)),
                  _const_spec(alibi.shape), _const_spec(forget.shape), _const_spec((1, LANES))],
        out_specs=[pl.BlockSpec((1, N_GATE_BLOCKS, tm, LANES), lambda b, i: (b, 0, i, 0)),
                   pl.BlockSpec((1, tm, LANES), lambda b, i: (b, i, 0)),
                   q_t, k_aug, pl.BlockSpec((1, HEADS, 1, tm), lambda b, i: (b, 0, 0, i)), v_t,
                   q_t, k_aug, v_t,
                   pl.BlockSpec((1, PAIRS, LANES, tm), lambda b, i: (b, 0, 0, i)),
                   pl.BlockSpec((1, HEADS, tm), lambda b, i: (b, 0, i))],
        out_shape=[jax.ShapeDtypeStruct((B, N_GATE_BLOCKS, L, LANES), BF16),
                   jax.ShapeDtypeStruct((B, L, LANES), F32),
                   q_t_shape, k_aug_shape, jax.ShapeDtypeStruct((B, HEADS, 1, L), F32), v_t_shape,
                   q_t_shape, k_aug_shape, v_t_shape,
                   jax.ShapeDtypeStruct((B, PAIRS, LANES, L), BF16),
                   jax.ShapeDtypeStruct((B, HEADS, L), F32)],
        scratch_shapes=[pltpu.VMEM((1, LANES), F32)],
        compiler_params=_params(("parallel", "arbitrary")),
    )(x, w_main, w_small, alibi, forget, bias_row)


def _split3(v):
    hi = v.astype(BF16).astype(F32)
    r = v - hi
    mid = r.astype(BF16).astype(F32)
    lo = (r - mid).astype(BF16).astype(F32)
    return hi, mid, lo


def _cumsum_rows(x):
    n = CUMSUM_BLOCK
    L = x.shape[0]
    r = lax.broadcasted_iota(I32, (n, n), 0)
    c = lax.broadcasted_iota(I32, (n, n), 1)
    tri = jnp.where(c <= r, 1.0, 0.0).astype(BF16)
    carry = jnp.zeros((1, x.shape[1]), F32)
    outs = []
    w = x.shape[1]
    for blk in range(L // n):
        parts = jnp.concatenate([p.astype(BF16) for p in _split3(x[blk * n:(blk + 1) * n])], axis=1)
        y3 = jnp.dot(tri, parts, preferred_element_type=F32)
        y = (y3[:, :w] + y3[:, w:2 * w] + y3[:, 2 * w:]) + carry
        carry = y[n - 1:n, :]
        outs.append(y)
    return jnp.concatenate(outs, axis=0)


def _fold8(x, op):
    return op(x.reshape(x.shape[0] // 8, 8, x.shape[1]), axis=0)


def _causal(shape):
    return lax.broadcasted_iota(I32, shape, 0) <= lax.broadcasted_iota(I32, shape, 1)


L_SUM_MIN, L_SUM_MAX = 1e-30, 1e30
NEAR_BLOCK = 64


def _attend_heads(n_full, bias_fn, ref_score_fn, qT_ref, k_ref, vT_ref, g_sc, acc_sc, o_ref):
    def for_chunks(chunk):
        def body(j, carry):
            chunk(j, False)
            return carry

        lax.fori_loop(0, n_full, body, 0)
        chunk(n_full, True)

    def scores(h, rows, diag, raw):
        s = raw * LOG2E
        bias = bias_fn(rows, diag)
        return s if bias is None else s + bias

    def sweep():
        acc_sc[...] = jnp.zeros(acc_sc.shape, F32)

        def chunk(j, diag):
            rows = pl.ds(pl.multiple_of(j * TK, TK), TK)

            def raw_scores(h):
                return jnp.dot(k_ref[0, h, rows, :], qT_ref[0, h], preferred_element_type=F32)

            nxt = raw_scores(0)
            for h in range(HEADS):
                raw = nxt
                if h + 1 < HEADS:
                    nxt = raw_scores(h + 1)
                p = jnp.exp2(scores(h, rows, diag, raw) - g_sc[h]).astype(BF16)
                acc_sc[h] += jnp.dot(vT_ref[0, h, j], p, preferred_element_type=F32)

        for_chunks(chunk)
        bad = jnp.zeros((1, TQ), I32)
        for p in range(PAIRS):
            halves = []
            for e in range(2):
                acc = acc_sc[2 * p + e]
                l = acc[HEAD_DIM:HEAD_DIM + 1]
                bad = jnp.maximum(bad, jnp.where((l > L_SUM_MIN) & (l < L_SUM_MAX), 0, 1))
                halves.append(acc[:HEAD_DIM] / l)
            o_ref[0, :, p * LANES:(p + 1) * LANES] = jnp.concatenate(halves, axis=0).T.astype(o_ref.dtype)
        return jnp.max(bad) > 0

    def exact_row_maxima():
        acc_sc[:, 0:8, :] = jnp.full((HEADS, 8, TQ), NEG_INF, F32)

        def chunk(j, diag):
            rows = pl.ds(pl.multiple_of(j * TK, TK), TK)
            for h in range(HEADS):
                raw = jnp.dot(k_ref[0, h, rows, :], qT_ref[0, h], preferred_element_type=F32)
                acc_sc[h, 0:8, :] = jnp.maximum(acc_sc[h, 0:8, :], _fold8(scores(h, rows, diag, raw), jnp.max))

        for_chunks(chunk)
        for h in range(HEADS):
            g_sc[h] = jnp.max(acc_sc[h, 0:8, :], axis=0, keepdims=True)

    for h in range(HEADS):
        g_sc[h] = ref_score_fn(h)
    unsafe = sweep()

    @pl.when(unsafe)
    def _redo_with_exact_maxima():
        exact_row_maxima()
        sweep()


def _head_specs(L, index):
    nk = L // TK
    return [pl.BlockSpec((1, HEADS, L, LANES), lambda *g: (index(*g), 0, 0, 0)),
            pl.BlockSpec((1, HEADS, nk, V_ROWS, TK), lambda *g: (index(*g), 0, 0, 0, 0))]


def _head_scratch():
    return [pltpu.VMEM((HEADS, 1, TQ), F32), pltpu.VMEM((HEADS, V_ROWS, TQ), F32)]


def _fox_kernel(qT_ref, own_ref, k_ref, vT_ref, o_ref, g_sc, acc_sc):
    i = pl.program_id(1)

    def bias_fn(rows, diag):
        return jnp.where(_causal((TK, TQ)), 0.0, NEG_INF) if diag else None

    _attend_heads(i, bias_fn, lambda h: own_ref[0, h], qT_ref, k_ref, vT_ref, g_sc, acc_sc, o_ref)


def _fox_attention(qT, own, ka, vT):
    B, _, _, L = qT.shape
    return pl.pallas_call(
        _fox_kernel,
        grid=(B, L // TQ),
        in_specs=[pl.BlockSpec((1, HEADS, LANES, TQ), lambda b, i: (b, 0, 0, i)),
                  pl.BlockSpec((1, HEADS, 1, TQ), lambda b, i: (b, 0, 0, i))] + _head_specs(L, lambda b, i: b),
        out_specs=pl.BlockSpec((1, TQ, PAIRS * LANES), lambda b, i: (b, i, 0)),
        out_shape=jax.ShapeDtypeStruct((B, L, PAIRS * LANES), BF16),
        scratch_shapes=_head_scratch(),
        compiler_params=_params(("parallel", "arbitrary")),
    )(qT, own, ka, vT)


def _count_rows(mask):
    return _fold8(jnp.where(mask, 1, 0).astype(I32), jnp.sum)


def _dsa_kernel(qT_ref, k_ref, vT_ref, iqT_ref, iwT_ref, s_ref, o_ref,
                keys_sc, hi_sc, lo_sc, gmax_sc, g_sc, acc_sc):
    i = pl.program_id(1)
    n_chunks = i + 1

    def _select():
        wts = iwT_ref[0]

        def score_chunk(j, masked):
            ks = pl.multiple_of(j * TK, TK)
            ik = s_ref[0, pl.ds(ks, TK), :][:, _SM_IK:_SM_IK + HEAD_DIM].astype(BF16)
            acc = jnp.zeros((TK, TQ), F32)
            for h in range(HEADS):
                iq = iqT_ref[0, h // 2, (h % 2) * HEAD_DIM:(h % 2 + 1) * HEAD_DIM, :]
                rel = jnp.maximum(jnp.dot(ik, iq, preferred_element_type=F32), 0.0)
                acc = acc + wts[h:h + 1, :] * rel
            if masked:
                acc = jnp.where(_causal(acc.shape), acc, NEG_INF)
            bits = pltpu.bitcast(acc, I32)
            sign = lax.shift_right_arithmetic(bits, 31)
            key = (bits ^ (sign & 0x7FFFFFFF)) - sign
            keys_sc[pl.ds(ks, TK), :] = key
            hi_sc[pl.ds(ks, TK), :] = lax.shift_right_arithmetic(key, 16).astype(I16)
            lo_sc[pl.ds(ks, TK), :] = key.astype(I16) ^ jnp.full((), -(2 ** 15), I16)

        def score_body(j, carry):
            score_chunk(j, False)
            return carry

        lax.fori_loop(0, i, score_body, 0)
        score_chunk(i, True)

        def count(pred):
            def body(j, cnt):
                ks = pl.multiple_of(j * TK, TK)
                return cnt + _count_rows(pred(keys_sc[pl.ds(ks, TK), :]))
            cnt8 = lax.fori_loop(0, n_chunks, body, jnp.zeros((8, TQ), I32))
            return jnp.sum(cnt8, axis=0, keepdims=True)

        def count16(ref, pred):
            one, nil = jnp.ones((), I16), jnp.zeros((), I16)

            def body(j, cnt):
                ks = pl.multiple_of(j * TK, TK)
                hit = jnp.where(pred(ref[pl.ds(ks, TK), :]), one, nil)
                for r in range(TK // 16):
                    cnt = cnt + hit[r * 16:(r + 1) * 16, :]
                return cnt
            cnt16 = lax.fori_loop(0, n_chunks, body, jnp.zeros((16, TQ), I16))
            return jnp.sum(cnt16.astype(I32), axis=0, keepdims=True)

        def select16(ref, k):
            zero = jnp.zeros((1, TQ), I16)
            c0 = count16(ref, lambda v: v >= zero)
            ok0 = c0 >= k
            init = (jnp.where(ok0, 0, -(2 ** 15)).astype(I32),
                    jnp.where(ok0, c0, n_chunks * TK),
                    jnp.where(ok0, 0, c0))

            def bit_body(bi, carry):
                t, n_ge, n_gt = carry
                cand = t | jnp.left_shift(jnp.ones((1, TQ), I32), 14 - bi)
                cand16 = cand.astype(I16)
                c = count16(ref, lambda v: v >= cand16)
                ok = c >= k
                return jnp.where(ok, cand, t), jnp.where(ok, c, n_ge), jnp.where(ok, n_gt, c)

            return lax.fori_loop(0, 15, bit_body, init)

        def bracket_high_halves():
            def larger(a, b):
                return jnp.where(a >= b, a, b)

            def smaller(a, b):
                return jnp.where(a <= b, a, b)

            def body(j, carry):
                ks = pl.multiple_of(j * TK, TK)
                folded = hi_sc[pl.ds(ks, TOPK), :]
                for part in range(1, TK // TOPK):
                    folded = larger(folded, hi_sc[pl.ds(pl.multiple_of(ks + part * TOPK, TOPK), TOPK), :])
                gmax_sc[...] = larger(gmax_sc[...], folded)
                return carry

            gmax_sc[...] = jnp.full(gmax_sc.shape, -(2 ** 15), I16)
            lax.fori_loop(0, n_chunks, body, 0)
            gmax = gmax_sc[...]
            low, high = gmax[0:16], gmax[0:16]
            for r in range(1, TOPK // 16):
                low = smaller(low, gmax[r * 16:(r + 1) * 16])
                high = larger(high, gmax[r * 16:(r + 1) * 16])
            return (jnp.min(low.astype(I32), axis=0, keepdims=True),
                    jnp.max(high.astype(I32), axis=0, keepdims=True))

        def select_high_half():
            lower, upper = bracket_high_halves()
            span = jnp.max(upper - lower) + 1
            n_steps = sum((span > (1 << b)).astype(I32) for b in range(16))

            def step(_, carry):
                lo, hi, n_at_hi = carry
                mid = lax.shift_right_arithmetic(lo + hi, 1)
                mid16 = mid.astype(I16)
                c = count16(hi_sc, lambda v: v >= mid16)
                ok = c >= TOPK
                return jnp.where(ok, mid, lo), jnp.where(ok, hi, mid), jnp.where(ok, n_at_hi, c)

            lo, _, n_above = lax.fori_loop(0, n_steps, step, (lower, upper + 1, jnp.zeros((1, TQ), I32)))
            return lo, n_above

        t_hi, n_above = select_high_half()
        t_hi16 = t_hi.astype(I16)
        k_lo = TOPK - n_above

        def restrict_body(j, carry):
            ks = pl.multiple_of(j * TK, TK)
            in_bucket = hi_sc[pl.ds(ks, TK), :] == t_hi16
            lo_sc[pl.ds(ks, TK), :] = jnp.where(in_bucket, lo_sc[pl.ds(ks, TK), :], jnp.full((), -(2 ** 15), I16))
            return carry

        lax.fori_loop(0, n_chunks, restrict_body, 0)
        t_lo, n_ge_lo, _ = select16(lo_sc, k_lo)
        thr = jnp.left_shift(t_hi, 16) | (t_lo + 2 ** 15)

        excess = jnp.where((n_above + n_ge_lo > TOPK) & (thr > NEG_INF_KEY), 1, 0)

        @pl.when(jnp.max(excess) > 0)
        def _break_ties():
            need = TOPK - count(lambda kk: kk > thr)
            r = lax.broadcasted_iota(I32, (TK, TK), 0)
            c = lax.broadcasted_iota(I32, (TK, TK), 1)
            strict_lower = jnp.where(c < r, 1.0, 0.0).astype(BF16)

            def body(j, seen):
                ks = pl.multiple_of(j * TK, TK)
                kk = keys_sc[pl.ds(ks, TK), :]
                tie = kk == thr
                tie_f = jnp.where(tie, 1.0, 0.0)
                before = jnp.dot(strict_lower, tie_f.astype(BF16), preferred_element_type=F32) + seen
                drop = tie & (before >= need.astype(F32))
                keys_sc[pl.ds(ks, TK), :] = jnp.where(drop, INT_MIN, kk)
                return seen + jnp.sum(tie_f, axis=0, keepdims=True)

            lax.fori_loop(0, n_chunks, body, jnp.zeros((1, TQ), F32))

        def bias_chunk(j, masked, near8):
            ks = pl.multiple_of(j * TK, TK)
            bias = jnp.where(keys_sc[pl.ds(ks, TK), :] >= thr, 0.0, NEG_INF)
            if masked:
                bias = jnp.where(_causal(bias.shape), bias, NEG_INF)
            keys_sc[pl.ds(ks, TK), :] = pltpu.bitcast(bias, I32)
            for blk in range(TK // NEAR_BLOCK):
                any_selected = _fold8(bias[blk * NEAR_BLOCK:(blk + 1) * NEAR_BLOCK], jnp.max) == 0.0
                near8 = jnp.maximum(near8, jnp.where(any_selected, ks + (blk + 1) * NEAR_BLOCK - 1, -1))
            return near8

        near8 = lax.fori_loop(0, i, lambda j, n8: bias_chunk(j, False, n8), jnp.full((8, TQ), -1, I32))
        return jnp.max(bias_chunk(i, True, near8), axis=0, keepdims=True)

    nearest = _select()

    def stored_bias(rows, diag):
        return pltpu.bitcast(keys_sc[rows, :], F32)

    def nearest_key_bias(h):
        query = i * TQ + lax.broadcasted_iota(I32, (1, TQ), 1)
        return (query - nearest).astype(F32) * (-(2.0 ** -(h + 1)) * LOG2E)

    _attend_heads(i, stored_bias, nearest_key_bias, qT_ref, k_ref, vT_ref, g_sc, acc_sc, o_ref)


def _dsa_attention(qT, ka, vT, iqT, iwT, small):
    B, _, _, L = qT.shape
    return pl.pallas_call(
        _dsa_kernel,
        grid=(B, L // TQ),
        in_specs=[pl.BlockSpec((1, HEADS, LANES, TQ), lambda b, i: (b, 0, 0, i))] + _head_specs(L, lambda b, i: b)
        + [pl.BlockSpec((1, PAIRS, LANES, TQ), lambda b, i: (b, 0, 0, i)),
           pl.BlockSpec((1, HEADS, TQ), lambda b, i: (b, 0, i)),
           pl.BlockSpec((1, L, LANES), lambda b, i: (b, 0, 0))],
        out_specs=pl.BlockSpec((1, TQ, PAIRS * LANES), lambda b, i: (b, i, 0)),
        out_shape=jax.ShapeDtypeStruct((B, L, PAIRS * LANES), BF16),
        scratch_shapes=[pltpu.VMEM((L, TQ), I32), pltpu.VMEM((L, TQ), I16), pltpu.VMEM((L, TQ), I16),
                        pltpu.VMEM((TOPK, TQ), I16)]
        + _head_scratch(),
        compiler_params=_params(("parallel", "arbitrary")),
    )(qT, ka, vT, iqT, iwT, small)


def _layer_norm(z, g, b):
    mu = jnp.mean(z, axis=-1, keepdims=True)
    zc = z - mu
    var = jnp.mean(zc * zc, axis=-1, keepdims=True)
    return zc * lax.rsqrt(var + LN_EPS) * g + b


def _merge_kernel(x_ref, oa_ref, ob_ref, ga_ref, gb_ref, wa_ref, wb_ref, wo_ref, g_ref, b_ref,
                  out_ref, *, alpha):
    a = jnp.dot(oa_ref[0], wa_ref[...], preferred_element_type=F32)
    b = jnp.dot(ob_ref[0], wb_ref[...], preferred_element_type=F32)
    n_gate_blocks = D_MODEL // LANES
    gate_a = jnp.concatenate([ga_ref[0, c] for c in range(n_gate_blocks)], axis=1).astype(F32)
    gate_b = jnp.concatenate([gb_ref[0, c] for c in range(n_gate_blocks)], axis=1).astype(F32)
    merged = jax.nn.sigmoid(gate_a) * a + jax.nn.sigmoid(gate_b) * b
    y = jnp.dot(merged.astype(BF16), wo_ref[...], preferred_element_type=F32)
    out_ref[0] = _layer_norm(alpha * x_ref[0] + y, g_ref[...], b_ref[...])


def _merge(x, oa, ob, gates, wa, wb, wo, g, b, alpha):
    B, L, D = x.shape
    tm = ROW_TILE
    nb = D // LANES
    row = lambda b_, i: (b_, i, 0)
    return pl.pallas_call(
        functools.partial(_merge_kernel, alpha=alpha),
        grid=(B, L // tm),
        in_specs=[pl.BlockSpec((1, tm, D), row),
                  pl.BlockSpec((1, tm, PAIRS * LANES), row),
                  pl.BlockSpec((1, tm, PAIRS * LANES), row),
                  pl.BlockSpec((1, nb, tm, LANES), lambda b_, i: (b_, _CB_GA // nb, i, 0)),
                  pl.BlockSpec((1, nb, tm, LANES), lambda b_, i: (b_, _CB_GB // nb, i, 0)),
                  _const_spec(wa.shape), _const_spec(wb.shape), _const_spec(wo.shape),
                  _const_spec(g.shape), _const_spec(b.shape)],
        out_specs=pl.BlockSpec((1, tm, D), row),
        out_shape=jax.ShapeDtypeStruct((B, L, D), F32),
        compiler_params=_params(("parallel", "parallel")),
    )(x, oa, ob, gates, gates, wa, wb, wo, g, b)


FF_CHUNK = 256


def _ffn_kernel(x_ref, wi_ref, wo_ref, g_ref, b_ref, out_ref, h_sc, *, alpha):
    x = x_ref[0]
    xb = x.astype(BF16)
    for c in range(D_FF // FF_CHUNK):
        lo = c * FF_CHUNK
        gate = jnp.dot(xb, wi_ref[:, lo:lo + FF_CHUNK], preferred_element_type=F32)
        up = jnp.dot(xb, wi_ref[:, D_FF + lo:D_FF + lo + FF_CHUNK], preferred_element_type=F32)
        h_sc[:, lo:lo + FF_CHUNK] = (jax.nn.silu(gate) * up).astype(BF16)
    y = jnp.dot(h_sc[...], wo_ref[...], preferred_element_type=F32)
    out_ref[0] = _layer_norm(alpha * x + y, g_ref[...], b_ref[...])


def _ffn(x, wi, wo, g, b, alpha):
    B, L, D = x.shape
    tm = ROW_TILE
    row = lambda b_, i: (b_, i, 0)
    return pl.pallas_call(
        functools.partial(_ffn_kernel, alpha=alpha),
        grid=(B, L // tm),
        in_specs=[pl.BlockSpec((1, tm, D), row),
                  _const_spec(wi.shape), _const_spec(wo.shape),
                  _const_spec(g.shape), _const_spec(b.shape)],
        out_specs=pl.BlockSpec((1, tm, D), row),
        out_shape=jax.ShapeDtypeStruct((B, L, D), F32),
        scratch_shapes=[pltpu.VMEM((tm, D_FF), BF16)],
        compiler_params=_params(("parallel", "parallel")),
    )(x, wi, wo, g, b)


def _split_w_in(w):
    fq_fk = w[:, 0:2 * GROUP]
    fv = w[:, 2 * GROUP:_OFF_FLOGIT]
    dsa_iq = w[:, _OFF_DQ:_OFF_IK]
    gates = w[:, _OFF_GA:_N_IN]
    w_main = jnp.concatenate([fq_fk, gates, fv, dsa_iq], axis=1).astype(BF16)
    pad = jnp.zeros((w.shape[0], LANES - HEAD_DIM - 2 * HEADS), w.dtype)
    w_small = jnp.concatenate([w[:, _OFF_IK:_OFF_IW], w[:, _OFF_IW:_OFF_GA],
                               w[:, _OFF_FLOGIT:_OFF_DQ], pad], axis=1).astype(BF16)
    return w_main, w_small


def kernel(x, w_in, b_forget, w_branch_a, w_branch_b, w_out, ln1_g, ln1_b, w_ffn_in, w_ffn_out, ln2_g, ln2_b):
    depth = w_in.shape[0]
    alpha = (2.0 * depth) ** 0.25
    for l in range(depth):
        w_main, w_small = _split_w_in(w_in[l])
        bias_row = jnp.zeros((1, LANES), F32).at[0, _SM_FL:_SM_FL + HEADS].set(b_forget[l])
        gates, small, fqT, fka, f_own, fvT, dqT, dka, dvT, iqT, iwT = _project(x, w_main, w_small, bias_row)
        o_a = _fox_attention(fqT, f_own, fka, fvT)
        o_b = _dsa_attention(dqT, dka, dvT, iqT, iwT, small)
        x = _merge(x, o_a, o_b, gates,
                   w_branch_a[l].astype(BF16), w_branch_b[l].astype(BF16), w_out[l].astype(BF16),
                   ln1_g[l][None, :], ln1_b[l][None, :], alpha)
        x = _ffn(x, w_ffn_in[l].astype(BF16), w_ffn_out[l].astype(BF16),
                 ln2_g[l][None, :], ln2_b[l][None, :], alpha)
    return x
```

```python
import functools

import numpy as np
import jax
import jax.numpy as jnp
from jax import lax
from jax.experimental import pallas as pl
from jax.experimental.pallas import tpu as pltpu

F32 = jnp.float32
BF16 = jnp.bfloat16
I32 = jnp.int32
I16 = jnp.int16

D_MODEL = 1024
HEADS = 8
HEAD_DIM = 64
PAIRS = HEADS // 2
D_FF = 2816
TOPK = 256
LANES = 128
NEG_INF = -1e30
LN_EPS = 1e-5
ATTN_SCALE = HEAD_DIM ** -0.5
IDX_SCALE = (HEADS ** -0.5) * (HEAD_DIM ** -0.5)

_OFF_FLOGIT = 1536
_OFF_DQ = 1544
_OFF_IK = 3592
_OFF_IW = 3656
_OFF_GA = 3664
_N_IN = 5712

_G_FQ, _G_FK, _G_GATES, _G_FV, _G_DQ, _G_DK, _G_DV, _G_IQ = 0, 1, (2, 3, 4, 5), 6, 7, 8, 9, 10
_CB_GA, _CB_GB = 0, 8
N_GATE_BLOCKS = 16
GROUP = 4 * LANES
N_MAIN = 11 * GROUP
_SM_IK, _SM_IW, _SM_FL = 0, 64, 72

ROW_TILE = 512
TQ = 512
TK = 512
CUMSUM_BLOCK = 256
V_ROWS = HEAD_DIM + 16
LOG2E = 1.4426950408889634

INT_MIN = -(2 ** 31)


def _monotone_key_of(value):
    b = int(np.float32(value).view(np.int32))
    return (b ^ ((b >> 31) & 0x7FFFFFFF)) - (b >> 31)


NEG_INF_KEY = _monotone_key_of(NEG_INF)
VMEM_LIMIT = 60 * 1024 * 1024


def _params(sem):
    return pltpu.CompilerParams(dimension_semantics=sem, vmem_limit_bytes=VMEM_LIMIT)


def _const_spec(shape):
    nd = len(shape)
    return pl.BlockSpec(shape, lambda *_: (0,) * nd, pipeline_mode=pl.Buffered(1))


def _lane_row(lo, hi, value=1.0):
    lane = lax.broadcasted_iota(I32, (1, LANES), 1)
    return jnp.where((lane >= lo) & (lane < hi), value, 0.0)


BIAS_COLS = 8


def _alibi_placement():
    put = np.zeros((LANES, 2 * LANES), np.float32)
    for h in range(HEADS):
        slope = 2.0 ** -(h + 1)
        qc, kc = BIAS_COLS * h, LANES + BIAS_COLS * h
        put[0, qc], put[1, qc + 1], put[2, qc + 2], put[2, qc + 3] = -64.0 * slope, -slope, 64.0 * slope, slope
        put[2, kc], put[2, kc + 1], put[0, kc + 2], put[1, kc + 3] = 1.0, 1.0, 1.0, 1.0
    return jnp.asarray(put, BF16)


def _forget_placement():
    put = np.zeros((LANES, 2 * LANES), np.float32)
    for h in range(HEADS):
        for part in range(3):
            src = _SM_FL + part * HEADS + h
            put[src, BIAS_COLS * h + part] = 1.0
            put[src, LANES + BIAS_COLS * h + 3 + part] = -1.0
    return jnp.asarray(put, BF16)


def _proj_kernel(x_ref, wm_ref, ws_ref, alibi_ref, forget_ref, bias_ref,
                 gates_ref, os_ref, fqT_ref, fka_ref, own_ref, fvT_ref, dqT_ref, dka_ref, dvT_ref, iqT_ref, iwT_ref,
                 carry_sc):
    tm = x_ref.shape[1]
    xb = x_ref[0].astype(BF16)

    def group(g):
        return jnp.dot(xb, wm_ref[:, g * GROUP:(g + 1) * GROUP], preferred_element_type=F32)

    def store_gates(n):
        r = group(_G_GATES[n]).astype(BF16)
        for c in range(4):
            gates_ref[0, 4 * n + c] = r[:, c * LANES:(c + 1) * LANES]

    def pair_and_half(h):
        p, e = divmod(h, 2)
        return slice(p * LANES, (p + 1) * LANES), _lane_row(e * HEAD_DIM, (e + 1) * HEAD_DIM), (1 - e) * HEAD_DIM

    def bias_columns(compact, h, first, n):
        base = (1 - h % 2) * HEAD_DIM
        moved = pltpu.roll(compact, (base - BIAS_COLS * h) % LANES, 1)
        return moved * _lane_row(base + first, base + first + n)

    @pl.when(pl.program_id(1) == 0)
    def _new_sequence():
        carry_sc[...] = jnp.zeros(carry_sc.shape, F32)

    store_gates(0)
    small = jnp.dot(xb, ws_ref[...], preferred_element_type=F32)
    os_ref[0] = small
    iwT_ref[0] = small.T[_SM_IW:_SM_IW + HEADS, :] * IDX_SCALE
    c = _cumsum_rows(jax.nn.log_sigmoid(small + bias_ref[...])) + carry_sc[...]
    carry_sc[...] = c[tm - 1:tm, :]
    hi, mid, lo = _split3(c)
    lane = lax.broadcasted_iota(I32, (tm, LANES), 1)
    parts = jnp.where(lane < _SM_FL + HEADS, hi,
                      jnp.where(lane < _SM_FL + 2 * HEADS, pltpu.roll(mid, HEADS, 1), pltpu.roll(lo, 2 * HEADS, 1)))
    store_gates(1)
    aug = jnp.dot(parts.astype(BF16), forget_ref[...], preferred_element_type=F32)
    q = group(_G_FQ) * ATTN_SCALE
    k = group(_G_FK)
    for h in range(HEADS):
        pair, data, base = pair_and_half(h)
        q_aug = q[:, pair] * data + bias_columns(aug[:, :LANES], h, 0, 3) + _lane_row(base + 3, base + 6)
        k_aug = k[:, pair] * data + bias_columns(aug[:, LANES:], h, 3, 3) + _lane_row(base, base + 3)
        fqT_ref[0, h] = q_aug.T.astype(BF16)
        fka_ref[0, h] = k_aug.astype(BF16)
        own_ref[0, h] = jnp.sum((q_aug * k_aug).T, axis=0, keepdims=True) * LOG2E

    def store_values(ref, r):
        ones = jnp.ones((V_ROWS - HEAD_DIM, tm), BF16)
        for p in range(PAIRS):
            vT = r[:, p * LANES:(p + 1) * LANES].T.astype(BF16)
            for e in range(2):
                ref[0, 2 * p + e, 0] = jnp.concatenate([vT[e * HEAD_DIM:(e + 1) * HEAD_DIM], ones], axis=0)

    store_values(fvT_ref, group(_G_FV))
    store_gates(2)
    store_values(dvT_ref, group(_G_DV))

    lane = lax.broadcasted_iota(I32, (tm, LANES), 1)
    pos = lax.broadcasted_iota(I32, (tm, LANES), 0) + pl.program_id(1) * tm
    pos_terms = jnp.where(lane == 0, lax.shift_right_logical(pos, 6),
                          jnp.where(lane == 1, pos & 63, jnp.where(lane == 2, 1, 0))).astype(F32).astype(BF16)
    aug = jnp.dot(pos_terms, alibi_ref[...], preferred_element_type=F32)
    q = group(_G_DQ) * ATTN_SCALE
    k = group(_G_DK)
    for h in range(HEADS):
        pair, data, _ = pair_and_half(h)
        q_aug = q[:, pair] * data + bias_columns(aug[:, :LANES], h, 0, 4)
        k_aug = k[:, pair] * data + bias_columns(aug[:, LANES:], h, 0, 4)
        dqT_ref[0, h] = q_aug.T.astype(BF16)
        dka_ref[0, h] = k_aug.astype(BF16)
    iq = group(_G_IQ)
    for p in range(PAIRS):
        iqT_ref[0, p] = iq[:, p * LANES:(p + 1) * LANES].T.astype(BF16)
    store_gates(3)


def _project(x, w_main, w_small, bias_row):
    B, L, D = x.shape
    tm = TK
    nk = L // TK
    alibi, forget = _alibi_placement(), _forget_placement()
    q_t = pl.BlockSpec((1, HEADS, LANES, tm), lambda b, i: (b, 0, 0, i))
    k_aug = pl.BlockSpec((1, HEADS, tm, LANES), lambda b, i: (b, 0, i, 0))
    v_t = pl.BlockSpec((1, HEADS, 1, V_ROWS, TK), lambda b, i: (b, 0, i, 0, 0))
    q_t_shape = jax.ShapeDtypeStruct((B, HEADS, LANES, L), BF16)
    k_aug_shape = jax.ShapeDtypeStruct((B, HEADS, L, LANES), BF16)
    v_t_shape = jax.ShapeDtypeStruct((B, HEADS, nk, V_ROWS, TK), BF16)
    return pl.pallas_call(
        _proj_kernel,
        grid=(B, nk),
        in_specs=[pl.BlockSpec((1, tm, D), lambda b, i: (b, i, 0)),
                  _const_spec((D, N_MAIN)), _const_spec((D, LANES)),
                  _const_spec(alibi.shape), _const_spec(forget.shape), _const_spec((1, LANES))],
        out_specs=[pl.BlockSpec((1, N_GATE_BLOCKS, tm, LANES), lambda b, i: (b, 0, i, 0)),
                   pl.BlockSpec((1, tm, LANES), lambda b, i: (b, i, 0)),
                   q_t, k_aug, pl.BlockSpec((1, HEADS, 1, tm), lambda b, i: (b, 0, 0, i)), v_t,
                   q_t, k_aug, v_t,
                   pl.BlockSpec((1, PAIRS, LANES, tm), lambda b, i: (b, 0, 0, i)),
                   pl.BlockSpec((1, HEADS, tm), lambda b, i: (b, 0, i))],
        out_shape=[jax.ShapeDtypeStruct((B, N_GATE_BLOCKS, L, LANES), BF16),
                   jax.ShapeDtypeStruct((B, L, LANES), F32),
                   q_t_shape, k_aug_shape, jax.ShapeDtypeStruct((B, HEADS, 1, L), F32), v_t_shape,
                   q_t_shape, k_aug_shape, v_t_shape,
                   jax.ShapeDtypeStruct((B, PAIRS, LANES, L), BF16),
                   jax.ShapeDtypeStruct((B, HEADS, L), F32)],
        scratch_shapes=[pltpu.VMEM((1, LANES), F32)],
        compiler_params=_params(("parallel", "arbitrary")),
    )(x, w_main, w_small, alibi, forget, bias_row)


def _split3(v):
    hi = v.astype(BF16).astype(F32)
    r = v - hi
    mid = r.astype(BF16).astype(F32)
    lo = (r - mid).astype(BF16).astype(F32)
    return hi, mid, lo


def _cumsum_rows(x):
    n = CUMSUM_BLOCK
    L = x.shape[0]
    r = lax.broadcasted_iota(I32, (n, n), 0)
    c = lax.broadcasted_iota(I32, (n, n), 1)
    tri = jnp.where(c <= r, 1.0, 0.0).astype(BF16)
    carry = jnp.zeros((1, x.shape[1]), F32)
    outs = []
    w = x.shape[1]
    for blk in range(L // n):
        parts = jnp.concatenate([p.astype(BF16) for p in _split3(x[blk * n:(blk + 1) * n])], axis=1)
        y3 = jnp.dot(tri, parts, preferred_element_type=F32)
        y = (y3[:, :w] + y3[:, w:2 * w] + y3[:, 2 * w:]) + carry
        carry = y[n - 1:n, :]
        outs.append(y)
    return jnp.concatenate(outs, axis=0)


def _fold8(x, op):
    return op(x.reshape(x.shape[0] // 8, 8, x.shape[1]), axis=0)


def _causal(shape):
    return lax.broadcasted_iota(I32, shape, 0) <= lax.broadcasted_iota(I32, shape, 1)


L_SUM_MIN, L_SUM_MAX = 1e-30, 1e30
NEAR_BLOCK = 64


def _attend_heads(n_full, bias_fn, ref_score_fn, qT_ref, k_ref, vT_ref, g_sc, acc_sc, o_ref):
    def for_chunks(chunk):
        def body(j, carry):
            chunk(j, False)
            return carry

        lax.fori_loop(0, n_full, body, 0)
        chunk(n_full, True)

    def scores(h, rows, diag, raw):
        s = raw * LOG2E
        bias = bias_fn(rows, diag)
        return s if bias is None else s + bias

    def sweep():
        acc_sc[...] = jnp.zeros(acc_sc.shape, F32)

        def chunk(j, diag):
            rows = pl.ds(pl.multiple_of(j * TK, TK), TK)

            def raw_scores(h):
                return jnp.dot(k_ref[0, h, rows, :], qT_ref[0, h], preferred_element_type=F32)

            nxt = raw_scores(0)
            for h in range(HEADS):
                raw = nxt
                if h + 1 < HEADS:
                    nxt = raw_scores(h + 1)
                p = jnp.exp2(scores(h, rows, diag, raw) - g_sc[h]).astype(BF16)
                acc_sc[h] += jnp.dot(vT_ref[0, h, j], p, preferred_element_type=F32)

        for_chunks(chunk)
        bad = jnp.zeros((1, TQ), I32)
        for p in range(PAIRS):
            halves = []
            for e in range(2):
                acc = acc_sc[2 * p + e]
                l = acc[HEAD_DIM:HEAD_DIM + 1]
                bad = jnp.maximum(bad, jnp.where((l > L_SUM_MIN) & (l < L_SUM_MAX), 0, 1))
                halves.append(acc[:HEAD_DIM] / l)
            o_ref[0, :, p * LANES:(p + 1) * LANES] = jnp.concatenate(halves, axis=0).T.astype(o_ref.dtype)
        return jnp.max(bad) > 0

    def exact_row_maxima():
        acc_sc[:, 0:8, :] = jnp.full((HEADS, 8, TQ), NEG_INF, F32)

        def chunk(j, diag):
            rows = pl.ds(pl.multiple_of(j * TK, TK), TK)
            for h in range(HEADS):
                raw = jnp.dot(k_ref[0, h, rows, :], qT_ref[0, h], preferred_element_type=F32)
                acc_sc[h, 0:8, :] = jnp.maximum(acc_sc[h, 0:8, :], _fold8(scores(h, rows, diag, raw), jnp.max))

        for_chunks(chunk)
        for h in range(HEADS):
            g_sc[h] = jnp.max(acc_sc[h, 0:8, :], axis=0, keepdims=True)

    for h in range(HEADS):
        g_sc[h] = ref_score_fn(h)
    unsafe = sweep()

    @pl.when(unsafe)
    def _redo_with_exact_maxima():
        exact_row_maxima()
        sweep()


def _head_specs(L, index):
    nk = L // TK
    return [pl.BlockSpec((1, HEADS, L, LANES), lambda *g: (index(*g), 0, 0, 0)),
            pl.BlockSpec((1, HEADS, nk, V_ROWS, TK), lambda *g: (index(*g), 0, 0, 0, 0))]


def _head_scratch():
    return [pltpu.VMEM((HEADS, 1, TQ), F32), pltpu.VMEM((HEADS, V_ROWS, TQ), F32)]


def _fox_kernel(qT_ref, own_ref, k_ref, vT_ref, o_ref, g_sc, acc_sc):
    i = pl.program_id(1)

    def bias_fn(rows, diag):
        return jnp.where(_causal((TK, TQ)), 0.0, NEG_INF) if diag else None

    _attend_heads(i, bias_fn, lambda h: own_ref[0, h], qT_ref, k_ref, vT_ref, g_sc, acc_sc, o_ref)


def _fox_attention(qT, own, ka, vT):
    B, _, _, L = qT.shape
    return pl.pallas_call(
        _fox_kernel,
        grid=(B, L // TQ),
        in_specs=[pl.BlockSpec((1, HEADS, LANES, TQ), lambda b, i: (b, 0, 0, i)),
                  pl.BlockSpec((1, HEADS, 1, TQ), lambda b, i: (b, 0, 0, i))] + _head_specs(L, lambda b, i: b),
        out_specs=pl.BlockSpec((1, TQ, PAIRS * LANES), lambda b, i: (b, i, 0)),
        out_shape=jax.ShapeDtypeStruct((B, L, PAIRS * LANES), BF16),
        scratch_shapes=_head_scratch(),
        compiler_params=_params(("parallel", "arbitrary")),
    )(qT, own, ka, vT)


def _count_rows(mask):
    return _fold8(jnp.where(mask, 1, 0).astype(I32), jnp.sum)


def _dsa_kernel(qT_ref, k_ref, vT_ref, iqT_ref, iwT_ref, s_ref, o_ref,
                keys_sc, hi_sc, lo_sc, g_sc, acc_sc):
    i = pl.program_id(1)
    n_chunks = i + 1

    def _select():
        wts = iwT_ref[0]

        def score_chunk(j, masked):
            ks = pl.multiple_of(j * TK, TK)
            ik = s_ref[0, pl.ds(ks, TK), :][:, _SM_IK:_SM_IK + HEAD_DIM].astype(BF16)
            acc = jnp.zeros((TK, TQ), F32)
            for h in range(HEADS):
                iq = iqT_ref[0, h // 2, (h % 2) * HEAD_DIM:(h % 2 + 1) * HEAD_DIM, :]
                rel = jnp.maximum(jnp.dot(ik, iq, preferred_element_type=F32), 0.0)
                acc = acc + wts[h:h + 1, :] * rel
            if masked:
                acc = jnp.where(_causal(acc.shape), acc, NEG_INF)
            bits = pltpu.bitcast(acc, I32)
            sign = lax.shift_right_arithmetic(bits, 31)
            key = (bits ^ (sign & 0x7FFFFFFF)) - sign
            keys_sc[pl.ds(ks, TK), :] = key
            hi_sc[pl.ds(ks, TK), :] = lax.shift_right_arithmetic(key, 16).astype(I16)
            lo_sc[pl.ds(ks, TK), :] = key.astype(I16) ^ jnp.full((), -(2 ** 15), I16)

        def score_body(j, carry):
            score_chunk(j, False)
            return carry

        lax.fori_loop(0, i, score_body, 0)
        score_chunk(i, True)

        def count(pred):
            def body(j, cnt):
                ks = pl.multiple_of(j * TK, TK)
                return cnt + _count_rows(pred(keys_sc[pl.ds(ks, TK), :]))
            cnt8 = lax.fori_loop(0, n_chunks, body, jnp.zeros((8, TQ), I32))
            return jnp.sum(cnt8, axis=0, keepdims=True)

        def count16(ref, pred):
            one, nil = jnp.ones((), I16), jnp.zeros((), I16)

            def body(j, cnt):
                ks = pl.multiple_of(j * TK, TK)
                hit = jnp.where(pred(ref[pl.ds(ks, TK), :]), one, nil)
                for r in range(TK // 16):
                    cnt = cnt + hit[r * 16:(r + 1) * 16, :]
                return cnt
            cnt16 = lax.fori_loop(0, n_chunks, body, jnp.zeros((16, TQ), I16))
            return jnp.sum(cnt16.astype(I32), axis=0, keepdims=True)

        def select16(ref, k):
            zero = jnp.zeros((1, TQ), I16)
            c0 = count16(ref, lambda v: v >= zero)
            ok0 = c0 >= k
            init = (jnp.where(ok0, 0, -(2 ** 15)).astype(I32),
                    jnp.where(ok0, c0, n_chunks * TK),
                    jnp.where(ok0, 0, c0))

            def bit_body(bi, carry):
                t, n_ge, n_gt = carry
                cand = t | jnp.left_shift(jnp.ones((1, TQ), I32), 14 - bi)
                cand16 = cand.astype(I16)
                c = count16(ref, lambda v: v >= cand16)
                ok = c >= k
                return jnp.where(ok, cand, t), jnp.where(ok, c, n_ge), jnp.where(ok, n_gt, c)

            return lax.fori_loop(0, 15, bit_body, init)

        t_hi, _, n_above = select16(hi_sc, TOPK)
        t_hi16 = t_hi.astype(I16)
        k_lo = TOPK - n_above

        def restrict_body(j, carry):
            ks = pl.multiple_of(j * TK, TK)
            in_bucket = hi_sc[pl.ds(ks, TK), :] == t_hi16
            lo_sc[pl.ds(ks, TK), :] = jnp.where(in_bucket, lo_sc[pl.ds(ks, TK), :], jnp.full((), -(2 ** 15), I16))
            return carry

        lax.fori_loop(0, n_chunks, restrict_body, 0)
        t_lo, n_ge_lo, _ = select16(lo_sc, k_lo)
        thr = jnp.left_shift(t_hi, 16) | (t_lo + 2 ** 15)

        excess = jnp.where((n_above + n_ge_lo > TOPK) & (thr > NEG_INF_KEY), 1, 0)

        @pl.when(jnp.max(excess) > 0)
        def _break_ties():
            need = TOPK - count(lambda kk: kk > thr)
            r = lax.broadcasted_iota(I32, (TK, TK), 0)
            c = lax.broadcasted_iota(I32, (TK, TK), 1)
            strict_lower = jnp.where(c < r, 1.0, 0.0).astype(BF16)

            def body(j, seen):
                ks = pl.multiple_of(j * TK, TK)
                kk = keys_sc[pl.ds(ks, TK), :]
                tie = kk == thr
                tie_f = jnp.where(tie, 1.0, 0.0)
                before = jnp.dot(strict_lower, tie_f.astype(BF16), preferred_element_type=F32) + seen
                drop = tie & (before >= need.astype(F32))
                keys_sc[pl.ds(ks, TK), :] = jnp.where(drop, INT_MIN, kk)
                return seen + jnp.sum(tie_f, axis=0, keepdims=True)

            lax.fori_loop(0, n_chunks, body, jnp.zeros((1, TQ), F32))

        def bias_chunk(j, masked, near8):
            ks = pl.multiple_of(j * TK, TK)
            bias = jnp.where(keys_sc[pl.ds(ks, TK), :] >= thr, 0.0, NEG_INF)
            if masked:
                bias = jnp.where(_causal(bias.shape), bias, NEG_INF)
            keys_sc[pl.ds(ks, TK), :] = pltpu.bitcast(bias, I32)
            for blk in range(TK // NEAR_BLOCK):
                any_selected = _fold8(bias[blk * NEAR_BLOCK:(blk + 1) * NEAR_BLOCK], jnp.max) == 0.0
                near8 = jnp.maximum(near8, jnp.where(any_selected, ks + (blk + 1) * NEAR_BLOCK - 1, -1))
            return near8

        near8 = lax.fori_loop(0, i, lambda j, n8: bias_chunk(j, False, n8), jnp.full((8, TQ), -1, I32))
        return jnp.max(bias_chunk(i, True, near8), axis=0, keepdims=True)

    nearest = _select()

    def stored_bias(rows, diag):
        return pltpu.bitcast(keys_sc[rows, :], F32)

    def nearest_key_bias(h):
        query = i * TQ + lax.broadcasted_iota(I32, (1, TQ), 1)
        return (query - nearest).astype(F32) * (-(2.0 ** -(h + 1)) * LOG2E)

    _attend_heads(i, stored_bias, nearest_key_bias, qT_ref, k_ref, vT_ref, g_sc, acc_sc, o_ref)


def _dsa_attention(qT, ka, vT, iqT, iwT, small):
    B, _, _, L = qT.shape
    return pl.pallas_call(
        _dsa_kernel,
        grid=(B, L // TQ),
        in_specs=[pl.BlockSpec((1, HEADS, LANES, TQ), lambda b, i: (b, 0, 0, i))] + _head_specs(L, lambda b, i: b)
        + [pl.BlockSpec((1, PAIRS, LANES, TQ), lambda b, i: (b, 0, 0, i)),
           pl.BlockSpec((1, HEADS, TQ), lambda b, i: (b, 0, i)),
           pl.BlockSpec((1, L, LANES), lambda b, i: (b, 0, 0))],
        out_specs=pl.BlockSpec((1, TQ, PAIRS * LANES), lambda b, i: (b, i, 0)),
        out_shape=jax.ShapeDtypeStruct((B, L, PAIRS * LANES), BF16),
        scratch_shapes=[pltpu.VMEM((L, TQ), I32), pltpu.VMEM((L, TQ), I16), pltpu.VMEM((L, TQ), I16)]
        + _head_scratch(),
        compiler_params=_params(("parallel", "arbitrary")),
    )(qT, ka, vT, iqT, iwT, small)


def _layer_norm(z, g, b):
    mu = jnp.mean(z, axis=-1, keepdims=True)
    zc = z - mu
    var = jnp.mean(zc * zc, axis=-1, keepdims=True)
    return zc * lax.rsqrt(var + LN_EPS) * g + b


def _merge_kernel(x_ref, oa_ref, ob_ref, ga_ref, gb_ref, wa_ref, wb_ref, wo_ref, g_ref, b_ref,
                  out_ref, *, alpha):
    a = jnp.dot(oa_ref[0], wa_ref[...], preferred_element_type=F32)
    b = jnp.dot(ob_ref[0], wb_ref[...], preferred_element_type=F32)
    n_gate_blocks = D_MODEL // LANES
    gate_a = jnp.concatenate([ga_ref[0, c] for c in range(n_gate_blocks)], axis=1).astype(F32)
    gate_b = jnp.concatenate([gb_ref[0, c] for c in range(n_gate_blocks)], axis=1).astype(F32)
    merged = jax.nn.sigmoid(gate_a) * a + jax.nn.sigmoid(gate_b) * b
    y = jnp.dot(merged.astype(BF16), wo_ref[...], preferred_element_type=F32)
    out_ref[0] = _layer_norm(alpha * x_ref[0] + y, g_ref[...], b_ref[...])


def _merge(x, oa, ob, gates, wa, wb, wo, g, b, alpha):
    B, L, D = x.shape
    tm = ROW_TILE
    nb = D // LANES
    row = lambda b_, i: (b_, i, 0)
    return pl.pallas_call(
        functools.partial(_merge_kernel, alpha=alpha),
        grid=(B, L // tm),
        in_specs=[pl.BlockSpec((1, tm, D), row),
                  pl.BlockSpec((1, tm, PAIRS * LANES), row),
                  pl.BlockSpec((1, tm, PAIRS * LANES), row),
                  pl.BlockSpec((1, nb, tm, LANES), lambda b_, i: (b_, _CB_GA // nb, i, 0)),
                  pl.BlockSpec((1, nb, tm, LANES), lambda b_, i: (b_, _CB_GB // nb, i, 0)),
                  _const_spec(wa.shape), _const_spec(wb.shape), _const_spec(wo.shape),
                  _const_spec(g.shape), _const_spec(b.shape)],
        out_specs=pl.BlockSpec((1, tm, D), row),
        out_shape=jax.ShapeDtypeStruct((B, L, D), F32),
        compiler_params=_params(("parallel", "parallel")),
    )(x, oa, ob, gates, gates, wa, wb, wo, g, b)


FF_CHUNK = 256


def _ffn_kernel(x_ref, wi_ref, wo_ref, g_ref, b_ref, out_ref, h_sc, *, alpha):
    x = x_ref[0]
    xb = x.astype(BF16)
    for c in range(D_FF // FF_CHUNK):
        lo = c * FF_CHUNK
        gate = jnp.dot(xb, wi_ref[:, lo:lo + FF_CHUNK], preferred_element_type=F32)
        up = jnp.dot(xb, wi_ref[:, D_FF + lo:D_FF + lo + FF_CHUNK], preferred_element_type=F32)
        h_sc[:, lo:lo + FF_CHUNK] = (jax.nn.silu(gate) * up).astype(BF16)
    y = jnp.dot(h_sc[...], wo_ref[...], preferred_element_type=F32)
    out_ref[0] = _layer_norm(alpha * x + y, g_ref[...], b_ref[...])


def _ffn(x, wi, wo, g, b, alpha):
    B, L, D = x.shape
    tm = ROW_TILE
    row = lambda b_, i: (b_, i, 0)
    return pl.pallas_call(
        functools.partial(_ffn_kernel, alpha=alpha),
        grid=(B, L // tm),
        in_specs=[pl.BlockSpec((1, tm, D), row),
                  _const_spec(wi.shape), _const_spec(wo.shape),
                  _const_spec(g.shape), _const_spec(b.shape)],
        out_specs=pl.BlockSpec((1, tm, D), row),
        out_shape=jax.ShapeDtypeStruct((B, L, D), F32),
        scratch_shapes=[pltpu.VMEM((tm, D_FF), BF16)],
        compiler_params=_params(("parallel", "parallel")),
    )(x, wi, wo, g, b)


def _split_w_in(w):
    fq_fk = w[:, 0:2 * GROUP]
    fv = w[:, 2 * GROUP:_OFF_FLOGIT]
    dsa_iq = w[:, _OFF_DQ:_OFF_IK]
    gates = w[:, _OFF_GA:_N_IN]
    w_main = jnp.concatenate([fq_fk, gates, fv, dsa_iq], axis=1).astype(BF16)
    pad = jnp.zeros((w.shape[0], LANES - HEAD_DIM - 2 * HEADS), w.dtype)
    w_small = jnp.concatenate([w[:, _OFF_IK:_OFF_IW], w[:, _OFF_IW:_OFF_GA],
                               w[:, _OFF_FLOGIT:_OFF_DQ], pad], axis=1).astype(BF16)
    return w_main, w_small


def kernel(x, w_in, b_forget, w_branch_a, w_branch_b, w_out, ln1_g, ln1_b, w_ffn_in, w_ffn_out, ln2_g, ln2_b):
    depth = w_in.shape[0]
    alpha = (2.0 * depth) ** 0.25
    for l in range(depth):
        w_main, w_small = _split_w_in(w_in[l])
        bias_row = jnp.zeros((1, LANES), F32).at[0, _SM_FL:_SM_FL + HEADS].set(b_forget[l])
        gates, small, fqT, fka, f_own, fvT, dqT, dka, dvT, iqT, iwT = _project(x, w_main, w_small, bias_row)
        o_a = _fox_attention(fqT, f_own, fka, fvT)
        o_b = _dsa_attention(dqT, dka, dvT, iqT, iwT, small)
        x = _merge(x, o_a, o_b, gates,
                   w_branch_a[l].astype(BF16), w_branch_b[l].astype(BF16), w_out[l].astype(BF16),
                   ln1_g[l][None, :], ln1_b[l][None, :], alpha)
        x = _ffn(x, w_ffn_in[l].astype(BF16), w_ffn_out[l].astype(BF16),
                 ln2_g[l][None, :], ln2_b[l][None, :], alpha)
    return x
```

```python
import functools

import numpy as np
import jax
import jax.numpy as jnp
from jax import lax
from jax.experimental import pallas as pl
from jax.experimental.pallas import tpu as pltpu

F32 = jnp.float32
BF16 = jnp.bfloat16
I32 = jnp.int32
I16 = jnp.int16

D_MODEL = 1024
HEADS = 8
HEAD_DIM = 64
PAIRS = HEADS // 2
D_FF = 2816
TOPK = 256
LANES = 128
NEG_INF = -1e30
LN_EPS = 1e-5
ATTN_SCALE = HEAD_DIM ** -0.5
IDX_SCALE = (HEADS ** -0.5) * (HEAD_DIM ** -0.5)

_OFF_FLOGIT = 1536
_OFF_DQ = 1544
_OFF_IK = 3592
_OFF_IW = 3656
_OFF_GA = 3664
_N_IN = 5712

_G_FQ, _G_FK, _G_GATES, _G_FV, _G_DQ, _G_DK, _G_DV, _G_IQ = 0, 1, (2, 3, 4, 5), 6, 7, 8, 9, 10
_CB_GA, _CB_GB = 0, 8
N_GATE_BLOCKS = 16
GROUP = 4 * LANES
N_MAIN = 11 * GROUP
_SM_IK, _SM_IW, _SM_FL = 0, 64, 72

ROW_TILE = 512
TQ = 512
TK = 512
CUMSUM_BLOCK = 256
V_ROWS = HEAD_DIM + 16
LOG2E = 1.4426950408889634

INT_MIN = -(2 ** 31)


def _monotone_key_of(value):
    b = int(np.float32(value).view(np.int32))
    return (b ^ ((b >> 31) & 0x7FFFFFFF)) - (b >> 31)


NEG_INF_KEY = _monotone_key_of(NEG_INF)
VMEM_LIMIT = 60 * 1024 * 1024


def _params(sem):
    return pltpu.CompilerParams(dimension_semantics=sem, vmem_limit_bytes=VMEM_LIMIT)


def _const_spec(shape):
    nd = len(shape)
    return pl.BlockSpec(shape, lambda *_: (0,) * nd, pipeline_mode=pl.Buffered(1))


def _layer_spec(stacked, layer):
    tail = stacked.shape[1:]
    return pl.BlockSpec((None,) + tail, lambda *_: (layer,) + (0,) * len(tail), pipeline_mode=pl.Buffered(1))


def _lane_row(lo, hi, value=1.0):
    lane = lax.broadcasted_iota(I32, (1, LANES), 1)
    return jnp.where((lane >= lo) & (lane < hi), value, 0.0)


BIAS_COLS = 8


def _alibi_placement():
    put = np.zeros((LANES, 2 * LANES), np.float32)
    for h in range(HEADS):
        slope = 2.0 ** -(h + 1)
        qc, kc = BIAS_COLS * h, LANES + BIAS_COLS * h
        put[0, qc], put[1, qc + 1], put[2, qc + 2], put[2, qc + 3] = -64.0 * slope, -slope, 64.0 * slope, slope
        put[2, kc], put[2, kc + 1], put[0, kc + 2], put[1, kc + 3] = 1.0, 1.0, 1.0, 1.0
    return jnp.asarray(put, BF16)


def _forget_placement():
    put = np.zeros((LANES, 2 * LANES), np.float32)
    for h in range(HEADS):
        for part in range(3):
            src = _SM_FL + part * HEADS + h
            put[src, BIAS_COLS * h + part] = 1.0
            put[src, LANES + BIAS_COLS * h + 3 + part] = -1.0
    return jnp.asarray(put, BF16)


def _proj_kernel(x_ref, wm_ref, ws_ref, alibi_ref, forget_ref, bias_ref,
                 gates_ref, os_ref, fqT_ref, fka_ref, own_ref, fvT_ref, dqT_ref, dka_ref, dvT_ref, iqT_ref, iwT_ref,
                 carry_sc):
    tm = x_ref.shape[1]
    xb = x_ref[0].astype(BF16)

    def group(g):
        return jnp.dot(xb, wm_ref[:, g * GROUP:(g + 1) * GROUP], preferred_element_type=F32)

    def store_gates(n):
        r = group(_G_GATES[n]).astype(BF16)
        for c in range(4):
            gates_ref[0, 4 * n + c] = r[:, c * LANES:(c + 1) * LANES]

    def pair_and_half(h):
        p, e = divmod(h, 2)
        return slice(p * LANES, (p + 1) * LANES), _lane_row(e * HEAD_DIM, (e + 1) * HEAD_DIM), (1 - e) * HEAD_DIM

    def bias_columns(compact, h, first, n):
        base = (1 - h % 2) * HEAD_DIM
        moved = pltpu.roll(compact, (base - BIAS_COLS * h) % LANES, 1)
        return moved * _lane_row(base + first, base + first + n)

    @pl.when(pl.program_id(1) == 0)
    def _new_sequence():
        carry_sc[...] = jnp.zeros(carry_sc.shape, F32)

    store_gates(0)
    small = jnp.dot(xb, ws_ref[...], preferred_element_type=F32)
    os_ref[0] = small
    iwT_ref[0] = small.T[_SM_IW:_SM_IW + HEADS, :] * IDX_SCALE
    c = _cumsum_rows(jax.nn.log_sigmoid(small + bias_ref[...])) + carry_sc[...]
    carry_sc[...] = c[tm - 1:tm, :]
    hi, mid, lo = _split3(c)
    lane = lax.broadcasted_iota(I32, (tm, LANES), 1)
    parts = jnp.where(lane < _SM_FL + HEADS, hi,
                      jnp.where(lane < _SM_FL + 2 * HEADS, pltpu.roll(mid, HEADS, 1), pltpu.roll(lo, 2 * HEADS, 1)))
    store_gates(1)
    aug = jnp.dot(parts.astype(BF16), forget_ref[...], preferred_element_type=F32)
    q = group(_G_FQ) * ATTN_SCALE
    k = group(_G_FK)
    for h in range(HEADS):
        pair, data, base = pair_and_half(h)
        q_aug = q[:, pair] * data + bias_columns(aug[:, :LANES], h, 0, 3) + _lane_row(base + 3, base + 6)
        k_aug = k[:, pair] * data + bias_columns(aug[:, LANES:], h, 3, 3) + _lane_row(base, base + 3)
        fqT_ref[0, h] = q_aug.T.astype(BF16)
        fka_ref[0, h] = k_aug.astype(BF16)
        own_ref[0, h] = jnp.sum((q_aug * k_aug).T, axis=0, keepdims=True) * LOG2E

    def store_values(ref, r):
        ones = jnp.ones((V_ROWS - HEAD_DIM, tm), BF16)
        for p in range(PAIRS):
            vT = r[:, p * LANES:(p + 1) * LANES].T.astype(BF16)
            for e in range(2):
                ref[0, 2 * p + e, 0] = jnp.concatenate([vT[e * HEAD_DIM:(e + 1) * HEAD_DIM], ones], axis=0)

    store_values(fvT_ref, group(_G_FV))
    store_gates(2)
    store_values(dvT_ref, group(_G_DV))

    lane = lax.broadcasted_iota(I32, (tm, LANES), 1)
    pos = lax.broadcasted_iota(I32, (tm, LANES), 0) + pl.program_id(1) * tm
    pos_terms = jnp.where(lane == 0, lax.shift_right_logical(pos, 6),
                          jnp.where(lane == 1, pos & 63, jnp.where(lane == 2, 1, 0))).astype(F32).astype(BF16)
    aug = jnp.dot(pos_terms, alibi_ref[...], preferred_element_type=F32)
    q = group(_G_DQ) * ATTN_SCALE
    k = group(_G_DK)
    for h in range(HEADS):
        pair, data, _ = pair_and_half(h)
        q_aug = q[:, pair] * data + bias_columns(aug[:, :LANES], h, 0, 4)
        k_aug = k[:, pair] * data + bias_columns(aug[:, LANES:], h, 0, 4)
        dqT_ref[0, h] = q_aug.T.astype(BF16)
        dka_ref[0, h] = k_aug.astype(BF16)
    iq = group(_G_IQ)
    for p in range(PAIRS):
        iqT_ref[0, p] = iq[:, p * LANES:(p + 1) * LANES].T.astype(BF16)
    store_gates(3)


def _project(x, w_main, w_small, bias_row, layer):
    B, L, D = x.shape
    tm = TK
    nk = L // TK
    alibi, forget = _alibi_placement(), _forget_placement()
    q_t = pl.BlockSpec((1, HEADS, LANES, tm), lambda b, i: (b, 0, 0, i))
    k_aug = pl.BlockSpec((1, HEADS, tm, LANES), lambda b, i: (b, 0, i, 0))
    v_t = pl.BlockSpec((1, HEADS, 1, V_ROWS, TK), lambda b, i: (b, 0, i, 0, 0))
    q_t_shape = jax.ShapeDtypeStruct((B, HEADS, LANES, L), BF16)
    k_aug_shape = jax.ShapeDtypeStruct((B, HEADS, L, LANES), BF16)
    v_t_shape = jax.ShapeDtypeStruct((B, HEADS, nk, V_ROWS, TK), BF16)
    return pl.pallas_call(
        _proj_kernel,
        grid=(B, nk),
        in_specs=[pl.BlockSpec((1, tm, D), lambda b, i: (b, i, 0)),
                  _layer_spec(w_main, layer), _layer_spec(w_small, layer),
                  _const_spec(alibi.shape), _const_spec(forget.shape), _layer_spec(bias_row, layer)],
        out_specs=[pl.BlockSpec((1, N_GATE_BLOCKS, tm, LANES), lambda b, i: (b, 0, i, 0)),
                   pl.BlockSpec((1, tm, LANES), lambda b, i: (b, i, 0)),
                   q_t, k_aug, pl.BlockSpec((1, HEADS, 1, tm), lambda b, i: (b, 0, 0, i)), v_t,
                   q_t, k_aug, v_t,
                   pl.BlockSpec((1, PAIRS, LANES, tm), lambda b, i: (b, 0, 0, i)),
                   pl.BlockSpec((1, HEADS, tm), lambda b, i: (b, 0, i))],
        out_shape=[jax.ShapeDtypeStruct((B, N_GATE_BLOCKS, L, LANES), BF16),
                   jax.ShapeDtypeStruct((B, L, LANES), F32),
                   q_t_shape, k_aug_shape, jax.ShapeDtypeStruct((B, HEADS, 1, L), F32), v_t_shape,
                   q_t_shape, k_aug_shape, v_t_shape,
                   jax.ShapeDtypeStruct((B, PAIRS, LANES, L), BF16),
                   jax.ShapeDtypeStruct((B, HEADS, L), F32)],
        scratch_shapes=[pltpu.VMEM((1, LANES), F32)],
        compiler_params=_params(("parallel", "arbitrary")),
    )(x, w_main, w_small, alibi, forget, bias_row)


def _split3(v):
    hi = v.astype(BF16).astype(F32)
    r = v - hi
    mid = r.astype(BF16).astype(F32)
    lo = (r - mid).astype(BF16).astype(F32)
    return hi, mid, lo


def _cumsum_rows(x):
    n = CUMSUM_BLOCK
    L = x.shape[0]
    r = lax.broadcasted_iota(I32, (n, n), 0)
    c = lax.broadcasted_iota(I32, (n, n), 1)
    tri = jnp.where(c <= r, 1.0, 0.0).astype(BF16)
    carry = jnp.zeros((1, x.shape[1]), F32)
    outs = []
    w = x.shape[1]
    for blk in range(L // n):
        parts = jnp.concatenate([p.astype(BF16) for p in _split3(x[blk * n:(blk + 1) * n])], axis=1)
        y3 = jnp.dot(tri, parts, preferred_element_type=F32)
        y = (y3[:, :w] + y3[:, w:2 * w] + y3[:, 2 * w:]) + carry
        carry = y[n - 1:n, :]
        outs.append(y)
    return jnp.concatenate(outs, axis=0)


def _fold8(x, op):
    return op(x.reshape(x.shape[0] // 8, 8, x.shape[1]), axis=0)


def _causal(shape):
    return lax.broadcasted_iota(I32, shape, 0) <= lax.broadcasted_iota(I32, shape, 1)


L_SUM_MIN, L_SUM_MAX = 1e-30, 1e30
NEAR_BLOCK = 64


def _attend_heads(n_full, bias_fn, ref_score_fn, qT_ref, k_ref, vT_ref, g_sc, acc_sc, o_ref):
    def for_chunks(chunk):
        def body(j, carry):
            chunk(j, False)
            return carry

        lax.fori_loop(0, n_full, body, 0)
        chunk(n_full, True)

    def scores(h, rows, diag, raw):
        s = raw * LOG2E
        bias = bias_fn(rows, diag)
        return s if bias is None else s + bias

    def sweep():
        acc_sc[...] = jnp.zeros(acc_sc.shape, F32)

        def chunk(j, diag):
            rows = pl.ds(pl.multiple_of(j * TK, TK), TK)

            def raw_scores(h):
                return jnp.dot(k_ref[0, h, rows, :], qT_ref[0, h], preferred_element_type=F32)

            nxt = raw_scores(0)
            for h in range(HEADS):
                raw = nxt
                if h + 1 < HEADS:
                    nxt = raw_scores(h + 1)
                p = jnp.exp2(scores(h, rows, diag, raw) - g_sc[h]).astype(BF16)
                acc_sc[h] += jnp.dot(vT_ref[0, h, j], p, preferred_element_type=F32)

        for_chunks(chunk)
        bad = jnp.zeros((1, TQ), I32)
        for p in range(PAIRS):
            halves = []
            for e in range(2):
                acc = acc_sc[2 * p + e]
                l = acc[HEAD_DIM:HEAD_DIM + 1]
                bad = jnp.maximum(bad, jnp.where((l > L_SUM_MIN) & (l < L_SUM_MAX), 0, 1))
                halves.append(acc[:HEAD_DIM] / l)
            o_ref[0, :, p * LANES:(p + 1) * LANES] = jnp.concatenate(halves, axis=0).T.astype(o_ref.dtype)
        return jnp.max(bad) > 0

    def exact_row_maxima():
        acc_sc[:, 0:8, :] = jnp.full((HEADS, 8, TQ), NEG_INF, F32)

        def chunk(j, diag):
            rows = pl.ds(pl.multiple_of(j * TK, TK), TK)
            for h in range(HEADS):
                raw = jnp.dot(k_ref[0, h, rows, :], qT_ref[0, h], preferred_element_type=F32)
                acc_sc[h, 0:8, :] = jnp.maximum(acc_sc[h, 0:8, :], _fold8(scores(h, rows, diag, raw), jnp.max))

        for_chunks(chunk)
        for h in range(HEADS):
            g_sc[h] = jnp.max(acc_sc[h, 0:8, :], axis=0, keepdims=True)

    for h in range(HEADS):
        g_sc[h] = ref_score_fn(h)
    unsafe = sweep()

    @pl.when(unsafe)
    def _redo_with_exact_maxima():
        exact_row_maxima()
        sweep()


def _head_specs(L, index):
    nk = L // TK
    return [pl.BlockSpec((1, HEADS, L, LANES), lambda *g: (index(*g), 0, 0, 0)),
            pl.BlockSpec((1, HEADS, nk, V_ROWS, TK), lambda *g: (index(*g), 0, 0, 0, 0))]


def _head_scratch():
    return [pltpu.VMEM((HEADS, 1, TQ), F32), pltpu.VMEM((HEADS, V_ROWS, TQ), F32)]


def _fox_kernel(qT_ref, own_ref, k_ref, vT_ref, o_ref, g_sc, acc_sc):
    i = pl.program_id(1)

    def bias_fn(rows, diag):
        return jnp.where(_causal((TK, TQ)), 0.0, NEG_INF) if diag else None

    _attend_heads(i, bias_fn, lambda h: own_ref[0, h], qT_ref, k_ref, vT_ref, g_sc, acc_sc, o_ref)


def _fox_attention(qT, own, ka, vT):
    B, _, _, L = qT.shape
    return pl.pallas_call(
        _fox_kernel,
        grid=(B, L // TQ),
        in_specs=[pl.BlockSpec((1, HEADS, LANES, TQ), lambda b, i: (b, 0, 0, i)),
                  pl.BlockSpec((1, HEADS, 1, TQ), lambda b, i: (b, 0, 0, i))] + _head_specs(L, lambda b, i: b),
        out_specs=pl.BlockSpec((1, TQ, PAIRS * LANES), lambda b, i: (b, i, 0)),
        out_shape=jax.ShapeDtypeStruct((B, L, PAIRS * LANES), BF16),
        scratch_shapes=_head_scratch(),
        compiler_params=_params(("parallel", "arbitrary")),
    )(qT, own, ka, vT)


def _count_rows(mask):
    return _fold8(jnp.where(mask, 1, 0).astype(I32), jnp.sum)


def _dsa_kernel(qT_ref, k_ref, vT_ref, iqT_ref, iwT_ref, s_ref, o_ref,
                keys_sc, hi_sc, lo_sc, g_sc, acc_sc):
    i = pl.program_id(1)
    n_chunks = i + 1

    def _select():
        wts = iwT_ref[0]

        def score_chunk(j, masked):
            ks = pl.multiple_of(j * TK, TK)
            ik = s_ref[0, pl.ds(ks, TK), :][:, _SM_IK:_SM_IK + HEAD_DIM].astype(BF16)
            acc = jnp.zeros((TK, TQ), F32)
            for h in range(HEADS):
                iq = iqT_ref[0, h // 2, (h % 2) * HEAD_DIM:(h % 2 + 1) * HEAD_DIM, :]
                rel = jnp.maximum(jnp.dot(ik, iq, preferred_element_type=F32), 0.0)
                acc = acc + wts[h:h + 1, :] * rel
            if masked:
                acc = jnp.where(_causal(acc.shape), acc, NEG_INF)
            bits = pltpu.bitcast(acc, I32)
            sign = lax.shift_right_arithmetic(bits, 31)
            key = (bits ^ (sign & 0x7FFFFFFF)) - sign
            keys_sc[pl.ds(ks, TK), :] = key
            hi_sc[pl.ds(ks, TK), :] = lax.shift_right_arithmetic(key, 16).astype(I16)
            lo_sc[pl.ds(ks, TK), :] = key.astype(I16) ^ jnp.full((), -(2 ** 15), I16)

        def score_body(j, carry):
            score_chunk(j, False)
            return carry

        lax.fori_loop(0, i, score_body, 0)
        score_chunk(i, True)

        def count(pred):
            def body(j, cnt):
                ks = pl.multiple_of(j * TK, TK)
                return cnt + _count_rows(pred(keys_sc[pl.ds(ks, TK), :]))
            cnt8 = lax.fori_loop(0, n_chunks, body, jnp.zeros((8, TQ), I32))
            return jnp.sum(cnt8, axis=0, keepdims=True)

        def count16(ref, pred):
            one, nil = jnp.ones((), I16), jnp.zeros((), I16)

            def body(j, cnt):
                ks = pl.multiple_of(j * TK, TK)
                hit = jnp.where(pred(ref[pl.ds(ks, TK), :]), one, nil)
                for r in range(TK // 16):
                    cnt = cnt + hit[r * 16:(r + 1) * 16, :]
                return cnt
            cnt16 = lax.fori_loop(0, n_chunks, body, jnp.zeros((16, TQ), I16))
            return jnp.sum(cnt16.astype(I32), axis=0, keepdims=True)

        def select16(ref, k):
            zero = jnp.zeros((1, TQ), I16)
            c0 = count16(ref, lambda v: v >= zero)
            ok0 = c0 >= k
            init = (jnp.where(ok0, 0, -(2 ** 15)).astype(I32),
                    jnp.where(ok0, c0, n_chunks * TK),
                    jnp.where(ok0, 0, c0))

            def bit_body(bi, carry):
                t, n_ge, n_gt = carry
                cand = t | jnp.left_shift(jnp.ones((1, TQ), I32), 14 - bi)
                cand16 = cand.astype(I16)
                c = count16(ref, lambda v: v >= cand16)
                ok = c >= k
                return jnp.where(ok, cand, t), jnp.where(ok, c, n_ge), jnp.where(ok, n_gt, c)

            return lax.fori_loop(0, 15, bit_body, init)

        t_hi, _, n_above = select16(hi_sc, TOPK)
        t_hi16 = t_hi.astype(I16)
        k_lo = TOPK - n_above

        def restrict_body(j, carry):
            ks = pl.multiple_of(j * TK, TK)
            in_bucket = hi_sc[pl.ds(ks, TK), :] == t_hi16
            lo_sc[pl.ds(ks, TK), :] = jnp.where(in_bucket, lo_sc[pl.ds(ks, TK), :], jnp.full((), -(2 ** 15), I16))
            return carry

        lax.fori_loop(0, n_chunks, restrict_body, 0)
        t_lo, n_ge_lo, _ = select16(lo_sc, k_lo)
        thr = jnp.left_shift(t_hi, 16) | (t_lo + 2 ** 15)

        excess = jnp.where((n_above + n_ge_lo > TOPK) & (thr > NEG_INF_KEY), 1, 0)

        @pl.when(jnp.max(excess) > 0)
        def _break_ties():
            need = TOPK - count(lambda kk: kk > thr)
            r = lax.broadcasted_iota(I32, (TK, TK), 0)
            c = lax.broadcasted_iota(I32, (TK, TK), 1)
            strict_lower = jnp.where(c < r, 1.0, 0.0).astype(BF16)

            def body(j, seen):
                ks = pl.multiple_of(j * TK, TK)
                kk = keys_sc[pl.ds(ks, TK), :]
                tie = kk == thr
                tie_f = jnp.where(tie, 1.0, 0.0)
                before = jnp.dot(strict_lower, tie_f.astype(BF16), preferred_element_type=F32) + seen
                drop = tie & (before >= need.astype(F32))
                keys_sc[pl.ds(ks, TK), :] = jnp.where(drop, INT_MIN, kk)
                return seen + jnp.sum(tie_f, axis=0, keepdims=True)

            lax.fori_loop(0, n_chunks, body, jnp.zeros((1, TQ), F32))

        def bias_chunk(j, masked, near8):
            ks = pl.multiple_of(j * TK, TK)
            bias = jnp.where(keys_sc[pl.ds(ks, TK), :] >= thr, 0.0, NEG_INF)
            if masked:
                bias = jnp.where(_causal(bias.shape), bias, NEG_INF)
            keys_sc[pl.ds(ks, TK), :] = pltpu.bitcast(bias, I32)
            for blk in range(TK // NEAR_BLOCK):
                any_selected = _fold8(bias[blk * NEAR_BLOCK:(blk + 1) * NEAR_BLOCK], jnp.max) == 0.0
                near8 = jnp.maximum(near8, jnp.where(any_selected, ks + (blk + 1) * NEAR_BLOCK - 1, -1))
            return near8

        near8 = lax.fori_loop(0, i, lambda j, n8: bias_chunk(j, False, n8), jnp.full((8, TQ), -1, I32))
        return jnp.max(bias_chunk(i, True, near8), axis=0, keepdims=True)

    nearest = _select()

    def stored_bias(rows, diag):
        return pltpu.bitcast(keys_sc[rows, :], F32)

    def nearest_key_bias(h):
        query = i * TQ + lax.broadcasted_iota(I32, (1, TQ), 1)
        return (query - nearest).astype(F32) * (-(2.0 ** -(h + 1)) * LOG2E)

    _attend_heads(i, stored_bias, nearest_key_bias, qT_ref, k_ref, vT_ref, g_sc, acc_sc, o_ref)


def _dsa_attention(qT, ka, vT, iqT, iwT, small):
    B, _, _, L = qT.shape
    return pl.pallas_call(
        _dsa_kernel,
        grid=(B, L // TQ),
        in_specs=[pl.BlockSpec((1, HEADS, LANES, TQ), lambda b, i: (b, 0, 0, i))] + _head_specs(L, lambda b, i: b)
        + [pl.BlockSpec((1, PAIRS, LANES, TQ), lambda b, i: (b, 0, 0, i)),
           pl.BlockSpec((1, HEADS, TQ), lambda b, i: (b, 0, i)),
           pl.BlockSpec((1, L, LANES), lambda b, i: (b, 0, 0))],
        out_specs=pl.BlockSpec((1, TQ, PAIRS * LANES), lambda b, i: (b, i, 0)),
        out_shape=jax.ShapeDtypeStruct((B, L, PAIRS * LANES), BF16),
        scratch_shapes=[pltpu.VMEM((L, TQ), I32), pltpu.VMEM((L, TQ), I16), pltpu.VMEM((L, TQ), I16)]
        + _head_scratch(),
        compiler_params=_params(("parallel", "arbitrary")),
    )(qT, ka, vT, iqT, iwT, small)


def _layer_norm(z, g, b):
    mu = jnp.mean(z, axis=-1, keepdims=True)
    zc = z - mu
    var = jnp.mean(zc * zc, axis=-1, keepdims=True)
    return zc * lax.rsqrt(var + LN_EPS) * g + b


def _merge_kernel(x_ref, oa_ref, ob_ref, ga_ref, gb_ref, wa_ref, wb_ref, wo_ref, g_ref, b_ref,
                  out_ref, *, alpha):
    a = jnp.dot(oa_ref[0], wa_ref[...], preferred_element_type=F32)
    b = jnp.dot(ob_ref[0], wb_ref[...], preferred_element_type=F32)
    n_gate_blocks = D_MODEL // LANES
    gate_a = jnp.concatenate([ga_ref[0, c] for c in range(n_gate_blocks)], axis=1).astype(F32)
    gate_b = jnp.concatenate([gb_ref[0, c] for c in range(n_gate_blocks)], axis=1).astype(F32)
    merged = jax.nn.sigmoid(gate_a) * a + jax.nn.sigmoid(gate_b) * b
    y = jnp.dot(merged.astype(BF16), wo_ref[...], preferred_element_type=F32)
    out_ref[0] = _layer_norm(alpha * x_ref[0] + y, g_ref[...], b_ref[...])


def _merge(x, oa, ob, gates, wa, wb, wo, g, b, layer, alpha):
    B, L, D = x.shape
    tm = ROW_TILE
    nb = D // LANES
    row = lambda b_, i: (b_, i, 0)
    return pl.pallas_call(
        functools.partial(_merge_kernel, alpha=alpha),
        grid=(B, L // tm),
        in_specs=[pl.BlockSpec((1, tm, D), row),
                  pl.BlockSpec((1, tm, PAIRS * LANES), row),
                  pl.BlockSpec((1, tm, PAIRS * LANES), row),
                  pl.BlockSpec((1, nb, tm, LANES), lambda b_, i: (b_, _CB_GA // nb, i, 0)),
                  pl.BlockSpec((1, nb, tm, LANES), lambda b_, i: (b_, _CB_GB // nb, i, 0)),
                  _layer_spec(wa, layer), _layer_spec(wb, layer), _layer_spec(wo, layer),
                  _layer_spec(g, layer), _layer_spec(b, layer)],
        out_specs=pl.BlockSpec((1, tm, D), row),
        out_shape=jax.ShapeDtypeStruct((B, L, D), F32),
        compiler_params=_params(("parallel", "parallel")),
    )(x, oa, ob, gates, gates, wa, wb, wo, g, b)


FF_CHUNK = 256


def _ffn_kernel(x_ref, wi_ref, wo_ref, g_ref, b_ref, out_ref, h_sc, *, alpha):
    x = x_ref[0]
    xb = x.astype(BF16)
    for c in range(D_FF // FF_CHUNK):
        lo = c * FF_CHUNK
        gate = jnp.dot(xb, wi_ref[:, lo:lo + FF_CHUNK], preferred_element_type=F32)
        up = jnp.dot(xb, wi_ref[:, D_FF + lo:D_FF + lo + FF_CHUNK], preferred_element_type=F32)
        h_sc[:, lo:lo + FF_CHUNK] = (jax.nn.silu(gate) * up).astype(BF16)
    y = jnp.dot(h_sc[...], wo_ref[...], preferred_element_type=F32)
    out_ref[0] = _layer_norm(alpha * x + y, g_ref[...], b_ref[...])


def _ffn(x, wi, wo, g, b, layer, alpha):
    B, L, D = x.shape
    tm = ROW_TILE
    row = lambda b_, i: (b_, i, 0)
    return pl.pallas_call(
        functools.partial(_ffn_kernel, alpha=alpha),
        grid=(B, L // tm),
        in_specs=[pl.BlockSpec((1, tm, D), row),
                  _layer_spec(wi, layer), _layer_spec(wo, layer),
                  _layer_spec(g, layer), _layer_spec(b, layer)],
        out_specs=pl.BlockSpec((1, tm, D), row),
        out_shape=jax.ShapeDtypeStruct((B, L, D), F32),
        scratch_shapes=[pltpu.VMEM((tm, D_FF), BF16)],
        compiler_params=_params(("parallel", "parallel")),
    )(x, wi, wo, g, b)


def _split_w_in(w):
    fq_fk = w[..., 0:2 * GROUP]
    fv = w[..., 2 * GROUP:_OFF_FLOGIT]
    dsa_iq = w[..., _OFF_DQ:_OFF_IK]
    gates = w[..., _OFF_GA:_N_IN]
    w_main = jnp.concatenate([fq_fk, gates, fv, dsa_iq], axis=-1).astype(BF16)
    pad = jnp.zeros(w.shape[:-1] + (LANES - HEAD_DIM - 2 * HEADS,), w.dtype)
    w_small = jnp.concatenate([w[..., _OFF_IK:_OFF_IW], w[..., _OFF_IW:_OFF_GA],
                               w[..., _OFF_FLOGIT:_OFF_DQ], pad], axis=-1).astype(BF16)
    return w_main, w_small


def kernel(x, w_in, b_forget, w_branch_a, w_branch_b, w_out, ln1_g, ln1_b, w_ffn_in, w_ffn_out, ln2_g, ln2_b):
    depth = w_in.shape[0]
    alpha = (2.0 * depth) ** 0.25
    w_main, w_small = _split_w_in(w_in)
    bias_rows = jnp.zeros((depth, 1, LANES), F32).at[:, 0, _SM_FL:_SM_FL + HEADS].set(b_forget)
    wa, wb, wo = w_branch_a.astype(BF16), w_branch_b.astype(BF16), w_out.astype(BF16)
    wi, wf = w_ffn_in.astype(BF16), w_ffn_out.astype(BF16)
    row = lambda p: p[:, None, :]
    for l in range(depth):
        gates, small, fqT, fka, f_own, fvT, dqT, dka, dvT, iqT, iwT = _project(x, w_main, w_small, bias_rows, l)
        o_a = _fox_attention(fqT, f_own, fka, fvT)
        o_b = _dsa_attention(dqT, dka, dvT, iqT, iwT, small)
        x = _merge(x, o_a, o_b, gates, wa, wb, wo, row(ln1_g), row(ln1_b), l, alpha)
        x = _ffn(x, wi, wf, row(ln2_g), row(ln2_b), l, alpha)
    return x
```

```python
import functools

import numpy as np
import jax
import jax.numpy as jnp
from jax import lax
from jax.experimental import pallas as pl
from jax.experimental.pallas import tpu as pltpu

F32 = jnp.float32
BF16 = jnp.bfloat16
I32 = jnp.int32
I16 = jnp.int16

D_MODEL = 1024
HEADS = 8
HEAD_DIM = 64
PAIRS = HEADS // 2
D_FF = 2816
TOPK = 256
LANES = 128
NEG_INF = -1e30
LN_EPS = 1e-5
ATTN_SCALE = HEAD_DIM ** -0.5
IDX_SCALE = (HEADS ** -0.5) * (HEAD_DIM ** -0.5)

_OFF_FLOGIT = 1536
_OFF_DQ = 1544
_OFF_IK = 3592
_OFF_IW = 3656
_OFF_GA = 3664
_N_IN = 5712

_G_FQ, _G_FK, _G_GATES, _G_FV, _G_DQ, _G_DK, _G_DV, _G_IQ = 0, 1, (2, 3, 4, 5), 6, 7, 8, 9, 10
_CB_GA, _CB_GB = 0, 8
N_GATE_BLOCKS = 16
GROUP = 4 * LANES
N_MAIN = 11 * GROUP
_SM_IK, _SM_IW, _SM_FL = 0, 64, 72

ROW_TILE = 512
TQ = 512
TK = 512
CUMSUM_BLOCK = 256
V_ROWS = HEAD_DIM + 16
LOG2E = 1.4426950408889634

INT_MIN = -(2 ** 31)


def _monotone_key_of(value):
    b = int(np.float32(value).view(np.int32))
    return (b ^ ((b >> 31) & 0x7FFFFFFF)) - (b >> 31)


NEG_INF_KEY = _monotone_key_of(NEG_INF)
VMEM_LIMIT = 60 * 1024 * 1024


def _params(sem):
    return pltpu.CompilerParams(dimension_semantics=sem, vmem_limit_bytes=VMEM_LIMIT)


def _const_spec(shape):
    nd = len(shape)
    return pl.BlockSpec(shape, lambda *_: (0,) * nd, pipeline_mode=pl.Buffered(1))


def _layer_spec(stacked, layer):
    tail = stacked.shape[1:]
    return pl.BlockSpec((None,) + tail, lambda *_: (layer,) + (0,) * len(tail), pipeline_mode=pl.Buffered(1))


def _lane_row(lo, hi, value=1.0):
    lane = lax.broadcasted_iota(I32, (1, LANES), 1)
    return jnp.where((lane >= lo) & (lane < hi), value, 0.0)


BIAS_COLS = 8


def _alibi_placement():
    put = np.zeros((LANES, 2 * LANES), np.float32)
    for h in range(HEADS):
        slope = 2.0 ** -(h + 1)
        qc, kc = BIAS_COLS * h, LANES + BIAS_COLS * h
        put[0, qc], put[1, qc + 1], put[2, qc + 2], put[2, qc + 3] = -64.0 * slope, -slope, 64.0 * slope, slope
        put[2, kc], put[2, kc + 1], put[0, kc + 2], put[1, kc + 3] = 1.0, 1.0, 1.0, 1.0
    return jnp.asarray(put, BF16)


def _forget_placement():
    put = np.zeros((LANES, 2 * LANES), np.float32)
    for h in range(HEADS):
        for part in range(3):
            src = _SM_FL + part * HEADS + h
            put[src, BIAS_COLS * h + part] = 1.0
            put[src, LANES + BIAS_COLS * h + 3 + part] = -1.0
    return jnp.asarray(put, BF16)


def _proj_kernel(x_ref, wm_ref, ws_ref, alibi_ref, forget_ref, bias_ref,
                 gates_ref, os_ref, fqT_ref, fka_ref, own_ref, fvT_ref, dqT_ref, dka_ref, dvT_ref, iqT_ref, iwT_ref,
                 carry_sc):
    tm = x_ref.shape[1]
    xb = x_ref[0].astype(BF16)

    def group(g):
        return jnp.dot(xb, wm_ref[:, g * GROUP:(g + 1) * GROUP], preferred_element_type=F32)

    def store_gates(n):
        r = group(_G_GATES[n]).astype(BF16)
        for c in range(4):
            gates_ref[0, 4 * n + c] = r[:, c * LANES:(c + 1) * LANES]

    def pair_and_half(h):
        p, e = divmod(h, 2)
        return slice(p * LANES, (p + 1) * LANES), _lane_row(e * HEAD_DIM, (e + 1) * HEAD_DIM), (1 - e) * HEAD_DIM

    def bias_columns(compact, h, first, n):
        base = (1 - h % 2) * HEAD_DIM
        moved = pltpu.roll(compact, (base - BIAS_COLS * h) % LANES, 1)
        return moved * _lane_row(base + first, base + first + n)

    @pl.when(pl.program_id(1) == 0)
    def _new_sequence():
        carry_sc[...] = jnp.zeros(carry_sc.shape, F32)

    store_gates(0)
    small = jnp.dot(xb, ws_ref[...], preferred_element_type=F32)
    os_ref[0] = small
    iwT_ref[0] = small.T[_SM_IW:_SM_IW + HEADS, :] * IDX_SCALE
    c = _cumsum_rows(jax.nn.log_sigmoid(small + bias_ref[...])) + carry_sc[...]
    carry_sc[...] = c[tm - 1:tm, :]
    hi, mid, lo = _split3(c)
    lane = lax.broadcasted_iota(I32, (tm, LANES), 1)
    parts = jnp.where(lane < _SM_FL + HEADS, hi,
                      jnp.where(lane < _SM_FL + 2 * HEADS, pltpu.roll(mid, HEADS, 1), pltpu.roll(lo, 2 * HEADS, 1)))
    store_gates(1)
    aug = jnp.dot(parts.astype(BF16), forget_ref[...], preferred_element_type=F32)
    q = group(_G_FQ) * ATTN_SCALE
    k = group(_G_FK)
    for h in range(HEADS):
        pair, data, base = pair_and_half(h)
        q_aug = q[:, pair] * data + bias_columns(aug[:, :LANES], h, 0, 3) + _lane_row(base + 3, base + 6)
        k_aug = k[:, pair] * data + bias_columns(aug[:, LANES:], h, 3, 3) + _lane_row(base, base + 3)
        fqT_ref[0, h] = q_aug.T.astype(BF16)
        fka_ref[0, h] = k_aug.astype(BF16)
        own_ref[0, h] = jnp.sum((q_aug * k_aug).T, axis=0, keepdims=True) * LOG2E

    def store_values(ref, r):
        ones = jnp.ones((V_ROWS - HEAD_DIM, tm), BF16)
        for p in range(PAIRS):
            vT = r[:, p * LANES:(p + 1) * LANES].T.astype(BF16)
            for e in range(2):
                ref[0, 2 * p + e, 0] = jnp.concatenate([vT[e * HEAD_DIM:(e + 1) * HEAD_DIM], ones], axis=0)

    store_values(fvT_ref, group(_G_FV))
    store_gates(2)
    store_values(dvT_ref, group(_G_DV))

    lane = lax.broadcasted_iota(I32, (tm, LANES), 1)
    pos = lax.broadcasted_iota(I32, (tm, LANES), 0) + pl.program_id(1) * tm
    pos_terms = jnp.where(lane == 0, lax.shift_right_logical(pos, 6),
                          jnp.where(lane == 1, pos & 63, jnp.where(lane == 2, 1, 0))).astype(F32).astype(BF16)
    aug = jnp.dot(pos_terms, alibi_ref[...], preferred_element_type=F32)
    q = group(_G_DQ) * ATTN_SCALE
    k = group(_G_DK)
    for h in range(HEADS):
        pair, data, _ = pair_and_half(h)
        q_aug = q[:, pair] * data + bias_columns(aug[:, :LANES], h, 0, 4)
        k_aug = k[:, pair] * data + bias_columns(aug[:, LANES:], h, 0, 4)
        dqT_ref[0, h] = q_aug.T.astype(BF16)
        dka_ref[0, h] = k_aug.astype(BF16)
    iq = group(_G_IQ)
    for p in range(PAIRS):
        iqT_ref[0, p] = iq[:, p * LANES:(p + 1) * LANES].T.astype(BF16)
    store_gates(3)


def _project(x, w_main, w_small, bias_row, layer):
    B, L, D = x.shape
    tm = TK
    nk = L // TK
    alibi, forget = _alibi_placement(), _forget_placement()
    q_t = pl.BlockSpec((1, HEADS, LANES, tm), lambda b, i: (b, 0, 0, i))
    k_aug = pl.BlockSpec((1, HEADS, tm, LANES), lambda b, i: (b, 0, i, 0))
    v_t = pl.BlockSpec((1, HEADS, 1, V_ROWS, TK), lambda b, i: (b, 0, i, 0, 0))
    q_t_shape = jax.ShapeDtypeStruct((B, HEADS, LANES, L), BF16)
    k_aug_shape = jax.ShapeDtypeStruct((B, HEADS, L, LANES), BF16)
    v_t_shape = jax.ShapeDtypeStruct((B, HEADS, nk, V_ROWS, TK), BF16)
    return pl.pallas_call(
        _proj_kernel,
        grid=(B, nk),
        in_specs=[pl.BlockSpec((1, tm, D), lambda b, i: (b, i, 0)),
                  _layer_spec(w_main, layer), _layer_spec(w_small, layer),
                  _const_spec(alibi.shape), _const_spec(forget.shape), _layer_spec(bias_row, layer)],
        out_specs=[pl.BlockSpec((1, N_GATE_BLOCKS, tm, LANES), lambda b, i: (b, 0, i, 0)),
                   pl.BlockSpec((1, tm, LANES), lambda b, i: (b, i, 0)),
                   q_t, k_aug, pl.BlockSpec((1, HEADS, 1, tm), lambda b, i: (b, 0, 0, i)), v_t,
                   q_t, k_aug, v_t,
                   pl.BlockSpec((1, PAIRS, LANES, tm), lambda b, i: (b, 0, 0, i)),
                   pl.BlockSpec((1, HEADS, tm), lambda b, i: (b, 0, i))],
        out_shape=[jax.ShapeDtypeStruct((B, N_GATE_BLOCKS, L, LANES), BF16),
                   jax.ShapeDtypeStruct((B, L, LANES), F32),
                   q_t_shape, k_aug_shape, jax.ShapeDtypeStruct((B, HEADS, 1, L), F32), v_t_shape,
                   q_t_shape, k_aug_shape, v_t_shape,
                   jax.ShapeDtypeStruct((B, PAIRS, LANES, L), BF16),
                   jax.ShapeDtypeStruct((B, HEADS, L), F32)],
        scratch_shapes=[pltpu.VMEM((1, LANES), F32)],
        compiler_params=_params(("parallel", "arbitrary")),
    )(x, w_main, w_small, alibi, forget, bias_row)


def _split3(v):
    hi = v.astype(BF16).astype(F32)
    r = v - hi
    mid = r.astype(BF16).astype(F32)
    lo = (r - mid).astype(BF16).astype(F32)
    return hi, mid, lo


def _cumsum_rows(x):
    n = CUMSUM_BLOCK
    L = x.shape[0]
    r = lax.broadcasted_iota(I32, (n, n), 0)
    c = lax.broadcasted_iota(I32, (n, n), 1)
    tri = jnp.where(c <= r, 1.0, 0.0).astype(BF16)
    carry = jnp.zeros((1, x.shape[1]), F32)
    outs = []
    w = x.shape[1]
    for blk in range(L // n):
        parts = jnp.concatenate([p.astype(BF16) for p in _split3(x[blk * n:(blk + 1) * n])], axis=1)
        y3 = jnp.dot(tri, parts, preferred_element_type=F32)
        y = (y3[:, :w] + y3[:, w:2 * w] + y3[:, 2 * w:]) + carry
        carry = y[n - 1:n, :]
        outs.append(y)
    return jnp.concatenate(outs, axis=0)


def _fold8(x, op):
    return op(x.reshape(x.shape[0] // 8, 8, x.shape[1]), axis=0)


def _causal(shape):
    return lax.broadcasted_iota(I32, shape, 0) <= lax.broadcasted_iota(I32, shape, 1)


L_SUM_MIN, L_SUM_MAX = 1e-30, 1e30
NEAR_BLOCK = 64


def _attend_heads(n_full, bias_fn, ref_score_fn, qT_ref, k_ref, vT_ref, g_sc, acc_sc, o_ref):
    def for_chunks(chunk):
        def body(j, carry):
            chunk(j, False)
            return carry

        lax.fori_loop(0, n_full, body, 0)
        chunk(n_full, True)

    def scores(h, rows, diag, raw):
        s = raw * LOG2E
        bias = bias_fn(rows, diag)
        return s if bias is None else s + bias

    def sweep():
        acc_sc[...] = jnp.zeros(acc_sc.shape, F32)

        def chunk(j, diag):
            rows = pl.ds(pl.multiple_of(j * TK, TK), TK)

            def raw_scores(h):
                return jnp.dot(k_ref[0, h, rows, :], qT_ref[0, h], preferred_element_type=F32)

            nxt = raw_scores(0)
            for h in range(HEADS):
                raw = nxt
                if h + 1 < HEADS:
                    nxt = raw_scores(h + 1)
                p = jnp.exp2(scores(h, rows, diag, raw) - g_sc[h]).astype(BF16)
                acc_sc[h] += jnp.dot(vT_ref[0, h, j], p, preferred_element_type=F32)

        for_chunks(chunk)
        bad = jnp.zeros((1, TQ), I32)
        for p in range(PAIRS):
            halves = []
            for e in range(2):
                acc = acc_sc[2 * p + e]
                l = acc[HEAD_DIM:HEAD_DIM + 1]
                bad = jnp.maximum(bad, jnp.where((l > L_SUM_MIN) & (l < L_SUM_MAX), 0, 1))
                halves.append(acc[:HEAD_DIM] / l)
            o_ref[0, :, p * LANES:(p + 1) * LANES] = jnp.concatenate(halves, axis=0).T.astype(o_ref.dtype)
        return jnp.max(bad) > 0

    def exact_row_maxima():
        acc_sc[:, 0:8, :] = jnp.full((HEADS, 8, TQ), NEG_INF, F32)

        def chunk(j, diag):
            rows = pl.ds(pl.multiple_of(j * TK, TK), TK)
            for h in range(HEADS):
                raw = jnp.dot(k_ref[0, h, rows, :], qT_ref[0, h], preferred_element_type=F32)
                acc_sc[h, 0:8, :] = jnp.maximum(acc_sc[h, 0:8, :], _fold8(scores(h, rows, diag, raw), jnp.max))

        for_chunks(chunk)
        for h in range(HEADS):
            g_sc[h] = jnp.max(acc_sc[h, 0:8, :], axis=0, keepdims=True)

    for h in range(HEADS):
        g_sc[h] = ref_score_fn(h)
    unsafe = sweep()

    @pl.when(unsafe)
    def _redo_with_exact_maxima():
        exact_row_maxima()
        sweep()


def _head_specs(L, index):
    nk = L // TK
    return [pl.BlockSpec((1, HEADS, L, LANES), lambda *g: (index(*g), 0, 0, 0)),
            pl.BlockSpec((1, HEADS, nk, V_ROWS, TK), lambda *g: (index(*g), 0, 0, 0, 0))]


def _head_scratch():
    return [pltpu.VMEM((HEADS, 1, TQ), F32), pltpu.VMEM((HEADS, V_ROWS, TQ), F32)]


def _fox_kernel(qT_ref, own_ref, k_ref, vT_ref, o_ref, g_sc, acc_sc):
    i = pl.program_id(1)

    def bias_fn(rows, diag):
        return jnp.where(_causal((TK, TQ)), 0.0, NEG_INF) if diag else None

    _attend_heads(i, bias_fn, lambda h: own_ref[0, h], qT_ref, k_ref, vT_ref, g_sc, acc_sc, o_ref)


def _fox_attention(qT, own, ka, vT):
    B, _, _, L = qT.shape
    return pl.pallas_call(
        _fox_kernel,
        grid=(B, L // TQ),
        in_specs=[pl.BlockSpec((1, HEADS, LANES, TQ), lambda b, i: (b, 0, 0, i)),
                  pl.BlockSpec((1, HEADS, 1, TQ), lambda b, i: (b, 0, 0, i))] + _head_specs(L, lambda b, i: b),
        out_specs=pl.BlockSpec((1, TQ, PAIRS * LANES), lambda b, i: (b, i, 0)),
        out_shape=jax.ShapeDtypeStruct((B, L, PAIRS * LANES), BF16),
        scratch_shapes=_head_scratch(),
        compiler_params=_params(("parallel", "arbitrary")),
    )(qT, own, ka, vT)


def _count_rows(mask):
    return _fold8(jnp.where(mask, 1, 0).astype(I32), jnp.sum)


def _dsa_kernel(qT_ref, k_ref, vT_ref, iqT_ref, iwT_ref, s_ref, o_ref,
                keys_sc, hi_sc, lo_sc, g_sc, acc_sc):
    i = pl.program_id(1)
    n_chunks = i + 1

    def _select():
        wts = iwT_ref[0]

        def score_chunk(j, masked):
            ks = pl.multiple_of(j * TK, TK)
            ik = s_ref[0, pl.ds(ks, TK), :][:, _SM_IK:_SM_IK + HEAD_DIM].astype(BF16)
            acc = jnp.zeros((TK, TQ), F32)
            for h in range(HEADS):
                iq = iqT_ref[0, h // 2, (h % 2) * HEAD_DIM:(h % 2 + 1) * HEAD_DIM, :]
                rel = jnp.maximum(jnp.dot(ik, iq, preferred_element_type=F32), 0.0)
                acc = acc + wts[h:h + 1, :] * rel
            if masked:
                acc = jnp.where(_causal(acc.shape), acc, NEG_INF)
            bits = pltpu.bitcast(acc, I32)
            sign = lax.shift_right_arithmetic(bits, 31)
            key = (bits ^ (sign & 0x7FFFFFFF)) - sign
            keys_sc[pl.ds(ks, TK), :] = key
            hi_sc[pl.ds(ks, TK), :] = lax.shift_right_arithmetic(key, 16).astype(I16)
            lo_sc[pl.ds(ks, TK), :] = key.astype(I16) ^ jnp.full((), -(2 ** 15), I16)

        def score_body(j, carry):
            score_chunk(j, False)
            return carry

        lax.fori_loop(0, i, score_body, 0)
        score_chunk(i, True)

        def count(pred):
            def body(j, cnt):
                ks = pl.multiple_of(j * TK, TK)
                return cnt + _count_rows(pred(keys_sc[pl.ds(ks, TK), :]))
            cnt8 = lax.fori_loop(0, n_chunks, body, jnp.zeros((8, TQ), I32))
            return jnp.sum(cnt8, axis=0, keepdims=True)

        def count16(ref, pred):
            one, nil = jnp.ones((), I16), jnp.zeros((), I16)

            def body(j, cnt):
                ks = pl.multiple_of(j * TK, TK)
                hit = jnp.where(pred(ref[pl.ds(ks, TK), :]), one, nil)
                for r in range(TK // 16):
                    cnt = cnt + hit[r * 16:(r + 1) * 16, :]
                return cnt
            cnt16 = lax.fori_loop(0, n_chunks, body, jnp.zeros((16, TQ), I16))
            return jnp.sum(cnt16.astype(I32), axis=0, keepdims=True)

        def select16(ref, k):
            zero = jnp.zeros((1, TQ), I16)
            c0 = count16(ref, lambda v: v >= zero)
            ok0 = c0 >= k
            init = (jnp.where(ok0, 0, -(2 ** 15)).astype(I32),
                    jnp.where(ok0, c0, n_chunks * TK),
                    jnp.where(ok0, 0, c0))

            def bit_body(bi, carry):
                t, n_ge, n_gt = carry
                cand = t | jnp.left_shift(jnp.ones((1, TQ), I32), 14 - bi)
                cand16 = cand.astype(I16)
                c = count16(ref, lambda v: v >= cand16)
                ok = c >= k
                return jnp.where(ok, cand, t), jnp.where(ok, c, n_ge), jnp.where(ok, n_gt, c)

            return lax.fori_loop(0, 15, bit_body, init)

        t_hi, _, n_above = select16(hi_sc, TOPK)
        t_hi16 = t_hi.astype(I16)
        k_lo = TOPK - n_above

        def restrict_body(j, carry):
            ks = pl.multiple_of(j * TK, TK)
            in_bucket = hi_sc[pl.ds(ks, TK), :] == t_hi16
            lo_sc[pl.ds(ks, TK), :] = jnp.where(in_bucket, lo_sc[pl.ds(ks, TK), :], jnp.full((), -(2 ** 15), I16))
            return carry

        lax.fori_loop(0, n_chunks, restrict_body, 0)
        t_lo, n_ge_lo, _ = select16(lo_sc, k_lo)
        thr = jnp.left_shift(t_hi, 16) | (t_lo + 2 ** 15)

        excess = jnp.where((n_above + n_ge_lo > TOPK) & (thr > NEG_INF_KEY), 1, 0)

        @pl.when(jnp.max(excess) > 0)
        def _break_ties():
            need = TOPK - count(lambda kk: kk > thr)
            r = lax.broadcasted_iota(I32, (TK, TK), 0)
            c = lax.broadcasted_iota(I32, (TK, TK), 1)
            strict_lower = jnp.where(c < r, 1.0, 0.0).astype(BF16)

            def body(j, seen):
                ks = pl.multiple_of(j * TK, TK)
                kk = keys_sc[pl.ds(ks, TK), :]
                tie = kk == thr
                tie_f = jnp.where(tie, 1.0, 0.0)
                before = jnp.dot(strict_lower, tie_f.astype(BF16), preferred_element_type=F32) + seen
                drop = tie & (before >= need.astype(F32))
                keys_sc[pl.ds(ks, TK), :] = jnp.where(drop, INT_MIN, kk)
                return seen + jnp.sum(tie_f, axis=0, keepdims=True)

            lax.fori_loop(0, n_chunks, body, jnp.zeros((1, TQ), F32))

        def bias_chunk(j, masked, near8):
            ks = pl.multiple_of(j * TK, TK)
            bias = jnp.where(keys_sc[pl.ds(ks, TK), :] >= thr, 0.0, NEG_INF)
            if masked:
                bias = jnp.where(_causal(bias.shape), bias, NEG_INF)
            keys_sc[pl.ds(ks, TK), :] = pltpu.bitcast(bias, I32)
            for blk in range(TK // NEAR_BLOCK):
                any_selected = _fold8(bias[blk * NEAR_BLOCK:(blk + 1) * NEAR_BLOCK], jnp.max) == 0.0
                near8 = jnp.maximum(near8, jnp.where(any_selected, ks + (blk + 1) * NEAR_BLOCK - 1, -1))
            return near8

        near8 = lax.fori_loop(0, i, lambda j, n8: bias_chunk(j, False, n8), jnp.full((8, TQ), -1, I32))
        return jnp.max(bias_chunk(i, True, near8), axis=0, keepdims=True)

    nearest = _select()

    def stored_bias(rows, diag):
        return pltpu.bitcast(keys_sc[rows, :], F32)

    def nearest_key_bias(h):
        query = i * TQ + lax.broadcasted_iota(I32, (1, TQ), 1)
        return (query - nearest).astype(F32) * (-(2.0 ** -(h + 1)) * LOG2E)

    _attend_heads(i, stored_bias, nearest_key_bias, qT_ref, k_ref, vT_ref, g_sc, acc_sc, o_ref)


def _dsa_attention(qT, ka, vT, iqT, iwT, small):
    B, _, _, L = qT.shape
    return pl.pallas_call(
        _dsa_kernel,
        grid=(B, L // TQ),
        in_specs=[pl.BlockSpec((1, HEADS, LANES, TQ), lambda b, i: (b, 0, 0, i))] + _head_specs(L, lambda b, i: b)
        + [pl.BlockSpec((1, PAIRS, LANES, TQ), lambda b, i: (b, 0, 0, i)),
           pl.BlockSpec((1, HEADS, TQ), lambda b, i: (b, 0, i)),
           pl.BlockSpec((1, L, LANES), lambda b, i: (b, 0, 0))],
        out_specs=pl.BlockSpec((1, TQ, PAIRS * LANES), lambda b, i: (b, i, 0)),
        out_shape=jax.ShapeDtypeStruct((B, L, PAIRS * LANES), BF16),
        scratch_shapes=[pltpu.VMEM((L, TQ), I32), pltpu.VMEM((L, TQ), I16), pltpu.VMEM((L, TQ), I16)]
        + _head_scratch(),
        compiler_params=_params(("parallel", "arbitrary")),
    )(qT, ka, vT, iqT, iwT, small)


def _layer_norm(z, g, b):
    mu = jnp.mean(z, axis=-1, keepdims=True)
    zc = z - mu
    var = jnp.mean(zc * zc, axis=-1, keepdims=True)
    return zc * lax.rsqrt(var + LN_EPS) * g + b


FF_CHUNK = 256


def _merge_ffn_kernel(x_ref, oa_ref, ob_ref, ga_ref, gb_ref, wa_ref, wb_ref, wo_ref, g1_ref, b1_ref,
                      wi_ref, wf_ref, g2_ref, b2_ref, out_ref, h_sc, *, alpha):
    a = jnp.dot(oa_ref[0], wa_ref[...], preferred_element_type=F32)
    b = jnp.dot(ob_ref[0], wb_ref[...], preferred_element_type=F32)
    n_gate_blocks = D_MODEL // LANES
    gate_a = jnp.concatenate([ga_ref[0, c] for c in range(n_gate_blocks)], axis=1).astype(F32)
    gate_b = jnp.concatenate([gb_ref[0, c] for c in range(n_gate_blocks)], axis=1).astype(F32)
    merged = jax.nn.sigmoid(gate_a) * a + jax.nn.sigmoid(gate_b) * b
    y = jnp.dot(merged.astype(BF16), wo_ref[...], preferred_element_type=F32)
    x = _layer_norm(alpha * x_ref[0] + y, g1_ref[...], b1_ref[...])
    xb = x.astype(BF16)
    for c in range(D_FF // FF_CHUNK):
        lo = c * FF_CHUNK
        gate = jnp.dot(xb, wi_ref[:, lo:lo + FF_CHUNK], preferred_element_type=F32)
        up = jnp.dot(xb, wi_ref[:, D_FF + lo:D_FF + lo + FF_CHUNK], preferred_element_type=F32)
        h_sc[:, lo:lo + FF_CHUNK] = (jax.nn.silu(gate) * up).astype(BF16)
    y = jnp.dot(h_sc[...], wf_ref[...], preferred_element_type=F32)
    out_ref[0] = _layer_norm(alpha * x + y, g2_ref[...], b2_ref[...])


def _merge_ffn(x, oa, ob, gates, stacked, layer, alpha):
    B, L, D = x.shape
    tm = ROW_TILE
    nb = D // LANES
    row = lambda b_, i: (b_, i, 0)
    return pl.pallas_call(
        functools.partial(_merge_ffn_kernel, alpha=alpha),
        grid=(B, L // tm),
        in_specs=[pl.BlockSpec((1, tm, D), row),
                  pl.BlockSpec((1, tm, PAIRS * LANES), row),
                  pl.BlockSpec((1, tm, PAIRS * LANES), row),
                  pl.BlockSpec((1, nb, tm, LANES), lambda b_, i: (b_, _CB_GA // nb, i, 0)),
                  pl.BlockSpec((1, nb, tm, LANES), lambda b_, i: (b_, _CB_GB // nb, i, 0))]
        + [_layer_spec(p, layer) for p in stacked],
        out_specs=pl.BlockSpec((1, tm, D), row),
        out_shape=jax.ShapeDtypeStruct((B, L, D), F32),
        scratch_shapes=[pltpu.VMEM((tm, D_FF), BF16)],
        compiler_params=_params(("parallel", "parallel")),
    )(x, oa, ob, gates, gates, *stacked)


def _split_w_in(w):
    fq_fk = w[..., 0:2 * GROUP]
    fv = w[..., 2 * GROUP:_OFF_FLOGIT]
    dsa_iq = w[..., _OFF_DQ:_OFF_IK]
    gates = w[..., _OFF_GA:_N_IN]
    w_main = jnp.concatenate([fq_fk, gates, fv, dsa_iq], axis=-1).astype(BF16)
    pad = jnp.zeros(w.shape[:-1] + (LANES - HEAD_DIM - 2 * HEADS,), w.dtype)
    w_small = jnp.concatenate([w[..., _OFF_IK:_OFF_IW], w[..., _OFF_IW:_OFF_GA],
                               w[..., _OFF_FLOGIT:_OFF_DQ], pad], axis=-1).astype(BF16)
    return w_main, w_small


def kernel(x, w_in, b_forget, w_branch_a, w_branch_b, w_out, ln1_g, ln1_b, w_ffn_in, w_ffn_out, ln2_g, ln2_b):
    depth = w_in.shape[0]
    alpha = (2.0 * depth) ** 0.25
    w_main, w_small = _split_w_in(w_in)
    bias_rows = jnp.zeros((depth, 1, LANES), F32).at[:, 0, _SM_FL:_SM_FL + HEADS].set(b_forget)
    wa, wb, wo = w_branch_a.astype(BF16), w_branch_b.astype(BF16), w_out.astype(BF16)
    wi, wf = w_ffn_in.astype(BF16), w_ffn_out.astype(BF16)
    row = lambda p: p[:, None, :]
    stacked = (wa, wb, wo, row(ln1_g), row(ln1_b), wi, wf, row(ln2_g), row(ln2_b))
    for l in range(depth):
        gates, small, fqT, fka, f_own, fvT, dqT, dka, dvT, iqT, iwT = _project(x, w_main, w_small, bias_rows, l)
        o_a = _fox_attention(fqT, f_own, fka, fvT)
        o_b = _dsa_attention(dqT, dka, dvT, iqT, iwT, small)
        x = _merge_ffn(x, o_a, o_b, gates, stacked, l, alpha)
    return x
```

```python
import functools

import numpy as np
import jax
import jax.numpy as jnp
from jax import lax
from jax.experimental import pallas as pl
from jax.experimental.pallas import tpu as pltpu

F32 = jnp.float32
BF16 = jnp.bfloat16
I32 = jnp.int32
I16 = jnp.int16

D_MODEL = 1024
HEADS = 8
HEAD_DIM = 64
PAIRS = HEADS // 2
D_FF = 2816
TOPK = 256
LANES = 128
NEG_INF = -1e30
LN_EPS = 1e-5
ATTN_SCALE = HEAD_DIM ** -0.5
IDX_SCALE = (HEADS ** -0.5) * (HEAD_DIM ** -0.5)

_OFF_FLOGIT = 1536
_OFF_DQ = 1544
_OFF_IK = 3592
_OFF_IW = 3656
_OFF_GA = 3664
_N_IN = 5712

_G_FQ, _G_FK, _G_GATES, _G_FV, _G_DQ, _G_DK, _G_DV, _G_IQ = 0, 1, (2, 3, 4, 5), 6, 7, 8, 9, 10
_CB_GA, _CB_GB = 0, 8
N_GATE_BLOCKS = 16
GROUP = 4 * LANES
N_MAIN = 11 * GROUP
_SM_IK, _SM_IW, _SM_FL = 0, 64, 72

ROW_TILE = 512
TQ = 512
TK = 512
CUMSUM_BLOCK = 256
V_ROWS = HEAD_DIM + 16
LOG2E = 1.4426950408889634

INT_MIN = -(2 ** 31)


def _monotone_key_of(value):
    b = int(np.float32(value).view(np.int32))
    return (b ^ ((b >> 31) & 0x7FFFFFFF)) - (b >> 31)


NEG_INF_KEY = _monotone_key_of(NEG_INF)
VMEM_LIMIT = 60 * 1024 * 1024


def _params(sem):
    return pltpu.CompilerParams(dimension_semantics=sem, vmem_limit_bytes=VMEM_LIMIT)


def _const_spec(shape):
    nd = len(shape)
    return pl.BlockSpec(shape, lambda *_: (0,) * nd, pipeline_mode=pl.Buffered(1))


def _layer_spec(stacked, layer):
    tail = stacked.shape[1:]
    return pl.BlockSpec((None,) + tail, lambda *_: (layer,) + (0,) * len(tail), pipeline_mode=pl.Buffered(1))


def _lane_row(lo, hi, value=1.0):
    lane = lax.broadcasted_iota(I32, (1, LANES), 1)
    return jnp.where((lane >= lo) & (lane < hi), value, 0.0)


BIAS_COLS = 8


def _alibi_placement():
    put = np.zeros((LANES, 2 * LANES), np.float32)
    for h in range(HEADS):
        slope = 2.0 ** -(h + 1)
        qc, kc = BIAS_COLS * h, LANES + BIAS_COLS * h
        put[0, qc], put[1, qc + 1], put[2, qc + 2], put[2, qc + 3] = -64.0 * slope, -slope, 64.0 * slope, slope
        put[2, kc], put[2, kc + 1], put[0, kc + 2], put[1, kc + 3] = 1.0, 1.0, 1.0, 1.0
    return jnp.asarray(put, BF16)


def _forget_placement():
    put = np.zeros((LANES, 2 * LANES), np.float32)
    for h in range(HEADS):
        for part in range(3):
            src = _SM_FL + part * HEADS + h
            put[src, BIAS_COLS * h + part] = 1.0
            put[src, LANES + BIAS_COLS * h + 3 + part] = -1.0
    return jnp.asarray(put, BF16)


def _proj_kernel(x_ref, wm_ref, ws_ref, alibi_ref, forget_ref, bias_ref,
                 gates_ref, os_ref, fqT_ref, fka_ref, own_ref, fvT_ref, dqT_ref, dka_ref, dvT_ref, iqT_ref, iwT_ref,
                 carry_sc):
    tm = x_ref.shape[1]
    xb = x_ref[0].astype(BF16)

    def group(g):
        return jnp.dot(xb, wm_ref[:, g * GROUP:(g + 1) * GROUP], preferred_element_type=F32)

    def store_gates(n):
        r = group(_G_GATES[n]).astype(BF16)
        for c in range(4):
            gates_ref[0, 4 * n + c] = r[:, c * LANES:(c + 1) * LANES]

    def pair_and_half(h):
        p, e = divmod(h, 2)
        return slice(p * LANES, (p + 1) * LANES), _lane_row(e * HEAD_DIM, (e + 1) * HEAD_DIM), (1 - e) * HEAD_DIM

    def bias_columns(compact, h, first, n):
        base = (1 - h % 2) * HEAD_DIM
        moved = pltpu.roll(compact, (base - BIAS_COLS * h) % LANES, 1)
        return moved * _lane_row(base + first, base + first + n)

    @pl.when(pl.program_id(1) == 0)
    def _new_sequence():
        carry_sc[...] = jnp.zeros(carry_sc.shape, F32)

    store_gates(0)
    small = jnp.dot(xb, ws_ref[...], preferred_element_type=F32)
    os_ref[0] = small
    iwT_ref[0] = small.T[_SM_IW:_SM_IW + HEADS, :] * IDX_SCALE
    c = _cumsum_rows(jax.nn.log_sigmoid(small + bias_ref[...])) + carry_sc[...]
    carry_sc[...] = c[tm - 1:tm, :]
    hi, mid, lo = _split3(c)
    lane = lax.broadcasted_iota(I32, (tm, LANES), 1)
    parts = jnp.where(lane < _SM_FL + HEADS, hi,
                      jnp.where(lane < _SM_FL + 2 * HEADS, pltpu.roll(mid, HEADS, 1), pltpu.roll(lo, 2 * HEADS, 1)))
    store_gates(1)
    aug = jnp.dot(parts.astype(BF16), forget_ref[...], preferred_element_type=F32)
    q = group(_G_FQ) * ATTN_SCALE
    k = group(_G_FK)
    for h in range(HEADS):
        pair, data, base = pair_and_half(h)
        q_aug = q[:, pair] * data + bias_columns(aug[:, :LANES], h, 0, 3) + _lane_row(base + 3, base + 6)
        k_aug = k[:, pair] * data + bias_columns(aug[:, LANES:], h, 3, 3) + _lane_row(base, base + 3)
        fqT_ref[0, h] = q_aug.T.astype(BF16)
        fka_ref[0, h] = k_aug.astype(BF16)
        own_ref[0, h] = jnp.sum((q_aug * k_aug).T, axis=0, keepdims=True) * LOG2E

    def store_values(ref, r):
        ones = jnp.ones((V_ROWS - HEAD_DIM, tm), BF16)
        for p in range(PAIRS):
            vT = r[:, p * LANES:(p + 1) * LANES].T.astype(BF16)
            for e in range(2):
                ref[0, 2 * p + e, 0] = jnp.concatenate([vT[e * HEAD_DIM:(e + 1) * HEAD_DIM], ones], axis=0)

    store_values(fvT_ref, group(_G_FV))
    store_gates(2)
    store_values(dvT_ref, group(_G_DV))

    lane = lax.broadcasted_iota(I32, (tm, LANES), 1)
    pos = lax.broadcasted_iota(I32, (tm, LANES), 0) + pl.program_id(1) * tm
    pos_terms = jnp.where(lane == 0, lax.shift_right_logical(pos, 6),
                          jnp.where(lane == 1, pos & 63, jnp.where(lane == 2, 1, 0))).astype(F32).astype(BF16)
    aug = jnp.dot(pos_terms, alibi_ref[...], preferred_element_type=F32)
    q = group(_G_DQ) * ATTN_SCALE
    k = group(_G_DK)
    for h in range(HEADS):
        pair, data, _ = pair_and_half(h)
        q_aug = q[:, pair] * data + bias_columns(aug[:, :LANES], h, 0, 4)
        k_aug = k[:, pair] * data + bias_columns(aug[:, LANES:], h, 0, 4)
        dqT_ref[0, h] = q_aug.T.astype(BF16)
        dka_ref[0, h] = k_aug.astype(BF16)
    iq = group(_G_IQ)
    for p in range(PAIRS):
        iqT_ref[0, p] = iq[:, p * LANES:(p + 1) * LANES].T.astype(BF16)
    store_gates(3)


def _project(x, w_main, w_small, bias_row, layer):
    B, L, D = x.shape
    tm = TK
    nk = L // TK
    alibi, forget = _alibi_placement(), _forget_placement()
    q_t = pl.BlockSpec((1, HEADS, LANES, tm), lambda b, i: (b, 0, 0, i))
    k_aug = pl.BlockSpec((1, HEADS, tm, LANES), lambda b, i: (b, 0, i, 0))
    v_t = pl.BlockSpec((1, HEADS, 1, V_ROWS, TK), lambda b, i: (b, 0, i, 0, 0))
    q_t_shape = jax.ShapeDtypeStruct((B, HEADS, LANES, L), BF16)
    k_aug_shape = jax.ShapeDtypeStruct((B, HEADS, L, LANES), BF16)
    v_t_shape = jax.ShapeDtypeStruct((B, HEADS, nk, V_ROWS, TK), BF16)
    return pl.pallas_call(
        _proj_kernel,
        grid=(B, nk),
        in_specs=[pl.BlockSpec((1, tm, D), lambda b, i: (b, i, 0)),
                  _layer_spec(w_main, layer), _layer_spec(w_small, layer),
                  _const_spec(alibi.shape), _const_spec(forget.shape), _layer_spec(bias_row, layer)],
        out_specs=[pl.BlockSpec((1, N_GATE_BLOCKS, tm, LANES), lambda b, i: (b, 0, i, 0)),
                   pl.BlockSpec((1, tm, LANES), lambda b, i: (b, i, 0)),
                   q_t, k_aug, pl.BlockSpec((1, HEADS, 1, tm), lambda b, i: (b, 0, 0, i)), v_t,
                   q_t, k_aug, v_t,
                   pl.BlockSpec((1, PAIRS, LANES, tm), lambda b, i: (b, 0, 0, i)),
                   pl.BlockSpec((1, HEADS, tm), lambda b, i: (b, 0, i))],
        out_shape=[jax.ShapeDtypeStruct((B, N_GATE_BLOCKS, L, LANES), BF16),
                   jax.ShapeDtypeStruct((B, L, LANES), F32),
                   q_t_shape, k_aug_shape, jax.ShapeDtypeStruct((B, HEADS, 1, L), F32), v_t_shape,
                   q_t_shape, k_aug_shape, v_t_shape,
                   jax.ShapeDtypeStruct((B, PAIRS, LANES, L), BF16),
                   jax.ShapeDtypeStruct((B, HEADS, L), F32)],
        scratch_shapes=[pltpu.VMEM((1, LANES), F32)],
        compiler_params=_params(("parallel", "arbitrary")),
    )(x, w_main, w_small, alibi, forget, bias_row)


def _split3(v):
    hi = v.astype(BF16).astype(F32)
    r = v - hi
    mid = r.astype(BF16).astype(F32)
    lo = (r - mid).astype(BF16).astype(F32)
    return hi, mid, lo


def _cumsum_rows(x):
    n = CUMSUM_BLOCK
    L = x.shape[0]
    r = lax.broadcasted_iota(I32, (n, n), 0)
    c = lax.broadcasted_iota(I32, (n, n), 1)
    tri = jnp.where(c <= r, 1.0, 0.0).astype(BF16)
    carry = jnp.zeros((1, x.shape[1]), F32)
    outs = []
    w = x.shape[1]
    for blk in range(L // n):
        parts = jnp.concatenate([p.astype(BF16) for p in _split3(x[blk * n:(blk + 1) * n])], axis=1)
        y3 = jnp.dot(tri, parts, preferred_element_type=F32)
        y = (y3[:, :w] + y3[:, w:2 * w] + y3[:, 2 * w:]) + carry
        carry = y[n - 1:n, :]
        outs.append(y)
    return jnp.concatenate(outs, axis=0)


def _fold8(x, op):
    return op(x.reshape(x.shape[0] // 8, 8, x.shape[1]), axis=0)


def _causal(shape):
    return lax.broadcasted_iota(I32, shape, 0) <= lax.broadcasted_iota(I32, shape, 1)


L_SUM_MIN, L_SUM_MAX = 1e-30, 1e30
NEAR_BLOCK = 64


def _attend_heads(n_full, bias_fn, ref_score_fn, qT_ref, k_ref, vT_ref, g_sc, acc_sc, o_ref):
    def for_chunks(chunk):
        def pair(jj, carry):
            chunk(2 * jj, False)
            chunk(2 * jj + 1, False)
            return carry

        def single(_, carry):
            chunk(n_full - 1, False)
            return carry

        lax.fori_loop(0, lax.shift_right_logical(n_full, 1), pair, 0)
        lax.fori_loop(0, n_full & 1, single, 0)
        chunk(n_full, True)

    def scores(h, rows, diag, raw):
        s = raw * LOG2E
        bias = bias_fn(rows, diag)
        return s if bias is None else s + bias

    def sweep():
        acc_sc[...] = jnp.zeros(acc_sc.shape, F32)

        def chunk(j, diag):
            rows = pl.ds(pl.multiple_of(j * TK, TK), TK)

            def raw_scores(h):
                return jnp.dot(k_ref[0, h, rows, :], qT_ref[0, h], preferred_element_type=F32)

            nxt = raw_scores(0)
            for h in range(HEADS):
                raw = nxt
                if h + 1 < HEADS:
                    nxt = raw_scores(h + 1)
                p = jnp.exp2(scores(h, rows, diag, raw) - g_sc[h]).astype(BF16)
                acc_sc[h] += jnp.dot(vT_ref[0, h, j], p, preferred_element_type=F32)

        for_chunks(chunk)
        bad = jnp.zeros((1, TQ), I32)
        for p in range(PAIRS):
            halves = []
            for e in range(2):
                acc = acc_sc[2 * p + e]
                l = acc[HEAD_DIM:HEAD_DIM + 1]
                bad = jnp.maximum(bad, jnp.where((l > L_SUM_MIN) & (l < L_SUM_MAX), 0, 1))
                halves.append(acc[:HEAD_DIM] / l)
            o_ref[0, :, p * LANES:(p + 1) * LANES] = jnp.concatenate(halves, axis=0).T.astype(o_ref.dtype)
        return jnp.max(bad) > 0

    def exact_row_maxima():
        acc_sc[:, 0:8, :] = jnp.full((HEADS, 8, TQ), NEG_INF, F32)

        def chunk(j, diag):
            rows = pl.ds(pl.multiple_of(j * TK, TK), TK)
            for h in range(HEADS):
                raw = jnp.dot(k_ref[0, h, rows, :], qT_ref[0, h], preferred_element_type=F32)
                acc_sc[h, 0:8, :] = jnp.maximum(acc_sc[h, 0:8, :], _fold8(scores(h, rows, diag, raw), jnp.max))

        for_chunks(chunk)
        for h in range(HEADS):
            g_sc[h] = jnp.max(acc_sc[h, 0:8, :], axis=0, keepdims=True)

    for h in range(HEADS):
        g_sc[h] = ref_score_fn(h)
    unsafe = sweep()

    @pl.when(unsafe)
    def _redo_with_exact_maxima():
        exact_row_maxima()
        sweep()


def _head_specs(L, index):
    nk = L // TK
    return [pl.BlockSpec((1, HEADS, L, LANES), lambda *g: (index(*g), 0, 0, 0)),
            pl.BlockSpec((1, HEADS, nk, V_ROWS, TK), lambda *g: (index(*g), 0, 0, 0, 0))]


def _head_scratch():
    return [pltpu.VMEM((HEADS, 1, TQ), F32), pltpu.VMEM((HEADS, V_ROWS, TQ), F32)]


def _fox_kernel(qT_ref, own_ref, k_ref, vT_ref, o_ref, g_sc, acc_sc):
    i = pl.program_id(1)

    def bias_fn(rows, diag):
        return jnp.where(_causal((TK, TQ)), 0.0, NEG_INF) if diag else None

    _attend_heads(i, bias_fn, lambda h: own_ref[0, h], qT_ref, k_ref, vT_ref, g_sc, acc_sc, o_ref)


def _fox_attention(qT, own, ka, vT):
    B, _, _, L = qT.shape
    return pl.pallas_call(
        _fox_kernel,
        grid=(B, L // TQ),
        in_specs=[pl.BlockSpec((1, HEADS, LANES, TQ), lambda b, i: (b, 0, 0, i)),
                  pl.BlockSpec((1, HEADS, 1, TQ), lambda b, i: (b, 0, 0, i))] + _head_specs(L, lambda b, i: b),
        out_specs=pl.BlockSpec((1, TQ, PAIRS * LANES), lambda b, i: (b, i, 0)),
        out_shape=jax.ShapeDtypeStruct((B, L, PAIRS * LANES), BF16),
        scratch_shapes=_head_scratch(),
        compiler_params=_params(("parallel", "arbitrary")),
    )(qT, own, ka, vT)


def _count_rows(mask):
    return _fold8(jnp.where(mask, 1, 0).astype(I32), jnp.sum)


def _dsa_kernel(qT_ref, k_ref, vT_ref, iqT_ref, iwT_ref, s_ref, o_ref,
                keys_sc, hi_sc, lo_sc, g_sc, acc_sc):
    i = pl.program_id(1)
    n_chunks = i + 1

    def _select():
        wts = iwT_ref[0]

        def score_chunk(j, masked):
            ks = pl.multiple_of(j * TK, TK)
            ik = s_ref[0, pl.ds(ks, TK), :][:, _SM_IK:_SM_IK + HEAD_DIM].astype(BF16)
            acc = jnp.zeros((TK, TQ), F32)
            for h in range(HEADS):
                iq = iqT_ref[0, h // 2, (h % 2) * HEAD_DIM:(h % 2 + 1) * HEAD_DIM, :]
                rel = jnp.maximum(jnp.dot(ik, iq, preferred_element_type=F32), 0.0)
                acc = acc + wts[h:h + 1, :] * rel
            if masked:
                acc = jnp.where(_causal(acc.shape), acc, NEG_INF)
            bits = pltpu.bitcast(acc, I32)
            sign = lax.shift_right_arithmetic(bits, 31)
            key = (bits ^ (sign & 0x7FFFFFFF)) - sign
            keys_sc[pl.ds(ks, TK), :] = key
            hi_sc[pl.ds(ks, TK), :] = lax.shift_right_arithmetic(key, 16).astype(I16)
            lo_sc[pl.ds(ks, TK), :] = key.astype(I16) ^ jnp.full((), -(2 ** 15), I16)

        def score_body(j, carry):
            score_chunk(j, False)
            return carry

        lax.fori_loop(0, i, score_body, 0)
        score_chunk(i, True)

        def count(pred):
            def body(j, cnt):
                ks = pl.multiple_of(j * TK, TK)
                return cnt + _count_rows(pred(keys_sc[pl.ds(ks, TK), :]))
            cnt8 = lax.fori_loop(0, n_chunks, body, jnp.zeros((8, TQ), I32))
            return jnp.sum(cnt8, axis=0, keepdims=True)

        def count16(ref, pred):
            one, nil = jnp.ones((), I16), jnp.zeros((), I16)

            def body(j, cnt):
                ks = pl.multiple_of(j * TK, TK)
                hit = jnp.where(pred(ref[pl.ds(ks, TK), :]), one, nil)
                for r in range(TK // 16):
                    cnt = cnt + hit[r * 16:(r + 1) * 16, :]
                return cnt
            cnt16 = lax.fori_loop(0, n_chunks, body, jnp.zeros((16, TQ), I16))
            return jnp.sum(cnt16.astype(I32), axis=0, keepdims=True)

        def select16(ref, k):
            zero = jnp.zeros((1, TQ), I16)
            c0 = count16(ref, lambda v: v >= zero)
            ok0 = c0 >= k
            init = (jnp.where(ok0, 0, -(2 ** 15)).astype(I32),
                    jnp.where(ok0, c0, n_chunks * TK),
                    jnp.where(ok0, 0, c0))

            def bit_body(bi, carry):
                t, n_ge, n_gt = carry
                cand = t | jnp.left_shift(jnp.ones((1, TQ), I32), 14 - bi)
                cand16 = cand.astype(I16)
                c = count16(ref, lambda v: v >= cand16)
                ok = c >= k
                return jnp.where(ok, cand, t), jnp.where(ok, c, n_ge), jnp.where(ok, n_gt, c)

            return lax.fori_loop(0, 15, bit_body, init)

        t_hi, _, n_above = select16(hi_sc, TOPK)
        t_hi16 = t_hi.astype(I16)
        k_lo = TOPK - n_above

        def restrict_body(j, carry):
            ks = pl.multiple_of(j * TK, TK)
            in_bucket = hi_sc[pl.ds(ks, TK), :] == t_hi16
            lo_sc[pl.ds(ks, TK), :] = jnp.where(in_bucket, lo_sc[pl.ds(ks, TK), :], jnp.full((), -(2 ** 15), I16))
            return carry

        lax.fori_loop(0, n_chunks, restrict_body, 0)
        t_lo, n_ge_lo, _ = select16(lo_sc, k_lo)
        thr = jnp.left_shift(t_hi, 16) | (t_lo + 2 ** 15)

        excess = jnp.where((n_above + n_ge_lo > TOPK) & (thr > NEG_INF_KEY), 1, 0)

        @pl.when(jnp.max(excess) > 0)
        def _break_ties():
            need = TOPK - count(lambda kk: kk > thr)
            r = lax.broadcasted_iota(I32, (TK, TK), 0)
            c = lax.broadcasted_iota(I32, (TK, TK), 1)
            strict_lower = jnp.where(c < r, 1.0, 0.0).astype(BF16)

            def body(j, seen):
                ks = pl.multiple_of(j * TK, TK)
                kk = keys_sc[pl.ds(ks, TK), :]
                tie = kk == thr
                tie_f = jnp.where(tie, 1.0, 0.0)
                before = jnp.dot(strict_lower, tie_f.astype(BF16), preferred_element_type=F32) + seen
                drop = tie & (before >= need.astype(F32))
                keys_sc[pl.ds(ks, TK), :] = jnp.where(drop, INT_MIN, kk)
                return seen + jnp.sum(tie_f, axis=0, keepdims=True)

            lax.fori_loop(0, n_chunks, body, jnp.zeros((1, TQ), F32))

        def bias_chunk(j, masked, near8):
            ks = pl.multiple_of(j * TK, TK)
            bias = jnp.where(keys_sc[pl.ds(ks, TK), :] >= thr, 0.0, NEG_INF)
            if masked:
                bias = jnp.where(_causal(bias.shape), bias, NEG_INF)
            keys_sc[pl.ds(ks, TK), :] = pltpu.bitcast(bias, I32)
            for blk in range(TK // NEAR_BLOCK):
                any_selected = _fold8(bias[blk * NEAR_BLOCK:(blk + 1) * NEAR_BLOCK], jnp.max) == 0.0
                near8 = jnp.maximum(near8, jnp.where(any_selected, ks + (blk + 1) * NEAR_BLOCK - 1, -1))
            return near8

        near8 = lax.fori_loop(0, i, lambda j, n8: bias_chunk(j, False, n8), jnp.full((8, TQ), -1, I32))
        return jnp.max(bias_chunk(i, True, near8), axis=0, keepdims=True)

    nearest = _select()

    def stored_bias(rows, diag):
        return pltpu.bitcast(keys_sc[rows, :], F32)

    def nearest_key_bias(h):
        query = i * TQ + lax.broadcasted_iota(I32, (1, TQ), 1)
        return (query - nearest).astype(F32) * (-(2.0 ** -(h + 1)) * LOG2E)

    _attend_heads(i, stored_bias, nearest_key_bias, qT_ref, k_ref, vT_ref, g_sc, acc_sc, o_ref)


def _dsa_attention(qT, ka, vT, iqT, iwT, small):
    B, _, _, L = qT.shape
    return pl.pallas_call(
        _dsa_kernel,
        grid=(B, L // TQ),
        in_specs=[pl.BlockSpec((1, HEADS, LANES, TQ), lambda b, i: (b, 0, 0, i))] + _head_specs(L, lambda b, i: b)
        + [pl.BlockSpec((1, PAIRS, LANES, TQ), lambda b, i: (b, 0, 0, i)),
           pl.BlockSpec((1, HEADS, TQ), lambda b, i: (b, 0, i)),
           pl.BlockSpec((1, L, LANES), lambda b, i: (b, 0, 0))],
        out_specs=pl.BlockSpec((1, TQ, PAIRS * LANES), lambda b, i: (b, i, 0)),
        out_shape=jax.ShapeDtypeStruct((B, L, PAIRS * LANES), BF16),
        scratch_shapes=[pltpu.VMEM((L, TQ), I32), pltpu.VMEM((L, TQ), I16), pltpu.VMEM((L, TQ), I16)]
        + _head_scratch(),
        compiler_params=_params(("parallel", "arbitrary")),
    )(qT, ka, vT, iqT, iwT, small)


def _layer_norm(z, g, b):
    mu = jnp.mean(z, axis=-1, keepdims=True)
    zc = z - mu
    var = jnp.mean(zc * zc, axis=-1, keepdims=True)
    return zc * lax.rsqrt(var + LN_EPS) * g + b


FF_CHUNK = 256


def _merge_ffn_kernel(x_ref, oa_ref, ob_ref, ga_ref, gb_ref, wa_ref, wb_ref, wo_ref, g1_ref, b1_ref,
                      wi_ref, wf_ref, g2_ref, b2_ref, out_ref, h_sc, *, alpha):
    a = jnp.dot(oa_ref[0], wa_ref[...], preferred_element_type=F32)
    b = jnp.dot(ob_ref[0], wb_ref[...], preferred_element_type=F32)
    n_gate_blocks = D_MODEL // LANES
    gate_a = jnp.concatenate([ga_ref[0, c] for c in range(n_gate_blocks)], axis=1).astype(F32)
    gate_b = jnp.concatenate([gb_ref[0, c] for c in range(n_gate_blocks)], axis=1).astype(F32)
    merged = jax.nn.sigmoid(gate_a) * a + jax.nn.sigmoid(gate_b) * b
    y = jnp.dot(merged.astype(BF16), wo_ref[...], preferred_element_type=F32)
    x = _layer_norm(alpha * x_ref[0] + y, g1_ref[...], b1_ref[...])
    xb = x.astype(BF16)
    for c in range(D_FF // FF_CHUNK):
        lo = c * FF_CHUNK
        gate = jnp.dot(xb, wi_ref[:, lo:lo + FF_CHUNK], preferred_element_type=F32)
        up = jnp.dot(xb, wi_ref[:, D_FF + lo:D_FF + lo + FF_CHUNK], preferred_element_type=F32)
        h_sc[:, lo:lo + FF_CHUNK] = (jax.nn.silu(gate) * up).astype(BF16)
    y = jnp.dot(h_sc[...], wf_ref[...], preferred_element_type=F32)
    out_ref[0] = _layer_norm(alpha * x + y, g2_ref[...], b2_ref[...])


def _merge_ffn(x, oa, ob, gates, stacked, layer, alpha):
    B, L, D = x.shape
    tm = ROW_TILE
    nb = D // LANES
    row = lambda b_, i: (b_, i, 0)
    return pl.pallas_call(
        functools.partial(_merge_ffn_kernel, alpha=alpha),
        grid=(B, L // tm),
        in_specs=[pl.BlockSpec((1, tm, D), row),
                  pl.BlockSpec((1, tm, PAIRS * LANES), row),
                  pl.BlockSpec((1, tm, PAIRS * LANES), row),
                  pl.BlockSpec((1, nb, tm, LANES), lambda b_, i: (b_, _CB_GA // nb, i, 0)),
                  pl.BlockSpec((1, nb, tm, LANES), lambda b_, i: (b_, _CB_GB // nb, i, 0))]
        + [_layer_spec(p, layer) for p in stacked],
        out_specs=pl.BlockSpec((1, tm, D), row),
        out_shape=jax.ShapeDtypeStruct((B, L, D), F32),
        scratch_shapes=[pltpu.VMEM((tm, D_FF), BF16)],
        compiler_params=_params(("parallel", "parallel")),
    )(x, oa, ob, gates, gates, *stacked)


def _split_w_in(w):
    fq_fk = w[..., 0:2 * GROUP]
    fv = w[..., 2 * GROUP:_OFF_FLOGIT]
    dsa_iq = w[..., _OFF_DQ:_OFF_IK]
    gates = w[..., _OFF_GA:_N_IN]
    w_main = jnp.concatenate([fq_fk, gates, fv, dsa_iq], axis=-1).astype(BF16)
    pad = jnp.zeros(w.shape[:-1] + (LANES - HEAD_DIM - 2 * HEADS,), w.dtype)
    w_small = jnp.concatenate([w[..., _OFF_IK:_OFF_IW], w[..., _OFF_IW:_OFF_GA],
                               w[..., _OFF_FLOGIT:_OFF_DQ], pad], axis=-1).astype(BF16)
    return w_main, w_small


def kernel(x, w_in, b_forget, w_branch_a, w_branch_b, w_out, ln1_g, ln1_b, w_ffn_in, w_ffn_out, ln2_g, ln2_b):
    depth = w_in.shape[0]
    alpha = (2.0 * depth) ** 0.25
    w_main, w_small = _split_w_in(w_in)
    bias_rows = jnp.zeros((depth, 1, LANES), F32).at[:, 0, _SM_FL:_SM_FL + HEADS].set(b_forget)
    wa, wb, wo = w_branch_a.astype(BF16), w_branch_b.astype(BF16), w_out.astype(BF16)
    wi, wf = w_ffn_in.astype(BF16), w_ffn_out.astype(BF16)
    row = lambda p: p[:, None, :]
    stacked = (wa, wb, wo, row(ln1_g), row(ln1_b), wi, wf, row(ln2_g), row(ln2_b))
    for l in range(depth):
        gates, small, fqT, fka, f_own, fvT, dqT, dka, dvT, iqT, iwT = _project(x, w_main, w_small, bias_rows, l)
        o_a = _fox_attention(fqT, f_own, fka, fvT)
        o_b = _dsa_attention(dqT, dka, dvT, iqT, iwT, small)
        x = _merge_ffn(x, o_a, o_b, gates, stacked, l, alpha)
    return x
```

```python
import functools

import numpy as np
import jax
import jax.numpy as jnp
from jax import lax
from jax.experimental import pallas as pl
from jax.experimental.pallas import tpu as pltpu

F32 = jnp.float32
BF16 = jnp.bfloat16
I32 = jnp.int32
I16 = jnp.int16

D_MODEL = 1024
HEADS = 8
HEAD_DIM = 64
PAIRS = HEADS // 2
D_FF = 2816
TOPK = 256
LANES = 128
NEG_INF = -1e30
LN_EPS = 1e-5
ATTN_SCALE = HEAD_DIM ** -0.5
IDX_SCALE = (HEADS ** -0.5) * (HEAD_DIM ** -0.5)

_OFF_FLOGIT = 1536
_OFF_DQ = 1544
_OFF_IK = 3592
_OFF_IW = 3656
_OFF_GA = 3664
_N_IN = 5712

_G_FQ, _G_FK, _G_GATES, _G_FV, _G_DQ, _G_DK, _G_DV, _G_IQ = 0, 1, (2, 3, 4, 5), 6, 7, 8, 9, 10
_CB_GA, _CB_GB = 0, 8
N_GATE_BLOCKS = 16
GROUP = 4 * LANES
N_MAIN = 11 * GROUP
_SM_IK, _SM_IW, _SM_FL = 0, 64, 72

ROW_TILE = 512
TQ = 512
TK = 512
CUMSUM_BLOCK = 256
V_ROWS = HEAD_DIM + 16
LOG2E = 1.4426950408889634

INT_MIN = -(2 ** 31)


def _monotone_key_of(value):
    b = int(np.float32(value).view(np.int32))
    return (b ^ ((b >> 31) & 0x7FFFFFFF)) - (b >> 31)


NEG_INF_KEY = _monotone_key_of(NEG_INF)
VMEM_LIMIT = 60 * 1024 * 1024


def _params(sem):
    return pltpu.CompilerParams(dimension_semantics=sem, vmem_limit_bytes=VMEM_LIMIT)


def _const_spec(shape):
    nd = len(shape)
    return pl.BlockSpec(shape, lambda *_: (0,) * nd, pipeline_mode=pl.Buffered(1))


def _layer_spec(stacked, layer):
    tail = stacked.shape[1:]
    return pl.BlockSpec((None,) + tail, lambda *_: (layer,) + (0,) * len(tail), pipeline_mode=pl.Buffered(1))


def _lane_row(lo, hi, value=1.0):
    lane = lax.broadcasted_iota(I32, (1, LANES), 1)
    return jnp.where((lane >= lo) & (lane < hi), value, 0.0)


BIAS_COLS = 8


def _alibi_placement():
    put = np.zeros((LANES, 2 * LANES), np.float32)
    for h in range(HEADS):
        slope = 2.0 ** -(h + 1)
        qc, kc = BIAS_COLS * h, LANES + BIAS_COLS * h
        put[0, qc], put[1, qc + 1], put[2, qc + 2], put[2, qc + 3] = -64.0 * slope, -slope, 64.0 * slope, slope
        put[2, kc], put[2, kc + 1], put[0, kc + 2], put[1, kc + 3] = 1.0, 1.0, 1.0, 1.0
    return jnp.asarray(put, BF16)


def _forget_placement():
    put = np.zeros((LANES, 2 * LANES), np.float32)
    for h in range(HEADS):
        for part in range(3):
            src = _SM_FL + part * HEADS + h
            put[src, BIAS_COLS * h + part] = 1.0
            put[src, LANES + BIAS_COLS * h + 3 + part] = -1.0
    return jnp.asarray(put, BF16)


def _proj_kernel(x_ref, wm_ref, ws_ref, alibi_ref, forget_ref, bias_ref,
                 gates_ref, os_ref, fqT_ref, fka_ref, own_ref, fvT_ref, dqT_ref, dka_ref, dvT_ref, iqT_ref, iwT_ref,
                 carry_sc):
    tm = x_ref.shape[1]
    xb = x_ref[0].astype(BF16)

    def group(g):
        return jnp.dot(xb, wm_ref[:, g * GROUP:(g + 1) * GROUP], preferred_element_type=F32)

    def store_gates(n):
        r = group(_G_GATES[n]).astype(BF16)
        for c in range(4):
            gates_ref[0, 4 * n + c] = r[:, c * LANES:(c + 1) * LANES]

    def pair_and_half(h):
        p, e = divmod(h, 2)
        return slice(p * LANES, (p + 1) * LANES), _lane_row(e * HEAD_DIM, (e + 1) * HEAD_DIM), (1 - e) * HEAD_DIM

    def bias_columns(compact, h, first, n):
        base = (1 - h % 2) * HEAD_DIM
        moved = pltpu.roll(compact, (base - BIAS_COLS * h) % LANES, 1)
        return moved * _lane_row(base + first, base + first + n)

    @pl.when(pl.program_id(1) == 0)
    def _new_sequence():
        carry_sc[...] = jnp.zeros(carry_sc.shape, F32)

    store_gates(0)
    small = jnp.dot(xb, ws_ref[...], preferred_element_type=F32)
    os_ref[0] = small
    iwT_ref[0] = small.T[_SM_IW:_SM_IW + HEADS, :] * IDX_SCALE
    c = _cumsum_rows(jax.nn.log_sigmoid(small + bias_ref[...])) + carry_sc[...]
    carry_sc[...] = c[tm - 1:tm, :]
    hi, mid, lo = _split3(c)
    lane = lax.broadcasted_iota(I32, (tm, LANES), 1)
    parts = jnp.where(lane < _SM_FL + HEADS, hi,
                      jnp.where(lane < _SM_FL + 2 * HEADS, pltpu.roll(mid, HEADS, 1), pltpu.roll(lo, 2 * HEADS, 1)))
    store_gates(1)
    aug = jnp.dot(parts.astype(BF16), forget_ref[...], preferred_element_type=F32)
    q = group(_G_FQ) * ATTN_SCALE
    k = group(_G_FK)
    for h in range(HEADS):
        pair, data, base = pair_and_half(h)
        q_aug = q[:, pair] * data + bias_columns(aug[:, :LANES], h, 0, 3) + _lane_row(base + 3, base + 6)
        k_aug = k[:, pair] * data + bias_columns(aug[:, LANES:], h, 3, 3) + _lane_row(base, base + 3)
        fqT_ref[0, h] = q_aug.T.astype(BF16)
        fka_ref[0, h] = k_aug.astype(BF16)
        own_ref[0, h] = jnp.sum((q_aug * k_aug).T, axis=0, keepdims=True) * LOG2E

    def store_values(ref, r):
        ones = jnp.ones((V_ROWS - HEAD_DIM, tm), BF16)
        for p in range(PAIRS):
            vT = r[:, p * LANES:(p + 1) * LANES].T.astype(BF16)
            for e in range(2):
                ref[0, 2 * p + e, 0] = jnp.concatenate([vT[e * HEAD_DIM:(e + 1) * HEAD_DIM], ones], axis=0)

    store_values(fvT_ref, group(_G_FV))
    store_gates(2)
    store_values(dvT_ref, group(_G_DV))

    lane = lax.broadcasted_iota(I32, (tm, LANES), 1)
    pos = lax.broadcasted_iota(I32, (tm, LANES), 0) + pl.program_id(1) * tm
    pos_terms = jnp.where(lane == 0, lax.shift_right_logical(pos, 6),
                          jnp.where(lane == 1, pos & 63, jnp.where(lane == 2, 1, 0))).astype(F32).astype(BF16)
    aug = jnp.dot(pos_terms, alibi_ref[...], preferred_element_type=F32)
    q = group(_G_DQ) * ATTN_SCALE
    k = group(_G_DK)
    for h in range(HEADS):
        pair, data, _ = pair_and_half(h)
        q_aug = q[:, pair] * data + bias_columns(aug[:, :LANES], h, 0, 4)
        k_aug = k[:, pair] * data + bias_columns(aug[:, LANES:], h, 0, 4)
        dqT_ref[0, h] = q_aug.T.astype(BF16)
        dka_ref[0, h] = k_aug.astype(BF16)
    iq = group(_G_IQ)
    for p in range(PAIRS):
        iqT_ref[0, p] = iq[:, p * LANES:(p + 1) * LANES].T.astype(BF16)
    store_gates(3)


def _project(x, w_main, w_small, bias_row, layer):
    B, L, D = x.shape
    tm = TK
    nk = L // TK
    alibi, forget = _alibi_placement(), _forget_placement()
    q_t = pl.BlockSpec((1, HEADS, LANES, tm), lambda b, i: (b, 0, 0, i))
    k_aug = pl.BlockSpec((1, HEADS, tm, LANES), lambda b, i: (b, 0, i, 0))
    v_t = pl.BlockSpec((1, HEADS, 1, V_ROWS, TK), lambda b, i: (b, 0, i, 0, 0))
    q_t_shape = jax.ShapeDtypeStruct((B, HEADS, LANES, L), BF16)
    k_aug_shape = jax.ShapeDtypeStruct((B, HEADS, L, LANES), BF16)
    v_t_shape = jax.ShapeDtypeStruct((B, HEADS, nk, V_ROWS, TK), BF16)
    return pl.pallas_call(
        _proj_kernel,
        grid=(B, nk),
        in_specs=[pl.BlockSpec((1, tm, D), lambda b, i: (b, i, 0)),
                  _layer_spec(w_main, layer), _layer_spec(w_small, layer),
                  _const_spec(alibi.shape), _const_spec(forget.shape), _layer_spec(bias_row, layer)],
        out_specs=[pl.BlockSpec((1, N_GATE_BLOCKS, tm, LANES), lambda b, i: (b, 0, i, 0)),
                   pl.BlockSpec((1, tm, LANES), lambda b, i: (b, i, 0)),
                   q_t, k_aug, pl.BlockSpec((1, HEADS, 1, tm), lambda b, i: (b, 0, 0, i)), v_t,
                   q_t, k_aug, v_t,
                   pl.BlockSpec((1, PAIRS, LANES, tm), lambda b, i: (b, 0, 0, i)),
                   pl.BlockSpec((1, HEADS, tm), lambda b, i: (b, 0, i))],
        out_shape=[jax.ShapeDtypeStruct((B, N_GATE_BLOCKS, L, LANES), BF16),
                   jax.ShapeDtypeStruct((B, L, LANES), F32),
                   q_t_shape, k_aug_shape, jax.ShapeDtypeStruct((B, HEADS, 1, L), F32), v_t_shape,
                   q_t_shape, k_aug_shape, v_t_shape,
                   jax.ShapeDtypeStruct((B, PAIRS, LANES, L), BF16),
                   jax.ShapeDtypeStruct((B, HEADS, L), F32)],
        scratch_shapes=[pltpu.VMEM((1, LANES), F32)],
        compiler_params=_params(("parallel", "arbitrary")),
    )(x, w_main, w_small, alibi, forget, bias_row)


def _split3(v):
    hi = v.astype(BF16).astype(F32)
    r = v - hi
    mid = r.astype(BF16).astype(F32)
    lo = (r - mid).astype(BF16).astype(F32)
    return hi, mid, lo


def _cumsum_rows(x):
    n = CUMSUM_BLOCK
    L = x.shape[0]
    r = lax.broadcasted_iota(I32, (n, n), 0)
    c = lax.broadcasted_iota(I32, (n, n), 1)
    tri = jnp.where(c <= r, 1.0, 0.0).astype(BF16)
    carry = jnp.zeros((1, x.shape[1]), F32)
    outs = []
    w = x.shape[1]
    for blk in range(L // n):
        parts = jnp.concatenate([p.astype(BF16) for p in _split3(x[blk * n:(blk + 1) * n])], axis=1)
        y3 = jnp.dot(tri, parts, preferred_element_type=F32)
        y = (y3[:, :w] + y3[:, w:2 * w] + y3[:, 2 * w:]) + carry
        carry = y[n - 1:n, :]
        outs.append(y)
    return jnp.concatenate(outs, axis=0)


def _fold8(x, op):
    return op(x.reshape(x.shape[0] // 8, 8, x.shape[1]), axis=0)


def _causal(shape):
    return lax.broadcasted_iota(I32, shape, 0) <= lax.broadcasted_iota(I32, shape, 1)


L_SUM_MIN, L_SUM_MAX = 1e-30, 1e30
NEAR_BLOCK = 64


def _attend_heads(n_full, bias_fn, ref_score_fn, qT_ref, k_ref, vT_ref, g_sc, acc_sc, o_ref):
    def for_chunks(chunk):
        def pair(jj, carry):
            chunk(2 * jj, False)
            chunk(2 * jj + 1, False)
            return carry

        def single(_, carry):
            chunk(n_full - 1, False)
            return carry

        lax.fori_loop(0, lax.shift_right_logical(n_full, 1), pair, 0)
        lax.fori_loop(0, n_full & 1, single, 0)
        chunk(n_full, True)

    def scores(h, rows, diag, raw):
        s = raw * LOG2E
        bias = bias_fn(rows, diag)
        return s if bias is None else s + bias

    def sweep():
        acc_sc[...] = jnp.zeros(acc_sc.shape, F32)

        def chunk(j, diag):
            rows = pl.ds(pl.multiple_of(j * TK, TK), TK)

            def raw_scores(h):
                return jnp.dot(k_ref[0, h, rows, :], qT_ref[0, h], preferred_element_type=F32)

            nxt = raw_scores(0)
            for h in range(HEADS):
                raw = nxt
                if h + 1 < HEADS:
                    nxt = raw_scores(h + 1)
                p = jnp.exp2(scores(h, rows, diag, raw) - g_sc[h]).astype(BF16)
                acc_sc[h] += jnp.dot(vT_ref[0, h, j], p, preferred_element_type=F32)

        for_chunks(chunk)
        bad = jnp.zeros((1, TQ), I32)
        for p in range(PAIRS):
            halves = []
            for e in range(2):
                acc = acc_sc[2 * p + e]
                l = acc[HEAD_DIM:HEAD_DIM + 1]
                bad = jnp.maximum(bad, jnp.where((l > L_SUM_MIN) & (l < L_SUM_MAX), 0, 1))
                halves.append(acc[:HEAD_DIM] / l)
            o_ref[0, :, p * LANES:(p + 1) * LANES] = jnp.concatenate(halves, axis=0).T.astype(o_ref.dtype)
        return jnp.max(bad) > 0

    def exact_row_maxima():
        acc_sc[:, 0:8, :] = jnp.full((HEADS, 8, TQ), NEG_INF, F32)

        def chunk(j, diag):
            rows = pl.ds(pl.multiple_of(j * TK, TK), TK)
            for h in range(HEADS):
                raw = jnp.dot(k_ref[0, h, rows, :], qT_ref[0, h], preferred_element_type=F32)
                acc_sc[h, 0:8, :] = jnp.maximum(acc_sc[h, 0:8, :], _fold8(scores(h, rows, diag, raw), jnp.max))

        for_chunks(chunk)
        for h in range(HEADS):
            g_sc[h] = jnp.max(acc_sc[h, 0:8, :], axis=0, keepdims=True)

    for h in range(HEADS):
        g_sc[h] = ref_score_fn(h)
    unsafe = sweep()

    @pl.when(unsafe)
    def _redo_with_exact_maxima():
        exact_row_maxima()
        sweep()


def _head_specs(L, index):
    nk = L // TK
    return [pl.BlockSpec((1, HEADS, L, LANES), lambda *g: (index(*g), 0, 0, 0)),
            pl.BlockSpec((1, HEADS, nk, V_ROWS, TK), lambda *g: (index(*g), 0, 0, 0, 0))]


def _head_scratch():
    return [pltpu.VMEM((HEADS, 1, TQ), F32), pltpu.VMEM((HEADS, V_ROWS, TQ), F32)]


def _fox_kernel(qT_ref, own_ref, k_ref, vT_ref, o_ref, g_sc, acc_sc):
    i = pl.program_id(1)

    def bias_fn(rows, diag):
        return jnp.where(_causal((TK, TQ)), 0.0, NEG_INF) if diag else None

    _attend_heads(i, bias_fn, lambda h: own_ref[0, h], qT_ref, k_ref, vT_ref, g_sc, acc_sc, o_ref)


def _fox_attention(qT, own, ka, vT):
    B, _, _, L = qT.shape
    return pl.pallas_call(
        _fox_kernel,
        grid=(B, L // TQ),
        in_specs=[pl.BlockSpec((1, HEADS, LANES, TQ), lambda b, i: (b, 0, 0, i)),
                  pl.BlockSpec((1, HEADS, 1, TQ), lambda b, i: (b, 0, 0, i))] + _head_specs(L, lambda b, i: b),
        out_specs=pl.BlockSpec((1, TQ, PAIRS * LANES), lambda b, i: (b, i, 0)),
        out_shape=jax.ShapeDtypeStruct((B, L, PAIRS * LANES), BF16),
        scratch_shapes=_head_scratch(),
        compiler_params=_params(("parallel", "arbitrary")),
    )(qT, own, ka, vT)


def _count_rows(mask):
    return _fold8(jnp.where(mask, 1, 0).astype(I32), jnp.sum)


def _dsa_kernel(qT_ref, k_ref, vT_ref, iqT_ref, iwT_ref, s_ref, o_ref,
                keys_sc, hi_sc, lo_sc, g_sc, acc_sc):
    i = pl.program_id(1)
    n_chunks = i + 1

    def _select():
        wts = iwT_ref[0]

        def score_chunk(j, masked):
            ks = pl.multiple_of(j * TK, TK)
            ik = s_ref[0, pl.ds(ks, TK), :][:, _SM_IK:_SM_IK + HEAD_DIM].astype(BF16)
            acc = jnp.zeros((TK, TQ), F32)
            for h in range(HEADS):
                iq = iqT_ref[0, h // 2, (h % 2) * HEAD_DIM:(h % 2 + 1) * HEAD_DIM, :]
                rel = jnp.maximum(jnp.dot(ik, iq, preferred_element_type=F32), 0.0)
                acc = acc + wts[h:h + 1, :] * rel
            if masked:
                acc = jnp.where(_causal(acc.shape), acc, NEG_INF)
            bits = pltpu.bitcast(acc, I32)
            sign = lax.shift_right_arithmetic(bits, 31)
            key = (bits ^ (sign & 0x7FFFFFFF)) - sign
            keys_sc[pl.ds(ks, TK), :] = key
            hi_sc[pl.ds(ks, TK), :] = lax.shift_right_arithmetic(key, 16).astype(I16)
            lo_sc[pl.ds(ks, TK), :] = key.astype(I16) ^ jnp.full((), -(2 ** 15), I16)

        def score_pair(jj, carry):
            score_chunk(2 * jj, False)
            score_chunk(2 * jj + 1, False)
            return carry

        def score_single(_, carry):
            score_chunk(i - 1, False)
            return carry

        lax.fori_loop(0, lax.shift_right_logical(i, 1), score_pair, 0)
        lax.fori_loop(0, i & 1, score_single, 0)
        score_chunk(i, True)

        def count(pred):
            def body(j, cnt):
                ks = pl.multiple_of(j * TK, TK)
                return cnt + _count_rows(pred(keys_sc[pl.ds(ks, TK), :]))
            cnt8 = lax.fori_loop(0, n_chunks, body, jnp.zeros((8, TQ), I32))
            return jnp.sum(cnt8, axis=0, keepdims=True)

        def count16(ref, pred):
            one, nil = jnp.ones((), I16), jnp.zeros((), I16)

            def add_chunk(j, cnt):
                ks = pl.multiple_of(j * TK, TK)
                hit = jnp.where(pred(ref[pl.ds(ks, TK), :]), one, nil)
                for r in range(TK // 16):
                    cnt = cnt + hit[r * 16:(r + 1) * 16, :]
                return cnt

            cnt16 = lax.fori_loop(0, lax.shift_right_logical(n_chunks, 1),
                                  lambda jj, cnt: add_chunk(2 * jj + 1, add_chunk(2 * jj, cnt)),
                                  jnp.zeros((16, TQ), I16))
            cnt16 = lax.fori_loop(0, n_chunks & 1, lambda _, cnt: add_chunk(n_chunks - 1, cnt), cnt16)
            return jnp.sum(cnt16.astype(I32), axis=0, keepdims=True)

        def select16(ref, k):
            zero = jnp.zeros((1, TQ), I16)
            c0 = count16(ref, lambda v: v >= zero)
            ok0 = c0 >= k
            init = (jnp.where(ok0, 0, -(2 ** 15)).astype(I32),
                    jnp.where(ok0, c0, n_chunks * TK),
                    jnp.where(ok0, 0, c0))

            def bit_body(bi, carry):
                t, n_ge, n_gt = carry
                cand = t | jnp.left_shift(jnp.ones((1, TQ), I32), 14 - bi)
                cand16 = cand.astype(I16)
                c = count16(ref, lambda v: v >= cand16)
                ok = c >= k
                return jnp.where(ok, cand, t), jnp.where(ok, c, n_ge), jnp.where(ok, n_gt, c)

            return lax.fori_loop(0, 15, bit_body, init)

        t_hi, _, n_above = select16(hi_sc, TOPK)
        t_hi16 = t_hi.astype(I16)
        k_lo = TOPK - n_above

        def restrict_body(j, carry):
            ks = pl.multiple_of(j * TK, TK)
            in_bucket = hi_sc[pl.ds(ks, TK), :] == t_hi16
            lo_sc[pl.ds(ks, TK), :] = jnp.where(in_bucket, lo_sc[pl.ds(ks, TK), :], jnp.full((), -(2 ** 15), I16))
            return carry

        lax.fori_loop(0, n_chunks, restrict_body, 0)
        t_lo, n_ge_lo, _ = select16(lo_sc, k_lo)
        thr = jnp.left_shift(t_hi, 16) | (t_lo + 2 ** 15)

        excess = jnp.where((n_above + n_ge_lo > TOPK) & (thr > NEG_INF_KEY), 1, 0)

        @pl.when(jnp.max(excess) > 0)
        def _break_ties():
            need = TOPK - count(lambda kk: kk > thr)
            r = lax.broadcasted_iota(I32, (TK, TK), 0)
            c = lax.broadcasted_iota(I32, (TK, TK), 1)
            strict_lower = jnp.where(c < r, 1.0, 0.0).astype(BF16)

            def body(j, seen):
                ks = pl.multiple_of(j * TK, TK)
                kk = keys_sc[pl.ds(ks, TK), :]
                tie = kk == thr
                tie_f = jnp.where(tie, 1.0, 0.0)
                before = jnp.dot(strict_lower, tie_f.astype(BF16), preferred_element_type=F32) + seen
                drop = tie & (before >= need.astype(F32))
                keys_sc[pl.ds(ks, TK), :] = jnp.where(drop, INT_MIN, kk)
                return seen + jnp.sum(tie_f, axis=0, keepdims=True)

            lax.fori_loop(0, n_chunks, body, jnp.zeros((1, TQ), F32))

        def bias_chunk(j, masked, near8):
            ks = pl.multiple_of(j * TK, TK)
            bias = jnp.where(keys_sc[pl.ds(ks, TK), :] >= thr, 0.0, NEG_INF)
            if masked:
                bias = jnp.where(_causal(bias.shape), bias, NEG_INF)
            keys_sc[pl.ds(ks, TK), :] = pltpu.bitcast(bias, I32)
            for blk in range(TK // NEAR_BLOCK):
                any_selected = _fold8(bias[blk * NEAR_BLOCK:(blk + 1) * NEAR_BLOCK], jnp.max) == 0.0
                near8 = jnp.maximum(near8, jnp.where(any_selected, ks + (blk + 1) * NEAR_BLOCK - 1, -1))
            return near8

        near8 = lax.fori_loop(0, i, lambda j, n8: bias_chunk(j, False, n8), jnp.full((8, TQ), -1, I32))
        return jnp.max(bias_chunk(i, True, near8), axis=0, keepdims=True)

    nearest = _select()

    def stored_bias(rows, diag):
        return pltpu.bitcast(keys_sc[rows, :], F32)

    def nearest_key_bias(h):
        query = i * TQ + lax.broadcasted_iota(I32, (1, TQ), 1)
        return (query - nearest).astype(F32) * (-(2.0 ** -(h + 1)) * LOG2E)

    _attend_heads(i, stored_bias, nearest_key_bias, qT_ref, k_ref, vT_ref, g_sc, acc_sc, o_ref)


def _dsa_attention(qT, ka, vT, iqT, iwT, small):
    B, _, _, L = qT.shape
    return pl.pallas_call(
        _dsa_kernel,
        grid=(B, L // TQ),
        in_specs=[pl.BlockSpec((1, HEADS, LANES, TQ), lambda b, i: (b, 0, 0, i))] + _head_specs(L, lambda b, i: b)
        + [pl.BlockSpec((1, PAIRS, LANES, TQ), lambda b, i: (b, 0, 0, i)),
           pl.BlockSpec((1, HEADS, TQ), lambda b, i: (b, 0, i)),
           pl.BlockSpec((1, L, LANES), lambda b, i: (b, 0, 0))],
        out_specs=pl.BlockSpec((1, TQ, PAIRS * LANES), lambda b, i: (b, i, 0)),
        out_shape=jax.ShapeDtypeStruct((B, L, PAIRS * LANES), BF16),
        scratch_shapes=[pltpu.VMEM((L, TQ), I32), pltpu.VMEM((L, TQ), I16), pltpu.VMEM((L, TQ), I16)]
        + _head_scratch(),
        compiler_params=_params(("parallel", "arbitrary")),
    )(qT, ka, vT, iqT, iwT, small)


def _layer_norm(z, g, b):
    mu = jnp.mean(z, axis=-1, keepdims=True)
    zc = z - mu
    var = jnp.mean(zc * zc, axis=-1, keepdims=True)
    return zc * lax.rsqrt(var + LN_EPS) * g + b


FF_CHUNK = 256


def _merge_ffn_kernel(x_ref, oa_ref, ob_ref, ga_ref, gb_ref, wa_ref, wb_ref, wo_ref, g1_ref, b1_ref,
                      wi_ref, wf_ref, g2_ref, b2_ref, out_ref, h_sc, *, alpha):
    a = jnp.dot(oa_ref[0], wa_ref[...], preferred_element_type=F32)
    b = jnp.dot(ob_ref[0], wb_ref[...], preferred_element_type=F32)
    n_gate_blocks = D_MODEL // LANES
    gate_a = jnp.concatenate([ga_ref[0, c] for c in range(n_gate_blocks)], axis=1).astype(F32)
    gate_b = jnp.concatenate([gb_ref[0, c] for c in range(n_gate_blocks)], axis=1).astype(F32)
    merged = jax.nn.sigmoid(gate_a) * a + jax.nn.sigmoid(gate_b) * b
    y = jnp.dot(merged.astype(BF16), wo_ref[...], preferred_element_type=F32)
    x = _layer_norm(alpha * x_ref[0] + y, g1_ref[...], b1_ref[...])
    xb = x.astype(BF16)
    for c in range(D_FF // FF_CHUNK):
        lo = c * FF_CHUNK
        gate = jnp.dot(xb, wi_ref[:, lo:lo + FF_CHUNK], preferred_element_type=F32)
        up = jnp.dot(xb, wi_ref[:, D_FF + lo:D_FF + lo + FF_CHUNK], preferred_element_type=F32)
        h_sc[:, lo:lo + FF_CHUNK] = (jax.nn.silu(gate) * up).astype(BF16)
    y = jnp.dot(h_sc[...], wf_ref[...], preferred_element_type=F32)
    out_ref[0] = _layer_norm(alpha * x + y, g2_ref[...], b2_ref[...])


def _merge_ffn(x, oa, ob, gates, stacked, layer, alpha):
    B, L, D = x.shape
    tm = ROW_TILE
    nb = D // LANES
    row = lambda b_, i: (b_, i, 0)
    return pl.pallas_call(
        functools.partial(_merge_ffn_kernel, alpha=alpha),
        grid=(B, L // tm),
        in_specs=[pl.BlockSpec((1, tm, D), row),
                  pl.BlockSpec((1, tm, PAIRS * LANES), row),
                  pl.BlockSpec((1, tm, PAIRS * LANES), row),
                  pl.BlockSpec((1, nb, tm, LANES), lambda b_, i: (b_, _CB_GA // nb, i, 0)),
                  pl.BlockSpec((1, nb, tm, LANES), lambda b_, i: (b_, _CB_GB // nb, i, 0))]
        + [_layer_spec(p, layer) for p in stacked],
        out_specs=pl.BlockSpec((1, tm, D), row),
        out_shape=jax.ShapeDtypeStruct((B, L, D), F32),
        scratch_shapes=[pltpu.VMEM((tm, D_FF), BF16)],
        compiler_params=_params(("parallel", "parallel")),
    )(x, oa, ob, gates, gates, *stacked)


def _split_w_in(w):
    fq_fk = w[..., 0:2 * GROUP]
    fv = w[..., 2 * GROUP:_OFF_FLOGIT]
    dsa_iq = w[..., _OFF_DQ:_OFF_IK]
    gates = w[..., _OFF_GA:_N_IN]
    w_main = jnp.concatenate([fq_fk, gates, fv, dsa_iq], axis=-1).astype(BF16)
    pad = jnp.zeros(w.shape[:-1] + (LANES - HEAD_DIM - 2 * HEADS,), w.dtype)
    w_small = jnp.concatenate([w[..., _OFF_IK:_OFF_IW], w[..., _OFF_IW:_OFF_GA],
                               w[..., _OFF_FLOGIT:_OFF_DQ], pad], axis=-1).astype(BF16)
    return w_main, w_small


def kernel(x, w_in, b_forget, w_branch_a, w_branch_b, w_out, ln1_g, ln1_b, w_ffn_in, w_ffn_out, ln2_g, ln2_b):
    depth = w_in.shape[0]
    alpha = (2.0 * depth) ** 0.25
    w_main, w_small = _split_w_in(w_in)
    bias_rows = jnp.zeros((depth, 1, LANES), F32).at[:, 0, _SM_FL:_SM_FL + HEADS].set(b_forget)
    wa, wb, wo = w_branch_a.astype(BF16), w_branch_b.astype(BF16), w_out.astype(BF16)
    wi, wf = w_ffn_in.astype(BF16), w_ffn_out.astype(BF16)
    row = lambda p: p[:, None, :]
    stacked = (wa, wb, wo, row(ln1_g), row(ln1_b), wi, wf, row(ln2_g), row(ln2_b))
    for l in range(depth):
        gates, small, fqT, fka, f_own, fvT, dqT, dka, dvT, iqT, iwT = _project(x, w_main, w_small, bias_rows, l)
        o_a = _fox_attention(fqT, f_own, fka, fvT)
        o_b = _dsa_attention(dqT, dka, dvT, iqT, iwT, small)
        x = _merge_ffn(x, o_a, o_b, gates, stacked, l, alpha)
    return x
```

```python
import functools

import numpy as np
import jax
import jax.numpy as jnp
from jax import lax
from jax.experimental import pallas as pl
from jax.experimental.pallas import tpu as pltpu

F32 = jnp.float32
BF16 = jnp.bfloat16
I32 = jnp.int32

D_MODEL = 1024
HEADS = 8
HEAD_DIM = 64
PAIRS = HEADS // 2
D_FF = 2816
TOPK = 256
LANES = 128
NEG_INF = -1e30
LN_EPS = 1e-5
ATTN_SCALE = HEAD_DIM ** -0.5
IDX_SCALE = (HEADS ** -0.5) * (HEAD_DIM ** -0.5)

_OFF_FLOGIT = 1536
_OFF_DQ = 1544
_OFF_IK = 3592
_OFF_IW = 3656
_OFF_GA = 3664
_N_IN = 5712

_G_FQ, _G_FK, _G_GATES, _G_FV, _G_DQ, _G_DK, _G_DV, _G_IQ = 0, 1, (2, 3, 4, 5), 6, 7, 8, 9, 10
_CB_GA, _CB_GB = 0, 8
N_GATE_BLOCKS = 16
GROUP = 4 * LANES
N_MAIN = 11 * GROUP
_SM_IK, _SM_IW, _SM_FL = 0, 64, 72

ROW_TILE = 512
TQ = 512
TK = 512
CUMSUM_BLOCK = 256
V_ROWS = HEAD_DIM + 16
LOG2E = 1.4426950408889634

INT_MIN = -(2 ** 31)


def _monotone_key_of(value):
    b = int(np.float32(value).view(np.int32))
    return (b ^ ((b >> 31) & 0x7FFFFFFF)) - (b >> 31)


NEG_INF_KEY = _monotone_key_of(NEG_INF)
VMEM_LIMIT = 60 * 1024 * 1024


def _params(sem):
    return pltpu.CompilerParams(dimension_semantics=sem, vmem_limit_bytes=VMEM_LIMIT)


def _const_spec(shape):
    nd = len(shape)
    return pl.BlockSpec(shape, lambda *_: (0,) * nd, pipeline_mode=pl.Buffered(1))


def _layer_spec(stacked, layer):
    tail = stacked.shape[1:]
    return pl.BlockSpec((None,) + tail, lambda *_: (layer,) + (0,) * len(tail), pipeline_mode=pl.Buffered(1))


def _lane_row(lo, hi, value=1.0):
    lane = lax.broadcasted_iota(I32, (1, LANES), 1)
    return jnp.where((lane >= lo) & (lane < hi), value, 0.0)


BIAS_COLS = 8


def _alibi_placement():
    put = np.zeros((LANES, 2 * LANES), np.float32)
    for h in range(HEADS):
        slope = 2.0 ** -(h + 1)
        qc, kc = BIAS_COLS * h, LANES + BIAS_COLS * h
        put[0, qc], put[1, qc + 1], put[2, qc + 2], put[2, qc + 3] = -64.0 * slope, -slope, 64.0 * slope, slope
        put[2, kc], put[2, kc + 1], put[0, kc + 2], put[1, kc + 3] = 1.0, 1.0, 1.0, 1.0
    return jnp.asarray(put, BF16)


def _forget_placement():
    put = np.zeros((LANES, 2 * LANES), np.float32)
    for h in range(HEADS):
        for part in range(3):
            src = _SM_FL + part * HEADS + h
            put[src, BIAS_COLS * h + part] = 1.0
            put[src, LANES + BIAS_COLS * h + 3 + part] = -1.0
    return jnp.asarray(put, BF16)


def _proj_kernel(x_ref, wm_ref, ws_ref, alibi_ref, forget_ref, bias_ref,
                 gates_ref, os_ref, fqT_ref, fka_ref, own_ref, fvT_ref, dqT_ref, dka_ref, dvT_ref, iqT_ref, iwT_ref,
                 carry_sc):
    tm = x_ref.shape[1]
    xb = x_ref[0].astype(BF16)

    def group(g):
        return jnp.dot(xb, wm_ref[:, g * GROUP:(g + 1) * GROUP], preferred_element_type=F32)

    def store_gates(n):
        r = group(_G_GATES[n]).astype(BF16)
        for c in range(4):
            gates_ref[0, 4 * n + c] = r[:, c * LANES:(c + 1) * LANES]

    def pair_and_half(h):
        p, e = divmod(h, 2)
        return slice(p * LANES, (p + 1) * LANES), _lane_row(e * HEAD_DIM, (e + 1) * HEAD_DIM), (1 - e) * HEAD_DIM

    def bias_columns(compact, h, first, n):
        base = (1 - h % 2) * HEAD_DIM
        moved = pltpu.roll(compact, (base - BIAS_COLS * h) % LANES, 1)
        return moved * _lane_row(base + first, base + first + n)

    @pl.when(pl.program_id(1) == 0)
    def _new_sequence():
        carry_sc[...] = jnp.zeros(carry_sc.shape, F32)

    store_gates(0)
    small = jnp.dot(xb, ws_ref[...], preferred_element_type=F32)
    os_ref[0] = small
    iwT_ref[0] = small.T[_SM_IW:_SM_IW + HEADS, :] * IDX_SCALE
    c = _cumsum_rows(jax.nn.log_sigmoid(small + bias_ref[...])) + carry_sc[...]
    carry_sc[...] = c[tm - 1:tm, :]
    hi, mid, lo = _split3(c)
    lane = lax.broadcasted_iota(I32, (tm, LANES), 1)
    parts = jnp.where(lane < _SM_FL + HEADS, hi,
                      jnp.where(lane < _SM_FL + 2 * HEADS, pltpu.roll(mid, HEADS, 1), pltpu.roll(lo, 2 * HEADS, 1)))
    store_gates(1)
    aug = jnp.dot(parts.astype(BF16), forget_ref[...], preferred_element_type=F32)
    q = group(_G_FQ) * ATTN_SCALE
    k = group(_G_FK)
    for h in range(HEADS):
        pair, data, base = pair_and_half(h)
        q_aug = q[:, pair] * data + bias_columns(aug[:, :LANES], h, 0, 3) + _lane_row(base + 3, base + 6)
        k_aug = k[:, pair] * data + bias_columns(aug[:, LANES:], h, 3, 3) + _lane_row(base, base + 3)
        fqT_ref[0, h] = q_aug.T.astype(BF16)
        fka_ref[0, h] = k_aug.astype(BF16)
        own_ref[0, h] = jnp.sum((q_aug * k_aug).T, axis=0, keepdims=True) * LOG2E

    def store_values(ref, r):
        ones = jnp.ones((V_ROWS - HEAD_DIM, tm), BF16)
        for p in range(PAIRS):
            vT = r[:, p * LANES:(p + 1) * LANES].T.astype(BF16)
            for e in range(2):
                ref[0, 2 * p + e, 0] = jnp.concatenate([vT[e * HEAD_DIM:(e + 1) * HEAD_DIM], ones], axis=0)

    store_values(fvT_ref, group(_G_FV))
    store_gates(2)
    store_values(dvT_ref, group(_G_DV))

    lane = lax.broadcasted_iota(I32, (tm, LANES), 1)
    pos = lax.broadcasted_iota(I32, (tm, LANES), 0) + pl.program_id(1) * tm
    pos_terms = jnp.where(lane == 0, lax.shift_right_logical(pos, 6),
                          jnp.where(lane == 1, pos & 63, jnp.where(lane == 2, 1, 0))).astype(F32).astype(BF16)
    aug = jnp.dot(pos_terms, alibi_ref[...], preferred_element_type=F32)
    q = group(_G_DQ) * ATTN_SCALE
    k = group(_G_DK)
    for h in range(HEADS):
        pair, data, _ = pair_and_half(h)
        q_aug = q[:, pair] * data + bias_columns(aug[:, :LANES], h, 0, 4)
        k_aug = k[:, pair] * data + bias_columns(aug[:, LANES:], h, 0, 4)
        dqT_ref[0, h] = q_aug.T.astype(BF16)
        dka_ref[0, h] = k_aug.astype(BF16)
    iq = group(_G_IQ)
    for p in range(PAIRS):
        iqT_ref[0, p] = iq[:, p * LANES:(p + 1) * LANES].T.astype(BF16)
    store_gates(3)


def _project(x, w_main, w_small, bias_row, layer):
    B, L, D = x.shape
    tm = TK
    nk = L // TK
    alibi, forget = _alibi_placement(), _forget_placement()
    q_t = pl.BlockSpec((1, HEADS, LANES, tm), lambda b, i: (b, 0, 0, i))
    k_aug = pl.BlockSpec((1, HEADS, tm, LANES), lambda b, i: (b, 0, i, 0))
    v_t = pl.BlockSpec((1, HEADS, 1, V_ROWS, TK), lambda b, i: (b, 0, i, 0, 0))
    q_t_shape = jax.ShapeDtypeStruct((B, HEADS, LANES, L), BF16)
    k_aug_shape = jax.ShapeDtypeStruct((B, HEADS, L, LANES), BF16)
    v_t_shape = jax.ShapeDtypeStruct((B, HEADS, nk, V_ROWS, TK), BF16)
    return pl.pallas_call(
        _proj_kernel,
        grid=(B, nk),
        in_specs=[pl.BlockSpec((1, tm, D), lambda b, i: (b, i, 0)),
                  _layer_spec(w_main, layer), _layer_spec(w_small, layer),
                  _const_spec(alibi.shape), _const_spec(forget.shape), _layer_spec(bias_row, layer)],
        out_specs=[pl.BlockSpec((1, N_GATE_BLOCKS, tm, LANES), lambda b, i: (b, 0, i, 0)),
                   pl.BlockSpec((1, tm, LANES), lambda b, i: (b, i, 0)),
                   q_t, k_aug, pl.BlockSpec((1, HEADS, 1, tm), lambda b, i: (b, 0, 0, i)), v_t,
                   q_t, k_aug, v_t,
                   pl.BlockSpec((1, PAIRS, LANES, tm), lambda b, i: (b, 0, 0, i)),
                   pl.BlockSpec((1, HEADS, tm), lambda b, i: (b, 0, i))],
        out_shape=[jax.ShapeDtypeStruct((B, N_GATE_BLOCKS, L, LANES), BF16),
                   jax.ShapeDtypeStruct((B, L, LANES), F32),
                   q_t_shape, k_aug_shape, jax.ShapeDtypeStruct((B, HEADS, 1, L), F32), v_t_shape,
                   q_t_shape, k_aug_shape, v_t_shape,
                   jax.ShapeDtypeStruct((B, PAIRS, LANES, L), BF16),
                   jax.ShapeDtypeStruct((B, HEADS, L), F32)],
        scratch_shapes=[pltpu.VMEM((1, LANES), F32)],
        compiler_params=_params(("parallel", "arbitrary")),
    )(x, w_main, w_small, alibi, forget, bias_row)


def _split3(v):
    hi = v.astype(BF16).astype(F32)
    r = v - hi
    mid = r.astype(BF16).astype(F32)
    lo = (r - mid).astype(BF16).astype(F32)
    return hi, mid, lo


def _cumsum_rows(x):
    n = CUMSUM_BLOCK
    L = x.shape[0]
    r = lax.broadcasted_iota(I32, (n, n), 0)
    c = lax.broadcasted_iota(I32, (n, n), 1)
    tri = jnp.where(c <= r, 1.0, 0.0).astype(BF16)
    carry = jnp.zeros((1, x.shape[1]), F32)
    outs = []
    w = x.shape[1]
    for blk in range(L // n):
        parts = jnp.concatenate([p.astype(BF16) for p in _split3(x[blk * n:(blk + 1) * n])], axis=1)
        y3 = jnp.dot(tri, parts, preferred_element_type=F32)
        y = (y3[:, :w] + y3[:, w:2 * w] + y3[:, 2 * w:]) + carry
        carry = y[n - 1:n, :]
        outs.append(y)
    return jnp.concatenate(outs, axis=0)


def _fold8(x, op):
    return op(x.reshape(x.shape[0] // 8, 8, x.shape[1]), axis=0)


def _causal(shape):
    return lax.broadcasted_iota(I32, shape, 0) <= lax.broadcasted_iota(I32, shape, 1)


L_SUM_MIN, L_SUM_MAX = 1e-30, 1e30
NEAR_BLOCK = 64


def _attend_heads(n_full, bias_fn, ref_score_fn, qT_ref, k_ref, vT_ref, g_sc, acc_sc, o_ref):
    def for_chunks(chunk):
        def pair(jj, carry):
            chunk(2 * jj, False)
            chunk(2 * jj + 1, False)
            return carry

        def single(_, carry):
            chunk(n_full - 1, False)
            return carry

        lax.fori_loop(0, lax.shift_right_logical(n_full, 1), pair, 0)
        lax.fori_loop(0, n_full & 1, single, 0)
        chunk(n_full, True)

    def scores(h, rows, diag, raw):
        s = raw * LOG2E
        bias = bias_fn(rows, diag)
        return s if bias is None else s + bias

    def sweep():
        acc_sc[...] = jnp.zeros(acc_sc.shape, F32)

        def chunk(j, diag):
            rows = pl.ds(pl.multiple_of(j * TK, TK), TK)

            def raw_scores(h):
                return jnp.dot(k_ref[0, h, rows, :], qT_ref[0, h], preferred_element_type=F32)

            nxt = raw_scores(0)
            for h in range(HEADS):
                raw = nxt
                if h + 1 < HEADS:
                    nxt = raw_scores(h + 1)
                p = jnp.exp2(scores(h, rows, diag, raw) - g_sc[h]).astype(BF16)
                acc_sc[h] += jnp.dot(vT_ref[0, h, j], p, preferred_element_type=F32)

        for_chunks(chunk)
        bad = jnp.zeros((1, TQ), I32)
        for p in range(PAIRS):
            halves = []
            for e in range(2):
                acc = acc_sc[2 * p + e]
                l = acc[HEAD_DIM:HEAD_DIM + 1]
                bad = jnp.maximum(bad, jnp.where((l > L_SUM_MIN) & (l < L_SUM_MAX), 0, 1))
                halves.append(acc[:HEAD_DIM] / l)
            o_ref[0, :, p * LANES:(p + 1) * LANES] = jnp.concatenate(halves, axis=0).T.astype(o_ref.dtype)
        return jnp.max(bad) > 0

    def exact_row_maxima():
        acc_sc[:, 0:8, :] = jnp.full((HEADS, 8, TQ), NEG_INF, F32)

        def chunk(j, diag):
            rows = pl.ds(pl.multiple_of(j * TK, TK), TK)
            for h in range(HEADS):
                raw = jnp.dot(k_ref[0, h, rows, :], qT_ref[0, h], preferred_element_type=F32)
                acc_sc[h, 0:8, :] = jnp.maximum(acc_sc[h, 0:8, :], _fold8(scores(h, rows, diag, raw), jnp.max))

        for_chunks(chunk)
        for h in range(HEADS):
            g_sc[h] = jnp.max(acc_sc[h, 0:8, :], axis=0, keepdims=True)

    for h in range(HEADS):
        g_sc[h] = ref_score_fn(h)
    unsafe = sweep()

    @pl.when(unsafe)
    def _redo_with_exact_maxima():
        exact_row_maxima()
        sweep()


def _head_specs(L, index):
    nk = L // TK
    return [pl.BlockSpec((1, HEADS, L, LANES), lambda *g: (index(*g), 0, 0, 0)),
            pl.BlockSpec((1, HEADS, nk, V_ROWS, TK), lambda *g: (index(*g), 0, 0, 0, 0))]


def _head_scratch():
    return [pltpu.VMEM((HEADS, 1, TQ), F32), pltpu.VMEM((HEADS, V_ROWS, TQ), F32)]


def _fox_kernel(qT_ref, own_ref, k_ref, vT_ref, o_ref, g_sc, acc_sc):
    i = pl.program_id(1)

    def bias_fn(rows, diag):
        return jnp.where(_causal((TK, TQ)), 0.0, NEG_INF) if diag else None

    _attend_heads(i, bias_fn, lambda h: own_ref[0, h], qT_ref, k_ref, vT_ref, g_sc, acc_sc, o_ref)


def _fox_attention(qT, own, ka, vT):
    B, _, _, L = qT.shape
    return pl.pallas_call(
        _fox_kernel,
        grid=(B, L // TQ),
        in_specs=[pl.BlockSpec((1, HEADS, LANES, TQ), lambda b, i: (b, 0, 0, i)),
                  pl.BlockSpec((1, HEADS, 1, TQ), lambda b, i: (b, 0, 0, i))] + _head_specs(L, lambda b, i: b),
        out_specs=pl.BlockSpec((1, TQ, PAIRS * LANES), lambda b, i: (b, i, 0)),
        out_shape=jax.ShapeDtypeStruct((B, L, PAIRS * LANES), BF16),
        scratch_shapes=_head_scratch(),
        compiler_params=_params(("parallel", "arbitrary")),
    )(qT, own, ka, vT)


def _count_rows(mask):
    return _fold8(jnp.where(mask, 1, 0).astype(I32), jnp.sum)


WORD_BITS = 32
PLANE_ROWS = TK // WORD_BITS
SUBLANES = 8


def _transpose_bits(words):
    a = list(words)
    shift, mask = WORD_BITS // 2, 0x0000FFFF
    while shift:
        k = 0
        while k < WORD_BITS:
            t = (a[k] ^ lax.shift_right_logical(a[k + shift], shift)) & mask
            a[k] = a[k] ^ t
            a[k + shift] = a[k + shift] ^ lax.shift_left(t, shift)
            k = (k + shift + 1) & ~shift
        shift >>= 1
        mask ^= mask << shift
    return a


def _dsa_kernel(qT_ref, k_ref, vT_ref, iqT_ref, iwT_ref, s_ref, o_ref,
                keys_sc, planes_sc, alive_sc, g_sc, acc_sc):
    i = pl.program_id(1)
    n_chunks = i + 1

    def plane_rows(j):
        return pl.ds(pl.multiple_of(j * PLANE_ROWS, PLANE_ROWS), PLANE_ROWS)

    def _select():
        wts = iwT_ref[0]

        def score_chunk(j, masked):
            ks = pl.multiple_of(j * TK, TK)
            ik = s_ref[0, pl.ds(ks, TK), :][:, _SM_IK:_SM_IK + HEAD_DIM].astype(BF16)
            acc = jnp.zeros((TK, TQ), F32)
            for h in range(HEADS):
                iq = iqT_ref[0, h // 2, (h % 2) * HEAD_DIM:(h % 2 + 1) * HEAD_DIM, :]
                rel = jnp.maximum(jnp.dot(ik, iq, preferred_element_type=F32), 0.0)
                acc = acc + wts[h:h + 1, :] * rel
            if masked:
                acc = jnp.where(_causal(acc.shape), acc, NEG_INF)
            bits = pltpu.bitcast(acc, I32)
            sign = lax.shift_right_arithmetic(bits, 31)
            key = (bits ^ (sign & 0x7FFFFFFF)) - sign
            keys_sc[pl.ds(ks, TK), :] = key
            ordered = key ^ INT_MIN
            half = WORD_BITS * SUBLANES
            for hf in range(TK // half):
                for lt in range(TQ // LANES):
                    lanes = slice(lt * LANES, (lt + 1) * LANES)
                    words = [ordered[hf * half + SUBLANES * m:hf * half + SUBLANES * (m + 1), lanes]
                             for m in range(WORD_BITS)]
                    rows = pl.ds(pl.multiple_of(j * PLANE_ROWS + hf * SUBLANES, SUBLANES), SUBLANES)
                    for b, plane in enumerate(_transpose_bits(words)):
                        planes_sc[b, rows, lanes] = plane
            alive_sc[plane_rows(j), :] = jnp.full((PLANE_ROWS, TQ), -1, I32)

        def score_body(j, carry):
            score_chunk(j, False)
            return carry

        lax.fori_loop(0, i, score_body, 0)
        score_chunk(i, True)

        def count(pred):
            def body(j, cnt):
                ks = pl.multiple_of(j * TK, TK)
                return cnt + _count_rows(pred(keys_sc[pl.ds(ks, TK), :]))
            cnt8 = lax.fori_loop(0, n_chunks, body, jnp.zeros((8, TQ), I32))
            return jnp.sum(cnt8, axis=0, keepdims=True)

        def bit_pass(bi, carry):
            need, thr_bits, flip = carry
            prev = jnp.maximum(bi - 1, 0)
            keep_all = jnp.where(bi == 0, -1, 0)

            def body(j, cnt):
                rows = plane_rows(j)
                alive = alive_sc[rows, :] & ((planes_sc[prev, rows, :] ^ flip) | keep_all)
                alive_sc[rows, :] = alive
                return cnt + lax.population_count(alive & planes_sc[bi, rows, :])

            cnt = lax.fori_loop(0, n_chunks, body, jnp.zeros((PLANE_ROWS, TQ), I32))
            ones_here = jnp.sum(cnt, axis=0, keepdims=True)
            ok = ones_here >= need
            bit = jnp.left_shift(jnp.ones((1, TQ), I32), WORD_BITS - 1 - bi)
            return (jnp.where(ok, need, need - ones_here), jnp.where(ok, thr_bits | bit, thr_bits),
                    jnp.where(ok, 0, -1))

        need, thr_bits, flip = lax.fori_loop(
            0, WORD_BITS, bit_pass,
            (jnp.full((1, TQ), TOPK, I32), jnp.zeros((1, TQ), I32), jnp.zeros((1, TQ), I32)))
        thr = thr_bits ^ INT_MIN

        def count_equal(j, cnt):
            rows = plane_rows(j)
            alive = alive_sc[rows, :] & (planes_sc[WORD_BITS - 1, rows, :] ^ flip)
            return cnt + lax.population_count(alive)

        n_equal = jnp.sum(lax.fori_loop(0, n_chunks, count_equal, jnp.zeros((PLANE_ROWS, TQ), I32)),
                          axis=0, keepdims=True)

        excess = jnp.where((n_equal > need) & (thr > NEG_INF_KEY), 1, 0)

        @pl.when(jnp.max(excess) > 0)
        def _break_ties():
            need = TOPK - count(lambda kk: kk > thr)
            r = lax.broadcasted_iota(I32, (TK, TK), 0)
            c = lax.broadcasted_iota(I32, (TK, TK), 1)
            strict_lower = jnp.where(c < r, 1.0, 0.0).astype(BF16)

            def body(j, seen):
                ks = pl.multiple_of(j * TK, TK)
                kk = keys_sc[pl.ds(ks, TK), :]
                tie = kk == thr
                tie_f = jnp.where(tie, 1.0, 0.0)
                before = jnp.dot(strict_lower, tie_f.astype(BF16), preferred_element_type=F32) + seen
                drop = tie & (before >= need.astype(F32))
                keys_sc[pl.ds(ks, TK), :] = jnp.where(drop, INT_MIN, kk)
                return seen + jnp.sum(tie_f, axis=0, keepdims=True)

            lax.fori_loop(0, n_chunks, body, jnp.zeros((1, TQ), F32))

        def bias_chunk(j, masked, near8):
            ks = pl.multiple_of(j * TK, TK)
            bias = jnp.where(keys_sc[pl.ds(ks, TK), :] >= thr, 0.0, NEG_INF)
            if masked:
                bias = jnp.where(_causal(bias.shape), bias, NEG_INF)
            keys_sc[pl.ds(ks, TK), :] = pltpu.bitcast(bias, I32)
            for blk in range(TK // NEAR_BLOCK):
                any_selected = _fold8(bias[blk * NEAR_BLOCK:(blk + 1) * NEAR_BLOCK], jnp.max) == 0.0
                near8 = jnp.maximum(near8, jnp.where(any_selected, ks + (blk + 1) * NEAR_BLOCK - 1, -1))
            return near8

        near8 = lax.fori_loop(0, i, lambda j, n8: bias_chunk(j, False, n8), jnp.full((8, TQ), -1, I32))
        return jnp.max(bias_chunk(i, True, near8), axis=0, keepdims=True)

    nearest = _select()

    def stored_bias(rows, diag):
        return pltpu.bitcast(keys_sc[rows, :], F32)

    def nearest_key_bias(h):
        query = i * TQ + lax.broadcasted_iota(I32, (1, TQ), 1)
        return (query - nearest).astype(F32) * (-(2.0 ** -(h + 1)) * LOG2E)

    _attend_heads(i, stored_bias, nearest_key_bias, qT_ref, k_ref, vT_ref, g_sc, acc_sc, o_ref)


def _dsa_attention(qT, ka, vT, iqT, iwT, small):
    B, _, _, L = qT.shape
    return pl.pallas_call(
        _dsa_kernel,
        grid=(B, L // TQ),
        in_specs=[pl.BlockSpec((1, HEADS, LANES, TQ), lambda b, i: (b, 0, 0, i))] + _head_specs(L, lambda b, i: b)
        + [pl.BlockSpec((1, PAIRS, LANES, TQ), lambda b, i: (b, 0, 0, i)),
           pl.BlockSpec((1, HEADS, TQ), lambda b, i: (b, 0, i)),
           pl.BlockSpec((1, L, LANES), lambda b, i: (b, 0, 0))],
        out_specs=pl.BlockSpec((1, TQ, PAIRS * LANES), lambda b, i: (b, i, 0)),
        out_shape=jax.ShapeDtypeStruct((B, L, PAIRS * LANES), BF16),
        scratch_shapes=[pltpu.VMEM((L, TQ), I32), pltpu.VMEM((WORD_BITS, L // WORD_BITS, TQ), I32),
                        pltpu.VMEM((L // WORD_BITS, TQ), I32)]
        + _head_scratch(),
        compiler_params=_params(("parallel", "arbitrary")),
    )(qT, ka, vT, iqT, iwT, small)


def _layer_norm(z, g, b):
    mu = jnp.mean(z, axis=-1, keepdims=True)
    zc = z - mu
    var = jnp.mean(zc * zc, axis=-1, keepdims=True)
    return zc * lax.rsqrt(var + LN_EPS) * g + b


FF_CHUNK = 256


def _merge_ffn_kernel(x_ref, oa_ref, ob_ref, ga_ref, gb_ref, wa_ref, wb_ref, wo_ref, g1_ref, b1_ref,
                      wi_ref, wf_ref, g2_ref, b2_ref, out_ref, h_sc, *, alpha):
    a = jnp.dot(oa_ref[0], wa_ref[...], preferred_element_type=F32)
    b = jnp.dot(ob_ref[0], wb_ref[...], preferred_element_type=F32)
    n_gate_blocks = D_MODEL // LANES
    gate_a = jnp.concatenate([ga_ref[0, c] for c in range(n_gate_blocks)], axis=1).astype(F32)
    gate_b = jnp.concatenate([gb_ref[0, c] for c in range(n_gate_blocks)], axis=1).astype(F32)
    merged = jax.nn.sigmoid(gate_a) * a + jax.nn.sigmoid(gate_b) * b
    y = jnp.dot(merged.astype(BF16), wo_ref[...], preferred_element_type=F32)
    x = _layer_norm(alpha * x_ref[0] + y, g1_ref[...], b1_ref[...])
    xb = x.astype(BF16)
    for c in range(D_FF // FF_CHUNK):
        lo = c * FF_CHUNK
        gate = jnp.dot(xb, wi_ref[:, lo:lo + FF_CHUNK], preferred_element_type=F32)
        up = jnp.dot(xb, wi_ref[:, D_FF + lo:D_FF + lo + FF_CHUNK], preferred_element_type=F32)
        h_sc[:, lo:lo + FF_CHUNK] = (jax.nn.silu(gate) * up).astype(BF16)
    y = jnp.dot(h_sc[...], wf_ref[...], preferred_element_type=F32)
    out_ref[0] = _layer_norm(alpha * x + y, g2_ref[...], b2_ref[...])


def _merge_ffn(x, oa, ob, gates, stacked, layer, alpha):
    B, L, D = x.shape
    tm = ROW_TILE
    nb = D // LANES
    row = lambda b_, i: (b_, i, 0)
    return pl.pallas_call(
        functools.partial(_merge_ffn_kernel, alpha=alpha),
        grid=(B, L // tm),
        in_specs=[pl.BlockSpec((1, tm, D), row),
                  pl.BlockSpec((1, tm, PAIRS * LANES), row),
                  pl.BlockSpec((1, tm, PAIRS * LANES), row),
                  pl.BlockSpec((1, nb, tm, LANES), lambda b_, i: (b_, _CB_GA // nb, i, 0)),
                  pl.BlockSpec((1, nb, tm, LANES), lambda b_, i: (b_, _CB_GB // nb, i, 0))]
        + [_layer_spec(p, layer) for p in stacked],
        out_specs=pl.BlockSpec((1, tm, D), row),
        out_shape=jax.ShapeDtypeStruct((B, L, D), F32),
        scratch_shapes=[pltpu.VMEM((tm, D_FF), BF16)],
        compiler_params=_params(("parallel", "parallel")),
    )(x, oa, ob, gates, gates, *stacked)


def _split_w_in(w):
    fq_fk = w[..., 0:2 * GROUP]
    fv = w[..., 2 * GROUP:_OFF_FLOGIT]
    dsa_iq = w[..., _OFF_DQ:_OFF_IK]
    gates = w[..., _OFF_GA:_N_IN]
    w_main = jnp.concatenate([fq_fk, gates, fv, dsa_iq], axis=-1).astype(BF16)
    pad = jnp.zeros(w.shape[:-1] + (LANES - HEAD_DIM - 2 * HEADS,), w.dtype)
    w_small = jnp.concatenate([w[..., _OFF_IK:_OFF_IW], w[..., _OFF_IW:_OFF_GA],
                               w[..., _OFF_FLOGIT:_OFF_DQ], pad], axis=-1).astype(BF16)
    return w_main, w_small


def kernel(x, w_in, b_forget, w_branch_a, w_branch_b, w_out, ln1_g, ln1_b, w_ffn_in, w_ffn_out, ln2_g, ln2_b):
    depth = w_in.shape[0]
    alpha = (2.0 * depth) ** 0.25
    w_main, w_small = _split_w_in(w_in)
    bias_rows = jnp.zeros((depth, 1, LANES), F32).at[:, 0, _SM_FL:_SM_FL + HEADS].set(b_forget)
    wa, wb, wo = w_branch_a.astype(BF16), w_branch_b.astype(BF16), w_out.astype(BF16)
    wi, wf = w_ffn_in.astype(BF16), w_ffn_out.astype(BF16)
    row = lambda p: p[:, None, :]
    stacked = (wa, wb, wo, row(ln1_g), row(ln1_b), wi, wf, row(ln2_g), row(ln2_b))
    for l in range(depth):
        gates, small, fqT, fka, f_own, fvT, dqT, dka, dvT, iqT, iwT = _project(x, w_main, w_small, bias_rows, l)
        o_a = _fox_attention(fqT, f_own, fka, fvT)
        o_b = _dsa_attention(dqT, dka, dvT, iqT, iwT, small)
        x = _merge_ffn(x, o_a, o_b, gates, stacked, l, alpha)
    return x
```

```python
import functools

import numpy as np
import jax
import jax.numpy as jnp
from jax import lax
from jax.experimental import pallas as pl
from jax.experimental.pallas import tpu as pltpu

F32 = jnp.float32
BF16 = jnp.bfloat16
I32 = jnp.int32

D_MODEL = 1024
HEADS = 8
HEAD_DIM = 64
PAIRS = HEADS // 2
D_FF = 2816
TOPK = 256
LANES = 128
NEG_INF = -1e30
LN_EPS = 1e-5
ATTN_SCALE = HEAD_DIM ** -0.5
IDX_SCALE = (HEADS ** -0.5) * (HEAD_DIM ** -0.5)

_OFF_FLOGIT = 1536
_OFF_DQ = 1544
_OFF_IK = 3592
_OFF_IW = 3656
_OFF_GA = 3664
_N_IN = 5712

_G_FQ, _G_FK, _G_GATES, _G_FV, _G_DQ, _G_DK, _G_DV, _G_IQ = 0, 1, (2, 3, 4, 5), 6, 7, 8, 9, 10
_CB_GA, _CB_GB = 0, 8
N_GATE_BLOCKS = 16
GROUP = 4 * LANES
N_MAIN = 11 * GROUP
_SM_IK, _SM_IW, _SM_FL = 0, 64, 72

ROW_TILE = 512
TQ = 512
TK = 512
CUMSUM_BLOCK = 256
V_ROWS = HEAD_DIM + 16
LOG2E = 1.4426950408889634

INT_MIN = -(2 ** 31)


def _monotone_key_of(value):
    b = int(np.float32(value).view(np.int32))
    return (b ^ ((b >> 31) & 0x7FFFFFFF)) - (b >> 31)


NEG_INF_KEY = _monotone_key_of(NEG_INF)
VMEM_LIMIT = 60 * 1024 * 1024


def _params(sem):
    return pltpu.CompilerParams(dimension_semantics=sem, vmem_limit_bytes=VMEM_LIMIT)


def _const_spec(shape):
    nd = len(shape)
    return pl.BlockSpec(shape, lambda *_: (0,) * nd, pipeline_mode=pl.Buffered(1))


def _layer_spec(stacked, layer):
    tail = stacked.shape[1:]
    return pl.BlockSpec((None,) + tail, lambda *_: (layer,) + (0,) * len(tail), pipeline_mode=pl.Buffered(1))


def _lane_row(lo, hi, value=1.0):
    lane = lax.broadcasted_iota(I32, (1, LANES), 1)
    return jnp.where((lane >= lo) & (lane < hi), value, 0.0)


BIAS_COLS = 8


def _alibi_placement():
    put = np.zeros((LANES, 2 * LANES), np.float32)
    for h in range(HEADS):
        slope = 2.0 ** -(h + 1)
        qc, kc = BIAS_COLS * h, LANES + BIAS_COLS * h
        put[0, qc], put[1, qc + 1], put[2, qc + 2], put[2, qc + 3] = -64.0 * slope, -slope, 64.0 * slope, slope
        put[2, kc], put[2, kc + 1], put[0, kc + 2], put[1, kc + 3] = 1.0, 1.0, 1.0, 1.0
    return jnp.asarray(put, BF16)


def _forget_placement():
    put = np.zeros((LANES, 2 * LANES), np.float32)
    for h in range(HEADS):
        for part in range(3):
            src = _SM_FL + part * HEADS + h
            put[src, BIAS_COLS * h + part] = 1.0
            put[src, LANES + BIAS_COLS * h + 3 + part] = -1.0
    return jnp.asarray(put, BF16)


def _proj_kernel(x_ref, wm_ref, ws_ref, alibi_ref, forget_ref, bias_ref,
                 gates_ref, os_ref, fqT_ref, fka_ref, own_ref, fvT_ref, dqT_ref, dka_ref, dvT_ref, iqT_ref, iwT_ref,
                 carry_sc):
    tm = x_ref.shape[1]
    xb = x_ref[0].astype(BF16)

    def group(g):
        return jnp.dot(xb, wm_ref[:, g * GROUP:(g + 1) * GROUP], preferred_element_type=F32)

    def store_gates(n):
        r = group(_G_GATES[n]).astype(BF16)
        for c in range(4):
            gates_ref[0, 4 * n + c] = r[:, c * LANES:(c + 1) * LANES]

    def pair_and_half(h):
        p, e = divmod(h, 2)
        return slice(p * LANES, (p + 1) * LANES), _lane_row(e * HEAD_DIM, (e + 1) * HEAD_DIM), (1 - e) * HEAD_DIM

    def bias_columns(compact, h, first, n):
        base = (1 - h % 2) * HEAD_DIM
        moved = pltpu.roll(compact, (base - BIAS_COLS * h) % LANES, 1)
        return moved * _lane_row(base + first, base + first + n)

    @pl.when(pl.program_id(1) == 0)
    def _new_sequence():
        carry_sc[...] = jnp.zeros(carry_sc.shape, F32)

    store_gates(0)
    small = jnp.dot(xb, ws_ref[...], preferred_element_type=F32)
    os_ref[0] = small
    iwT_ref[0] = small.T[_SM_IW:_SM_IW + HEADS, :] * IDX_SCALE
    c = _cumsum_rows(jax.nn.log_sigmoid(small + bias_ref[...])) + carry_sc[...]
    carry_sc[...] = c[tm - 1:tm, :]
    hi, mid, lo = _split3(c)
    lane = lax.broadcasted_iota(I32, (tm, LANES), 1)
    parts = jnp.where(lane < _SM_FL + HEADS, hi,
                      jnp.where(lane < _SM_FL + 2 * HEADS, pltpu.roll(mid, HEADS, 1), pltpu.roll(lo, 2 * HEADS, 1)))
    store_gates(1)
    aug = jnp.dot(parts.astype(BF16), forget_ref[...], preferred_element_type=F32)
    q = group(_G_FQ) * ATTN_SCALE
    k = group(_G_FK)
    for h in range(HEADS):
        pair, data, base = pair_and_half(h)
        q_aug = q[:, pair] * data + bias_columns(aug[:, :LANES], h, 0, 3) + _lane_row(base + 3, base + 6)
        k_aug = k[:, pair] * data + bias_columns(aug[:, LANES:], h, 3, 3) + _lane_row(base, base + 3)
        fqT_ref[0, h] = q_aug.T.astype(BF16)
        fka_ref[0, h] = k_aug.astype(BF16)
        own_ref[0, h] = jnp.sum((q_aug * k_aug).T, axis=0, keepdims=True) * LOG2E

    def store_values(ref, r):
        ones = jnp.ones((V_ROWS - HEAD_DIM, tm), BF16)
        for p in range(PAIRS):
            vT = r[:, p * LANES:(p + 1) * LANES].T.astype(BF16)
            for e in range(2):
                ref[0, 2 * p + e, 0] = jnp.concatenate([vT[e * HEAD_DIM:(e + 1) * HEAD_DIM], ones], axis=0)

    store_values(fvT_ref, group(_G_FV))
    store_gates(2)
    store_values(dvT_ref, group(_G_DV))

    lane = lax.broadcasted_iota(I32, (tm, LANES), 1)
    pos = lax.broadcasted_iota(I32, (tm, LANES), 0) + pl.program_id(1) * tm
    pos_terms = jnp.where(lane == 0, lax.shift_right_logical(pos, 6),
                          jnp.where(lane == 1, pos & 63, jnp.where(lane == 2, 1, 0))).astype(F32).astype(BF16)
    aug = jnp.dot(pos_terms, alibi_ref[...], preferred_element_type=F32)
    q = group(_G_DQ) * ATTN_SCALE
    k = group(_G_DK)
    for h in range(HEADS):
        pair, data, _ = pair_and_half(h)
        q_aug = q[:, pair] * data + bias_columns(aug[:, :LANES], h, 0, 4)
        k_aug = k[:, pair] * data + bias_columns(aug[:, LANES:], h, 0, 4)
        dqT_ref[0, h] = q_aug.T.astype(BF16)
        dka_ref[0, h] = k_aug.astype(BF16)
    iq = group(_G_IQ)
    for p in range(PAIRS):
        iqT_ref[0, p] = iq[:, p * LANES:(p + 1) * LANES].T.astype(BF16)
    store_gates(3)


def _project(x, w_main, w_small, bias_row, layer):
    B, L, D = x.shape
    tm = TK
    nk = L // TK
    alibi, forget = _alibi_placement(), _forget_placement()
    q_t = pl.BlockSpec((1, HEADS, LANES, tm), lambda b, i: (b, 0, 0, i))
    k_aug = pl.BlockSpec((1, HEADS, tm, LANES), lambda b, i: (b, 0, i, 0))
    v_t = pl.BlockSpec((1, HEADS, 1, V_ROWS, TK), lambda b, i: (b, 0, i, 0, 0))
    q_t_shape = jax.ShapeDtypeStruct((B, HEADS, LANES, L), BF16)
    k_aug_shape = jax.ShapeDtypeStruct((B, HEADS, L, LANES), BF16)
    v_t_shape = jax.ShapeDtypeStruct((B, HEADS, nk, V_ROWS, TK), BF16)
    return pl.pallas_call(
        _proj_kernel,
        grid=(B, nk),
        in_specs=[pl.BlockSpec((1, tm, D), lambda b, i: (b, i, 0)),
                  _layer_spec(w_main, layer), _layer_spec(w_small, layer),
                  _const_spec(alibi.shape), _const_spec(forget.shape), _layer_spec(bias_row, layer)],
        out_specs=[pl.BlockSpec((1, N_GATE_BLOCKS, tm, LANES), lambda b, i: (b, 0, i, 0)),
                   pl.BlockSpec((1, tm, LANES), lambda b, i: (b, i, 0)),
                   q_t, k_aug, pl.BlockSpec((1, HEADS, 1, tm), lambda b, i: (b, 0, 0, i)), v_t,
                   q_t, k_aug, v_t,
                   pl.BlockSpec((1, PAIRS, LANES, tm), lambda b, i: (b, 0, 0, i)),
                   pl.BlockSpec((1, HEADS, tm), lambda b, i: (b, 0, i))],
        out_shape=[jax.ShapeDtypeStruct((B, N_GATE_BLOCKS, L, LANES), BF16),
                   jax.ShapeDtypeStruct((B, L, LANES), F32),
                   q_t_shape, k_aug_shape, jax.ShapeDtypeStruct((B, HEADS, 1, L), F32), v_t_shape,
                   q_t_shape, k_aug_shape, v_t_shape,
                   jax.ShapeDtypeStruct((B, PAIRS, LANES, L), BF16),
                   jax.ShapeDtypeStruct((B, HEADS, L), F32)],
        scratch_shapes=[pltpu.VMEM((1, LANES), F32)],
        compiler_params=_params(("parallel", "arbitrary")),
    )(x, w_main, w_small, alibi, forget, bias_row)


def _split3(v):
    hi = v.astype(BF16).astype(F32)
    r = v - hi
    mid = r.astype(BF16).astype(F32)
    lo = (r - mid).astype(BF16).astype(F32)
    return hi, mid, lo


def _cumsum_rows(x):
    n = CUMSUM_BLOCK
    L = x.shape[0]
    r = lax.broadcasted_iota(I32, (n, n), 0)
    c = lax.broadcasted_iota(I32, (n, n), 1)
    tri = jnp.where(c <= r, 1.0, 0.0).astype(BF16)
    carry = jnp.zeros((1, x.shape[1]), F32)
    outs = []
    w = x.shape[1]
    for blk in range(L // n):
        parts = jnp.concatenate([p.astype(BF16) for p in _split3(x[blk * n:(blk + 1) * n])], axis=1)
        y3 = jnp.dot(tri, parts, preferred_element_type=F32)
        y = (y3[:, :w] + y3[:, w:2 * w] + y3[:, 2 * w:]) + carry
        carry = y[n - 1:n, :]
        outs.append(y)
    return jnp.concatenate(outs, axis=0)


def _fold8(x, op):
    return op(x.reshape(x.shape[0] // 8, 8, x.shape[1]), axis=0)


def _causal(shape):
    return lax.broadcasted_iota(I32, shape, 0) <= lax.broadcasted_iota(I32, shape, 1)


L_SUM_MIN, L_SUM_MAX = 1e-30, 1e30
NEAR_BLOCK = 64


def _attend_heads(n_full, bias_fn, ref_score_fn, qT_ref, k_ref, vT_ref, g_sc, acc_sc, o_ref):
    def for_chunks(chunk):
        def pair(jj, carry):
            chunk(2 * jj, False)
            chunk(2 * jj + 1, False)
            return carry

        def single(_, carry):
            chunk(n_full - 1, False)
            return carry

        lax.fori_loop(0, lax.shift_right_logical(n_full, 1), pair, 0)
        lax.fori_loop(0, n_full & 1, single, 0)
        chunk(n_full, True)

    def scores(h, rows, diag, raw):
        s = raw * LOG2E
        bias = bias_fn(rows, diag)
        return s if bias is None else s + bias

    def sweep():
        acc_sc[...] = jnp.zeros(acc_sc.shape, F32)

        def chunk(j, diag):
            rows = pl.ds(pl.multiple_of(j * TK, TK), TK)

            def raw_scores(h):
                return jnp.dot(k_ref[0, h, rows, :], qT_ref[0, h], preferred_element_type=F32)

            nxt = raw_scores(0)
            for h in range(HEADS):
                raw = nxt
                if h + 1 < HEADS:
                    nxt = raw_scores(h + 1)
                p = jnp.exp2(scores(h, rows, diag, raw) - g_sc[h]).astype(BF16)
                acc_sc[h] += jnp.dot(vT_ref[0, h, j], p, preferred_element_type=F32)

        for_chunks(chunk)
        bad = jnp.zeros((1, TQ), I32)
        for p in range(PAIRS):
            halves = []
            for e in range(2):
                acc = acc_sc[2 * p + e]
                l = acc[HEAD_DIM:HEAD_DIM + 1]
                bad = jnp.maximum(bad, jnp.where((l > L_SUM_MIN) & (l < L_SUM_MAX), 0, 1))
                halves.append(acc[:HEAD_DIM] / l)
            o_ref[0, :, p * LANES:(p + 1) * LANES] = jnp.concatenate(halves, axis=0).T.astype(o_ref.dtype)
        return jnp.max(bad) > 0

    def exact_row_maxima():
        acc_sc[:, 0:8, :] = jnp.full((HEADS, 8, TQ), NEG_INF, F32)

        def chunk(j, diag):
            rows = pl.ds(pl.multiple_of(j * TK, TK), TK)
            for h in range(HEADS):
                raw = jnp.dot(k_ref[0, h, rows, :], qT_ref[0, h], preferred_element_type=F32)
                acc_sc[h, 0:8, :] = jnp.maximum(acc_sc[h, 0:8, :], _fold8(scores(h, rows, diag, raw), jnp.max))

        for_chunks(chunk)
        for h in range(HEADS):
            g_sc[h] = jnp.max(acc_sc[h, 0:8, :], axis=0, keepdims=True)

    for h in range(HEADS):
        g_sc[h] = ref_score_fn(h)
    unsafe = sweep()

    @pl.when(unsafe)
    def _redo_with_exact_maxima():
        exact_row_maxima()
        sweep()


def _head_specs(L, index):
    nk = L // TK
    return [pl.BlockSpec((1, HEADS, L, LANES), lambda *g: (index(*g), 0, 0, 0)),
            pl.BlockSpec((1, HEADS, nk, V_ROWS, TK), lambda *g: (index(*g), 0, 0, 0, 0))]


def _head_scratch():
    return [pltpu.VMEM((HEADS, 1, TQ), F32), pltpu.VMEM((HEADS, V_ROWS, TQ), F32)]


def _fox_kernel(qT_ref, own_ref, k_ref, vT_ref, o_ref, g_sc, acc_sc):
    i = pl.program_id(1)

    def bias_fn(rows, diag):
        return jnp.where(_causal((TK, TQ)), 0.0, NEG_INF) if diag else None

    _attend_heads(i, bias_fn, lambda h: own_ref[0, h], qT_ref, k_ref, vT_ref, g_sc, acc_sc, o_ref)


def _fox_attention(qT, own, ka, vT):
    B, _, _, L = qT.shape
    return pl.pallas_call(
        _fox_kernel,
        grid=(B, L // TQ),
        in_specs=[pl.BlockSpec((1, HEADS, LANES, TQ), lambda b, i: (b, 0, 0, i)),
                  pl.BlockSpec((1, HEADS, 1, TQ), lambda b, i: (b, 0, 0, i))] + _head_specs(L, lambda b, i: b),
        out_specs=pl.BlockSpec((1, TQ, PAIRS * LANES), lambda b, i: (b, i, 0)),
        out_shape=jax.ShapeDtypeStruct((B, L, PAIRS * LANES), BF16),
        scratch_shapes=_head_scratch(),
        compiler_params=_params(("parallel", "arbitrary")),
    )(qT, own, ka, vT)


def _count_rows(mask):
    return _fold8(jnp.where(mask, 1, 0).astype(I32), jnp.sum)


WORD_BITS = 32
PLANE_ROWS = TK // WORD_BITS
SUBLANES = 8


def _transpose_bits(words):
    a = list(words)
    shift, mask = WORD_BITS // 2, 0x0000FFFF
    while shift:
        k = 0
        while k < WORD_BITS:
            t = (a[k] ^ lax.shift_right_logical(a[k + shift], shift)) & mask
            a[k] = a[k] ^ t
            a[k + shift] = a[k + shift] ^ lax.shift_left(t, shift)
            k = (k + shift + 1) & ~shift
        shift >>= 1
        mask ^= mask << shift
    return a


def _dsa_kernel(qT_ref, k_ref, vT_ref, iqT_ref, iwT_ref, s_ref, o_ref,
                keys_sc, planes_sc, alive_sc, g_sc, acc_sc):
    i = pl.program_id(1)
    n_chunks = i + 1

    def plane_rows(j):
        return pl.ds(pl.multiple_of(j * PLANE_ROWS, PLANE_ROWS), PLANE_ROWS)

    def _select():
        wts = iwT_ref[0]

        def score_chunk(j, masked):
            ks = pl.multiple_of(j * TK, TK)
            ik = s_ref[0, pl.ds(ks, TK), :][:, _SM_IK:_SM_IK + HEAD_DIM].astype(BF16)
            acc = jnp.zeros((TK, TQ), F32)
            for h in range(HEADS):
                iq = iqT_ref[0, h // 2, (h % 2) * HEAD_DIM:(h % 2 + 1) * HEAD_DIM, :]
                rel = jnp.maximum(jnp.dot(ik, iq, preferred_element_type=F32), 0.0)
                acc = acc + wts[h:h + 1, :] * rel
            if masked:
                acc = jnp.where(_causal(acc.shape), acc, NEG_INF)
            bits = pltpu.bitcast(acc, I32)
            sign = lax.shift_right_arithmetic(bits, 31)
            key = (bits ^ (sign & 0x7FFFFFFF)) - sign
            keys_sc[pl.ds(ks, TK), :] = key
            half = WORD_BITS * SUBLANES
            for hf in range(TK // half):
                for lt in range(TQ // LANES):
                    lanes = slice(lt * LANES, (lt + 1) * LANES)
                    words = [key[hf * half + SUBLANES * m:hf * half + SUBLANES * (m + 1), lanes]
                             for m in range(WORD_BITS)]
                    rows = pl.ds(pl.multiple_of(j * PLANE_ROWS + hf * SUBLANES, SUBLANES), SUBLANES)
                    for b, plane in enumerate(_transpose_bits(words)):
                        planes_sc[b, rows, lanes] = plane ^ -1 if b == 0 else plane
            alive_sc[plane_rows(j), :] = jnp.full((PLANE_ROWS, TQ), -1, I32)

        def score_body(j, carry):
            score_chunk(j, False)
            return carry

        lax.fori_loop(0, i, score_body, 0)
        score_chunk(i, True)

        def count(pred):
            def body(j, cnt):
                ks = pl.multiple_of(j * TK, TK)
                return cnt + _count_rows(pred(keys_sc[pl.ds(ks, TK), :]))
            cnt8 = lax.fori_loop(0, n_chunks, body, jnp.zeros((8, TQ), I32))
            return jnp.sum(cnt8, axis=0, keepdims=True)

        def count_over_chunks(step):
            cnt = lax.fori_loop(0, lax.shift_right_logical(n_chunks, 1),
                                lambda jj, cnt: step(2 * jj + 1, step(2 * jj, cnt)),
                                jnp.zeros((PLANE_ROWS, TQ), I32))
            cnt = lax.fori_loop(0, n_chunks & 1, lambda _, cnt: step(n_chunks - 1, cnt), cnt)
            return jnp.sum(cnt, axis=0, keepdims=True)

        def bit_pass(bi, carry):
            need, thr_bits, flip = carry
            prev = jnp.maximum(bi - 1, 0)
            keep_all = jnp.where(bi == 0, -1, 0)

            def body(j, cnt):
                rows = plane_rows(j)
                alive = alive_sc[rows, :] & ((planes_sc[prev, rows, :] ^ flip) | keep_all)
                alive_sc[rows, :] = alive
                return cnt + lax.population_count(alive & planes_sc[bi, rows, :])

            ones_here = count_over_chunks(body)
            ok = ones_here >= need
            bit = jnp.left_shift(jnp.ones((1, TQ), I32), WORD_BITS - 1 - bi)
            return (jnp.where(ok, need, need - ones_here), jnp.where(ok, thr_bits | bit, thr_bits),
                    jnp.where(ok, 0, -1))

        need, thr_bits, flip = lax.fori_loop(
            0, WORD_BITS, bit_pass,
            (jnp.full((1, TQ), TOPK, I32), jnp.zeros((1, TQ), I32), jnp.zeros((1, TQ), I32)))
        thr = thr_bits ^ INT_MIN

        def count_equal(j, cnt):
            rows = plane_rows(j)
            alive = alive_sc[rows, :] & (planes_sc[WORD_BITS - 1, rows, :] ^ flip)
            return cnt + lax.population_count(alive)

        n_equal = count_over_chunks(count_equal)

        excess = jnp.where((n_equal > need) & (thr > NEG_INF_KEY), 1, 0)

        @pl.when(jnp.max(excess) > 0)
        def _break_ties():
            need = TOPK - count(lambda kk: kk > thr)
            r = lax.broadcasted_iota(I32, (TK, TK), 0)
            c = lax.broadcasted_iota(I32, (TK, TK), 1)
            strict_lower = jnp.where(c < r, 1.0, 0.0).astype(BF16)

            def body(j, seen):
                ks = pl.multiple_of(j * TK, TK)
                kk = keys_sc[pl.ds(ks, TK), :]
                tie = kk == thr
                tie_f = jnp.where(tie, 1.0, 0.0)
                before = jnp.dot(strict_lower, tie_f.astype(BF16), preferred_element_type=F32) + seen
                drop = tie & (before >= need.astype(F32))
                keys_sc[pl.ds(ks, TK), :] = jnp.where(drop, INT_MIN, kk)
                return seen + jnp.sum(tie_f, axis=0, keepdims=True)

            lax.fori_loop(0, n_chunks, body, jnp.zeros((1, TQ), F32))

        def bias_chunk(j, masked, near8):
            ks = pl.multiple_of(j * TK, TK)
            bias = jnp.where(keys_sc[pl.ds(ks, TK), :] >= thr, 0.0, NEG_INF)
            if masked:
                bias = jnp.where(_causal(bias.shape), bias, NEG_INF)
            keys_sc[pl.ds(ks, TK), :] = pltpu.bitcast(bias, I32)
            for blk in range(TK // NEAR_BLOCK):
                any_selected = _fold8(bias[blk * NEAR_BLOCK:(blk + 1) * NEAR_BLOCK], jnp.max) == 0.0
                near8 = jnp.maximum(near8, jnp.where(any_selected, ks + (blk + 1) * NEAR_BLOCK - 1, -1))
            return near8

        near8 = lax.fori_loop(0, i, lambda j, n8: bias_chunk(j, False, n8), jnp.full((8, TQ), -1, I32))
        return jnp.max(bias_chunk(i, True, near8), axis=0, keepdims=True)

    nearest = _select()

    def stored_bias(rows, diag):
        return pltpu.bitcast(keys_sc[rows, :], F32)

    def nearest_key_bias(h):
        query = i * TQ + lax.broadcasted_iota(I32, (1, TQ), 1)
        return (query - nearest).astype(F32) * (-(2.0 ** -(h + 1)) * LOG2E)

    _attend_heads(i, stored_bias, nearest_key_bias, qT_ref, k_ref, vT_ref, g_sc, acc_sc, o_ref)


def _dsa_attention(qT, ka, vT, iqT, iwT, small):
    B, _, _, L = qT.shape
    return pl.pallas_call(
        _dsa_kernel,
        grid=(B, L // TQ),
        in_specs=[pl.BlockSpec((1, HEADS, LANES, TQ), lambda b, i: (b, 0, 0, i))] + _head_specs(L, lambda b, i: b)
        + [pl.BlockSpec((1, PAIRS, LANES, TQ), lambda b, i: (b, 0, 0, i)),
           pl.BlockSpec((1, HEADS, TQ), lambda b, i: (b, 0, i)),
           pl.BlockSpec((1, L, LANES), lambda b, i: (b, 0, 0))],
        out_specs=pl.BlockSpec((1, TQ, PAIRS * LANES), lambda b, i: (b, i, 0)),
        out_shape=jax.ShapeDtypeStruct((B, L, PAIRS * LANES), BF16),
        scratch_shapes=[pltpu.VMEM((L, TQ), I32), pltpu.VMEM((WORD_BITS, L // WORD_BITS, TQ), I32),
                        pltpu.VMEM((L // WORD_BITS, TQ), I32)]
        + _head_scratch(),
        compiler_params=_params(("parallel", "arbitrary")),
    )(qT, ka, vT, iqT, iwT, small)


def _layer_norm(z, g, b):
    mu = jnp.mean(z, axis=-1, keepdims=True)
    zc = z - mu
    var = jnp.mean(zc * zc, axis=-1, keepdims=True)
    return zc * lax.rsqrt(var + LN_EPS) * g + b


FF_CHUNK = 256


def _merge_ffn_kernel(x_ref, oa_ref, ob_ref, ga_ref, gb_ref, wa_ref, wb_ref, wo_ref, g1_ref, b1_ref,
                      wi_ref, wf_ref, g2_ref, b2_ref, out_ref, h_sc, *, alpha):
    a = jnp.dot(oa_ref[0], wa_ref[...], preferred_element_type=F32)
    b = jnp.dot(ob_ref[0], wb_ref[...], preferred_element_type=F32)
    n_gate_blocks = D_MODEL // LANES
    gate_a = jnp.concatenate([ga_ref[0, c] for c in range(n_gate_blocks)], axis=1).astype(F32)
    gate_b = jnp.concatenate([gb_ref[0, c] for c in range(n_gate_blocks)], axis=1).astype(F32)
    merged = jax.nn.sigmoid(gate_a) * a + jax.nn.sigmoid(gate_b) * b
    y = jnp.dot(merged.astype(BF16), wo_ref[...], preferred_element_type=F32)
    x = _layer_norm(alpha * x_ref[0] + y, g1_ref[...], b1_ref[...])
    xb = x.astype(BF16)
    for c in range(D_FF // FF_CHUNK):
        lo = c * FF_CHUNK
        gate = jnp.dot(xb, wi_ref[:, lo:lo + FF_CHUNK], preferred_element_type=F32)
        up = jnp.dot(xb, wi_ref[:, D_FF + lo:D_FF + lo + FF_CHUNK], preferred_element_type=F32)
        h_sc[:, lo:lo + FF_CHUNK] = (jax.nn.silu(gate) * up).astype(BF16)
    y = jnp.dot(h_sc[...], wf_ref[...], preferred_element_type=F32)
    out_ref[0] = _layer_norm(alpha * x + y, g2_ref[...], b2_ref[...])


def _merge_ffn(x, oa, ob, gates, stacked, layer, alpha):
    B, L, D = x.shape
    tm = ROW_TILE
    nb = D // LANES
    row = lambda b_, i: (b_, i, 0)
    return pl.pallas_call(
        functools.partial(_merge_ffn_kernel, alpha=alpha),
        grid=(B, L // tm),
        in_specs=[pl.BlockSpec((1, tm, D), row),
                  pl.BlockSpec((1, tm, PAIRS * LANES), row),
                  pl.BlockSpec((1, tm, PAIRS * LANES), row),
                  pl.BlockSpec((1, nb, tm, LANES), lambda b_, i: (b_, _CB_GA // nb, i, 0)),
                  pl.BlockSpec((1, nb, tm, LANES), lambda b_, i: (b_, _CB_GB // nb, i, 0))]
        + [_layer_spec(p, layer) for p in stacked],
        out_specs=pl.BlockSpec((1, tm, D), row),
        out_shape=jax.ShapeDtypeStruct((B, L, D), F32),
        scratch_shapes=[pltpu.VMEM((tm, D_FF), BF16)],
        compiler_params=_params(("parallel", "parallel")),
    )(x, oa, ob, gates, gates, *stacked)


def _split_w_in(w):
    fq_fk = w[..., 0:2 * GROUP]
    fv = w[..., 2 * GROUP:_OFF_FLOGIT]
    dsa_iq = w[..., _OFF_DQ:_OFF_IK]
    gates = w[..., _OFF_GA:_N_IN]
    w_main = jnp.concatenate([fq_fk, gates, fv, dsa_iq], axis=-1).astype(BF16)
    pad = jnp.zeros(w.shape[:-1] + (LANES - HEAD_DIM - 2 * HEADS,), w.dtype)
    w_small = jnp.concatenate([w[..., _OFF_IK:_OFF_IW], w[..., _OFF_IW:_OFF_GA],
                               w[..., _OFF_FLOGIT:_OFF_DQ], pad], axis=-1).astype(BF16)
    return w_main, w_small


def kernel(x, w_in, b_forget, w_branch_a, w_branch_b, w_out, ln1_g, ln1_b, w_ffn_in, w_ffn_out, ln2_g, ln2_b):
    depth = w_in.shape[0]
    alpha = (2.0 * depth) ** 0.25
    w_main, w_small = _split_w_in(w_in)
    bias_rows = jnp.zeros((depth, 1, LANES), F32).at[:, 0, _SM_FL:_SM_FL + HEADS].set(b_forget)
    wa, wb, wo = w_branch_a.astype(BF16), w_branch_b.astype(BF16), w_out.astype(BF16)
    wi, wf = w_ffn_in.astype(BF16), w_ffn_out.astype(BF16)
    row = lambda p: p[:, None, :]
    stacked = (wa, wb, wo, row(ln1_g), row(ln1_b), wi, wf, row(ln2_g), row(ln2_b))
    for l in range(depth):
        gates, small, fqT, fka, f_own, fvT, dqT, dka, dvT, iqT, iwT = _project(x, w_main, w_small, bias_rows, l)
        o_a = _fox_attention(fqT, f_own, fka, fvT)
        o_b = _dsa_attention(dqT, dka, dvT, iqT, iwT, small)
        x = _merge_ffn(x, o_a, o_b, gates, stacked, l, alpha)
    return x
```

```python
import functools

import numpy as np
import jax
import jax.numpy as jnp
from jax import lax
from jax.experimental import pallas as pl
from jax.experimental.pallas import tpu as pltpu

F32 = jnp.float32
BF16 = jnp.bfloat16
I32 = jnp.int32

D_MODEL = 1024
HEADS = 8
HEAD_DIM = 64
PAIRS = HEADS // 2
D_FF = 2816
TOPK = 256
LANES = 128
NEG_INF = -1e30
LN_EPS = 1e-5
ATTN_SCALE = HEAD_DIM ** -0.5
IDX_SCALE = (HEADS ** -0.5) * (HEAD_DIM ** -0.5)

_OFF_FLOGIT = 1536
_OFF_DQ = 1544
_OFF_IK = 3592
_OFF_IW = 3656
_OFF_GA = 3664
_N_IN = 5712

_G_FQ, _G_FK, _G_GATES, _G_FV, _G_DQ, _G_DK, _G_DV, _G_IQ = 0, 1, (2, 3, 4, 5), 6, 7, 8, 9, 10
_CB_GA, _CB_GB = 0, 8
N_GATE_BLOCKS = 16
GROUP = 4 * LANES
N_MAIN = 11 * GROUP
_SM_IK, _SM_IW, _SM_FL = 0, 64, 72

ROW_TILE = 512
TQ = 512
TK = 512
CUMSUM_BLOCK = 256
V_ROWS = HEAD_DIM + 16
LOG2E = 1.4426950408889634

INT_MIN = -(2 ** 31)


def _monotone_key_of(value):
    b = int(np.float32(value).view(np.int32))
    return (b ^ ((b >> 31) & 0x7FFFFFFF)) - (b >> 31)


NEG_INF_KEY = _monotone_key_of(NEG_INF)
VMEM_LIMIT = 60 * 1024 * 1024


def _params(sem):
    return pltpu.CompilerParams(dimension_semantics=sem, vmem_limit_bytes=VMEM_LIMIT)


def _const_spec(shape):
    nd = len(shape)
    return pl.BlockSpec(shape, lambda *_: (0,) * nd, pipeline_mode=pl.Buffered(1))


def _layer_spec(stacked, layer):
    tail = stacked.shape[1:]
    return pl.BlockSpec((None,) + tail, lambda *_: (layer,) + (0,) * len(tail), pipeline_mode=pl.Buffered(1))


def _lane_row(lo, hi, value=1.0):
    lane = lax.broadcasted_iota(I32, (1, LANES), 1)
    return jnp.where((lane >= lo) & (lane < hi), value, 0.0)


BIAS_COLS = 8


def _alibi_placement():
    put = np.zeros((LANES, 2 * LANES), np.float32)
    for h in range(HEADS):
        slope = 2.0 ** -(h + 1)
        qc, kc = BIAS_COLS * h, LANES + BIAS_COLS * h
        put[0, qc], put[1, qc + 1], put[2, qc + 2], put[2, qc + 3] = -64.0 * slope, -slope, 64.0 * slope, slope
        put[2, kc], put[2, kc + 1], put[0, kc + 2], put[1, kc + 3] = 1.0, 1.0, 1.0, 1.0
    return jnp.asarray(put, BF16)


def _forget_placement():
    put = np.zeros((LANES, 2 * LANES), np.float32)
    for h in range(HEADS):
        for part in range(3):
            src = _SM_FL + part * HEADS + h
            put[src, BIAS_COLS * h + part] = 1.0
            put[src, LANES + BIAS_COLS * h + 3 + part] = -1.0
    return jnp.asarray(put, BF16)


def _proj_kernel(x_ref, wm_ref, ws_ref, alibi_ref, forget_ref, bias_ref,
                 gates_ref, os_ref, fqT_ref, fka_ref, own_ref, fvT_ref, dqT_ref, dka_ref, dvT_ref, iqT_ref, iwT_ref,
                 carry_sc):
    tm = x_ref.shape[1]
    xb = x_ref[0].astype(BF16)

    def group(g):
        return jnp.dot(xb, wm_ref[:, g * GROUP:(g + 1) * GROUP], preferred_element_type=F32)

    def store_gates(n):
        r = group(_G_GATES[n]).astype(BF16)
        for c in range(4):
            gates_ref[0, 4 * n + c] = r[:, c * LANES:(c + 1) * LANES]

    def pair_and_half(h):
        p, e = divmod(h, 2)
        return slice(p * LANES, (p + 1) * LANES), _lane_row(e * HEAD_DIM, (e + 1) * HEAD_DIM), (1 - e) * HEAD_DIM

    def bias_columns(compact, h, first, n):
        base = (1 - h % 2) * HEAD_DIM
        moved = pltpu.roll(compact, (base - BIAS_COLS * h) % LANES, 1)
        return moved * _lane_row(base + first, base + first + n)

    @pl.when(pl.program_id(1) == 0)
    def _new_sequence():
        carry_sc[...] = jnp.zeros(carry_sc.shape, F32)

    store_gates(0)
    small = jnp.dot(xb, ws_ref[...], preferred_element_type=F32)
    os_ref[0] = small
    iwT_ref[0] = small.T[_SM_IW:_SM_IW + HEADS, :] * IDX_SCALE
    c = _cumsum_rows(jax.nn.log_sigmoid(small + bias_ref[...])) + carry_sc[...]
    carry_sc[...] = c[tm - 1:tm, :]
    hi, mid, lo = _split3(c)
    lane = lax.broadcasted_iota(I32, (tm, LANES), 1)
    parts = jnp.where(lane < _SM_FL + HEADS, hi,
                      jnp.where(lane < _SM_FL + 2 * HEADS, pltpu.roll(mid, HEADS, 1), pltpu.roll(lo, 2 * HEADS, 1)))
    store_gates(1)
    aug = jnp.dot(parts.astype(BF16), forget_ref[...], preferred_element_type=F32)
    q = group(_G_FQ) * ATTN_SCALE
    k = group(_G_FK)
    for h in range(HEADS):
        pair, data, base = pair_and_half(h)
        q_aug = q[:, pair] * data + bias_columns(aug[:, :LANES], h, 0, 3) + _lane_row(base + 3, base + 6)
        k_aug = k[:, pair] * data + bias_columns(aug[:, LANES:], h, 3, 3) + _lane_row(base, base + 3)
        fqT_ref[0, h] = q_aug.T.astype(BF16)
        fka_ref[0, h] = k_aug.astype(BF16)
        own_ref[0, h] = jnp.sum((q_aug * k_aug).T, axis=0, keepdims=True) * LOG2E

    def store_values(ref, r):
        ones = jnp.ones((V_ROWS - HEAD_DIM, tm), BF16)
        for p in range(PAIRS):
            vT = r[:, p * LANES:(p + 1) * LANES].T.astype(BF16)
            for e in range(2):
                ref[0, 2 * p + e, 0] = jnp.concatenate([vT[e * HEAD_DIM:(e + 1) * HEAD_DIM], ones], axis=0)

    store_values(fvT_ref, group(_G_FV))
    store_gates(2)
    store_values(dvT_ref, group(_G_DV))

    lane = lax.broadcasted_iota(I32, (tm, LANES), 1)
    pos = lax.broadcasted_iota(I32, (tm, LANES), 0) + pl.program_id(1) * tm
    pos_terms = jnp.where(lane == 0, lax.shift_right_logical(pos, 6),
                          jnp.where(lane == 1, pos & 63, jnp.where(lane == 2, 1, 0))).astype(F32).astype(BF16)
    aug = jnp.dot(pos_terms, alibi_ref[...], preferred_element_type=F32)
    q = group(_G_DQ) * ATTN_SCALE
    k = group(_G_DK)
    for h in range(HEADS):
        pair, data, _ = pair_and_half(h)
        q_aug = q[:, pair] * data + bias_columns(aug[:, :LANES], h, 0, 4)
        k_aug = k[:, pair] * data + bias_columns(aug[:, LANES:], h, 0, 4)
        dqT_ref[0, h] = q_aug.T.astype(BF16)
        dka_ref[0, h] = k_aug.astype(BF16)
    iq = group(_G_IQ)
    for p in range(PAIRS):
        iqT_ref[0, p] = iq[:, p * LANES:(p + 1) * LANES].T.astype(BF16)
    store_gates(3)


def _project(x, w_main, w_small, bias_row, layer):
    B, L, D = x.shape
    tm = TK
    nk = L // TK
    alibi, forget = _alibi_placement(), _forget_placement()
    q_t = pl.BlockSpec((1, HEADS, LANES, tm), lambda b, i: (b, 0, 0, i))
    k_aug = pl.BlockSpec((1, HEADS, tm, LANES), lambda b, i: (b, 0, i, 0))
    v_t = pl.BlockSpec((1, HEADS, 1, V_ROWS, TK), lambda b, i: (b, 0, i, 0, 0))
    q_t_shape = jax.ShapeDtypeStruct((B, HEADS, LANES, L), BF16)
    k_aug_shape = jax.ShapeDtypeStruct((B, HEADS, L, LANES), BF16)
    v_t_shape = jax.ShapeDtypeStruct((B, HEADS, nk, V_ROWS, TK), BF16)
    return pl.pallas_call(
        _proj_kernel,
        grid=(B, nk),
        in_specs=[pl.BlockSpec((1, tm, D), lambda b, i: (b, i, 0)),
                  _layer_spec(w_main, layer), _layer_spec(w_small, layer),
                  _const_spec(alibi.shape), _const_spec(forget.shape), _layer_spec(bias_row, layer)],
        out_specs=[pl.BlockSpec((1, N_GATE_BLOCKS, tm, LANES), lambda b, i: (b, 0, i, 0)),
                   pl.BlockSpec((1, tm, LANES), lambda b, i: (b, i, 0)),
                   q_t, k_aug, pl.BlockSpec((1, HEADS, 1, tm), lambda b, i: (b, 0, 0, i)), v_t,
                   q_t, k_aug, v_t,
                   pl.BlockSpec((1, PAIRS, LANES, tm), lambda b, i: (b, 0, 0, i)),
                   pl.BlockSpec((1, HEADS, tm), lambda b, i: (b, 0, i))],
        out_shape=[jax.ShapeDtypeStruct((B, N_GATE_BLOCKS, L, LANES), BF16),
                   jax.ShapeDtypeStruct((B, L, LANES), F32),
                   q_t_shape, k_aug_shape, jax.ShapeDtypeStruct((B, HEADS, 1, L), F32), v_t_shape,
                   q_t_shape, k_aug_shape, v_t_shape,
                   jax.ShapeDtypeStruct((B, PAIRS, LANES, L), BF16),
                   jax.ShapeDtypeStruct((B, HEADS, L), F32)],
        scratch_shapes=[pltpu.VMEM((1, LANES), F32)],
        compiler_params=_params(("parallel", "arbitrary")),
    )(x, w_main, w_small, alibi, forget, bias_row)


def _split3(v):
    hi = v.astype(BF16).astype(F32)
    r = v - hi
    mid = r.astype(BF16).astype(F32)
    lo = (r - mid).astype(BF16).astype(F32)
    return hi, mid, lo


def _cumsum_rows(x):
    n = CUMSUM_BLOCK
    L = x.shape[0]
    r = lax.broadcasted_iota(I32, (n, n), 0)
    c = lax.broadcasted_iota(I32, (n, n), 1)
    tri = jnp.where(c <= r, 1.0, 0.0).astype(BF16)
    carry = jnp.zeros((1, x.shape[1]), F32)
    outs = []
    w = x.shape[1]
    for blk in range(L // n):
        parts = jnp.concatenate([p.astype(BF16) for p in _split3(x[blk * n:(blk + 1) * n])], axis=1)
        y3 = jnp.dot(tri, parts, preferred_element_type=F32)
        y = (y3[:, :w] + y3[:, w:2 * w] + y3[:, 2 * w:]) + carry
        carry = y[n - 1:n, :]
        outs.append(y)
    return jnp.concatenate(outs, axis=0)


def _fold8(x, op):
    return op(x.reshape(x.shape[0] // 8, 8, x.shape[1]), axis=0)


def _causal(shape):
    return lax.broadcasted_iota(I32, shape, 0) <= lax.broadcasted_iota(I32, shape, 1)


L_SUM_MIN, L_SUM_MAX = 1e-30, 1e30
NEAR_BLOCK = 64


def _attend_heads(n_full, bias_fn, ref_score_fn, qT_ref, k_ref, vT_ref, g_sc, acc_sc, o_ref):
    def for_chunks(chunk):
        def pair(jj, carry):
            chunk(2 * jj, False)
            chunk(2 * jj + 1, False)
            return carry

        def single(_, carry):
            chunk(n_full - 1, False)
            return carry

        lax.fori_loop(0, lax.shift_right_logical(n_full, 1), pair, 0)
        lax.fori_loop(0, n_full & 1, single, 0)
        chunk(n_full, True)

    def scores(h, rows, diag, raw):
        s = raw * LOG2E
        bias = bias_fn(rows, diag)
        return s if bias is None else s + bias

    def sweep():
        acc_sc[...] = jnp.zeros(acc_sc.shape, F32)

        def chunk(j, diag):
            rows = pl.ds(pl.multiple_of(j * TK, TK), TK)

            def raw_scores(h):
                return jnp.dot(k_ref[0, h, rows, :], qT_ref[0, h], preferred_element_type=F32)

            nxt = raw_scores(0)
            for h in range(HEADS):
                raw = nxt
                if h + 1 < HEADS:
                    nxt = raw_scores(h + 1)
                p = jnp.exp2(scores(h, rows, diag, raw) - g_sc[h]).astype(BF16)
                acc_sc[h] += jnp.dot(vT_ref[0, h, j], p, preferred_element_type=F32)

        for_chunks(chunk)
        bad = jnp.zeros((1, TQ), I32)
        for p in range(PAIRS):
            halves = []
            for e in range(2):
                acc = acc_sc[2 * p + e]
                l = acc[HEAD_DIM:HEAD_DIM + 1]
                bad = jnp.maximum(bad, jnp.where((l > L_SUM_MIN) & (l < L_SUM_MAX), 0, 1))
                halves.append(acc[:HEAD_DIM] / l)
            o_ref[0, :, p * LANES:(p + 1) * LANES] = jnp.concatenate(halves, axis=0).T.astype(o_ref.dtype)
        return jnp.max(bad) > 0

    def exact_row_maxima():
        acc_sc[:, 0:8, :] = jnp.full((HEADS, 8, TQ), NEG_INF, F32)

        def chunk(j, diag):
            rows = pl.ds(pl.multiple_of(j * TK, TK), TK)
            for h in range(HEADS):
                raw = jnp.dot(k_ref[0, h, rows, :], qT_ref[0, h], preferred_element_type=F32)
                acc_sc[h, 0:8, :] = jnp.maximum(acc_sc[h, 0:8, :], _fold8(scores(h, rows, diag, raw), jnp.max))

        for_chunks(chunk)
        for h in range(HEADS):
            g_sc[h] = jnp.max(acc_sc[h, 0:8, :], axis=0, keepdims=True)

    for h in range(HEADS):
        g_sc[h] = ref_score_fn(h)
    unsafe = sweep()

    @pl.when(unsafe)
    def _redo_with_exact_maxima():
        exact_row_maxima()
        sweep()


def _head_specs(L, index):
    nk = L // TK
    return [pl.BlockSpec((1, HEADS, L, LANES), lambda *g: (index(*g), 0, 0, 0)),
            pl.BlockSpec((1, HEADS, nk, V_ROWS, TK), lambda *g: (index(*g), 0, 0, 0, 0))]


def _head_scratch():
    return [pltpu.VMEM((HEADS, 1, TQ), F32), pltpu.VMEM((HEADS, V_ROWS, TQ), F32)]


def _fox_kernel(qT_ref, own_ref, k_ref, vT_ref, o_ref, g_sc, acc_sc):
    i = pl.program_id(1)

    def bias_fn(rows, diag):
        return jnp.where(_causal((TK, TQ)), 0.0, NEG_INF) if diag else None

    _attend_heads(i, bias_fn, lambda h: own_ref[0, h], qT_ref, k_ref, vT_ref, g_sc, acc_sc, o_ref)


def _fox_attention(qT, own, ka, vT):
    B, _, _, L = qT.shape
    return pl.pallas_call(
        _fox_kernel,
        grid=(B, L // TQ),
        in_specs=[pl.BlockSpec((1, HEADS, LANES, TQ), lambda b, i: (b, 0, 0, i)),
                  pl.BlockSpec((1, HEADS, 1, TQ), lambda b, i: (b, 0, 0, i))] + _head_specs(L, lambda b, i: b),
        out_specs=pl.BlockSpec((1, TQ, PAIRS * LANES), lambda b, i: (b, i, 0)),
        out_shape=jax.ShapeDtypeStruct((B, L, PAIRS * LANES), BF16),
        scratch_shapes=_head_scratch(),
        compiler_params=_params(("parallel", "arbitrary")),
    )(qT, own, ka, vT)


def _count_rows(mask):
    return _fold8(jnp.where(mask, 1, 0).astype(I32), jnp.sum)


WORD_BITS = 32
PLANE_ROWS = TK // WORD_BITS
SUBLANES = 8


def _transpose_bits(words):
    a = list(words)
    shift, mask = WORD_BITS // 2, 0x0000FFFF
    while shift:
        k = 0
        while k < WORD_BITS:
            t = (a[k] ^ lax.shift_right_logical(a[k + shift], shift)) & mask
            a[k] = a[k] ^ t
            a[k + shift] = a[k + shift] ^ lax.shift_left(t, shift)
            k = (k + shift + 1) & ~shift
        shift >>= 1
        mask ^= mask << shift
    return a


def _dsa_kernel(qT_ref, k_ref, vT_ref, iqT_ref, iwT_ref, s_ref, o_ref,
                keys_sc, planes_sc, alive_sc, g_sc, acc_sc):
    i = pl.program_id(1)
    n_chunks = i + 1

    def plane_rows(j):
        return pl.ds(pl.multiple_of(j * PLANE_ROWS, PLANE_ROWS), PLANE_ROWS)

    def _select():
        wts = iwT_ref[0]

        def score_chunk(j, masked):
            ks = pl.multiple_of(j * TK, TK)
            ik = s_ref[0, pl.ds(ks, TK), :][:, _SM_IK:_SM_IK + HEAD_DIM].astype(BF16)
            acc = None
            for h in range(HEADS):
                iq = iqT_ref[0, h // 2, (h % 2) * HEAD_DIM:(h % 2 + 1) * HEAD_DIM, :]
                rel = jnp.maximum(jnp.dot(ik, iq, preferred_element_type=F32), 0.0)
                term = wts[h:h + 1, :] * rel
                acc = term if acc is None else acc + term
            if masked:
                acc = jnp.where(_causal(acc.shape), acc, NEG_INF)
            bits = pltpu.bitcast(acc, I32)
            sign = lax.shift_right_arithmetic(bits, 31)
            key = (bits ^ (sign & 0x7FFFFFFF)) - sign
            keys_sc[pl.ds(ks, TK), :] = key
            half = WORD_BITS * SUBLANES
            for hf in range(TK // half):
                for lt in range(TQ // LANES):
                    lanes = slice(lt * LANES, (lt + 1) * LANES)
                    words = [key[hf * half + SUBLANES * m:hf * half + SUBLANES * (m + 1), lanes]
                             for m in range(WORD_BITS)]
                    rows = pl.ds(pl.multiple_of(j * PLANE_ROWS + hf * SUBLANES, SUBLANES), SUBLANES)
                    for b, plane in enumerate(_transpose_bits(words)):
                        planes_sc[b, rows, lanes] = plane ^ -1 if b == 0 else plane
            alive_sc[plane_rows(j), :] = jnp.full((PLANE_ROWS, TQ), -1, I32)

        def score_body(j, carry):
            score_chunk(j, False)
            return carry

        lax.fori_loop(0, i, score_body, 0)
        score_chunk(i, True)

        def count(pred):
            def body(j, cnt):
                ks = pl.multiple_of(j * TK, TK)
                return cnt + _count_rows(pred(keys_sc[pl.ds(ks, TK), :]))
            cnt8 = lax.fori_loop(0, n_chunks, body, jnp.zeros((8, TQ), I32))
            return jnp.sum(cnt8, axis=0, keepdims=True)

        def count_over_chunks(step):
            cnt = lax.fori_loop(0, lax.shift_right_logical(n_chunks, 1),
                                lambda jj, cnt: step(2 * jj + 1, step(2 * jj, cnt)),
                                jnp.zeros((PLANE_ROWS, TQ), I32))
            cnt = lax.fori_loop(0, n_chunks & 1, lambda _, cnt: step(n_chunks - 1, cnt), cnt)
            return jnp.sum(cnt, axis=0, keepdims=True)

        def bit_pass(bi, carry):
            need, thr_bits, flip = carry
            prev = jnp.maximum(bi - 1, 0)
            keep_all = jnp.where(bi == 0, -1, 0)

            def body(j, cnt):
                rows = plane_rows(j)
                alive = alive_sc[rows, :] & ((planes_sc[prev, rows, :] ^ flip) | keep_all)
                alive_sc[rows, :] = alive
                return cnt + lax.population_count(alive & planes_sc[bi, rows, :])

            ones_here = count_over_chunks(body)
            ok = ones_here >= need
            bit = jnp.left_shift(jnp.ones((1, TQ), I32), WORD_BITS - 1 - bi)
            return (jnp.where(ok, need, need - ones_here), jnp.where(ok, thr_bits | bit, thr_bits),
                    jnp.where(ok, 0, -1))

        need, thr_bits, flip = lax.fori_loop(
            0, WORD_BITS, bit_pass,
            (jnp.full((1, TQ), TOPK, I32), jnp.zeros((1, TQ), I32), jnp.zeros((1, TQ), I32)))
        thr = thr_bits ^ INT_MIN

        def count_equal(j, cnt):
            rows = plane_rows(j)
            alive = alive_sc[rows, :] & (planes_sc[WORD_BITS - 1, rows, :] ^ flip)
            return cnt + lax.population_count(alive)

        n_equal = count_over_chunks(count_equal)

        excess = jnp.where((n_equal > need) & (thr > NEG_INF_KEY), 1, 0)

        @pl.when(jnp.max(excess) > 0)
        def _break_ties():
            need = TOPK - count(lambda kk: kk > thr)
            r = lax.broadcasted_iota(I32, (TK, TK), 0)
            c = lax.broadcasted_iota(I32, (TK, TK), 1)
            strict_lower = jnp.where(c < r, 1.0, 0.0).astype(BF16)

            def body(j, seen):
                ks = pl.multiple_of(j * TK, TK)
                kk = keys_sc[pl.ds(ks, TK), :]
                tie = kk == thr
                tie_f = jnp.where(tie, 1.0, 0.0)
                before = jnp.dot(strict_lower, tie_f.astype(BF16), preferred_element_type=F32) + seen
                drop = tie & (before >= need.astype(F32))
                keys_sc[pl.ds(ks, TK), :] = jnp.where(drop, INT_MIN, kk)
                return seen + jnp.sum(tie_f, axis=0, keepdims=True)

            lax.fori_loop(0, n_chunks, body, jnp.zeros((1, TQ), F32))

        def bias_chunk(j, masked, near8):
            ks = pl.multiple_of(j * TK, TK)
            bias = jnp.where(keys_sc[pl.ds(ks, TK), :] >= thr, 0.0, NEG_INF)
            if masked:
                bias = jnp.where(_causal(bias.shape), bias, NEG_INF)
            keys_sc[pl.ds(ks, TK), :] = pltpu.bitcast(bias, I32)
            for blk in range(TK // NEAR_BLOCK):
                any_selected = _fold8(bias[blk * NEAR_BLOCK:(blk + 1) * NEAR_BLOCK], jnp.max) == 0.0
                near8 = jnp.maximum(near8, jnp.where(any_selected, ks + (blk + 1) * NEAR_BLOCK - 1, -1))
            return near8

        near8 = lax.fori_loop(0, i, lambda j, n8: bias_chunk(j, False, n8), jnp.full((8, TQ), -1, I32))
        return jnp.max(bias_chunk(i, True, near8), axis=0, keepdims=True)

    nearest = _select()

    def stored_bias(rows, diag):
        return pltpu.bitcast(keys_sc[rows, :], F32)

    def nearest_key_bias(h):
        query = i * TQ + lax.broadcasted_iota(I32, (1, TQ), 1)
        return (query - nearest).astype(F32) * (-(2.0 ** -(h + 1)) * LOG2E)

    _attend_heads(i, stored_bias, nearest_key_bias, qT_ref, k_ref, vT_ref, g_sc, acc_sc, o_ref)


def _dsa_attention(qT, ka, vT, iqT, iwT, small):
    B, _, _, L = qT.shape
    return pl.pallas_call(
        _dsa_kernel,
        grid=(B, L // TQ),
        in_specs=[pl.BlockSpec((1, HEADS, LANES, TQ), lambda b, i: (b, 0, 0, i))] + _head_specs(L, lambda b, i: b)
        + [pl.BlockSpec((1, PAIRS, LANES, TQ), lambda b, i: (b, 0, 0, i)),
           pl.BlockSpec((1, HEADS, TQ), lambda b, i: (b, 0, i)),
           pl.BlockSpec((1, L, LANES), lambda b, i: (b, 0, 0))],
        out_specs=pl.BlockSpec((1, TQ, PAIRS * LANES), lambda b, i: (b, i, 0)),
        out_shape=jax.ShapeDtypeStruct((B, L, PAIRS * LANES), BF16),
        scratch_shapes=[pltpu.VMEM((L, TQ), I32), pltpu.VMEM((WORD_BITS, L // WORD_BITS, TQ), I32),
                        pltpu.VMEM((L // WORD_BITS, TQ), I32)]
        + _head_scratch(),
        compiler_params=_params(("parallel", "arbitrary")),
    )(qT, ka, vT, iqT, iwT, small)


def _layer_norm(z, g, b):
    mu = jnp.mean(z, axis=-1, keepdims=True)
    zc = z - mu
    var = jnp.mean(zc * zc, axis=-1, keepdims=True)
    return zc * lax.rsqrt(var + LN_EPS) * g + b


FF_CHUNK = 256


def _merge_ffn_kernel(x_ref, oa_ref, ob_ref, ga_ref, gb_ref, wa_ref, wb_ref, wo_ref, g1_ref, b1_ref,
                      wi_ref, wf_ref, g2_ref, b2_ref, out_ref, h_sc, *, alpha):
    a = jnp.dot(oa_ref[0], wa_ref[...], preferred_element_type=F32)
    b = jnp.dot(ob_ref[0], wb_ref[...], preferred_element_type=F32)
    n_gate_blocks = D_MODEL // LANES
    gate_a = jnp.concatenate([ga_ref[0, c] for c in range(n_gate_blocks)], axis=1).astype(F32)
    gate_b = jnp.concatenate([gb_ref[0, c] for c in range(n_gate_blocks)], axis=1).astype(F32)
    merged = jax.nn.sigmoid(gate_a) * a + jax.nn.sigmoid(gate_b) * b
    y = jnp.dot(merged.astype(BF16), wo_ref[...], preferred_element_type=F32)
    x = _layer_norm(alpha * x_ref[0] + y, g1_ref[...], b1_ref[...])
    xb = x.astype(BF16)
    for c in range(D_FF // FF_CHUNK):
        lo = c * FF_CHUNK
        gate = jnp.dot(xb, wi_ref[:, lo:lo + FF_CHUNK], preferred_element_type=F32)
        up = jnp.dot(xb, wi_ref[:, D_FF + lo:D_FF + lo + FF_CHUNK], preferred_element_type=F32)
        h_sc[:, lo:lo + FF_CHUNK] = (jax.nn.silu(gate) * up).astype(BF16)
    y = jnp.dot(h_sc[...], wf_ref[...], preferred_element_type=F32)
    out_ref[0] = _layer_norm(alpha * x + y, g2_ref[...], b2_ref[...])


def _merge_ffn(x, oa, ob, gates, stacked, layer, alpha):
    B, L, D = x.shape
    tm = ROW_TILE
    nb = D // LANES
    row = lambda b_, i: (b_, i, 0)
    return pl.pallas_call(
        functools.partial(_merge_ffn_kernel, alpha=alpha),
        grid=(B, L // tm),
        in_specs=[pl.BlockSpec((1, tm, D), row),
                  pl.BlockSpec((1, tm, PAIRS * LANES), row),
                  pl.BlockSpec((1, tm, PAIRS * LANES), row),
                  pl.BlockSpec((1, nb, tm, LANES), lambda b_, i: (b_, _CB_GA // nb, i, 0)),
                  pl.BlockSpec((1, nb, tm, LANES), lambda b_, i: (b_, _CB_GB // nb, i, 0))]
        + [_layer_spec(p, layer) for p in stacked],
        out_specs=pl.BlockSpec((1, tm, D), row),
        out_shape=jax.ShapeDtypeStruct((B, L, D), F32),
        scratch_shapes=[pltpu.VMEM((tm, D_FF), BF16)],
        compiler_params=_params(("parallel", "parallel")),
    )(x, oa, ob, gates, gates, *stacked)


def _split_w_in(w):
    fq_fk = w[..., 0:2 * GROUP]
    fv = w[..., 2 * GROUP:_OFF_FLOGIT]
    dsa_iq = w[..., _OFF_DQ:_OFF_IK]
    gates = w[..., _OFF_GA:_N_IN]
    w_main = jnp.concatenate([fq_fk, gates, fv, dsa_iq], axis=-1).astype(BF16)
    pad = jnp.zeros(w.shape[:-1] + (LANES - HEAD_DIM - 2 * HEADS,), w.dtype)
    w_small = jnp.concatenate([w[..., _OFF_IK:_OFF_IW], w[..., _OFF_IW:_OFF_GA],
                               w[..., _OFF_FLOGIT:_OFF_DQ], pad], axis=-1).astype(BF16)
    return w_main, w_small


def kernel(x, w_in, b_forget, w_branch_a, w_branch_b, w_out, ln1_g, ln1_b, w_ffn_in, w_ffn_out, ln2_g, ln2_b):
    depth = w_in.shape[0]
    alpha = (2.0 * depth) ** 0.25
    w_main, w_small = _split_w_in(w_in)
    bias_rows = jnp.zeros((depth, 1, LANES), F32).at[:, 0, _SM_FL:_SM_FL + HEADS].set(b_forget)
    wa, wb, wo = w_branch_a.astype(BF16), w_branch_b.astype(BF16), w_out.astype(BF16)
    wi, wf = w_ffn_in.astype(BF16), w_ffn_out.astype(BF16)
    row = lambda p: p[:, None, :]
    stacked = (wa, wb, wo, row(ln1_g), row(ln1_b), wi, wf, row(ln2_g), row(ln2_b))
    for l in range(depth):
        gates, small, fqT, fka, f_own, fvT, dqT, dka, dvT, iqT, iwT = _project(x, w_main, w_small, bias_rows, l)
        o_a = _fox_attention(fqT, f_own, fka, fvT)
        o_b = _dsa_attention(dqT, dka, dvT, iqT, iwT, small)
        x = _merge_ffn(x, o_a, o_b, gates, stacked, l, alpha)
    return x
```

```python
import functools

import numpy as np
import jax
import jax.numpy as jnp
from jax import lax
from jax.experimental import pallas as pl
from jax.experimental.pallas import tpu as pltpu

F32 = jnp.float32
BF16 = jnp.bfloat16
I32 = jnp.int32

D_MODEL = 1024
HEADS = 8
HEAD_DIM = 64
PAIRS = HEADS // 2
D_FF = 2816
TOPK = 256
LANES = 128
NEG_INF = -1e30
LN_EPS = 1e-5
ATTN_SCALE = HEAD_DIM ** -0.5
IDX_SCALE = (HEADS ** -0.5) * (HEAD_DIM ** -0.5)

_OFF_FLOGIT = 1536
_OFF_DQ = 1544
_OFF_IK = 3592
_OFF_IW = 3656
_OFF_GA = 3664
_N_IN = 5712

_G_FQ, _G_FK, _G_GATES, _G_FV, _G_DQ, _G_DK, _G_DV, _G_IQ = 0, 1, (2, 3, 4, 5), 6, 7, 8, 9, 10
_CB_GA, _CB_GB = 0, 8
N_GATE_BLOCKS = 16
GROUP = 4 * LANES
N_MAIN = 11 * GROUP
_SM_IK, _SM_IW, _SM_FL = 0, 64, 72

ROW_TILE = 512
TQ = 512
TK = 512
CUMSUM_BLOCK = 256
V_ROWS = HEAD_DIM + 16
LOG2E = 1.4426950408889634

INT_MIN = -(2 ** 31)


def _monotone_key_of(value):
    b = int(np.float32(value).view(np.int32))
    return (b ^ ((b >> 31) & 0x7FFFFFFF)) - (b >> 31)


NEG_INF_KEY = _monotone_key_of(NEG_INF)
VMEM_LIMIT = 60 * 1024 * 1024


def _params(sem):
    return pltpu.CompilerParams(dimension_semantics=sem, vmem_limit_bytes=VMEM_LIMIT)


def _const_spec(shape):
    nd = len(shape)
    return pl.BlockSpec(shape, lambda *_: (0,) * nd, pipeline_mode=pl.Buffered(1))


def _layer_spec(stacked, layer):
    tail = stacked.shape[1:]
    return pl.BlockSpec((None,) + tail, lambda *_: (layer,) + (0,) * len(tail), pipeline_mode=pl.Buffered(1))


def _lane_row(lo, hi, value=1.0):
    lane = lax.broadcasted_iota(I32, (1, LANES), 1)
    return jnp.where((lane >= lo) & (lane < hi), value, 0.0)


BIAS_COLS = 8


def _alibi_placement():
    put = np.zeros((LANES, 2 * LANES), np.float32)
    for h in range(HEADS):
        slope = 2.0 ** -(h + 1)
        qc, kc = BIAS_COLS * h, LANES + BIAS_COLS * h
        put[0, qc], put[1, qc + 1], put[2, qc + 2], put[2, qc + 3] = -64.0 * slope, -slope, 64.0 * slope, slope
        put[2, kc], put[2, kc + 1], put[0, kc + 2], put[1, kc + 3] = 1.0, 1.0, 1.0, 1.0
    return jnp.asarray(put, BF16)


def _forget_placement():
    put = np.zeros((LANES, 2 * LANES), np.float32)
    for h in range(HEADS):
        for part in range(3):
            src = _SM_FL + part * HEADS + h
            put[src, BIAS_COLS * h + part] = 1.0
            put[src, LANES + BIAS_COLS * h + 3 + part] = -1.0
    return jnp.asarray(put, BF16)


def _proj_kernel(x_ref, wm_ref, ws_ref, alibi_ref, forget_ref, bias_ref,
                 gates_ref, os_ref, fqT_ref, fka_ref, own_ref, fvT_ref, dqT_ref, dka_ref, dvT_ref, iqT_ref, iwT_ref,
                 carry_sc):
    tm = x_ref.shape[1]
    xb = x_ref[0].astype(BF16)

    def group(g):
        return jnp.dot(xb, wm_ref[:, g * GROUP:(g + 1) * GROUP], preferred_element_type=F32)

    def store_gates(n):
        r = group(_G_GATES[n]).astype(BF16)
        for c in range(4):
            gates_ref[0, 4 * n + c] = r[:, c * LANES:(c + 1) * LANES]

    def pair_and_half(h):
        p, e = divmod(h, 2)
        return slice(p * LANES, (p + 1) * LANES), _lane_row(e * HEAD_DIM, (e + 1) * HEAD_DIM), (1 - e) * HEAD_DIM

    def bias_columns(compact, h, first, n):
        base = (1 - h % 2) * HEAD_DIM
        moved = pltpu.roll(compact, (base - BIAS_COLS * h) % LANES, 1)
        return moved * _lane_row(base + first, base + first + n)

    @pl.when(pl.program_id(1) == 0)
    def _new_sequence():
        carry_sc[...] = jnp.zeros(carry_sc.shape, F32)

    store_gates(0)
    small = jnp.dot(xb, ws_ref[...], preferred_element_type=F32)
    os_ref[0] = small
    iwT_ref[0] = small.T[_SM_IW:_SM_IW + HEADS, :] * IDX_SCALE
    c = _cumsum_rows(jax.nn.log_sigmoid(small + bias_ref[...])) + carry_sc[...]
    carry_sc[...] = c[tm - 1:tm, :]
    hi, mid, lo = _split3(c)
    lane = lax.broadcasted_iota(I32, (tm, LANES), 1)
    parts = jnp.where(lane < _SM_FL + HEADS, hi,
                      jnp.where(lane < _SM_FL + 2 * HEADS, pltpu.roll(mid, HEADS, 1), pltpu.roll(lo, 2 * HEADS, 1)))
    store_gates(1)
    aug = jnp.dot(parts.astype(BF16), forget_ref[...], preferred_element_type=F32)
    q = group(_G_FQ) * ATTN_SCALE
    k = group(_G_FK)
    for h in range(HEADS):
        pair, data, base = pair_and_half(h)
        q_aug = q[:, pair] * data + bias_columns(aug[:, :LANES], h, 0, 3) + _lane_row(base + 3, base + 6)
        k_aug = k[:, pair] * data + bias_columns(aug[:, LANES:], h, 3, 3) + _lane_row(base, base + 3)
        fqT_ref[0, h] = q_aug.T.astype(BF16)
        fka_ref[0, h] = k_aug.astype(BF16)
        own_ref[0, h] = jnp.sum((q_aug * k_aug).T, axis=0, keepdims=True) * LOG2E

    def store_values(ref, r):
        ones = jnp.ones((V_ROWS - HEAD_DIM, tm), BF16)
        for p in range(PAIRS):
            vT = r[:, p * LANES:(p + 1) * LANES].T.astype(BF16)
            for e in range(2):
                ref[0, 2 * p + e, 0] = jnp.concatenate([vT[e * HEAD_DIM:(e + 1) * HEAD_DIM], ones], axis=0)

    store_values(fvT_ref, group(_G_FV))
    store_gates(2)
    store_values(dvT_ref, group(_G_DV))

    lane = lax.broadcasted_iota(I32, (tm, LANES), 1)
    pos = lax.broadcasted_iota(I32, (tm, LANES), 0) + pl.program_id(1) * tm
    pos_terms = jnp.where(lane == 0, lax.shift_right_logical(pos, 6),
                          jnp.where(lane == 1, pos & 63, jnp.where(lane == 2, 1, 0))).astype(F32).astype(BF16)
    aug = jnp.dot(pos_terms, alibi_ref[...], preferred_element_type=F32)
    q = group(_G_DQ) * ATTN_SCALE
    k = group(_G_DK)
    for h in range(HEADS):
        pair, data, _ = pair_and_half(h)
        q_aug = q[:, pair] * data + bias_columns(aug[:, :LANES], h, 0, 4)
        k_aug = k[:, pair] * data + bias_columns(aug[:, LANES:], h, 0, 4)
        dqT_ref[0, h] = q_aug.T.astype(BF16)
        dka_ref[0, h] = k_aug.astype(BF16)
    iq = group(_G_IQ)
    for p in range(PAIRS):
        iqT_ref[0, p] = iq[:, p * LANES:(p + 1) * LANES].T.astype(BF16)
    store_gates(3)


def _project(x, w_main, w_small, bias_row, layer):
    B, L, D = x.shape
    tm = TK
    nk = L // TK
    alibi, forget = _alibi_placement(), _forget_placement()
    q_t = pl.BlockSpec((1, HEADS, LANES, tm), lambda b, i: (b, 0, 0, i))
    k_aug = pl.BlockSpec((1, HEADS, tm, LANES), lambda b, i: (b, 0, i, 0))
    v_t = pl.BlockSpec((1, HEADS, 1, V_ROWS, TK), lambda b, i: (b, 0, i, 0, 0))
    q_t_shape = jax.ShapeDtypeStruct((B, HEADS, LANES, L), BF16)
    k_aug_shape = jax.ShapeDtypeStruct((B, HEADS, L, LANES), BF16)
    v_t_shape = jax.ShapeDtypeStruct((B, HEADS, nk, V_ROWS, TK), BF16)
    return pl.pallas_call(
        _proj_kernel,
        grid=(B, nk),
        in_specs=[pl.BlockSpec((1, tm, D), lambda b, i: (b, i, 0)),
                  _layer_spec(w_main, layer), _layer_spec(w_small, layer),
                  _const_spec(alibi.shape), _const_spec(forget.shape), _layer_spec(bias_row, layer)],
        out_specs=[pl.BlockSpec((1, N_GATE_BLOCKS, tm, LANES), lambda b, i: (b, 0, i, 0)),
                   pl.BlockSpec((1, tm, LANES), lambda b, i: (b, i, 0)),
                   q_t, k_aug, pl.BlockSpec((1, HEADS, 1, tm), lambda b, i: (b, 0, 0, i)), v_t,
                   q_t, k_aug, v_t,
                   pl.BlockSpec((1, PAIRS, LANES, tm), lambda b, i: (b, 0, 0, i)),
                   pl.BlockSpec((1, HEADS, tm), lambda b, i: (b, 0, i))],
        out_shape=[jax.ShapeDtypeStruct((B, N_GATE_BLOCKS, L, LANES), BF16),
                   jax.ShapeDtypeStruct((B, L, LANES), F32),
                   q_t_shape, k_aug_shape, jax.ShapeDtypeStruct((B, HEADS, 1, L), F32), v_t_shape,
                   q_t_shape, k_aug_shape, v_t_shape,
                   jax.ShapeDtypeStruct((B, PAIRS, LANES, L), BF16),
                   jax.ShapeDtypeStruct((B, HEADS, L), F32)],
        scratch_shapes=[pltpu.VMEM((1, LANES), F32)],
        compiler_params=_params(("parallel", "arbitrary")),
    )(x, w_main, w_small, alibi, forget, bias_row)


def _split3(v):
    hi = v.astype(BF16).astype(F32)
    r = v - hi
    mid = r.astype(BF16).astype(F32)
    lo = (r - mid).astype(BF16).astype(F32)
    return hi, mid, lo


def _cumsum_rows(x):
    n = CUMSUM_BLOCK
    L = x.shape[0]
    r = lax.broadcasted_iota(I32, (n, n), 0)
    c = lax.broadcasted_iota(I32, (n, n), 1)
    tri = jnp.where(c <= r, 1.0, 0.0).astype(BF16)
    carry = jnp.zeros((1, x.shape[1]), F32)
    outs = []
    w = x.shape[1]
    for blk in range(L // n):
        parts = jnp.concatenate([p.astype(BF16) for p in _split3(x[blk * n:(blk + 1) * n])], axis=1)
        y3 = jnp.dot(tri, parts, preferred_element_type=F32)
        y = (y3[:, :w] + y3[:, w:2 * w] + y3[:, 2 * w:]) + carry
        carry = y[n - 1:n, :]
        outs.append(y)
    return jnp.concatenate(outs, axis=0)


def _fold8(x, op):
    return op(x.reshape(x.shape[0] // 8, 8, x.shape[1]), axis=0)


def _causal(shape):
    return lax.broadcasted_iota(I32, shape, 0) <= lax.broadcasted_iota(I32, shape, 1)


L_SUM_MIN, L_SUM_MAX = 1e-30, 1e30
NEAR_BLOCK = 64


def _attend_heads(n_full, bias_fn, ref_score_fn, qT_ref, k_ref, vT_ref, g_sc, acc_sc, o_ref):
    def for_chunks(chunk):
        def pair(jj, carry):
            chunk(2 * jj, False)
            chunk(2 * jj + 1, False)
            return carry

        def single(_, carry):
            chunk(n_full - 1, False)
            return carry

        lax.fori_loop(0, lax.shift_right_logical(n_full, 1), pair, 0)
        lax.fori_loop(0, n_full & 1, single, 0)
        chunk(n_full, True)

    def scores(h, rows, diag, raw):
        s = raw * LOG2E
        bias = bias_fn(rows, diag)
        return s if bias is None else s + bias

    def sweep():
        acc_sc[...] = jnp.zeros(acc_sc.shape, F32)

        def chunk(j, diag):
            rows = pl.ds(pl.multiple_of(j * TK, TK), TK)

            def raw_scores(h):
                return jnp.dot(k_ref[0, h, rows, :], qT_ref[0, h], preferred_element_type=F32)

            nxt = raw_scores(0)
            for h in range(HEADS):
                raw = nxt
                if h + 1 < HEADS:
                    nxt = raw_scores(h + 1)
                p = jnp.exp2(scores(h, rows, diag, raw) - g_sc[h]).astype(BF16)
                acc_sc[h] += jnp.dot(vT_ref[0, h, j], p, preferred_element_type=F32)

        for_chunks(chunk)
        bad = jnp.zeros((1, TQ), I32)
        for p in range(PAIRS):
            halves = []
            for e in range(2):
                acc = acc_sc[2 * p + e]
                l = acc[HEAD_DIM:HEAD_DIM + 1]
                bad = jnp.maximum(bad, jnp.where((l > L_SUM_MIN) & (l < L_SUM_MAX), 0, 1))
                halves.append(acc[:HEAD_DIM] / l)
            o_ref[0, :, p * LANES:(p + 1) * LANES] = jnp.concatenate(halves, axis=0).T.astype(o_ref.dtype)
        return jnp.max(bad) > 0

    def exact_row_maxima():
        acc_sc[:, 0:8, :] = jnp.full((HEADS, 8, TQ), NEG_INF, F32)

        def chunk(j, diag):
            rows = pl.ds(pl.multiple_of(j * TK, TK), TK)
            for h in range(HEADS):
                raw = jnp.dot(k_ref[0, h, rows, :], qT_ref[0, h], preferred_element_type=F32)
                acc_sc[h, 0:8, :] = jnp.maximum(acc_sc[h, 0:8, :], _fold8(scores(h, rows, diag, raw), jnp.max))

        for_chunks(chunk)
        for h in range(HEADS):
            g_sc[h] = jnp.max(acc_sc[h, 0:8, :], axis=0, keepdims=True)

    for h in range(HEADS):
        g_sc[h] = ref_score_fn(h)
    unsafe = sweep()

    @pl.when(unsafe)
    def _redo_with_exact_maxima():
        exact_row_maxima()
        sweep()


def _head_specs(L, index):
    nk = L // TK
    return [pl.BlockSpec((1, HEADS, L, LANES), lambda *g: (index(*g), 0, 0, 0)),
            pl.BlockSpec((1, HEADS, nk, V_ROWS, TK), lambda *g: (index(*g), 0, 0, 0, 0))]


def _head_scratch():
    return [pltpu.VMEM((HEADS, 1, TQ), F32), pltpu.VMEM((HEADS, V_ROWS, TQ), F32)]


def _fox_kernel(qT_ref, own_ref, k_ref, vT_ref, o_ref, g_sc, acc_sc):
    i = pl.program_id(1)

    def bias_fn(rows, diag):
        return jnp.where(_causal((TK, TQ)), 0.0, NEG_INF) if diag else None

    _attend_heads(i, bias_fn, lambda h: own_ref[0, h], qT_ref, k_ref, vT_ref, g_sc, acc_sc, o_ref)


def _fox_attention(qT, own, ka, vT):
    B, _, _, L = qT.shape
    return pl.pallas_call(
        _fox_kernel,
        grid=(B, L // TQ),
        in_specs=[pl.BlockSpec((1, HEADS, LANES, TQ), lambda b, i: (b, 0, 0, i)),
                  pl.BlockSpec((1, HEADS, 1, TQ), lambda b, i: (b, 0, 0, i))] + _head_specs(L, lambda b, i: b),
        out_specs=pl.BlockSpec((1, TQ, PAIRS * LANES), lambda b, i: (b, i, 0)),
        out_shape=jax.ShapeDtypeStruct((B, L, PAIRS * LANES), BF16),
        scratch_shapes=_head_scratch(),
        compiler_params=_params(("parallel", "arbitrary")),
    )(qT, own, ka, vT)


def _count_rows(mask):
    return _fold8(jnp.where(mask, 1, 0).astype(I32), jnp.sum)


WORD_BITS = 32
PLANE_ROWS = TK // WORD_BITS
SUBLANES = 8


def _transpose_bits(words):
    a = list(words)
    shift, mask = WORD_BITS // 2, 0x0000FFFF
    while shift:
        k = 0
        while k < WORD_BITS:
            t = (a[k] ^ lax.shift_right_logical(a[k + shift], shift)) & mask
            a[k] = a[k] ^ t
            a[k + shift] = a[k + shift] ^ lax.shift_left(t, shift)
            k = (k + shift + 1) & ~shift
        shift >>= 1
        mask ^= mask << shift
    return a


def _dsa_kernel(qT_ref, k_ref, vT_ref, iqT_ref, iwT_ref, s_ref, o_ref,
                keys_sc, planes_sc, alive_sc, g_sc, acc_sc):
    i = pl.program_id(1)
    n_chunks = i + 1

    def plane_rows(j):
        return pl.ds(pl.multiple_of(j * PLANE_ROWS, PLANE_ROWS), PLANE_ROWS)

    def _select():
        wts = iwT_ref[0]

        def score_chunk(j, masked):
            ks = pl.multiple_of(j * TK, TK)
            ik = s_ref[0, pl.ds(ks, TK), :][:, _SM_IK:_SM_IK + HEAD_DIM].astype(BF16)
            acc = jnp.zeros((TK, TQ), F32)
            for h in range(HEADS):
                iq = iqT_ref[0, h // 2, (h % 2) * HEAD_DIM:(h % 2 + 1) * HEAD_DIM, :]
                rel = jnp.maximum(jnp.dot(ik, iq, preferred_element_type=F32), 0.0)
                acc = acc + wts[h:h + 1, :] * rel
            if masked:
                acc = jnp.where(_causal(acc.shape), acc, NEG_INF)
            bits = pltpu.bitcast(acc, I32)
            sign = lax.shift_right_arithmetic(bits, 31)
            key = (bits ^ (sign & 0x7FFFFFFF)) - sign
            keys_sc[pl.ds(ks, TK), :] = key
            half = WORD_BITS * SUBLANES
            for hf in range(TK // half):
                for lt in range(TQ // LANES):
                    lanes = slice(lt * LANES, (lt + 1) * LANES)
                    words = [key[hf * half + SUBLANES * m:hf * half + SUBLANES * (m + 1), lanes]
                             for m in range(WORD_BITS)]
                    rows = pl.ds(pl.multiple_of(j * PLANE_ROWS + hf * SUBLANES, SUBLANES), SUBLANES)
                    for b, plane in enumerate(_transpose_bits(words)):
                        planes_sc[b, rows, lanes] = plane ^ -1 if b == 0 else plane
            alive_sc[plane_rows(j), :] = jnp.full((PLANE_ROWS, TQ), -1, I32)

        def score_body(j, carry):
            score_chunk(j, False)
            return carry

        lax.fori_loop(0, i, score_body, 0)
        score_chunk(i, True)

        def count(pred):
            def body(j, cnt):
                ks = pl.multiple_of(j * TK, TK)
                return cnt + _count_rows(pred(keys_sc[pl.ds(ks, TK), :]))
            cnt8 = lax.fori_loop(0, n_chunks, body, jnp.zeros((8, TQ), I32))
            return jnp.sum(cnt8, axis=0, keepdims=True)

        def count_over_chunks(step, n_counts):
            zero = tuple(jnp.zeros((PLANE_ROWS, TQ), I32) for _ in range(n_counts))
            cnt = lax.fori_loop(0, lax.shift_right_logical(n_chunks, 1),
                                lambda jj, cnt: step(2 * jj + 1, step(2 * jj, cnt)), zero)
            cnt = lax.fori_loop(0, n_chunks & 1, lambda _, cnt: step(n_chunks - 1, cnt), cnt)
            return tuple(jnp.sum(c, axis=0, keepdims=True) for c in cnt)

        def matching(rows, first, flips, keep_all):
            alive = alive_sc[rows, :]
            for d, flip in enumerate(flips):
                alive = alive & ((planes_sc[first + d, rows, :] ^ flip) | keep_all)
            return alive

        def bit_pair_pass(pi, carry):
            need, thr_bits, flips = carry[0], carry[1], carry[2:]
            hi = 2 * pi
            prev = jnp.maximum(hi - 2, 0)
            keep_all = jnp.where(pi == 0, -1, 0)

            def body(j, cnt):
                rows = plane_rows(j)
                alive = matching(rows, prev, flips, keep_all)
                alive_sc[rows, :] = alive
                upper = alive & planes_sc[hi, rows, :]
                low_plane = planes_sc[hi + 1, rows, :]
                return (cnt[0] + lax.population_count(upper),
                        cnt[1] + lax.population_count(upper & low_plane),
                        cnt[2] + lax.population_count((alive ^ upper) & low_plane))

            n_1x, n_11, n_01 = count_over_chunks(body, 3)
            ok1 = n_1x >= need
            need = jnp.where(ok1, need, need - n_1x)
            n_x1 = jnp.where(ok1, n_11, n_01)
            ok2 = n_x1 >= need
            need = jnp.where(ok2, need, need - n_x1)
            one = jnp.ones((1, TQ), I32)
            bits = (jnp.where(ok1, jnp.left_shift(one, WORD_BITS - 1 - hi), 0)
                    | jnp.where(ok2, jnp.left_shift(one, WORD_BITS - 2 - hi), 0))
            return need, thr_bits | bits, jnp.where(ok1, 0, -1), jnp.where(ok2, 0, -1)

        zero_row = jnp.zeros((1, TQ), I32)
        need, thr_bits, *flips = lax.fori_loop(0, WORD_BITS // 2, bit_pair_pass,
                                               (jnp.full((1, TQ), TOPK, I32), zero_row, zero_row, zero_row))
        thr = thr_bits ^ INT_MIN

        def count_equal(j, cnt):
            return (cnt[0] + lax.population_count(matching(plane_rows(j), WORD_BITS - 2, flips, 0)),)

        (n_equal,) = count_over_chunks(count_equal, 1)

        excess = jnp.where((n_equal > need) & (thr > NEG_INF_KEY), 1, 0)

        @pl.when(jnp.max(excess) > 0)
        def _break_ties():
            need = TOPK - count(lambda kk: kk > thr)
            r = lax.broadcasted_iota(I32, (TK, TK), 0)
            c = lax.broadcasted_iota(I32, (TK, TK), 1)
            strict_lower = jnp.where(c < r, 1.0, 0.0).astype(BF16)

            def body(j, seen):
                ks = pl.multiple_of(j * TK, TK)
                kk = keys_sc[pl.ds(ks, TK), :]
                tie = kk == thr
                tie_f = jnp.where(tie, 1.0, 0.0)
                before = jnp.dot(strict_lower, tie_f.astype(BF16), preferred_element_type=F32) + seen
                drop = tie & (before >= need.astype(F32))
                keys_sc[pl.ds(ks, TK), :] = jnp.where(drop, INT_MIN, kk)
                return seen + jnp.sum(tie_f, axis=0, keepdims=True)

            lax.fori_loop(0, n_chunks, body, jnp.zeros((1, TQ), F32))

        def bias_chunk(j, masked, near8):
            ks = pl.multiple_of(j * TK, TK)
            bias = jnp.where(keys_sc[pl.ds(ks, TK), :] >= thr, 0.0, NEG_INF)
            if masked:
                bias = jnp.where(_causal(bias.shape), bias, NEG_INF)
            keys_sc[pl.ds(ks, TK), :] = pltpu.bitcast(bias, I32)
            for blk in range(TK // NEAR_BLOCK):
                any_selected = _fold8(bias[blk * NEAR_BLOCK:(blk + 1) * NEAR_BLOCK], jnp.max) == 0.0
                near8 = jnp.maximum(near8, jnp.where(any_selected, ks + (blk + 1) * NEAR_BLOCK - 1, -1))
            return near8

        near8 = lax.fori_loop(0, i, lambda j, n8: bias_chunk(j, False, n8), jnp.full((8, TQ), -1, I32))
        return jnp.max(bias_chunk(i, True, near8), axis=0, keepdims=True)

    nearest = _select()

    def stored_bias(rows, diag):
        return pltpu.bitcast(keys_sc[rows, :], F32)

    def nearest_key_bias(h):
        query = i * TQ + lax.broadcasted_iota(I32, (1, TQ), 1)
        return (query - nearest).astype(F32) * (-(2.0 ** -(h + 1)) * LOG2E)

    _attend_heads(i, stored_bias, nearest_key_bias, qT_ref, k_ref, vT_ref, g_sc, acc_sc, o_ref)


def _dsa_attention(qT, ka, vT, iqT, iwT, small):
    B, _, _, L = qT.shape
    return pl.pallas_call(
        _dsa_kernel,
        grid=(B, L // TQ),
        in_specs=[pl.BlockSpec((1, HEADS, LANES, TQ), lambda b, i: (b, 0, 0, i))] + _head_specs(L, lambda b, i: b)
        + [pl.BlockSpec((1, PAIRS, LANES, TQ), lambda b, i: (b, 0, 0, i)),
           pl.BlockSpec((1, HEADS, TQ), lambda b, i: (b, 0, i)),
           pl.BlockSpec((1, L, LANES), lambda b, i: (b, 0, 0))],
        out_specs=pl.BlockSpec((1, TQ, PAIRS * LANES), lambda b, i: (b, i, 0)),
        out_shape=jax.ShapeDtypeStruct((B, L, PAIRS * LANES), BF16),
        scratch_shapes=[pltpu.VMEM((L, TQ), I32), pltpu.VMEM((WORD_BITS, L // WORD_BITS, TQ), I32),
                        pltpu.VMEM((L // WORD_BITS, TQ), I32)]
        + _head_scratch(),
        compiler_params=_params(("parallel", "arbitrary")),
    )(qT, ka, vT, iqT, iwT, small)


def _layer_norm(z, g, b):
    mu = jnp.mean(z, axis=-1, keepdims=True)
    zc = z - mu
    var = jnp.mean(zc * zc, axis=-1, keepdims=True)
    return zc * lax.rsqrt(var + LN_EPS) * g + b


FF_CHUNK = 256


def _merge_ffn_kernel(x_ref, oa_ref, ob_ref, ga_ref, gb_ref, wa_ref, wb_ref, wo_ref, g1_ref, b1_ref,
                      wi_ref, wf_ref, g2_ref, b2_ref, out_ref, h_sc, *, alpha):
    a = jnp.dot(oa_ref[0], wa_ref[...], preferred_element_type=F32)
    b = jnp.dot(ob_ref[0], wb_ref[...], preferred_element_type=F32)
    n_gate_blocks = D_MODEL // LANES
    gate_a = jnp.concatenate([ga_ref[0, c] for c in range(n_gate_blocks)], axis=1).astype(F32)
    gate_b = jnp.concatenate([gb_ref[0, c] for c in range(n_gate_blocks)], axis=1).astype(F32)
    merged = jax.nn.sigmoid(gate_a) * a + jax.nn.sigmoid(gate_b) * b
    y = jnp.dot(merged.astype(BF16), wo_ref[...], preferred_element_type=F32)
    x = _layer_norm(alpha * x_ref[0] + y, g1_ref[...], b1_ref[...])
    xb = x.astype(BF16)
    for c in range(D_FF // FF_CHUNK):
        lo = c * FF_CHUNK
        gate = jnp.dot(xb, wi_ref[:, lo:lo + FF_CHUNK], preferred_element_type=F32)
        up = jnp.dot(xb, wi_ref[:, D_FF + lo:D_FF + lo + FF_CHUNK], preferred_element_type=F32)
        h_sc[:, lo:lo + FF_CHUNK] = (jax.nn.silu(gate) * up).astype(BF16)
    y = jnp.dot(h_sc[...], wf_ref[...], preferred_element_type=F32)
    out_ref[0] = _layer_norm(alpha * x + y, g2_ref[...], b2_ref[...])


def _merge_ffn(x, oa, ob, gates, stacked, layer, alpha):
    B, L, D = x.shape
    tm = ROW_TILE
    nb = D // LANES
    row = lambda b_, i: (b_, i, 0)
    return pl.pallas_call(
        functools.partial(_merge_ffn_kernel, alpha=alpha),
        grid=(B, L // tm),
        in_specs=[pl.BlockSpec((1, tm, D), row),
                  pl.BlockSpec((1, tm, PAIRS * LANES), row),
                  pl.BlockSpec((1, tm, PAIRS * LANES), row),
                  pl.BlockSpec((1, nb, tm, LANES), lambda b_, i: (b_, _CB_GA // nb, i, 0)),
                  pl.BlockSpec((1, nb, tm, LANES), lambda b_, i: (b_, _CB_GB // nb, i, 0))]
        + [_layer_spec(p, layer) for p in stacked],
        out_specs=pl.BlockSpec((1, tm, D), row),
        out_shape=jax.ShapeDtypeStruct((B, L, D), F32),
        scratch_shapes=[pltpu.VMEM((tm, D_FF), BF16)],
        compiler_params=_params(("parallel", "parallel")),
    )(x, oa, ob, gates, gates, *stacked)


def _split_w_in(w):
    fq_fk = w[..., 0:2 * GROUP]
    fv = w[..., 2 * GROUP:_OFF_FLOGIT]
    dsa_iq = w[..., _OFF_DQ:_OFF_IK]
    gates = w[..., _OFF_GA:_N_IN]
    w_main = jnp.concatenate([fq_fk, gates, fv, dsa_iq], axis=-1).astype(BF16)
    pad = jnp.zeros(w.shape[:-1] + (LANES - HEAD_DIM - 2 * HEADS,), w.dtype)
    w_small = jnp.concatenate([w[..., _OFF_IK:_OFF_IW], w[..., _OFF_IW:_OFF_GA],
                               w[..., _OFF_FLOGIT:_OFF_DQ], pad], axis=-1).astype(BF16)
    return w_main, w_small


def kernel(x, w_in, b_forget, w_branch_a, w_branch_b, w_out, ln1_g, ln1_b, w_ffn_in, w_ffn_out, ln2_g, ln2_b):
    depth = w_in.shape[0]
    alpha = (2.0 * depth) ** 0.25
    w_main, w_small = _split_w_in(w_in)
    bias_rows = jnp.zeros((depth, 1, LANES), F32).at[:, 0, _SM_FL:_SM_FL + HEADS].set(b_forget)
    wa, wb, wo = w_branch_a.astype(BF16), w_branch_b.astype(BF16), w_out.astype(BF16)
    wi, wf = w_ffn_in.astype(BF16), w_ffn_out.astype(BF16)
    row = lambda p: p[:, None, :]
    stacked = (wa, wb, wo, row(ln1_g), row(ln1_b), wi, wf, row(ln2_g), row(ln2_b))
    for l in range(depth):
        gates, small, fqT, fka, f_own, fvT, dqT, dka, dvT, iqT, iwT = _project(x, w_main, w_small, bias_rows, l)
        o_a = _fox_attention(fqT, f_own, fka, fvT)
        o_b = _dsa_attention(dqT, dka, dvT, iqT, iwT, small)
        x = _merge_ffn(x, o_a, o_b, gates, stacked, l, alpha)
    return x
```

```python
import functools

import numpy as np
import jax
import jax.numpy as jnp
from jax import lax
from jax.experimental import pallas as pl
from jax.experimental.pallas import tpu as pltpu

F32 = jnp.float32
BF16 = jnp.bfloat16
I32 = jnp.int32

D_MODEL = 1024
HEADS = 8
HEAD_DIM = 64
PAIRS = HEADS // 2
D_FF = 2816
TOPK = 256
LANES = 128
NEG_INF = -1e30
LN_EPS = 1e-5
ATTN_SCALE = HEAD_DIM ** -0.5
IDX_SCALE = (HEADS ** -0.5) * (HEAD_DIM ** -0.5)

_OFF_FLOGIT = 1536
_OFF_DQ = 1544
_OFF_IK = 3592
_OFF_IW = 3656
_OFF_GA = 3664
_N_IN = 5712

_G_FQ, _G_FK, _G_GATES, _G_FV, _G_DQ, _G_DK, _G_DV, _G_IQ = 0, 1, (2, 3, 4, 5), 6, 7, 8, 9, 10
_CB_GA, _CB_GB = 0, 8
N_GATE_BLOCKS = 16
GROUP = 4 * LANES
N_MAIN = 11 * GROUP
_SM_IK, _SM_IW, _SM_FL = 0, 64, 72

ROW_TILE = 512
TQ = 512
TK = 512
CUMSUM_BLOCK = 256
V_ROWS = HEAD_DIM + 16
LOG2E = 1.4426950408889634

INT_MIN = -(2 ** 31)


def _monotone_key_of(value):
    b = int(np.float32(value).view(np.int32))
    return (b ^ ((b >> 31) & 0x7FFFFFFF)) - (b >> 31)


NEG_INF_KEY = _monotone_key_of(NEG_INF)
VMEM_LIMIT = 60 * 1024 * 1024


def _params(sem):
    return pltpu.CompilerParams(dimension_semantics=sem, vmem_limit_bytes=VMEM_LIMIT)


def _const_spec(shape):
    nd = len(shape)
    return pl.BlockSpec(shape, lambda *_: (0,) * nd, pipeline_mode=pl.Buffered(1))


def _layer_spec(stacked, layer):
    tail = stacked.shape[1:]
    return pl.BlockSpec((None,) + tail, lambda *_: (layer,) + (0,) * len(tail), pipeline_mode=pl.Buffered(1))


def _lane_row(lo, hi, value=1.0):
    lane = lax.broadcasted_iota(I32, (1, LANES), 1)
    return jnp.where((lane >= lo) & (lane < hi), value, 0.0)


BIAS_COLS = 8


def _alibi_placement():
    put = np.zeros((LANES, 2 * LANES), np.float32)
    for h in range(HEADS):
        slope = 2.0 ** -(h + 1)
        qc, kc = BIAS_COLS * h, LANES + BIAS_COLS * h
        put[0, qc], put[1, qc + 1], put[2, qc + 2], put[2, qc + 3] = -64.0 * slope, -slope, 64.0 * slope, slope
        put[2, kc], put[2, kc + 1], put[0, kc + 2], put[1, kc + 3] = 1.0, 1.0, 1.0, 1.0
    return jnp.asarray(put, BF16)


def _forget_placement():
    put = np.zeros((LANES, 2 * LANES), np.float32)
    for h in range(HEADS):
        for part in range(3):
            src = _SM_FL + part * HEADS + h
            put[src, BIAS_COLS * h + part] = 1.0
            put[src, LANES + BIAS_COLS * h + 3 + part] = -1.0
    return jnp.asarray(put, BF16)


def _proj_kernel(x_ref, wm_ref, ws_ref, alibi_ref, forget_ref, bias_ref,
                 gates_ref, os_ref, fqT_ref, fka_ref, own_ref, fvT_ref, dqT_ref, dka_ref, dvT_ref, iqT_ref, iwT_ref,
                 carry_sc):
    tm = x_ref.shape[1]
    xb = x_ref[0].astype(BF16)

    def group(g):
        return jnp.dot(xb, wm_ref[:, g * GROUP:(g + 1) * GROUP], preferred_element_type=F32)

    def store_gates(n):
        r = group(_G_GATES[n]).astype(BF16)
        for c in range(4):
            gates_ref[0, 4 * n + c] = r[:, c * LANES:(c + 1) * LANES]

    def pair_and_half(h):
        p, e = divmod(h, 2)
        return slice(p * LANES, (p + 1) * LANES), _lane_row(e * HEAD_DIM, (e + 1) * HEAD_DIM), (1 - e) * HEAD_DIM

    def bias_columns(compact, h, first, n):
        base = (1 - h % 2) * HEAD_DIM
        moved = pltpu.roll(compact, (base - BIAS_COLS * h) % LANES, 1)
        return moved * _lane_row(base + first, base + first + n)

    @pl.when(pl.program_id(1) == 0)
    def _new_sequence():
        carry_sc[...] = jnp.zeros(carry_sc.shape, F32)

    store_gates(0)
    small = jnp.dot(xb, ws_ref[...], preferred_element_type=F32)
    os_ref[0] = small
    iwT_ref[0] = small.T[_SM_IW:_SM_IW + HEADS, :] * IDX_SCALE
    c = _cumsum_rows(jax.nn.log_sigmoid(small + bias_ref[...])) + carry_sc[...]
    carry_sc[...] = c[tm - 1:tm, :]
    hi, mid, lo = _split3(c)
    lane = lax.broadcasted_iota(I32, (tm, LANES), 1)
    parts = jnp.where(lane < _SM_FL + HEADS, hi,
                      jnp.where(lane < _SM_FL + 2 * HEADS, pltpu.roll(mid, HEADS, 1), pltpu.roll(lo, 2 * HEADS, 1)))
    store_gates(1)
    aug = jnp.dot(parts.astype(BF16), forget_ref[...], preferred_element_type=F32)
    q = group(_G_FQ) * ATTN_SCALE
    k = group(_G_FK)
    for h in range(HEADS):
        pair, data, base = pair_and_half(h)
        q_aug = q[:, pair] * data + bias_columns(aug[:, :LANES], h, 0, 3) + _lane_row(base + 3, base + 6)
        k_aug = k[:, pair] * data + bias_columns(aug[:, LANES:], h, 3, 3) + _lane_row(base, base + 3)
        fqT_ref[0, h] = q_aug.T.astype(BF16)
        fka_ref[0, h] = k_aug.astype(BF16)
        own_ref[0, h] = jnp.sum((q_aug * k_aug).T, axis=0, keepdims=True) * LOG2E

    def store_values(ref, r):
        ones = jnp.ones((V_ROWS - HEAD_DIM, tm), BF16)
        for p in range(PAIRS):
            vT = r[:, p * LANES:(p + 1) * LANES].T.astype(BF16)
            for e in range(2):
                ref[0, 2 * p + e, 0] = jnp.concatenate([vT[e * HEAD_DIM:(e + 1) * HEAD_DIM], ones], axis=0)

    store_values(fvT_ref, group(_G_FV))
    store_gates(2)
    store_values(dvT_ref, group(_G_DV))

    lane = lax.broadcasted_iota(I32, (tm, LANES), 1)
    pos = lax.broadcasted_iota(I32, (tm, LANES), 0) + pl.program_id(1) * tm
    pos_terms = jnp.where(lane == 0, lax.shift_right_logical(pos, 6),
                          jnp.where(lane == 1, pos & 63, jnp.where(lane == 2, 1, 0))).astype(F32).astype(BF16)
    aug = jnp.dot(pos_terms, alibi_ref[...], preferred_element_type=F32)
    q = group(_G_DQ) * ATTN_SCALE
    k = group(_G_DK)
    for h in range(HEADS):
        pair, data, _ = pair_and_half(h)
        q_aug = q[:, pair] * data + bias_columns(aug[:, :LANES], h, 0, 4)
        k_aug = k[:, pair] * data + bias_columns(aug[:, LANES:], h, 0, 4)
        dqT_ref[0, h] = q_aug.T.astype(BF16)
        dka_ref[0, h] = k_aug.astype(BF16)
    iq = group(_G_IQ)
    for p in range(PAIRS):
        iqT_ref[0, p] = iq[:, p * LANES:(p + 1) * LANES].T.astype(BF16)
    store_gates(3)


def _project(x, w_main, w_small, bias_row, layer):
    B, L, D = x.shape
    tm = TK
    nk = L // TK
    alibi, forget = _alibi_placement(), _forget_placement()
    q_t = pl.BlockSpec((1, HEADS, LANES, tm), lambda b, i: (b, 0, 0, i))
    k_aug = pl.BlockSpec((1, HEADS, tm, LANES), lambda b, i: (b, 0, i, 0))
    v_t = pl.BlockSpec((1, HEADS, 1, V_ROWS, TK), lambda b, i: (b, 0, i, 0, 0))
    q_t_shape = jax.ShapeDtypeStruct((B, HEADS, LANES, L), BF16)
    k_aug_shape = jax.ShapeDtypeStruct((B, HEADS, L, LANES), BF16)
    v_t_shape = jax.ShapeDtypeStruct((B, HEADS, nk, V_ROWS, TK), BF16)
    return pl.pallas_call(
        _proj_kernel,
        grid=(B, nk),
        in_specs=[pl.BlockSpec((1, tm, D), lambda b, i: (b, i, 0)),
                  _layer_spec(w_main, layer), _layer_spec(w_small, layer),
                  _const_spec(alibi.shape), _const_spec(forget.shape), _layer_spec(bias_row, layer)],
        out_specs=[pl.BlockSpec((1, N_GATE_BLOCKS, tm, LANES), lambda b, i: (b, 0, i, 0)),
                   pl.BlockSpec((1, tm, LANES), lambda b, i: (b, i, 0)),
                   q_t, k_aug, pl.BlockSpec((1, HEADS, 1, tm), lambda b, i: (b, 0, 0, i)), v_t,
                   q_t, k_aug, v_t,
                   pl.BlockSpec((1, PAIRS, LANES, tm), lambda b, i: (b, 0, 0, i)),
                   pl.BlockSpec((1, HEADS, tm), lambda b, i: (b, 0, i))],
        out_shape=[jax.ShapeDtypeStruct((B, N_GATE_BLOCKS, L, LANES), BF16),
                   jax.ShapeDtypeStruct((B, L, LANES), F32),
                   q_t_shape, k_aug_shape, jax.ShapeDtypeStruct((B, HEADS, 1, L), F32), v_t_shape,
                   q_t_shape, k_aug_shape, v_t_shape,
                   jax.ShapeDtypeStruct((B, PAIRS, LANES, L), BF16),
                   jax.ShapeDtypeStruct((B, HEADS, L), F32)],
        scratch_shapes=[pltpu.VMEM((1, LANES), F32)],
        compiler_params=_params(("parallel", "arbitrary")),
    )(x, w_main, w_small, alibi, forget, bias_row)


def _split3(v):
    hi = v.astype(BF16).astype(F32)
    r = v - hi
    mid = r.astype(BF16).astype(F32)
    lo = (r - mid).astype(BF16).astype(F32)
    return hi, mid, lo


def _cumsum_rows(x):
    n = CUMSUM_BLOCK
    L = x.shape[0]
    r = lax.broadcasted_iota(I32, (n, n), 0)
    c = lax.broadcasted_iota(I32, (n, n), 1)
    tri = jnp.where(c <= r, 1.0, 0.0).astype(BF16)
    carry = jnp.zeros((1, x.shape[1]), F32)
    outs = []
    w = x.shape[1]
    for blk in range(L // n):
        parts = jnp.concatenate([p.astype(BF16) for p in _split3(x[blk * n:(blk + 1) * n])], axis=1)
        y3 = jnp.dot(tri, parts, preferred_element_type=F32)
        y = (y3[:, :w] + y3[:, w:2 * w] + y3[:, 2 * w:]) + carry
        carry = y[n - 1:n, :]
        outs.append(y)
    return jnp.concatenate(outs, axis=0)


def _fold8(x, op):
    return op(x.reshape(x.shape[0] // 8, 8, x.shape[1]), axis=0)


def _causal(shape):
    return lax.broadcasted_iota(I32, shape, 0) <= lax.broadcasted_iota(I32, shape, 1)


L_SUM_MIN, L_SUM_MAX = 1e-30, 1e30
NEAR_BLOCK = 64


def _attend_heads(n_full, bias_fn, ref_score_fn, qT_ref, k_ref, vT_ref, g_sc, acc_sc, o_ref):
    def for_chunks(chunk):
        def pair(jj, carry):
            chunk(2 * jj, False)
            chunk(2 * jj + 1, False)
            return carry

        def single(_, carry):
            chunk(n_full - 1, False)
            return carry

        lax.fori_loop(0, lax.shift_right_logical(n_full, 1), pair, 0)
        lax.fori_loop(0, n_full & 1, single, 0)
        chunk(n_full, True)

    def scores(h, rows, diag, raw):
        s = raw * LOG2E
        bias = bias_fn(rows, diag)
        return s if bias is None else s + bias

    def sweep():
        acc_sc[...] = jnp.zeros(acc_sc.shape, F32)

        def chunk(j, diag):
            rows = pl.ds(pl.multiple_of(j * TK, TK), TK)

            def raw_scores(h):
                return jnp.dot(k_ref[0, h, rows, :], qT_ref[0, h], preferred_element_type=F32)

            nxt = raw_scores(0)
            for h in range(HEADS):
                raw = nxt
                if h + 1 < HEADS:
                    nxt = raw_scores(h + 1)
                p = jnp.exp2(scores(h, rows, diag, raw) - g_sc[h]).astype(BF16)
                acc_sc[h] += jnp.dot(vT_ref[0, h, j], p, preferred_element_type=F32)

        for_chunks(chunk)
        bad = jnp.zeros((1, TQ), I32)
        for p in range(PAIRS):
            halves = []
            for e in range(2):
                acc = acc_sc[2 * p + e]
                l = acc[HEAD_DIM:HEAD_DIM + 1]
                bad = jnp.maximum(bad, jnp.where((l > L_SUM_MIN) & (l < L_SUM_MAX), 0, 1))
                halves.append(acc[:HEAD_DIM] / l)
            o_ref[0, :, p * LANES:(p + 1) * LANES] = jnp.concatenate(halves, axis=0).T.astype(o_ref.dtype)
        return jnp.max(bad) > 0

    def exact_row_maxima():
        acc_sc[:, 0:8, :] = jnp.full((HEADS, 8, TQ), NEG_INF, F32)

        def chunk(j, diag):
            rows = pl.ds(pl.multiple_of(j * TK, TK), TK)
            for h in range(HEADS):
                raw = jnp.dot(k_ref[0, h, rows, :], qT_ref[0, h], preferred_element_type=F32)
                acc_sc[h, 0:8, :] = jnp.maximum(acc_sc[h, 0:8, :], _fold8(scores(h, rows, diag, raw), jnp.max))

        for_chunks(chunk)
        for h in range(HEADS):
            g_sc[h] = jnp.max(acc_sc[h, 0:8, :], axis=0, keepdims=True)

    for h in range(HEADS):
        g_sc[h] = ref_score_fn(h)
    unsafe = sweep()

    @pl.when(unsafe)
    def _redo_with_exact_maxima():
        exact_row_maxima()
        sweep()


def _head_specs(L, index):
    nk = L // TK
    return [pl.BlockSpec((1, HEADS, L, LANES), lambda *g: (index(*g), 0, 0, 0)),
            pl.BlockSpec((1, HEADS, nk, V_ROWS, TK), lambda *g: (index(*g), 0, 0, 0, 0))]


def _head_scratch():
    return [pltpu.VMEM((HEADS, 1, TQ), F32), pltpu.VMEM((HEADS, V_ROWS, TQ), F32)]


def _fox_kernel(qT_ref, own_ref, k_ref, vT_ref, o_ref, g_sc, acc_sc):
    i = pl.program_id(1)

    def bias_fn(rows, diag):
        return jnp.where(_causal((TK, TQ)), 0.0, NEG_INF) if diag else None

    _attend_heads(i, bias_fn, lambda h: own_ref[0, h], qT_ref, k_ref, vT_ref, g_sc, acc_sc, o_ref)


def _fox_attention(qT, own, ka, vT):
    B, _, _, L = qT.shape
    return pl.pallas_call(
        _fox_kernel,
        grid=(B, L // TQ),
        in_specs=[pl.BlockSpec((1, HEADS, LANES, TQ), lambda b, i: (b, 0, 0, i)),
                  pl.BlockSpec((1, HEADS, 1, TQ), lambda b, i: (b, 0, 0, i))] + _head_specs(L, lambda b, i: b),
        out_specs=pl.BlockSpec((1, TQ, PAIRS * LANES), lambda b, i: (b, i, 0)),
        out_shape=jax.ShapeDtypeStruct((B, L, PAIRS * LANES), BF16),
        scratch_shapes=_head_scratch(),
        compiler_params=_params(("parallel", "arbitrary")),
    )(qT, own, ka, vT)


def _count_rows(mask):
    return _fold8(jnp.where(mask, 1, 0).astype(I32), jnp.sum)


WORD_BITS = 32
PLANE_ROWS = TK // WORD_BITS
SUBLANES = 8


def _transpose_bits(words):
    a = list(words)
    shift, mask = WORD_BITS // 2, 0x0000FFFF
    while shift:
        k = 0
        while k < WORD_BITS:
            t = (a[k] ^ lax.shift_right_logical(a[k + shift], shift)) & mask
            a[k] = a[k] ^ t
            a[k + shift] = a[k + shift] ^ lax.shift_left(t, shift)
            k = (k + shift + 1) & ~shift
        shift >>= 1
        mask ^= mask << shift
    return a


def _dsa_kernel(qT_ref, k_ref, vT_ref, iqT_ref, iwT_ref, s_ref, o_ref,
                keys_sc, planes_sc, alive_sc, g_sc, acc_sc):
    i = pl.program_id(1)
    n_chunks = i + 1

    def plane_rows(j):
        return pl.ds(pl.multiple_of(j * PLANE_ROWS, PLANE_ROWS), PLANE_ROWS)

    def _select():
        wts = iwT_ref[0]

        def score_chunk(j, masked):
            ks = pl.multiple_of(j * TK, TK)
            ik = s_ref[0, pl.ds(ks, TK), :][:, _SM_IK:_SM_IK + HEAD_DIM].astype(BF16)
            acc = jnp.zeros((TK, TQ), F32)
            for h in range(HEADS):
                iq = iqT_ref[0, h // 2, (h % 2) * HEAD_DIM:(h % 2 + 1) * HEAD_DIM, :]
                rel = jnp.maximum(jnp.dot(ik, iq, preferred_element_type=F32), 0.0)
                acc = acc + wts[h:h + 1, :] * rel
            if masked:
                acc = jnp.where(_causal(acc.shape), acc, NEG_INF)
            bits = pltpu.bitcast(acc, I32)
            sign = lax.shift_right_arithmetic(bits, 31)
            key = (bits ^ (sign & 0x7FFFFFFF)) - sign
            keys_sc[pl.ds(ks, TK), :] = key
            half = WORD_BITS * SUBLANES
            for hf in range(TK // half):
                for lt in range(TQ // LANES):
                    lanes = slice(lt * LANES, (lt + 1) * LANES)
                    words = [key[hf * half + SUBLANES * m:hf * half + SUBLANES * (m + 1), lanes]
                             for m in range(WORD_BITS)]
                    rows = pl.ds(pl.multiple_of(j * PLANE_ROWS + hf * SUBLANES, SUBLANES), SUBLANES)
                    for b, plane in enumerate(_transpose_bits(words)):
                        planes_sc[b, rows, lanes] = plane ^ -1 if b == 0 else plane
            alive_sc[plane_rows(j), :] = jnp.full((PLANE_ROWS, TQ), -1, I32)

        def score_body(j, carry):
            score_chunk(j, False)
            return carry

        @pl.when(i == 0)
        def _():
            planes_sc[...] = jnp.zeros(planes_sc.shape, I32)

        alive_sc[...] = jnp.zeros(alive_sc.shape, I32)
        lax.fori_loop(0, i, score_body, 0)
        score_chunk(i, True)

        def count(pred):
            def body(j, cnt):
                ks = pl.multiple_of(j * TK, TK)
                return cnt + _count_rows(pred(keys_sc[pl.ds(ks, TK), :]))
            cnt8 = lax.fori_loop(0, n_chunks, body, jnp.zeros((8, TQ), I32))
            return jnp.sum(cnt8, axis=0, keepdims=True)

        def count_over_chunks(step):
            cnt = lax.fori_loop(0, lax.shift_right_logical(n_chunks, 1),
                                lambda jj, cnt: step(2 * jj + 1, step(2 * jj, cnt)),
                                jnp.zeros((PLANE_ROWS, TQ), I32))
            cnt = lax.fori_loop(0, n_chunks & 1, lambda _, cnt: step(n_chunks - 1, cnt), cnt)
            return jnp.sum(cnt, axis=0, keepdims=True)

        def bit_pass(bi, carry):
            need, thr_bits, flip = carry
            prev = jnp.maximum(bi - 1, 0)
            keep_all = jnp.where(bi == 0, -1, 0)

            alive = alive_sc[...] & ((planes_sc[prev] ^ flip) | keep_all)
            alive_sc[...] = alive
            ones_here = jnp.sum(lax.population_count(alive & planes_sc[bi]), axis=0, keepdims=True)
            ok = ones_here >= need
            bit = jnp.left_shift(jnp.ones((1, TQ), I32), WORD_BITS - 1 - bi)
            return (jnp.where(ok, need, need - ones_here), jnp.where(ok, thr_bits | bit, thr_bits),
                    jnp.where(ok, 0, -1))

        need, thr_bits, flip = lax.fori_loop(
            0, WORD_BITS, bit_pass,
            (jnp.full((1, TQ), TOPK, I32), jnp.zeros((1, TQ), I32), jnp.zeros((1, TQ), I32)))
        thr = thr_bits ^ INT_MIN

        def count_equal(j, cnt):
            rows = plane_rows(j)
            alive = alive_sc[rows, :] & (planes_sc[WORD_BITS - 1, rows, :] ^ flip)
            return cnt + lax.population_count(alive)

        n_equal = count_over_chunks(count_equal)

        excess = jnp.where((n_equal > need) & (thr > NEG_INF_KEY), 1, 0)

        @pl.when(jnp.max(excess) > 0)
        def _break_ties():
            need = TOPK - count(lambda kk: kk > thr)
            r = lax.broadcasted_iota(I32, (TK, TK), 0)
            c = lax.broadcasted_iota(I32, (TK, TK), 1)
            strict_lower = jnp.where(c < r, 1.0, 0.0).astype(BF16)

            def body(j, seen):
                ks = pl.multiple_of(j * TK, TK)
                kk = keys_sc[pl.ds(ks, TK), :]
                tie = kk == thr
                tie_f = jnp.where(tie, 1.0, 0.0)
                before = jnp.dot(strict_lower, tie_f.astype(BF16), preferred_element_type=F32) + seen
                drop = tie & (before >= need.astype(F32))
                keys_sc[pl.ds(ks, TK), :] = jnp.where(drop, INT_MIN, kk)
                return seen + jnp.sum(tie_f, axis=0, keepdims=True)

            lax.fori_loop(0, n_chunks, body, jnp.zeros((1, TQ), F32))

        def bias_chunk(j, masked, near8):
            ks = pl.multiple_of(j * TK, TK)
            bias = jnp.where(keys_sc[pl.ds(ks, TK), :] >= thr, 0.0, NEG_INF)
            if masked:
                bias = jnp.where(_causal(bias.shape), bias, NEG_INF)
            keys_sc[pl.ds(ks, TK), :] = pltpu.bitcast(bias, I32)
            for blk in range(TK // NEAR_BLOCK):
                any_selected = _fold8(bias[blk * NEAR_BLOCK:(blk + 1) * NEAR_BLOCK], jnp.max) == 0.0
                near8 = jnp.maximum(near8, jnp.where(any_selected, ks + (blk + 1) * NEAR_BLOCK - 1, -1))
            return near8

        near8 = lax.fori_loop(0, i, lambda j, n8: bias_chunk(j, False, n8), jnp.full((8, TQ), -1, I32))
        return jnp.max(bias_chunk(i, True, near8), axis=0, keepdims=True)

    nearest = _select()

    def stored_bias(rows, diag):
        return pltpu.bitcast(keys_sc[rows, :], F32)

    def nearest_key_bias(h):
        query = i * TQ + lax.broadcasted_iota(I32, (1, TQ), 1)
        return (query - nearest).astype(F32) * (-(2.0 ** -(h + 1)) * LOG2E)

    _attend_heads(i, stored_bias, nearest_key_bias, qT_ref, k_ref, vT_ref, g_sc, acc_sc, o_ref)


def _dsa_attention(qT, ka, vT, iqT, iwT, small):
    B, _, _, L = qT.shape
    return pl.pallas_call(
        _dsa_kernel,
        grid=(B, L // TQ),
        in_specs=[pl.BlockSpec((1, HEADS, LANES, TQ), lambda b, i: (b, 0, 0, i))] + _head_specs(L, lambda b, i: b)
        + [pl.BlockSpec((1, PAIRS, LANES, TQ), lambda b, i: (b, 0, 0, i)),
           pl.BlockSpec((1, HEADS, TQ), lambda b, i: (b, 0, i)),
           pl.BlockSpec((1, L, LANES), lambda b, i: (b, 0, 0))],
        out_specs=pl.BlockSpec((1, TQ, PAIRS * LANES), lambda b, i: (b, i, 0)),
        out_shape=jax.ShapeDtypeStruct((B, L, PAIRS * LANES), BF16),
        scratch_shapes=[pltpu.VMEM((L, TQ), I32), pltpu.VMEM((WORD_BITS, L // WORD_BITS, TQ), I32),
                        pltpu.VMEM((L // WORD_BITS, TQ), I32)]
        + _head_scratch(),
        compiler_params=_params(("parallel", "arbitrary")),
    )(qT, ka, vT, iqT, iwT, small)


def _layer_norm(z, g, b):
    mu = jnp.mean(z, axis=-1, keepdims=True)
    zc = z - mu
    var = jnp.mean(zc * zc, axis=-1, keepdims=True)
    return zc * lax.rsqrt(var + LN_EPS) * g + b


FF_CHUNK = 256


def _merge_ffn_kernel(x_ref, oa_ref, ob_ref, ga_ref, gb_ref, wa_ref, wb_ref, wo_ref, g1_ref, b1_ref,
                      wi_ref, wf_ref, g2_ref, b2_ref, out_ref, h_sc, *, alpha):
    a = jnp.dot(oa_ref[0], wa_ref[...], preferred_element_type=F32)
    b = jnp.dot(ob_ref[0], wb_ref[...], preferred_element_type=F32)
    n_gate_blocks = D_MODEL // LANES
    gate_a = jnp.concatenate([ga_ref[0, c] for c in range(n_gate_blocks)], axis=1).astype(F32)
    gate_b = jnp.concatenate([gb_ref[0, c] for c in range(n_gate_blocks)], axis=1).astype(F32)
    merged = jax.nn.sigmoid(gate_a) * a + jax.nn.sigmoid(gate_b) * b
    y = jnp.dot(merged.astype(BF16), wo_ref[...], preferred_element_type=F32)
    x = _layer_norm(alpha * x_ref[0] + y, g1_ref[...], b1_ref[...])
    xb = x.astype(BF16)
    for c in range(D_FF // FF_CHUNK):
        lo = c * FF_CHUNK
        gate = jnp.dot(xb, wi_ref[:, lo:lo + FF_CHUNK], preferred_element_type=F32)
        up = jnp.dot(xb, wi_ref[:, D_FF + lo:D_FF + lo + FF_CHUNK], preferred_element_type=F32)
        h_sc[:, lo:lo + FF_CHUNK] = (jax.nn.silu(gate) * up).astype(BF16)
    y = jnp.dot(h_sc[...], wf_ref[...], preferred_element_type=F32)
    out_ref[0] = _layer_norm(alpha * x + y, g2_ref[...], b2_ref[...])


def _merge_ffn(x, oa, ob, gates, stacked, layer, alpha):
    B, L, D = x.shape
    tm = ROW_TILE
    nb = D // LANES
    row = lambda b_, i: (b_, i, 0)
    return pl.pallas_call(
        functools.partial(_merge_ffn_kernel, alpha=alpha),
        grid=(B, L // tm),
        in_specs=[pl.BlockSpec((1, tm, D), row),
                  pl.BlockSpec((1, tm, PAIRS * LANES), row),
                  pl.BlockSpec((1, tm, PAIRS * LANES), row),
                  pl.BlockSpec((1, nb, tm, LANES), lambda b_, i: (b_, _CB_GA // nb, i, 0)),
                  pl.BlockSpec((1, nb, tm, LANES), lambda b_, i: (b_, _CB_GB // nb, i, 0))]
        + [_layer_spec(p, layer) for p in stacked],
        out_specs=pl.BlockSpec((1, tm, D), row),
        out_shape=jax.ShapeDtypeStruct((B, L, D), F32),
        scratch_shapes=[pltpu.VMEM((tm, D_FF), BF16)],
        compiler_params=_params(("parallel", "parallel")),
    )(x, oa, ob, gates, gates, *stacked)


def _split_w_in(w):
    fq_fk = w[..., 0:2 * GROUP]
    fv = w[..., 2 * GROUP:_OFF_FLOGIT]
    dsa_iq = w[..., _OFF_DQ:_OFF_IK]
    gates = w[..., _OFF_GA:_N_IN]
    w_main = jnp.concatenate([fq_fk, gates, fv, dsa_iq], axis=-1).astype(BF16)
    pad = jnp.zeros(w.shape[:-1] + (LANES - HEAD_DIM - 2 * HEADS,), w.dtype)
    w_small = jnp.concatenate([w[..., _OFF_IK:_OFF_IW], w[..., _OFF_IW:_OFF_GA],
                               w[..., _OFF_FLOGIT:_OFF_DQ], pad], axis=-1).astype(BF16)
    return w_main, w_small


def kernel(x, w_in, b_forget, w_branch_a, w_branch_b, w_out, ln1_g, ln1_b, w_ffn_in, w_ffn_out, ln2_g, ln2_b):
    depth = w_in.shape[0]
    alpha = (2.0 * depth) ** 0.25
    w_main, w_small = _split_w_in(w_in)
    bias_rows = jnp.zeros((depth, 1, LANES), F32).at[:, 0, _SM_FL:_SM_FL + HEADS].set(b_forget)
    wa, wb, wo = w_branch_a.astype(BF16), w_branch_b.astype(BF16), w_out.astype(BF16)
    wi, wf = w_ffn_in.astype(BF16), w_ffn_out.astype(BF16)
    row = lambda p: p[:, None, :]
    stacked = (wa, wb, wo, row(ln1_g), row(ln1_b), wi, wf, row(ln2_g), row(ln2_b))
    for l in range(depth):
        gates, small, fqT, fka, f_own, fvT, dqT, dka, dvT, iqT, iwT = _project(x, w_main, w_small, bias_rows, l)
        o_a = _fox_attention(fqT, f_own, fka, fvT)
        o_b = _dsa_attention(dqT, dka, dvT, iqT, iwT, small)
        x = _merge_ffn(x, o_a, o_b, gates, stacked, l, alpha)
    return x
```
